```python
import math
import jax, jax.numpy as jnp
from jax import lax
import numpy as np

D_MODEL = 1024
BATCH = 32
SEQ = 2048
DEPTH = 1

HEAD_DIM = 64
HEADS_DIFF = 4
HEADS_MOBA = 8
N_HEADS_TOTAL = HEADS_DIFF + HEADS_MOBA
DIFF_WIDTH = HEADS_DIFF * 2 * HEAD_DIM
MOBA_WIDTH = HEADS_MOBA * HEAD_DIM
MIX_WIDTH = DIFF_WIDTH + MOBA_WIDTH
IN_PROJ_WIDTH = 3 * MIX_WIDTH
DIFF_Q_BLOCK = 128
MOBA_BLOCK = 256
MOBA_TOPK = 3
MOBA_Q_CHUNK = 16
NUM_BUCKETS = 32
MAX_DISTANCE = 2048
N_GROUPS = 4
EXPERTS_PER_GROUP = 8
N_EXPERTS = N_GROUPS * EXPERTS_PER_GROUP
EXPERTS_PER_TOKEN = 2
EXPERT_HIDDEN = 256
DISPATCH_CHUNK = 256
RMS_EPS = 1e-6
NEG_INF = -1e30

kernel_name = 'hybrid_diffattn_moba_hier_moe'


def rms_norm(x, g):
    xf = x.astype(jnp.float32)
    y = xf * lax.rsqrt(jnp.mean(xf * xf, axis=-1, keepdims=True) + RMS_EPS)
    return (y * g.astype(jnp.float32)).astype(x.dtype)


def t5_bucket(dist):
    n = jnp.maximum(dist, 0)
    max_exact = NUM_BUCKETS // 2
    nf = jnp.maximum(n, 1).astype(jnp.float32)
    large = max_exact + (jnp.log(nf / max_exact) / math.log(MAX_DISTANCE / max_exact)
                         * (NUM_BUCKETS - max_exact)).astype(jnp.int32)
    large = jnp.minimum(large, NUM_BUCKETS - 1)
    return jnp.where(n < max_exact, n, large)


def diff_attention(q, k, v, q_g, k_g, lq1, lk1, lq2, lk2, sub_g, bias_table, lambda_init):
    B, S, H = q.shape[:3]
    q = rms_norm(q, q_g).transpose(0, 2, 3, 1, 4)
    k = rms_norm(k, k_g).transpose(0, 2, 3, 1, 4)
    v = v.transpose(0, 2, 1, 3)
    f32 = jnp.float32
    lam = (jnp.exp(jnp.sum(lq1.astype(f32) * lk1.astype(f32)))
           - jnp.exp(jnp.sum(lq2.astype(f32) * lk2.astype(f32))) + lambda_init)
    scale = HEAD_DIM ** -0.5
    kpos = jnp.arange(S)

    def block(i):
        s0 = i * DIFF_Q_BLOCK
        qb = lax.dynamic_slice_in_dim(q, s0, DIFF_Q_BLOCK, axis=3)
        logits = jnp.einsum('bhmqd,bhmkd->bhmqk', qb, k, preferred_element_type=f32) * scale
        dist = (s0 + jnp.arange(DIFF_Q_BLOCK))[:, None] - kpos[None, :]
        bias = bias_table[t5_bucket(dist)].astype(f32).transpose(2, 0, 1)
        logits = jnp.where(dist >= 0, logits + bias[None, :, None], NEG_INF)
        p = jax.nn.softmax(logits, axis=-1)
        w = p[:, :, 0] - lam * p[:, :, 1]
        return jnp.einsum('bhqk,bhkc->bhqc', w.astype(v.dtype), v)

    out = lax.map(block, jnp.arange(S // DIFF_Q_BLOCK))
    out = out.transpose(1, 0, 3, 2, 4).reshape(B, S, H, 2 * HEAD_DIM)
    out = rms_norm(out, sub_g) * (1.0 - lambda_init)
    return out.reshape(B, S, H * 2 * HEAD_DIM)


def moba_attention(q, k, v, q_g, k_g, bias_table):
    B, S, H, d = q.shape
    f32 = jnp.float32
    q = rms_norm(q, q_g).transpose(0, 2, 1, 3)
    k = rms_norm(k, k_g).transpose(0, 2, 1, 3)
    v = v.transpose(0, 2, 1, 3)
    n_kb = -(-S // MOBA_BLOCK)
    pad = n_kb * MOBA_BLOCK - S
    k_blocks = jnp.pad(k, ((0, 0), (0, 0), (0, pad), (0, 0))).reshape(B, H, n_kb, MOBA_BLOCK, d)
    v_blocks = jnp.pad(v, ((0, 0), (0, 0), (0, pad), (0, 0))).reshape(B, H, n_kb, MOBA_BLOCK, d)
    k_mean = jnp.mean(k_blocks.astype(f32), axis=3)
    top = min(MOBA_TOPK, n_kb)
    scale = HEAD_DIM ** -0.5
    bi = jnp.arange(B)[:, None, None, None]
    hi = jnp.arange(H)[None, :, None, None]
    blk_off = jnp.arange(MOBA_BLOCK)

    def chunk(i):
        s0 = i * MOBA_Q_CHUNK
        qc = lax.dynamic_slice_in_dim(q, s0, MOBA_Q_CHUNK, axis=2)
        qpos = s0 + jnp.arange(MOBA_Q_CHUNK)
        own = s0 // MOBA_BLOCK
        gate = jnp.einsum('bhqd,bhnd->bhqn', qc.astype(f32), k_mean)
        gate = jnp.where(jnp.arange(n_kb) < own, gate, NEG_INF)
        _, sel = lax.top_k(gate, top)
        sel_valid = jnp.arange(top)[:, None] < own
        k_sel = k_blocks[bi, hi, sel]
        v_sel = v_blocks[bi, hi, sel]
        sel_logits = jnp.einsum('bhqd,bhqtkd->bhqtk', qc, k_sel, preferred_element_type=f32) * scale
        sel_pos = sel[..., None] * MOBA_BLOCK + blk_off
        sel_bias = bias_table[t5_bucket(qpos[:, None, None] - sel_pos), hi[..., None]].astype(f32)
        sel_logits = jnp.where(sel_valid, sel_logits + sel_bias, NEG_INF)
        k_own = lax.dynamic_index_in_dim(k_blocks, own, axis=2, keepdims=False)
        v_own = lax.dynamic_index_in_dim(v_blocks, own, axis=2, keepdims=False)
        dist_own = qpos[:, None] - (own * MOBA_BLOCK + blk_off)[None, :]
        own_bias = bias_table[t5_bucket(dist_own)].astype(f32).transpose(2, 0, 1)[None]
        own_logits = jnp.einsum('bhqd,bhkd->bhqk', qc, k_own, preferred_element_type=f32) * scale
        own_logits = jnp.where(dist_own >= 0, own_logits + own_bias, NEG_INF)
        logits = jnp.concatenate([sel_logits.reshape(B, H, MOBA_Q_CHUNK, top * MOBA_BLOCK), own_logits], axis=-1)
        p = jax.nn.softmax(logits, axis=-1).astype(v.dtype)
        p_sel = p[..., :top * MOBA_BLOCK].reshape(B, H, MOBA_Q_CHUNK, top, MOBA_BLOCK)
        p_own = p[..., top * MOBA_BLOCK:]
        return (jnp.einsum('bhqtk,bhqtkd->bhqd', p_sel, v_sel)
                + jnp.einsum('bhqk,bhkd->bhqd', p_own, v_own))

    out = lax.map(chunk, jnp.arange(S // MOBA_Q_CHUNK))
    return out.transpose(1, 0, 3, 2, 4).reshape(B, S, H * d)


def hier_moe(h, router_group, router_expert, w_gate, w_up, w_down):
    B, S, D = h.shape
    N = B * S
    t = h.reshape(N, D)
    f32 = jnp.float32
    g_logits = (t @ router_group).astype(f32)
    g_prob = jax.nn.softmax(g_logits, axis=-1)
    g = jnp.argmax(g_logits, axis=-1)
    p_group = jnp.take_along_axis(g_prob, g[:, None], axis=-1)
    e_logits_all = jnp.einsum('nd,gde->nge', t, router_expert).astype(f32)
    e_logits = jnp.take_along_axis(e_logits_all, g[:, None, None], axis=1)[:, 0]
    top_v, top_e = lax.top_k(e_logits, EXPERTS_PER_TOKEN)
    weight = p_group * jax.nn.softmax(top_v, axis=-1)
    expert_id = g[:, None] * EXPERTS_PER_GROUP + top_e
    A = N * EXPERTS_PER_TOKEN
    C = DISPATCH_CHUNK
    eid = expert_id.reshape(A)
    tok = jnp.repeat(jnp.arange(N, dtype=jnp.int32), EXPERTS_PER_TOKEN)
    w_flat = weight.reshape(A)
    order = jnp.argsort(eid)
    eid_s, tok_s, w_s = eid[order], tok[order], w_flat[order]
    counts = jax.ops.segment_sum(jnp.ones((A,), jnp.int32), eid, num_segments=N_EXPERTS)
    start = jnp.cumsum(counts) - counts
    pad_counts = ((counts + C - 1) // C) * C
    pad_end = jnp.cumsum(pad_counts)
    pad_start = pad_end - pad_counts
    dest = pad_start[eid_s] + (jnp.arange(A, dtype=jnp.int32) - start[eid_s])
    P = A + N_EXPERTS * C
    n_chunks = P // C
    slot_tok = jnp.zeros((P,), jnp.int32).at[dest].set(tok_s)
    slot_w = jnp.zeros((P,), f32).at[dest].set(w_s)
    chunk_e = jnp.minimum(jnp.searchsorted(pad_end, jnp.arange(n_chunks) * C, side='right'), N_EXPERTS - 1)

    def run(args):
        e, toks = args
        xc = t[toks]
        hid = jax.nn.silu(xc @ w_gate[e]) * (xc @ w_up[e])
        return hid @ w_down[e]

    out = lax.map(run, (chunk_e, slot_tok.reshape(n_chunks, C))).reshape(P, D)
    y = jnp.zeros((N, D), t.dtype).at[slot_tok].add((out * slot_w[:, None]).astype(t.dtype))
    return y.reshape(B, S, D)


def setup_inputs(seed: int = 0) -> dict:
    key = jax.random.key(seed)
    ks = jax.random.split(key, 20)
    f32 = jnp.float32

    def nrm(k, shape, scale):
        return jax.random.normal(k, shape, f32) * scale

    def gain(k, shape):
        return 1.0 + 0.05 * jax.random.normal(k, shape, f32)

    return {
        'x': nrm(ks[0], (BATCH, SEQ, D_MODEL), 1.0),
        'norm1_g': gain(ks[1], (DEPTH, D_MODEL)),
        'w_in': nrm(ks[2], (DEPTH, D_MODEL, IN_PROJ_WIDTH), D_MODEL ** -0.5),
        'diff_q_g': gain(ks[3], (DEPTH, HEAD_DIM)),
        'diff_k_g': gain(ks[4], (DEPTH, HEAD_DIM)),
        'lambda_q1': nrm(ks[5], (DEPTH, HEAD_DIM), 0.1),
        'lambda_k1': nrm(ks[6], (DEPTH, HEAD_DIM), 0.1),
        'lambda_q2': nrm(ks[7], (DEPTH, HEAD_DIM), 0.1),
        'lambda_k2': nrm(ks[8], (DEPTH, HEAD_DIM), 0.1),
        'diff_sub_g': gain(ks[9], (DEPTH, 2 * HEAD_DIM)),
        'moba_q_g': gain(ks[10], (DEPTH, HEAD_DIM)),
        'moba_k_g': gain(ks[11], (DEPTH, HEAD_DIM)),
        'rel_bias': nrm(ks[12], (NUM_BUCKETS, N_HEADS_TOTAL), 0.5),
        'w_out': nrm(ks[13], (DEPTH, MIX_WIDTH, D_MODEL), MIX_WIDTH ** -0.5),
        'norm2_g': gain(ks[14], (DEPTH, D_MODEL)),
        'router_group': nrm(ks[15], (DEPTH, D_MODEL, N_GROUPS), D_MODEL ** -0.5),
        'router_expert': nrm(ks[16], (DEPTH, N_GROUPS, D_MODEL, EXPERTS_PER_GROUP), D_MODEL ** -0.5),
        'w_gate': nrm(ks[17], (DEPTH, N_EXPERTS, D_MODEL, EXPERT_HIDDEN), D_MODEL ** -0.5),
        'w_up': nrm(ks[18], (DEPTH, N_EXPERTS, D_MODEL, EXPERT_HIDDEN), D_MODEL ** -0.5),
        'w_down': nrm(ks[19], (DEPTH, N_EXPERTS, EXPERT_HIDDEN, D_MODEL), EXPERT_HIDDEN ** -0.5),
    }


def reference(x, norm1_g, w_in, diff_q_g, diff_k_g, lambda_q1, lambda_k1, lambda_q2, lambda_k2,
              diff_sub_g, moba_q_g, moba_k_g, rel_bias, w_out, norm2_g, router_group,
              router_expert, w_gate, w_up, w_down):
    B, S, _ = x.shape
    bias_diff = rel_bias[:, :HEADS_DIFF]
    bias_moba = rel_bias[:, HEADS_DIFF:]
    splits = [DIFF_WIDTH, 2 * DIFF_WIDTH, 3 * DIFF_WIDTH,
              3 * DIFF_WIDTH + MOBA_WIDTH, 3 * DIFF_WIDTH + 2 * MOBA_WIDTH]
    for layer in range(DEPTH):
        lambda_init = 0.8 - 0.6 * math.exp(-0.3 * layer)
        h = rms_norm(x, norm1_g[layer])
        proj = h @ w_in[layer]
        q_d, k_d, v_d, q_m, k_m, v_m = jnp.split(proj, splits, axis=-1)
        y_d = diff_attention(q_d.reshape(B, S, HEADS_DIFF, 2, HEAD_DIM),
                             k_d.reshape(B, S, HEADS_DIFF, 2, HEAD_DIM),
                             v_d.reshape(B, S, HEADS_DIFF, 2 * HEAD_DIM),
                             diff_q_g[layer], diff_k_g[layer], lambda_q1[layer], lambda_k1[layer],
                             lambda_q2[layer], lambda_k2[layer], diff_sub_g[layer], bias_diff, lambda_init)
        y_m = moba_attention(q_m.reshape(B, S, HEADS_MOBA, HEAD_DIM),
                             k_m.reshape(B, S, HEADS_MOBA, HEAD_DIM),
                             v_m.reshape(B, S, HEADS_MOBA, HEAD_DIM),
                             moba_q_g[layer], moba_k_g[layer], bias_moba)
        x = x + jnp.concatenate([y_d, y_m], axis=-1) @ w_out[layer]
        x = x + hier_moe(rms_norm(x, norm2_g[layer]), router_group[layer], router_expert[layer],
                         w_gate[layer], w_up[layer], w_down[layer])
    return x
```

```python
import functools
import math

import numpy as np
import jax
import jax.numpy as jnp
from jax import lax
from jax.experimental import pallas as pl
from jax.experimental.pallas import tpu as pltpu

D_MODEL = 1024
HEAD_DIM = 64
HEADS_DIFF = 4
HEADS_MOBA = 8
N_HEADS_TOTAL = HEADS_DIFF + HEADS_MOBA
DIFF_WIDTH = HEADS_DIFF * 2 * HEAD_DIM
MOBA_WIDTH = HEADS_MOBA * HEAD_DIM
QK_WIDTH = 2 * DIFF_WIDTH + 2 * MOBA_WIDTH
V_WIDTH = DIFF_WIDTH + MOBA_WIDTH
MOBA_TOPK = 3
NUM_BUCKETS = 32
MAX_DISTANCE = 2048
N_GROUPS = 4
EXPERTS_PER_GROUP = 8
N_EXPERTS = N_GROUPS * EXPERTS_PER_GROUP
EXPERTS_PER_TOKEN = 2
EXPERT_HIDDEN = 256
RMS_EPS = 1e-6
NEG_INF = -1e30
LAMBDA_INIT = 0.8 - 0.6 * math.exp(-0.3 * 0)

LANES = 128
ROW_TILE = D_MODEL // LANES
ATT_TILE = 256
PROJ_ROWS = 512
OUT_ROWS = 256
MOE_CHUNK = 256
MOVE_ROWS = 256
ROUTER_ROWS = 40
VMEM_LIMIT = 48 * 1024 * 1024

F32 = jnp.float32
BF16 = jnp.bfloat16
_NT = (((1,), (1,)), ((), ()))


def _t5_thresholds():
    n = np.arange(0, 1 << 16)
    max_exact = NUM_BUCKETS // 2
    nf = np.maximum(n, 1).astype(np.float32)
    large = max_exact + (np.log(nf / np.float32(max_exact)) / np.float32(math.log(MAX_DISTANCE / max_exact))
                         * np.float32(NUM_BUCKETS - max_exact)).astype(np.int32)
    bucket = np.where(n < max_exact, n, np.minimum(large, NUM_BUCKETS - 1))
    return [int(np.searchsorted(bucket, b, side="left")) for b in range(1, NUM_BUCKETS)]


_T5_THRESHOLDS = _t5_thresholds()


def _params(*sem):
    return pltpu.CompilerParams(dimension_semantics=sem, vmem_limit_bytes=VMEM_LIMIT)


def _bias_kernel(tab_ref, out_ref):
    h = pl.program_id(0)
    d = pl.program_id(1)
    kj = lax.broadcasted_iota(jnp.int32, (ATT_TILE, ATT_TILE), 0)
    qi = lax.broadcasted_iota(jnp.int32, (ATT_TILE, ATT_TILE), 1)
    dist = d * ATT_TILE + qi - kj
    val = jnp.full((ATT_TILE, ATT_TILE), tab_ref[0, h], F32)
    for b, thr in enumerate(_T5_THRESHOLDS, start=1):
        val = jnp.where(dist >= thr, tab_ref[b, h], val)
    out_ref[...] = jnp.where(dist < 0, NEG_INF, val)


def _bias_tiles(rel_bias, n_diag):
    return pl.pallas_call(
        _bias_kernel,
        grid=(N_HEADS_TOTAL, n_diag),
        in_specs=[pl.BlockSpec(memory_space=pltpu.SMEM)],
        out_specs=pl.BlockSpec((None, None, ATT_TILE, ATT_TILE), lambda h, d: (h, d, 0, 0)),
        out_shape=jax.ShapeDtypeStruct((N_HEADS_TOTAL, n_diag, ATT_TILE, ATT_TILE), F32),
        compiler_params=_params("arbitrary", "arbitrary"),
        name="bias_tiles",
    )(rel_bias)


def _in_proj_kernel(x_ref, g1_ref, w_ref, pg_ref, grp_ref, qk_ref, vt_ref):
    x = x_ref[...]
    ms = jnp.mean(x * x, axis=-1, keepdims=True)
    h = (x * lax.rsqrt(ms + RMS_EPS) * g1_ref[...]).astype(BF16)
    cw = 2 * LANES
    for c in range(QK_WIDTH // cw):
        cols = slice(c * cw, (c + 1) * cw)
        p = jnp.dot(h, w_ref[:, cols], preferred_element_type=F32)
        msq = jnp.dot((p * p).astype(BF16), grp_ref[...], preferred_element_type=F32)
        qk_ref[:, cols] = (p * lax.rsqrt(msq + RMS_EPS) * pg_ref[:, cols]).astype(BF16)
    for c in range(V_WIDTH // cw):
        p = jnp.dot(h, w_ref[:, QK_WIDTH + c * cw:QK_WIDTH + (c + 1) * cw], preferred_element_type=F32)
        for t in range(PROJ_ROWS // ATT_TILE):
            vt_ref[t, c * cw:(c + 1) * cw, :] = p[t * ATT_TILE:(t + 1) * ATT_TILE, :].T.astype(BF16)


def _in_proj(x2, g1, w_perm, post_gain, grp, batch, seq):
    n = x2.shape[0]
    steps_per_seq = seq // PROJ_ROWS
    tiles_per_step = PROJ_ROWS // ATT_TILE
    return pl.pallas_call(
        _in_proj_kernel,
        grid=(n // PROJ_ROWS,),
        in_specs=[
            pl.BlockSpec((PROJ_ROWS, D_MODEL), lambda i: (i, 0)),
            pl.BlockSpec((1, D_MODEL), lambda i: (0, 0)),
            pl.BlockSpec((D_MODEL, QK_WIDTH + V_WIDTH), lambda i: (0, 0)),
            pl.BlockSpec((1, QK_WIDTH), lambda i: (0, 0)),
            pl.BlockSpec((2 * LANES, 2 * LANES), lambda i: (0, 0)),
        ],
        out_specs=[
            pl.BlockSpec((PROJ_ROWS, QK_WIDTH), lambda i: (i, 0)),
            pl.BlockSpec((None, tiles_per_step, V_WIDTH, ATT_TILE),
                         lambda i: (i // steps_per_seq, i % steps_per_seq, 0, 0)),
        ],
        out_shape=[
            jax.ShapeDtypeStruct((n, QK_WIDTH), BF16),
            jax.ShapeDtypeStruct((batch, seq // ATT_TILE, V_WIDTH, ATT_TILE), BF16),
        ],
        compiler_params=_params("arbitrary"),
        name="in_proj",
    )(x2, g1, w_perm, post_gain, grp)


def _softmax_step(s, m, l):
    m_new = jnp.maximum(m, jnp.max(s, axis=0, keepdims=True))
    alpha = jnp.exp(m - m_new)
    p = jnp.exp(s - m_new)
    l_new = alpha * l + jnp.sum(p, axis=0, keepdims=True)
    return p.astype(BF16), alpha, m_new, l_new


def _half_lane_split(q):
    lane = lax.broadcasted_iota(jnp.int32, q.shape, 1)
    zero = jnp.zeros_like(q)
    return jnp.where(lane < HEAD_DIM, q, zero), jnp.where(lane >= HEAD_DIM, q, zero)


def _diff_kernel(lq1_ref, lk1_ref, lq2_ref, lk2_ref, subg_ref, q_ref, k_ref, vt_ref, bias_ref, o_ref,
                 acc1_ref, acc2_ref):
    qi = pl.program_id(2)
    q1, q2 = _half_lane_split(q_ref[...])
    acc1_ref[...] = jnp.zeros_like(acc1_ref)
    acc2_ref[...] = jnp.zeros_like(acc2_ref)

    def step(j, carry):
        m1, l1, m2, l2 = carry
        kv = qi - j
        k = k_ref[kv]
        vt = vt_ref[kv]
        bias = bias_ref[j]
        s1 = lax.dot_general(k, q1, _NT, preferred_element_type=F32) + bias
        s2 = lax.dot_general(k, q2, _NT, preferred_element_type=F32) + bias
        p1, a1, m1, l1 = _softmax_step(s1, m1, l1)
        p2, a2, m2, l2 = _softmax_step(s2, m2, l2)
        acc1_ref[...] = acc1_ref[...] * a1 + jnp.dot(vt, p1, preferred_element_type=F32)
        acc2_ref[...] = acc2_ref[...] * a2 + jnp.dot(vt, p2, preferred_element_type=F32)
        return m1, l1, m2, l2

    m0 = jnp.full((1, ATT_TILE), NEG_INF, F32)
    l0 = jnp.zeros((1, ATT_TILE), F32)
    _, l1, _, l2 = lax.fori_loop(0, qi + 1, step, (m0, l0, m0, l0))

    lam = (jnp.exp(jnp.sum(lq1_ref[...] * lk1_ref[...], keepdims=True))
           - jnp.exp(jnp.sum(lq2_ref[...] * lk2_ref[...], keepdims=True)) + LAMBDA_INIT)
    o = acc1_ref[...] / l1 - lam * (acc2_ref[...] / l2)
    ms = jnp.mean(o * o, axis=0, keepdims=True)
    o = o * lax.rsqrt(ms + RMS_EPS) * subg_ref[...] * (1.0 - LAMBDA_INIT)
    o_ref[...] = o.T.astype(BF16)


def _diff_attention(lam_vecs, sub_g_col, qk4, vt4, bias):
    batch, n_tiles = qk4.shape[0], qk4.shape[1]
    k_col0 = DIFF_WIDTH // LANES
    vec = pl.BlockSpec((1, HEAD_DIM), lambda h, b, i: (0, 0))
    return pl.pallas_call(
        _diff_kernel,
        grid=(HEADS_DIFF, batch, n_tiles),
        in_specs=[
            vec, vec, vec, vec,
            pl.BlockSpec((2 * HEAD_DIM, 1), lambda h, b, i: (0, 0)),
            pl.BlockSpec((None, None, ATT_TILE, LANES), lambda h, b, i: (b, i, 0, h)),
            pl.BlockSpec((None, n_tiles, ATT_TILE, LANES), lambda h, b, i: (b, 0, 0, k_col0 + h)),
            pl.BlockSpec((None, n_tiles, LANES, ATT_TILE), lambda h, b, i: (b, 0, h, 0)),
            pl.BlockSpec((None, n_tiles, ATT_TILE, ATT_TILE), lambda h, b, i: (h, 0, 0, 0)),
        ],
        out_specs=pl.BlockSpec((None, None, ATT_TILE, LANES), lambda h, b, i: (b, i, 0, h)),
        out_shape=jax.ShapeDtypeStruct((batch, n_tiles, ATT_TILE, DIFF_WIDTH), BF16),
        scratch_shapes=[pltpu.VMEM((2 * HEAD_DIM, ATT_TILE), F32), pltpu.VMEM((2 * HEAD_DIM, ATT_TILE), F32)],
        compiler_params=_params("arbitrary", "arbitrary", "arbitrary"),
        name="diff_attention",
    )(*lam_vecs, sub_g_col, qk4, qk4, vt4, bias)


def _split3(v):
    hi = v.astype(BF16)
    r1 = v - hi.astype(F32)
    mid = r1.astype(BF16)
    lo = (r1 - mid.astype(F32)).astype(BF16)
    return hi, mid, lo


def _block_mask(gate, own):
    n_blocks = gate.shape[0]
    row = lax.broadcasted_iota(jnp.int32, gate.shape, 0)
    rank = jnp.zeros(gate.shape, jnp.int32)
    for m in range(n_blocks):
        gm = gate[m:m + 1, :]
        beats = (gm > gate) | ((gm == gate) & (row > m))
        rank = rank + jnp.where(beats & (own > m), 1, 0)
    keep = ((rank < MOBA_TOPK) & (row < own)) | (row == own)
    return jnp.where(keep, 0.0, NEG_INF).astype(F32)


def _moba_kernel(q_ref, k_ref, vt_ref, bias_ref, o_ref, kmean_ref, mask_a_ref, mask_b_ref, acc_a_ref, acc_b_ref):
    qi = pl.program_id(2)
    n_blocks = k_ref.shape[0]

    @pl.when(qi == 0)
    def _():
        for n in range(n_blocks):
            kmean_ref[n:n + 1, :] = jnp.mean(k_ref[n].astype(F32), axis=0, keepdims=True)

    qa, qb = _half_lane_split(q_ref[...])
    gate_a = jnp.zeros((n_blocks, ATT_TILE), F32)
    gate_b = jnp.zeros((n_blocks, ATT_TILE), F32)
    for part in _split3(kmean_ref[...]):
        gate_a = gate_a + lax.dot_general(part, qa, _NT, preferred_element_type=F32)
        gate_b = gate_b + lax.dot_general(part, qb, _NT, preferred_element_type=F32)
    mask_a_ref[...] = _block_mask(gate_a, qi)
    mask_b_ref[...] = _block_mask(gate_b, qi)
    acc_a_ref[...] = jnp.zeros_like(acc_a_ref)
    acc_b_ref[...] = jnp.zeros_like(acc_b_ref)

    def step(j, carry):
        ma, la, mb, lb = carry
        kv = qi - j
        k = k_ref[kv]
        vt = vt_ref[kv]
        sa = (lax.dot_general(k, qa, _NT, preferred_element_type=F32) + bias_ref[0, j]
              + mask_a_ref[pl.ds(kv, 1), :])
        sb = (lax.dot_general(k, qb, _NT, preferred_element_type=F32) + bias_ref[1, j]
              + mask_b_ref[pl.ds(kv, 1), :])
        pa, aa, ma, la = _softmax_step(sa, ma, la)
        pb, ab, mb, lb = _softmax_step(sb, mb, lb)
        acc_a_ref[...] = acc_a_ref[...] * aa + jnp.dot(vt[:HEAD_DIM, :], pa, preferred_element_type=F32)
        acc_b_ref[...] = acc_b_ref[...] * ab + jnp.dot(vt[HEAD_DIM:, :], pb, preferred_element_type=F32)
        return ma, la, mb, lb

    m0 = jnp.full((1, ATT_TILE), NEG_INF, F32)
    l0 = jnp.zeros((1, ATT_TILE), F32)
    _, la, _, lb = lax.fori_loop(0, qi + 1, step, (m0, l0, m0, l0))
    o = jnp.concatenate([acc_a_ref[...] / la, acc_b_ref[...] / lb], axis=0)
    o_ref[...] = o.T.astype(BF16)


def _moba_attention(qk4, vt4, bias_pairs):
    batch, n_tiles = qk4.shape[0], qk4.shape[1]
    q_col0 = 2 * DIFF_WIDTH // LANES
    k_col0 = q_col0 + MOBA_WIDTH // LANES
    v_row0 = DIFF_WIDTH // LANES
    pair0 = HEADS_DIFF // 2
    return pl.pallas_call(
        _moba_kernel,
        grid=(HEADS_MOBA // 2, batch, n_tiles),
        in_specs=[
            pl.BlockSpec((None, None, ATT_TILE, LANES), lambda h, b, i: (b, i, 0, q_col0 + h)),
            pl.BlockSpec((None, n_tiles, ATT_TILE, LANES), lambda h, b, i: (b, 0, 0, k_col0 + h)),
            pl.BlockSpec((None, n_tiles, LANES, ATT_TILE), lambda h, b, i: (b, 0, v_row0 + h, 0)),
            pl.BlockSpec((None, 2, n_tiles, ATT_TILE, ATT_TILE), lambda h, b, i: (pair0 + h, 0, 0, 0, 0)),
        ],
        out_specs=pl.BlockSpec((None, None, ATT_TILE, LANES), lambda h, b, i: (b, i, 0, h)),
        out_shape=jax.ShapeDtypeStruct((batch, n_tiles, ATT_TILE, MOBA_WIDTH), BF16),
        scratch_shapes=[
            pltpu.VMEM((n_tiles, LANES), F32),
            pltpu.VMEM((n_tiles, ATT_TILE), F32),
            pltpu.VMEM((n_tiles, ATT_TILE), F32),
            pltpu.VMEM((HEAD_DIM, ATT_TILE), F32),
            pltpu.VMEM((HEAD_DIM, ATT_TILE), F32),
        ],
        compiler_params=_params("arbitrary", "arbitrary", "arbitrary"),
        name="moba_attention",
    )(qk4, qk4, vt4, bias_pairs)


def _load_token_rows(ref, n_rows, lead=()):
    chunks = [ref[lead + (pl.ds(c, n_rows, stride=ROW_TILE), slice(None))] for c in range(ROW_TILE)]
    return jnp.concatenate(chunks, axis=1)


def _store_token_rows(ref, value):
    n_rows = value.shape[0]
    for c in range(ROW_TILE):
        ref[pl.ds(c, n_rows, stride=ROW_TILE), :] = value[:, c * LANES:(c + 1) * LANES]


def _token_tile(ref, t):
    return ref.at[pl.ds(pl.multiple_of(t * ROW_TILE, ROW_TILE), ROW_TILE)]


def _first_argmax(v):
    top = jnp.max(v, axis=0, keepdims=True)
    row = lax.broadcasted_iota(jnp.int32, v.shape, 0)
    idx = jnp.min(jnp.where(v == top, row, v.shape[0]), axis=0, keepdims=True)
    return top, idx


def _out_proj_kernel(yd_ref, ym_ref, x_ref, wd_ref, wm_ref, g2_ref, wr_ref, tri_ref,
                     x1_ref, ri_ref, rf_ref, cnt_ref, run_ref):
    @pl.when(pl.program_id(0) == 0)
    def _():
        run_ref[...] = jnp.zeros_like(run_ref)

    x1 = (x_ref[...] + jnp.dot(yd_ref[...], wd_ref[...], preferred_element_type=F32)
          + jnp.dot(ym_ref[...], wm_ref[...], preferred_element_type=F32))
    _store_token_rows(x1_ref, x1)
    ms = jnp.mean(x1 * x1, axis=-1, keepdims=True)
    h2 = (x1 * lax.rsqrt(ms + RMS_EPS) * g2_ref[...]).astype(BF16)
    logits = lax.dot_general(wr_ref[...], h2, _NT, preferred_element_type=F32)

    g_logits = logits[N_EXPERTS:N_EXPERTS + N_GROUPS, :]
    g_top, g_idx = _first_argmax(g_logits)
    p_group = 1.0 / jnp.sum(jnp.exp(g_logits - g_top), axis=0, keepdims=True)
    e_logits = logits[0:EXPERTS_PER_GROUP, :]
    for g in range(1, N_GROUPS):
        e_logits = jnp.where(g_idx == g, logits[g * EXPERTS_PER_GROUP:(g + 1) * EXPERTS_PER_GROUP, :], e_logits)
    v1, i1 = _first_argmax(e_logits)
    row = lax.broadcasted_iota(jnp.int32, e_logits.shape, 0)
    v2, i2 = _first_argmax(jnp.where(row == i1, -jnp.inf, e_logits))
    t = jnp.exp(v2 - v1)
    w1 = p_group / (1.0 + t)
    w2 = p_group * t / (1.0 + t)
    e1 = g_idx * EXPERTS_PER_GROUP + i1
    e2 = g_idx * EXPERTS_PER_GROUP + i2

    erow = lax.broadcasted_iota(jnp.int32, (N_EXPERTS, OUT_ROWS), 0)
    hit1 = erow == e1
    hit2 = erow == e2
    onehot = jnp.where(hit1 | hit2, 1.0, 0.0).astype(F32)
    before = jnp.dot(onehot.astype(BF16), tri_ref[...], preferred_element_type=F32) + run_ref[...]
    r1 = jnp.sum(jnp.where(hit1, before, 0.0), axis=0, keepdims=True).astype(jnp.int32)
    r2 = jnp.sum(jnp.where(hit2, before, 0.0), axis=0, keepdims=True).astype(jnp.int32)
    run_ref[...] = run_ref[...] + jnp.sum(onehot, axis=1, keepdims=True)

    out_row = lax.broadcasted_iota(jnp.int32, ri_ref.shape, 0)
    ri_ref[...] = jnp.where(out_row == 0, e1, jnp.where(out_row == 1, e2,
                            jnp.where(out_row == 2, r1, jnp.where(out_row == 3, r2, 0))))
    rf_ref[...] = jnp.where(out_row == 0, w1, jnp.where(out_row == 1, w2, 0.0))
    cnt_ref[...] = jnp.broadcast_to(run_ref[...], cnt_ref.shape).astype(jnp.int32)


def _out_proj(y_d, y_m, x2, wo_d, wo_m, g2, w_router, tri):
    n = x2.shape[0]
    const = lambda i: (0, 0)
    return pl.pallas_call(
        _out_proj_kernel,
        grid=(n // OUT_ROWS,),
        in_specs=[
            pl.BlockSpec((OUT_ROWS, DIFF_WIDTH), lambda i: (i, 0)),
            pl.BlockSpec((OUT_ROWS, MOBA_WIDTH), lambda i: (i, 0)),
            pl.BlockSpec((OUT_ROWS, D_MODEL), lambda i: (i, 0)),
            pl.BlockSpec((DIFF_WIDTH, D_MODEL), const),
            pl.BlockSpec((MOBA_WIDTH, D_MODEL), const),
            pl.BlockSpec((1, D_MODEL), const),
            pl.BlockSpec((ROUTER_ROWS, D_MODEL), const),
            pl.BlockSpec((OUT_ROWS, OUT_ROWS), const),
        ],
        out_specs=[
            pl.BlockSpec((OUT_ROWS * ROW_TILE, LANES), lambda i: (i, 0)),
            pl.BlockSpec((8, OUT_ROWS), lambda i: (0, i)),
            pl.BlockSpec((8, OUT_ROWS), lambda i: (0, i)),
            pl.BlockSpec((N_EXPERTS, LANES), const),
        ],
        out_shape=[
            jax.ShapeDtypeStruct((n * ROW_TILE, LANES), F32),
            jax.ShapeDtypeStruct((8, n), jnp.int32),
            jax.ShapeDtypeStruct((8, n), F32),
            jax.ShapeDtypeStruct((N_EXPERTS, LANES), jnp.int32),
        ],
        scratch_shapes=[pltpu.VMEM((N_EXPERTS, 1), F32)],
        compiler_params=_params("arbitrary"),
        name="out_proj_route",
    )(y_d, y_m, x2, wo_d, wo_m, g2, w_router, tri)


def _dispatch_kernel(zchunk_ref, dest_ref, x1_hbm, xs_hbm, zero_ref, zero_sem, row_sem):
    i = pl.program_id(0)
    chunk_rows = MOE_CHUNK * ROW_TILE

    def zero_copy(c):
        start = pl.multiple_of(c * chunk_rows, chunk_rows)
        return pltpu.make_async_copy(zero_ref, xs_hbm.at[pl.ds(start, chunk_rows)], zero_sem)

    @pl.when(i == 0)
    def _():
        zero_ref[...] = jnp.zeros_like(zero_ref)

        def start_one(c, carry):
            @pl.when(zchunk_ref[c] > 0)
            def _():
                zero_copy(c).start()
            return carry

        def wait_one(c, carry):
            @pl.when(zchunk_ref[c] > 0)
            def _():
                zero_copy(c).wait()
            return carry

        lax.fori_loop(0, zchunk_ref.shape[0], start_one, 0)
        lax.fori_loop(0, zchunk_ref.shape[0], wait_one, 0)

    def send_row(r, carry):
        src = _token_tile(x1_hbm, i * MOVE_ROWS + r)
        for k in range(EXPERTS_PER_TOKEN):
            pltpu.make_async_copy(src, _token_tile(xs_hbm, dest_ref[0, k, r]), row_sem).start()
        return carry

    lax.fori_loop(0, MOVE_ROWS, send_row, 0, unroll=8)
    n_sent = EXPERTS_PER_TOKEN * MOVE_ROWS * ROW_TILE
    pltpu.make_async_copy(x1_hbm.at[pl.ds(0, n_sent)], xs_hbm.at[pl.ds(0, n_sent)], row_sem).wait()


def _dispatch(zchunk, dest3, x1, n_slots):
    n = x1.shape[0] // ROW_TILE
    grid_spec = pltpu.PrefetchScalarGridSpec(
        num_scalar_prefetch=1,
        grid=(n // MOVE_ROWS,),
        in_specs=[
            pl.BlockSpec((1, EXPERTS_PER_TOKEN, MOVE_ROWS), lambda i, zc: (i, 0, 0), memory_space=pltpu.SMEM),
            pl.BlockSpec(memory_space=pl.ANY),
        ],
        out_specs=pl.BlockSpec(memory_space=pl.ANY),
        scratch_shapes=[
            pltpu.VMEM((MOE_CHUNK * ROW_TILE, LANES), F32),
            pltpu.SemaphoreType.DMA(()),
            pltpu.SemaphoreType.DMA(()),
        ],
    )
    return pl.pallas_call(
        _dispatch_kernel,
        grid_spec=grid_spec,
        out_shape=jax.ShapeDtypeStruct((n_slots * ROW_TILE, LANES), F32),
        compiler_params=_params("arbitrary"),
        name="moe_dispatch",
    )(zchunk, dest3, x1)


def _expert_kernel(ce_ref, na_ref, xs_ref, g2_ref, wg_ref, wu_ref, wd_ref, o_ref):
    active = pl.program_id(0) < na_ref[0]

    @pl.when(jnp.logical_not(active))
    def _():
        o_ref[...] = jnp.zeros_like(o_ref)

    @pl.when(active)
    def _():
        x = _load_token_rows(xs_ref, MOE_CHUNK)
        ms = jnp.mean(x * x, axis=-1, keepdims=True)
        h = (x * lax.rsqrt(ms + RMS_EPS) * g2_ref[...]).astype(BF16)
        gate = jnp.dot(h, wg_ref[...], preferred_element_type=F32)
        up = jnp.dot(h, wu_ref[...], preferred_element_type=F32)
        hid = (gate * jax.nn.sigmoid(gate) * up).astype(BF16)
        _store_token_rows(o_ref, jnp.dot(hid, wd_ref[...], preferred_element_type=F32))


def _experts(chunk_e, n_active, xs, g2, wg, wu, wd):
    n_slots = xs.shape[0] // ROW_TILE
    rows = lambda c, ce, na: (jnp.minimum(c, na[0] - 1), 0)
    grid_spec = pltpu.PrefetchScalarGridSpec(
        num_scalar_prefetch=2,
        grid=(n_slots // MOE_CHUNK,),
        in_specs=[
            pl.BlockSpec((MOE_CHUNK * ROW_TILE, LANES), rows),
            pl.BlockSpec((1, D_MODEL), lambda c, ce, na: (0, 0)),
            pl.BlockSpec((None, D_MODEL, EXPERT_HIDDEN), lambda c, ce, na: (ce[c], 0, 0)),
            pl.BlockSpec((None, D_MODEL, EXPERT_HIDDEN), lambda c, ce, na: (ce[c], 0, 0)),
            pl.BlockSpec((None, EXPERT_HIDDEN, D_MODEL), lambda c, ce, na: (ce[c], 0, 0)),
        ],
        out_specs=pl.BlockSpec((MOE_CHUNK * ROW_TILE, LANES), lambda c, ce, na: (c, 0)),
    )
    return pl.pallas_call(
        _expert_kernel,
        grid_spec=grid_spec,
        out_shape=jax.ShapeDtypeStruct((n_slots * ROW_TILE, LANES), F32),
        compiler_params=_params("arbitrary"),
        name="moe_experts",
    )(chunk_e, n_active, xs, g2, wg, wu, wd)


def _combine_kernel(dest_ref, dest_next_ref, w_ref, x1_ref, outs_hbm, y_ref, buf_ref, sem_ref):
    i = pl.program_id(0)
    slot = lax.rem(i, 2)

    def fetch(dref, s):
        def fetch_row(r, carry):
            for k in range(EXPERTS_PER_TOKEN):
                pltpu.make_async_copy(_token_tile(outs_hbm, dref[0, k, r]),
                                      _token_tile(buf_ref.at[s, k], r), sem_ref.at[s]).start()
            return carry
        lax.fori_loop(0, MOVE_ROWS, fetch_row, 0, unroll=8)

    @pl.when(i == 0)
    def _():
        fetch(dest_ref, 0)

    @pl.when(i + 1 < pl.num_programs(0))
    def _():
        fetch(dest_next_ref, 1 - slot)

    for k in range(EXPERTS_PER_TOKEN):
        pltpu.make_async_copy(outs_hbm.at[pl.ds(0, MOVE_ROWS * ROW_TILE)], buf_ref.at[slot, k],
                              sem_ref.at[slot]).wait()
    w = w_ref[...]
    y_ref[...] = (_load_token_rows(x1_ref, MOVE_ROWS)
                  + w[:, 0:1] * _load_token_rows(buf_ref, MOVE_ROWS, (slot, 0))
                  + w[:, 1:2] * _load_token_rows(buf_ref, MOVE_ROWS, (slot, 1)))


def _combine(dest3, wts, x1, outs):
    n = x1.shape[0] // ROW_TILE
    steps = n // MOVE_ROWS
    dest_block = (1, EXPERTS_PER_TOKEN, MOVE_ROWS)
    return pl.pallas_call(
        _combine_kernel,
        grid=(steps,),
        in_specs=[
            pl.BlockSpec(dest_block, lambda i: (i, 0, 0), memory_space=pltpu.SMEM),
            pl.BlockSpec(dest_block, lambda i: (jnp.minimum(i + 1, steps - 1), 0, 0), memory_space=pltpu.SMEM),
            pl.BlockSpec((MOVE_ROWS, EXPERTS_PER_TOKEN), lambda i: (i, 0)),
            pl.BlockSpec((MOVE_ROWS * ROW_TILE, LANES), lambda i: (i, 0)),
            pl.BlockSpec(memory_space=pl.ANY),
        ],
        out_specs=pl.BlockSpec((MOVE_ROWS, D_MODEL), lambda i: (i, 0)),
        out_shape=jax.ShapeDtypeStruct((n, D_MODEL), F32),
        scratch_shapes=[
            pltpu.VMEM((2, EXPERTS_PER_TOKEN, MOVE_ROWS * ROW_TILE, LANES), F32),
            pltpu.SemaphoreType.DMA((2,)),
        ],
        compiler_params=_params("arbitrary"),
        name="moe_combine",
    )(dest3, dest3, wts, x1, outs)


def _layer(x, norm1_g, w_in, diff_q_g, diff_k_g, lambda_q1, lambda_k1, lambda_q2, lambda_k2, diff_sub_g,
           moba_q_g, moba_k_g, rel_bias, w_out, norm2_g, router_group, router_expert, w_gate, w_up, w_down):
    batch, seq, _ = x.shape
    n = batch * seq
    n_tiles = seq // ATT_TILE
    scale = HEAD_DIM ** -0.5

    d3 = 3 * DIFF_WIDTH
    w_perm = jnp.concatenate([w_in[:, :2 * DIFF_WIDTH], w_in[:, d3:d3 + 2 * MOBA_WIDTH],
                              w_in[:, 2 * DIFF_WIDTH:d3], w_in[:, d3 + 2 * MOBA_WIDTH:]], axis=1).astype(BF16)
    reps_d, reps_m = DIFF_WIDTH // HEAD_DIM, MOBA_WIDTH // HEAD_DIM
    post_gain = jnp.concatenate([jnp.tile(diff_q_g * scale, reps_d), jnp.tile(diff_k_g, reps_d),
                                 jnp.tile(moba_q_g * scale, reps_m), jnp.tile(moba_k_g, reps_m)])[None, :]
    head_of = np.arange(2 * LANES) // HEAD_DIM
    grp = jnp.asarray((head_of[:, None] == head_of[None, :]) / HEAD_DIM, BF16)

    bias = _bias_tiles(rel_bias, n_tiles)
    qk, vt4 = _in_proj(x.reshape(n, D_MODEL), norm1_g[None, :], w_perm, post_gain, grp, batch, seq)
    qk4 = qk.reshape(batch, n_tiles, ATT_TILE, QK_WIDTH)
    lam_vecs = [v[None, :] for v in (lambda_q1, lambda_k1, lambda_q2, lambda_k2)]
    y_d = _diff_attention(lam_vecs, diff_sub_g[:, None], qk4, vt4, bias)
    y_m = _moba_attention(qk4, vt4, bias.reshape(N_HEADS_TOTAL // 2, 2, n_tiles, ATT_TILE, ATT_TILE))

    w_router = jnp.concatenate([
        router_expert.transpose(0, 2, 1).reshape(N_EXPERTS, D_MODEL), router_group.T,
        jnp.zeros((ROUTER_ROWS - N_EXPERTS - N_GROUPS, D_MODEL), F32)], axis=0).astype(BF16)
    tri = jnp.asarray(np.triu(np.ones((OUT_ROWS, OUT_ROWS)), k=1), BF16)
    wo = w_out.astype(BF16)
    x1, route_i, route_f, cnt = _out_proj(y_d.reshape(n, DIFF_WIDTH), y_m.reshape(n, MOBA_WIDTH),
                                          x.reshape(n, D_MODEL), wo[:DIFF_WIDTH], wo[DIFF_WIDTH:],
                                          norm2_g[None, :], w_router, tri)

    counts = cnt[:, 0]
    pad_counts = ((counts + MOE_CHUNK - 1) // MOE_CHUNK) * MOE_CHUNK
    pad_end = jnp.cumsum(pad_counts)
    pad_start = pad_end - pad_counts
    dest = pad_start[route_i[0:2]] + route_i[2:4]
    dest3 = dest.reshape(EXPERTS_PER_TOKEN, n // MOVE_ROWS, MOVE_ROWS).transpose(1, 0, 2)
    n_slots = n * EXPERTS_PER_TOKEN + N_EXPERTS * MOE_CHUNK
    n_chunks = n_slots // MOE_CHUNK
    chunk_e = jnp.minimum(jnp.searchsorted(pad_end, jnp.arange(n_chunks, dtype=jnp.int32) * MOE_CHUNK,
                                           side="right"), N_EXPERTS - 1).astype(jnp.int32)
    n_active = (pad_end[-1:] // MOE_CHUNK).astype(jnp.int32)
    chunk_id = jnp.arange(n_chunks, dtype=jnp.int32)
    next_e = jnp.concatenate([chunk_e[1:], jnp.full((1,), N_EXPERTS, jnp.int32)])
    zchunk = ((chunk_id >= n_active - 1) | (chunk_e != next_e)).astype(jnp.int32)

    xs = _dispatch(zchunk, dest3, x1, n_slots)
    outs = _experts(chunk_e, n_active, xs, norm2_g[None, :], w_gate.astype(BF16), w_up.astype(BF16),
                    w_down.astype(BF16))
    y = _combine(dest3, route_f[0:2].T, x1, outs)
    return y.reshape(batch, seq, D_MODEL)


def kernel(x, norm1_g, w_in, diff_q_g, diff_k_g, lambda_q1, lambda_k1, lambda_q2, lambda_k2, diff_sub_g,
           moba_q_g, moba_k_g, rel_bias, w_out, norm2_g, router_group, router_expert, w_gate, w_up, w_down):
    assert x.shape[1] % PROJ_ROWS == 0 and x.shape[2] == D_MODEL and norm1_g.shape[0] == 1
    return _layer(x, norm1_g[0], w_in[0], diff_q_g[0], diff_k_g[0], lambda_q1[0], lambda_k1[0], lambda_q2[0],
                  lambda_k2[0], diff_sub_g[0], moba_q_g[0], moba_k_g[0], rel_bias, w_out[0], norm2_g[0],
                  router_group[0], router_expert[0], w_gate[0], w_up[0], w_down[0])
```

```python
import functools
import math

import numpy as np
import jax
import jax.numpy as jnp
from jax import lax
from jax.experimental import pallas as pl
from jax.experimental.pallas import tpu as pltpu

D_MODEL = 1024
HEAD_DIM = 64
HEADS_DIFF = 4
HEADS_MOBA = 8
N_HEADS_TOTAL = HEADS_DIFF + HEADS_MOBA
DIFF_WIDTH = HEADS_DIFF * 2 * HEAD_DIM
MOBA_WIDTH = HEADS_MOBA * HEAD_DIM
QK_WIDTH = 2 * DIFF_WIDTH + 2 * MOBA_WIDTH
V_WIDTH = DIFF_WIDTH + MOBA_WIDTH
MOBA_TOPK = 3
NUM_BUCKETS = 32
MAX_DISTANCE = 2048
N_GROUPS = 4
EXPERTS_PER_GROUP = 8
N_EXPERTS = N_GROUPS * EXPERTS_PER_GROUP
EXPERTS_PER_TOKEN = 2
EXPERT_HIDDEN = 256
RMS_EPS = 1e-6
NEG_INF = -1e30
LAMBDA_INIT = 0.8 - 0.6 * math.exp(-0.3 * 0)

LANES = 128
ROW_TILE = D_MODEL // LANES
ATT_TILE = 256
ATT_BATCH_GROUP = 1
ONES_ROWS = 16
LOG2E = math.log2(math.e)
PROJ_ROWS = 512
OUT_ROWS = 256
MOE_CHUNK = 256
MOVE_ROWS = 256
ROUTER_ROWS = 40
VMEM_LIMIT = 48 * 1024 * 1024

F32 = jnp.float32
BF16 = jnp.bfloat16
_NT = (((1,), (1,)), ((), ()))


def _t5_thresholds():
    n = np.arange(0, 1 << 16)
    max_exact = NUM_BUCKETS // 2
    nf = np.maximum(n, 1).astype(np.float32)
    large = max_exact + (np.log(nf / np.float32(max_exact)) / np.float32(math.log(MAX_DISTANCE / max_exact))
                         * np.float32(NUM_BUCKETS - max_exact)).astype(np.int32)
    bucket = np.where(n < max_exact, n, np.minimum(large, NUM_BUCKETS - 1))
    return [int(np.searchsorted(bucket, b, side="left")) for b in range(1, NUM_BUCKETS)]


_T5_THRESHOLDS = _t5_thresholds()


def _params(*sem):
    return pltpu.CompilerParams(dimension_semantics=sem, vmem_limit_bytes=VMEM_LIMIT)


def _bias_kernel(tab_ref, out_ref):
    h = pl.program_id(0)
    d = pl.program_id(1)
    kj = lax.broadcasted_iota(jnp.int32, (ATT_TILE, ATT_TILE), 0)
    qi = lax.broadcasted_iota(jnp.int32, (ATT_TILE, ATT_TILE), 1)
    dist = d * ATT_TILE + qi - kj
    val = jnp.full((ATT_TILE, ATT_TILE), tab_ref[0, h], F32)
    for b, thr in enumerate(_T5_THRESHOLDS, start=1):
        val = jnp.where(dist >= thr, tab_ref[b, h], val)
    out_ref[...] = jnp.where(dist < 0, NEG_INF, val * LOG2E)


def _bias_tiles(rel_bias, n_diag):
    return pl.pallas_call(
        _bias_kernel,
        grid=(N_HEADS_TOTAL, n_diag),
        in_specs=[pl.BlockSpec(memory_space=pltpu.SMEM)],
        out_specs=pl.BlockSpec((None, None, ATT_TILE, ATT_TILE), lambda h, d: (h, d, 0, 0)),
        out_shape=jax.ShapeDtypeStruct((N_HEADS_TOTAL, n_diag, ATT_TILE, ATT_TILE), F32),
        compiler_params=_params("arbitrary", "arbitrary"),
        name="bias_tiles",
    )(rel_bias)


def _in_proj_kernel(x_ref, g1_ref, w_ref, pg_ref, grp_ref, qk_ref, vt_ref):
    x = x_ref[...]
    ms = jnp.mean(x * x, axis=-1, keepdims=True)
    h = (x * lax.rsqrt(ms + RMS_EPS) * g1_ref[...]).astype(BF16)
    cw = 2 * LANES
    for c in range(QK_WIDTH // cw):
        cols = slice(c * cw, (c + 1) * cw)
        p = jnp.dot(h, w_ref[:, cols], preferred_element_type=F32)
        msq = jnp.dot((p * p).astype(BF16), grp_ref[...], preferred_element_type=F32)
        qk_ref[:, cols] = (p * lax.rsqrt(msq + RMS_EPS) * pg_ref[:, cols]).astype(BF16)
    for c in range(V_WIDTH // cw):
        p = jnp.dot(h, w_ref[:, QK_WIDTH + c * cw:QK_WIDTH + (c + 1) * cw], preferred_element_type=F32)
        for t in range(PROJ_ROWS // ATT_TILE):
            vt_ref[t, c * cw:(c + 1) * cw, :] = p[t * ATT_TILE:(t + 1) * ATT_TILE, :].T.astype(BF16)


def _in_proj(x2, g1, w_perm, post_gain, grp, batch, seq):
    n = x2.shape[0]
    steps_per_seq = seq // PROJ_ROWS
    tiles_per_step = PROJ_ROWS // ATT_TILE
    return pl.pallas_call(
        _in_proj_kernel,
        grid=(n // PROJ_ROWS,),
        in_specs=[
            pl.BlockSpec((PROJ_ROWS, D_MODEL), lambda i: (i, 0)),
            pl.BlockSpec((1, D_MODEL), lambda i: (0, 0)),
            pl.BlockSpec((D_MODEL, QK_WIDTH + V_WIDTH), lambda i: (0, 0)),
            pl.BlockSpec((1, QK_WIDTH), lambda i: (0, 0)),
            pl.BlockSpec((2 * LANES, 2 * LANES), lambda i: (0, 0)),
        ],
        out_specs=[
            pl.BlockSpec((PROJ_ROWS, QK_WIDTH), lambda i: (i, 0)),
            pl.BlockSpec((None, tiles_per_step, V_WIDTH, ATT_TILE),
                         lambda i: (i // steps_per_seq, i % steps_per_seq, 0, 0)),
        ],
        out_shape=[
            jax.ShapeDtypeStruct((n, QK_WIDTH), BF16),
            jax.ShapeDtypeStruct((batch, seq // ATT_TILE, V_WIDTH, ATT_TILE), BF16),
        ],
        compiler_params=_params("arbitrary"),
        name="in_proj",
    )(x2, g1, w_perm, post_gain, grp)


def _flash_tiles(qi, n_chains, logits_fn, values_fn, bias_fn, mask_fn, s_ref, p_ref, acc_ref):
    for c in range(n_chains):
        s_ref[0, c] = logits_fn(c, qi)
    p_ref[...] = jnp.zeros_like(p_ref)
    acc_ref[...] = jnp.zeros_like(acc_ref)

    def add_values(c, kv, alpha):
        acc_ref[c] = acc_ref[c] * alpha + jnp.dot(values_fn(c, kv), p_ref[c], preferred_element_type=F32)

    def step(j, carry):
        cur = lax.rem(j, 2)
        kv = qi - j
        out = []
        for c in range(n_chains):
            add_values(c, jnp.minimum(kv + 1, qi), carry[2 * c + 1])
        for c in range(n_chains):
            s = s_ref[cur, c] + bias_fn(c, j)
            tile_max = jnp.max(s, axis=0, keepdims=True)
            shift = mask_fn(c, kv)
            if shift is not None:
                tile_max = tile_max + shift
            m_new = jnp.maximum(carry[2 * c], tile_max)
            alpha = jnp.exp2(carry[2 * c] - m_new)
            p_ref[c] = jnp.exp2(s - (m_new if shift is None else m_new - shift)).astype(BF16)
            out += [m_new, alpha]
        for c in range(n_chains):
            s_ref[1 - cur, c] = logits_fn(c, jnp.maximum(kv - 1, 0))
        return tuple(out)

    m0 = jnp.full((1, ATT_TILE), NEG_INF, F32)
    one = jnp.ones((1, ATT_TILE), F32)
    carry = lax.fori_loop(0, qi + 1, step, (m0, one) * n_chains)
    for c in range(n_chains):
        add_values(c, 0, carry[2 * c + 1])


def _with_ones_rows(vt):
    return jnp.concatenate([vt, jnp.ones((ONES_ROWS, vt.shape[1]), vt.dtype)], axis=0)


def _flash_scratch(n_chains, value_rows):
    return [pltpu.VMEM((2, n_chains, ATT_TILE, ATT_TILE), F32),
            pltpu.VMEM((n_chains, ATT_TILE, ATT_TILE), BF16),
            pltpu.VMEM((n_chains, value_rows + ONES_ROWS, ATT_TILE), F32)]


def _batch_group(batch):
    return ATT_BATCH_GROUP if batch % ATT_BATCH_GROUP == 0 else 1


def _half_lane_split(q):
    lane = lax.broadcasted_iota(jnp.int32, q.shape, 1)
    zero = jnp.zeros_like(q)
    return jnp.where(lane < HEAD_DIM, q, zero), jnp.where(lane >= HEAD_DIM, q, zero)


def _diff_kernel(lq1_ref, lk1_ref, lq2_ref, lk2_ref, subg_ref, q_ref, k_ref, vt_ref, bias_ref, o_ref,
                 s_ref, p_ref, acc_ref):
    qi = pl.program_id(2)
    group = q_ref.shape[0]
    qs = [half for g in range(group) for half in _half_lane_split(q_ref[g])]
    _flash_tiles(
        qi, 2 * group,
        logits_fn=lambda c, kv: lax.dot_general(k_ref[c // 2, kv], qs[c], _NT, preferred_element_type=F32),
        values_fn=lambda c, kv: _with_ones_rows(vt_ref[c // 2, kv]),
        bias_fn=lambda c, j: bias_ref[j],
        mask_fn=lambda c, kv: None,
        s_ref=s_ref, p_ref=p_ref, acc_ref=acc_ref)

    lam = (jnp.exp(jnp.sum(lq1_ref[...] * lk1_ref[...], keepdims=True))
           - jnp.exp(jnp.sum(lq2_ref[...] * lk2_ref[...], keepdims=True)) + LAMBDA_INIT)
    width = 2 * HEAD_DIM
    for g in range(group):
        a1, a2 = acc_ref[2 * g], acc_ref[2 * g + 1]
        o = a1[:width] / a1[width:width + 1] - lam * (a2[:width] / a2[width:width + 1])
        ms = jnp.mean(o * o, axis=0, keepdims=True)
        o = o * lax.rsqrt(ms + RMS_EPS) * subg_ref[...] * (1.0 - LAMBDA_INIT)
        o_ref[g] = o.T.astype(BF16)


def _diff_attention(lam_vecs, sub_g_col, qk4, vt4, bias):
    batch, n_tiles = qk4.shape[0], qk4.shape[1]
    group = _batch_group(batch)
    k_col0 = DIFF_WIDTH // LANES
    vec = pl.BlockSpec((1, HEAD_DIM), lambda h, b, i: (0, 0))
    return pl.pallas_call(
        _diff_kernel,
        grid=(HEADS_DIFF, batch // group, n_tiles),
        in_specs=[
            vec, vec, vec, vec,
            pl.BlockSpec((2 * HEAD_DIM, 1), lambda h, b, i: (0, 0)),
            pl.BlockSpec((group, None, ATT_TILE, LANES), lambda h, b, i: (b, i, 0, h)),
            pl.BlockSpec((group, n_tiles, ATT_TILE, LANES), lambda h, b, i: (b, 0, 0, k_col0 + h)),
            pl.BlockSpec((group, n_tiles, LANES, ATT_TILE), lambda h, b, i: (b, 0, h, 0)),
            pl.BlockSpec((None, n_tiles, ATT_TILE, ATT_TILE), lambda h, b, i: (h, 0, 0, 0)),
        ],
        out_specs=pl.BlockSpec((group, None, ATT_TILE, LANES), lambda h, b, i: (b, i, 0, h)),
        out_shape=jax.ShapeDtypeStruct((batch, n_tiles, ATT_TILE, DIFF_WIDTH), BF16),
        scratch_shapes=_flash_scratch(2 * group, 2 * HEAD_DIM),
        compiler_params=_params("arbitrary", "arbitrary", "arbitrary"),
        name="diff_attention",
    )(*lam_vecs, sub_g_col, qk4, qk4, vt4, bias)


def _split3(v):
    hi = v.astype(BF16)
    r1 = v - hi.astype(F32)
    mid = r1.astype(BF16)
    lo = (r1 - mid.astype(F32)).astype(BF16)
    return hi, mid, lo


def _block_mask(gate, own):
    n_blocks = gate.shape[0]
    row = lax.broadcasted_iota(jnp.int32, gate.shape, 0)
    rank = jnp.zeros(gate.shape, jnp.int32)
    for m in range(n_blocks):
        gm = gate[m:m + 1, :]
        beats = (gm > gate) | ((gm == gate) & (row > m))
        rank = rank + jnp.where(beats & (own > m), 1, 0)
    keep = ((rank < MOBA_TOPK) & (row < own)) | (row == own)
    return jnp.where(keep, 0.0, NEG_INF).astype(F32)


def _moba_kernel(q_ref, k_ref, vt_ref, bias_ref, o_ref, kmean_ref, mask_ref, s_ref, p_ref, acc_ref):
    qi = pl.program_id(2)
    group, n_blocks = k_ref.shape[0], k_ref.shape[1]

    @pl.when(qi == 0)
    def _():
        for g in range(group):
            for n in range(n_blocks):
                kmean_ref[g, n:n + 1, :] = jnp.mean(k_ref[g, n].astype(F32), axis=0, keepdims=True)

    qs = [_half_lane_split(q_ref[g]) for g in range(group)]
    for g in range(group):
        parts = _split3(kmean_ref[g])
        for half in range(2):
            gate = jnp.zeros((n_blocks, ATT_TILE), F32)
            for part in parts:
                gate = gate + lax.dot_general(part, qs[g][half], _NT, preferred_element_type=F32)
            mask_ref[2 * g + half] = _block_mask(gate, qi)

    def values(c, kv):
        half = c % 2
        return _with_ones_rows(vt_ref[c // 2, kv, half * HEAD_DIM:(half + 1) * HEAD_DIM, :])

    _flash_tiles(
        qi, 2 * group,
        logits_fn=lambda c, kv: lax.dot_general(k_ref[c // 2, kv], qs[c // 2][c % 2], _NT,
                                                preferred_element_type=F32),
        values_fn=values,
        bias_fn=lambda c, j: bias_ref[c % 2, j],
        mask_fn=lambda c, kv: mask_ref[c, pl.ds(kv, 1), :],
        s_ref=s_ref, p_ref=p_ref, acc_ref=acc_ref)

    for g in range(group):
        halves = [acc_ref[2 * g + half] for half in range(2)]
        o = jnp.concatenate([a[:HEAD_DIM] / a[HEAD_DIM:HEAD_DIM + 1] for a in halves], axis=0)
        o_ref[g] = o.T.astype(BF16)


def _moba_attention(qk4, vt4, bias_pairs):
    batch, n_tiles = qk4.shape[0], qk4.shape[1]
    q_col0 = 2 * DIFF_WIDTH // LANES
    k_col0 = q_col0 + MOBA_WIDTH // LANES
    v_row0 = DIFF_WIDTH // LANES
    pair0 = HEADS_DIFF // 2
    group = _batch_group(batch)
    return pl.pallas_call(
        _moba_kernel,
        grid=(HEADS_MOBA // 2, batch // group, n_tiles),
        in_specs=[
            pl.BlockSpec((group, None, ATT_TILE, LANES), lambda h, b, i: (b, i, 0, q_col0 + h)),
            pl.BlockSpec((group, n_tiles, ATT_TILE, LANES), lambda h, b, i: (b, 0, 0, k_col0 + h)),
            pl.BlockSpec((group, n_tiles, LANES, ATT_TILE), lambda h, b, i: (b, 0, v_row0 + h, 0)),
            pl.BlockSpec((None, 2, n_tiles, ATT_TILE, ATT_TILE), lambda h, b, i: (pair0 + h, 0, 0, 0, 0)),
        ],
        out_specs=pl.BlockSpec((group, None, ATT_TILE, LANES), lambda h, b, i: (b, i, 0, h)),
        out_shape=jax.ShapeDtypeStruct((batch, n_tiles, ATT_TILE, MOBA_WIDTH), BF16),
        scratch_shapes=[
            pltpu.VMEM((group, n_tiles, LANES), F32),
            pltpu.VMEM((2 * group, n_tiles, ATT_TILE), F32),
        ] + _flash_scratch(2 * group, HEAD_DIM),
        compiler_params=_params("arbitrary", "arbitrary", "arbitrary"),
        name="moba_attention",
    )(qk4, qk4, vt4, bias_pairs)


def _load_token_rows(ref, n_rows, lead=()):
    chunks = [ref[lead + (pl.ds(c, n_rows, stride=ROW_TILE), slice(None))] for c in range(ROW_TILE)]
    return jnp.concatenate(chunks, axis=1)


def _store_token_rows(ref, value):
    n_rows = value.shape[0]
    for c in range(ROW_TILE):
        ref[pl.ds(c, n_rows, stride=ROW_TILE), :] = value[:, c * LANES:(c + 1) * LANES]


def _token_tile(ref, t):
    return ref.at[pl.ds(pl.multiple_of(t * ROW_TILE, ROW_TILE), ROW_TILE)]


def _first_argmax(v):
    top = jnp.max(v, axis=0, keepdims=True)
    row = lax.broadcasted_iota(jnp.int32, v.shape, 0)
    idx = jnp.min(jnp.where(v == top, row, v.shape[0]), axis=0, keepdims=True)
    return top, idx


def _out_proj_kernel(yd_ref, ym_ref, x_ref, wd_ref, wm_ref, g2_ref, wr_ref, tri_ref,
                     x1_ref, ri_ref, rf_ref, cnt_ref, run_ref):
    @pl.when(pl.program_id(0) == 0)
    def _():
        run_ref[...] = jnp.zeros_like(run_ref)

    x1 = (x_ref[...] + jnp.dot(yd_ref[...], wd_ref[...], preferred_element_type=F32)
          + jnp.dot(ym_ref[...], wm_ref[...], preferred_element_type=F32))
    _store_token_rows(x1_ref, x1)
    ms = jnp.mean(x1 * x1, axis=-1, keepdims=True)
    h2 = (x1 * lax.rsqrt(ms + RMS_EPS) * g2_ref[...]).astype(BF16)
    logits = lax.dot_general(wr_ref[...], h2, _NT, preferred_element_type=F32)

    g_logits = logits[N_EXPERTS:N_EXPERTS + N_GROUPS, :]
    g_top, g_idx = _first_argmax(g_logits)
    p_group = 1.0 / jnp.sum(jnp.exp(g_logits - g_top), axis=0, keepdims=True)
    e_logits = logits[0:EXPERTS_PER_GROUP, :]
    for g in range(1, N_GROUPS):
        e_logits = jnp.where(g_idx == g, logits[g * EXPERTS_PER_GROUP:(g + 1) * EXPERTS_PER_GROUP, :], e_logits)
    v1, i1 = _first_argmax(e_logits)
    row = lax.broadcasted_iota(jnp.int32, e_logits.shape, 0)
    v2, i2 = _first_argmax(jnp.where(row == i1, -jnp.inf, e_logits))
    t = jnp.exp(v2 - v1)
    w1 = p_group / (1.0 + t)
    w2 = p_group * t / (1.0 + t)
    e1 = g_idx * EXPERTS_PER_GROUP + i1
    e2 = g_idx * EXPERTS_PER_GROUP + i2

    erow = lax.broadcasted_iota(jnp.int32, (N_EXPERTS, OUT_ROWS), 0)
    hit1 = erow == e1
    hit2 = erow == e2
    onehot = jnp.where(hit1 | hit2, 1.0, 0.0).astype(F32)
    before = jnp.dot(onehot.astype(BF16), tri_ref[...], preferred_element_type=F32) + run_ref[...]
    r1 = jnp.sum(jnp.where(hit1, before, 0.0), axis=0, keepdims=True).astype(jnp.int32)
    r2 = jnp.sum(jnp.where(hit2, before, 0.0), axis=0, keepdims=True).astype(jnp.int32)
    run_ref[...] = run_ref[...] + jnp.sum(onehot, axis=1, keepdims=True)

    out_row = lax.broadcasted_iota(jnp.int32, ri_ref.shape, 0)
    ri_ref[...] = jnp.where(out_row == 0, e1, jnp.where(out_row == 1, e2,
                            jnp.where(out_row == 2, r1, jnp.where(out_row == 3, r2, 0))))
    rf_ref[...] = jnp.where(out_row == 0, w1, jnp.where(out_row == 1, w2, 0.0))
    cnt_ref[...] = jnp.broadcast_to(run_ref[...], cnt_ref.shape).astype(jnp.int32)


def _out_proj(y_d, y_m, x2, wo_d, wo_m, g2, w_router, tri):
    n = x2.shape[0]
    const = lambda i: (0, 0)
    return pl.pallas_call(
        _out_proj_kernel,
        grid=(n // OUT_ROWS,),
        in_specs=[
            pl.BlockSpec((OUT_ROWS, DIFF_WIDTH), lambda i: (i, 0)),
            pl.BlockSpec((OUT_ROWS, MOBA_WIDTH), lambda i: (i, 0)),
            pl.BlockSpec((OUT_ROWS, D_MODEL), lambda i: (i, 0)),
            pl.BlockSpec((DIFF_WIDTH, D_MODEL), const),
            pl.BlockSpec((MOBA_WIDTH, D_MODEL), const),
            pl.BlockSpec((1, D_MODEL), const),
            pl.BlockSpec((ROUTER_ROWS, D_MODEL), const),
            pl.BlockSpec((OUT_ROWS, OUT_ROWS), const),
        ],
        out_specs=[
            pl.BlockSpec((OUT_ROWS * ROW_TILE, LANES), lambda i: (i, 0)),
            pl.BlockSpec((8, OUT_ROWS), lambda i: (0, i)),
            pl.BlockSpec((8, OUT_ROWS), lambda i: (0, i)),
            pl.BlockSpec((N_EXPERTS, LANES), const),
        ],
        out_shape=[
            jax.ShapeDtypeStruct((n * ROW_TILE, LANES), F32),
            jax.ShapeDtypeStruct((8, n), jnp.int32),
            jax.ShapeDtypeStruct((8, n), F32),
            jax.ShapeDtypeStruct((N_EXPERTS, LANES), jnp.int32),
        ],
        scratch_shapes=[pltpu.VMEM((N_EXPERTS, 1), F32)],
        compiler_params=_params("arbitrary"),
        name="out_proj_route",
    )(y_d, y_m, x2, wo_d, wo_m, g2, w_router, tri)


def _dispatch_kernel(zchunk_ref, dest_ref, x1_hbm, xs_hbm, zero_ref, stage_ref, zero_sem, load_sem, row_sem):
    i = pl.program_id(0)
    last = pl.num_programs(0) - 1
    slot = lax.rem(i, 2)
    chunk_rows = MOE_CHUNK * ROW_TILE
    tile_rows = MOVE_ROWS * ROW_TILE

    def load(t, s):
        start = pl.multiple_of(t * tile_rows, tile_rows)
        return pltpu.make_async_copy(x1_hbm.at[pl.ds(start, tile_rows)], stage_ref.at[s], load_sem.at[s])

    def wait_rows(s):
        for _ in range(EXPERTS_PER_TOKEN):
            pltpu.make_async_copy(stage_ref.at[s], xs_hbm.at[pl.ds(0, tile_rows)], row_sem.at[s]).wait()

    def zero_copy(c):
        start = pl.multiple_of(c * chunk_rows, chunk_rows)
        return pltpu.make_async_copy(zero_ref, xs_hbm.at[pl.ds(start, chunk_rows)], zero_sem)

    @pl.when(i == 0)
    def _():
        zero_ref[...] = jnp.zeros_like(zero_ref)

        def start_one(c, carry):
            @pl.when(zchunk_ref[c] > 0)
            def _():
                zero_copy(c).start()
            return carry

        def wait_one(c, carry):
            @pl.when(zchunk_ref[c] > 0)
            def _():
                zero_copy(c).wait()
            return carry

        lax.fori_loop(0, zchunk_ref.shape[0], start_one, 0)
        lax.fori_loop(0, zchunk_ref.shape[0], wait_one, 0)
        load(0, 0).start()

    @pl.when(i > 0)
    def _():
        wait_rows(1 - slot)

    @pl.when(i < last)
    def _():
        load(i + 1, 1 - slot).start()

    load(i, slot).wait()

    def send_row(r, carry):
        src = _token_tile(stage_ref.at[slot], r)
        for k in range(EXPERTS_PER_TOKEN):
            pltpu.make_async_copy(src, _token_tile(xs_hbm, dest_ref[0, k, r]), row_sem.at[slot]).start()
        return carry

    lax.fori_loop(0, MOVE_ROWS, send_row, 0, unroll=8)

    @pl.when(i == last)
    def _():
        wait_rows(slot)


def _dispatch(zchunk, dest3, x1, n_slots):
    n = x1.shape[0] // ROW_TILE
    grid_spec = pltpu.PrefetchScalarGridSpec(
        num_scalar_prefetch=1,
        grid=(n // MOVE_ROWS,),
        in_specs=[
            pl.BlockSpec((1, EXPERTS_PER_TOKEN, MOVE_ROWS), lambda i, zc: (i, 0, 0), memory_space=pltpu.SMEM),
            pl.BlockSpec(memory_space=pl.ANY),
        ],
        out_specs=pl.BlockSpec(memory_space=pl.ANY),
        scratch_shapes=[
            pltpu.VMEM((MOE_CHUNK * ROW_TILE, LANES), F32),
            pltpu.VMEM((2, MOVE_ROWS * ROW_TILE, LANES), F32),
            pltpu.SemaphoreType.DMA(()),
            pltpu.SemaphoreType.DMA((2,)),
            pltpu.SemaphoreType.DMA((2,)),
        ],
    )
    return pl.pallas_call(
        _dispatch_kernel,
        grid_spec=grid_spec,
        out_shape=jax.ShapeDtypeStruct((n_slots * ROW_TILE, LANES), F32),
        compiler_params=_params("arbitrary"),
        name="moe_dispatch",
    )(zchunk, dest3, x1)


def _expert_kernel(ce_ref, na_ref, xs_ref, g2_ref, wg_ref, wu_ref, wd_ref, o_ref):
    active = pl.program_id(0) < na_ref[0]

    @pl.when(jnp.logical_not(active))
    def _():
        o_ref[...] = jnp.zeros_like(o_ref)

    @pl.when(active)
    def _():
        x = _load_token_rows(xs_ref, MOE_CHUNK)
        ms = jnp.mean(x * x, axis=-1, keepdims=True)
        h = (x * lax.rsqrt(ms + RMS_EPS) * g2_ref[...]).astype(BF16)
        gate = jnp.dot(h, wg_ref[...], preferred_element_type=F32)
        up = jnp.dot(h, wu_ref[...], preferred_element_type=F32)
        hid = (gate * jax.nn.sigmoid(gate) * up).astype(BF16)
        _store_token_rows(o_ref, jnp.dot(hid, wd_ref[...], preferred_element_type=F32))


def _experts(chunk_e, n_active, xs, g2, wg, wu, wd):
    n_slots = xs.shape[0] // ROW_TILE
    rows = lambda c, ce, na: (jnp.minimum(c, na[0] - 1), 0)
    grid_spec = pltpu.PrefetchScalarGridSpec(
        num_scalar_prefetch=2,
        grid=(n_slots // MOE_CHUNK,),
        in_specs=[
            pl.BlockSpec((MOE_CHUNK * ROW_TILE, LANES), rows),
            pl.BlockSpec((1, D_MODEL), lambda c, ce, na: (0, 0)),
            pl.BlockSpec((None, D_MODEL, EXPERT_HIDDEN), lambda c, ce, na: (ce[c], 0, 0)),
            pl.BlockSpec((None, D_MODEL, EXPERT_HIDDEN), lambda c, ce, na: (ce[c], 0, 0)),
            pl.BlockSpec((None, EXPERT_HIDDEN, D_MODEL), lambda c, ce, na: (ce[c], 0, 0)),
        ],
        out_specs=pl.BlockSpec((MOE_CHUNK * ROW_TILE, LANES), lambda c, ce, na: (c, 0)),
    )
    return pl.pallas_call(
        _expert_kernel,
        grid_spec=grid_spec,
        out_shape=jax.ShapeDtypeStruct((n_slots * ROW_TILE, LANES), F32),
        compiler_params=_params("arbitrary"),
        name="moe_experts",
    )(chunk_e, n_active, xs, g2, wg, wu, wd)


def _combine_kernel(dest_ref, dest_next_ref, w_ref, x1_ref, outs_hbm, y_ref, buf_ref, sem_ref):
    i = pl.program_id(0)
    slot = lax.rem(i, 2)

    def fetch(dref, s):
        def fetch_row(r, carry):
            for k in range(EXPERTS_PER_TOKEN):
                pltpu.make_async_copy(_token_tile(outs_hbm, dref[0, k, r]),
                                      _token_tile(buf_ref.at[s, k], r), sem_ref.at[s]).start()
            return carry
        lax.fori_loop(0, MOVE_ROWS, fetch_row, 0, unroll=8)

    @pl.when(i == 0)
    def _():
        fetch(dest_ref, 0)

    @pl.when(i + 1 < pl.num_programs(0))
    def _():
        fetch(dest_next_ref, 1 - slot)

    for k in range(EXPERTS_PER_TOKEN):
        pltpu.make_async_copy(outs_hbm.at[pl.ds(0, MOVE_ROWS * ROW_TILE)], buf_ref.at[slot, k],
                              sem_ref.at[slot]).wait()
    w = w_ref[...]
    y_ref[...] = (_load_token_rows(x1_ref, MOVE_ROWS)
                  + w[:, 0:1] * _load_token_rows(buf_ref, MOVE_ROWS, (slot, 0))
                  + w[:, 1:2] * _load_token_rows(buf_ref, MOVE_ROWS, (slot, 1)))


def _combine(dest3, wts, x1, outs):
    n = x1.shape[0] // ROW_TILE
    steps = n // MOVE_ROWS
    dest_block = (1, EXPERTS_PER_TOKEN, MOVE_ROWS)
    return pl.pallas_call(
        _combine_kernel,
        grid=(steps,),
        in_specs=[
            pl.BlockSpec(dest_block, lambda i: (i, 0, 0), memory_space=pltpu.SMEM),
            pl.BlockSpec(dest_block, lambda i: (jnp.minimum(i + 1, steps - 1), 0, 0), memory_space=pltpu.SMEM),
            pl.BlockSpec((MOVE_ROWS, EXPERTS_PER_TOKEN), lambda i: (i, 0)),
            pl.BlockSpec((MOVE_ROWS * ROW_TILE, LANES), lambda i: (i, 0)),
            pl.BlockSpec(memory_space=pl.ANY),
        ],
        out_specs=pl.BlockSpec((MOVE_ROWS, D_MODEL), lambda i: (i, 0)),
        out_shape=jax.ShapeDtypeStruct((n, D_MODEL), F32),
        scratch_shapes=[
            pltpu.VMEM((2, EXPERTS_PER_TOKEN, MOVE_ROWS * ROW_TILE, LANES), F32),
            pltpu.SemaphoreType.DMA((2,)),
        ],
        compiler_params=_params("arbitrary"),
        name="moe_combine",
    )(dest3, dest3, wts, x1, outs)


def _layer(x, norm1_g, w_in, diff_q_g, diff_k_g, lambda_q1, lambda_k1, lambda_q2, lambda_k2, diff_sub_g,
           moba_q_g, moba_k_g, rel_bias, w_out, norm2_g, router_group, router_expert, w_gate, w_up, w_down):
    batch, seq, _ = x.shape
    n = batch * seq
    n_tiles = seq // ATT_TILE
    scale = HEAD_DIM ** -0.5 * LOG2E

    d3 = 3 * DIFF_WIDTH
    w_perm = jnp.concatenate([w_in[:, :2 * DIFF_WIDTH], w_in[:, d3:d3 + 2 * MOBA_WIDTH],
                              w_in[:, 2 * DIFF_WIDTH:d3], w_in[:, d3 + 2 * MOBA_WIDTH:]], axis=1).astype(BF16)
    reps_d, reps_m = DIFF_WIDTH // HEAD_DIM, MOBA_WIDTH // HEAD_DIM
    post_gain = jnp.concatenate([jnp.tile(diff_q_g * scale, reps_d), jnp.tile(diff_k_g, reps_d),
                                 jnp.tile(moba_q_g * scale, reps_m), jnp.tile(moba_k_g, reps_m)])[None, :]
    head_of = np.arange(2 * LANES) // HEAD_DIM
    grp = jnp.asarray((head_of[:, None] == head_of[None, :]) / HEAD_DIM, BF16)

    bias = _bias_tiles(rel_bias, n_tiles)
    qk, vt4 = _in_proj(x.reshape(n, D_MODEL), norm1_g[None, :], w_perm, post_gain, grp, batch, seq)
    qk4 = qk.reshape(batch, n_tiles, ATT_TILE, QK_WIDTH)
    lam_vecs = [v[None, :] for v in (lambda_q1, lambda_k1, lambda_q2, lambda_k2)]
    y_d = _diff_attention(lam_vecs, diff_sub_g[:, None], qk4, vt4, bias)
    y_m = _moba_attention(qk4, vt4, bias.reshape(N_HEADS_TOTAL // 2, 2, n_tiles, ATT_TILE, ATT_TILE))

    w_router = jnp.concatenate([
        router_expert.transpose(0, 2, 1).reshape(N_EXPERTS, D_MODEL), router_group.T,
        jnp.zeros((ROUTER_ROWS - N_EXPERTS - N_GROUPS, D_MODEL), F32)], axis=0).astype(BF16)
    tri = jnp.asarray(np.triu(np.ones((OUT_ROWS, OUT_ROWS)), k=1), BF16)
    wo = w_out.astype(BF16)
    x1, route_i, route_f, cnt = _out_proj(y_d.reshape(n, DIFF_WIDTH), y_m.reshape(n, MOBA_WIDTH),
                                          x.reshape(n, D_MODEL), wo[:DIFF_WIDTH], wo[DIFF_WIDTH:],
                                          norm2_g[None, :], w_router, tri)

    counts = cnt[:, 0]
    pad_counts = ((counts + MOE_CHUNK - 1) // MOE_CHUNK) * MOE_CHUNK
    pad_end = jnp.cumsum(pad_counts)
    pad_start = pad_end - pad_counts
    experts = jnp.arange(N_EXPERTS, dtype=jnp.int32)
    start_of = jnp.sum(jnp.where(route_i[0:2, :, None] == experts, pad_start, 0), axis=-1)
    dest = start_of + route_i[2:4]
    dest3 = dest.reshape(EXPERTS_PER_TOKEN, n // MOVE_ROWS, MOVE_ROWS).transpose(1, 0, 2)
    n_slots = n * EXPERTS_PER_TOKEN + N_EXPERTS * MOE_CHUNK
    n_chunks = n_slots // MOE_CHUNK
    chunk_id = jnp.arange(n_chunks, dtype=jnp.int32)
    chunk_e = jnp.minimum(jnp.sum((pad_end[None, :] <= chunk_id[:, None] * MOE_CHUNK).astype(jnp.int32), axis=1),
                          N_EXPERTS - 1)
    n_active = (pad_end[-1:] // MOE_CHUNK).astype(jnp.int32)
    next_e = jnp.concatenate([chunk_e[1:], jnp.full((1,), N_EXPERTS, jnp.int32)])
    zchunk = ((chunk_id >= n_active - 1) | (chunk_e != next_e)).astype(jnp.int32)

    xs = _dispatch(zchunk, dest3, x1, n_slots)
    outs = _experts(chunk_e, n_active, xs, norm2_g[None, :], w_gate.astype(BF16), w_up.astype(BF16),
                    w_down.astype(BF16))
    y = _combine(dest3, route_f[0:2].T, x1, outs)
    return y.reshape(batch, seq, D_MODEL)


def kernel(x, norm1_g, w_in, diff_q_g, diff_k_g, lambda_q1, lambda_k1, lambda_q2, lambda_k2, diff_sub_g,
           moba_q_g, moba_k_g, rel_bias, w_out, norm2_g, router_group, router_expert, w_gate, w_up, w_down):
    assert x.shape[1] % PROJ_ROWS == 0 and x.shape[2] == D_MODEL and norm1_g.shape[0] == 1
    return _layer(x, norm1_g[0], w_in[0], diff_q_g[0], diff_k_g[0], lambda_q1[0], lambda_k1[0], lambda_q2[0],
                  lambda_k2[0], diff_sub_g[0], moba_q_g[0], moba_k_g[0], rel_bias, w_out[0], norm2_g[0],
                  router_group[0], router_expert[0], w_gate[0], w_up[0], w_down[0])
```

```python
import functools
import math

import numpy as np
import jax
import jax.numpy as jnp
from jax import lax
from jax.experimental import pallas as pl
from jax.experimental.pallas import tpu as pltpu

D_MODEL = 1024
HEAD_DIM = 64
HEADS_DIFF = 4
HEADS_MOBA = 8
N_HEADS_TOTAL = HEADS_DIFF + HEADS_MOBA
DIFF_WIDTH = HEADS_DIFF * 2 * HEAD_DIM
MOBA_WIDTH = HEADS_MOBA * HEAD_DIM
QK_WIDTH = 2 * DIFF_WIDTH + 2 * MOBA_WIDTH
V_WIDTH = DIFF_WIDTH + MOBA_WIDTH
MOBA_TOPK = 3
NUM_BUCKETS = 32
MAX_DISTANCE = 2048
N_GROUPS = 4
EXPERTS_PER_GROUP = 8
N_EXPERTS = N_GROUPS * EXPERTS_PER_GROUP
EXPERTS_PER_TOKEN = 2
EXPERT_HIDDEN = 256
RMS_EPS = 1e-6
NEG_INF = -1e30
LAMBDA_INIT = 0.8 - 0.6 * math.exp(-0.3 * 0)

LANES = 128
ROW_TILE = D_MODEL // LANES
ATT_TILE = 256
ATT_BATCH_GROUP = 2
ONES_ROWS = 16
LOGIT_LIMIT = 80.0
ROUNDING_MARGIN = 1.05
LOG2E = math.log2(math.e)
PROJ_ROWS = 512
OUT_ROWS = 256
MOE_CHUNK = 256
MOVE_ROWS = 256
ROUTER_ROWS = 40
VMEM_LIMIT = 48 * 1024 * 1024

F32 = jnp.float32
BF16 = jnp.bfloat16
_NT = (((1,), (1,)), ((), ()))


def _t5_thresholds():
    n = np.arange(0, 1 << 16)
    max_exact = NUM_BUCKETS // 2
    nf = np.maximum(n, 1).astype(np.float32)
    large = max_exact + (np.log(nf / np.float32(max_exact)) / np.float32(math.log(MAX_DISTANCE / max_exact))
                         * np.float32(NUM_BUCKETS - max_exact)).astype(np.int32)
    bucket = np.where(n < max_exact, n, np.minimum(large, NUM_BUCKETS - 1))
    return [int(np.searchsorted(bucket, b, side="left")) for b in range(1, NUM_BUCKETS)]


_T5_THRESHOLDS = _t5_thresholds()


def _params(*sem):
    return pltpu.CompilerParams(dimension_semantics=sem, vmem_limit_bytes=VMEM_LIMIT)


def _bias_kernel(tab_ref, out_ref):
    h = pl.program_id(0)
    d = pl.program_id(1)
    kj = lax.broadcasted_iota(jnp.int32, (ATT_TILE, ATT_TILE), 0)
    qi = lax.broadcasted_iota(jnp.int32, (ATT_TILE, ATT_TILE), 1)
    dist = d * ATT_TILE + qi - kj
    val = jnp.full((ATT_TILE, ATT_TILE), tab_ref[0, h], F32)
    for b, thr in enumerate(_T5_THRESHOLDS, start=1):
        val = jnp.where(dist >= thr, tab_ref[b, h], val)
    out_ref[...] = jnp.where(dist < 0, NEG_INF, val * LOG2E)


def _bias_tiles(rel_bias, n_diag):
    return pl.pallas_call(
        _bias_kernel,
        grid=(N_HEADS_TOTAL, n_diag),
        in_specs=[pl.BlockSpec(memory_space=pltpu.SMEM)],
        out_specs=pl.BlockSpec((None, None, ATT_TILE, ATT_TILE), lambda h, d: (h, d, 0, 0)),
        out_shape=jax.ShapeDtypeStruct((N_HEADS_TOTAL, n_diag, ATT_TILE, ATT_TILE), F32),
        compiler_params=_params("arbitrary", "arbitrary"),
        name="bias_tiles",
    )(rel_bias)


def _in_proj_kernel(x_ref, g1_ref, w_ref, pg_ref, grp_ref, qk_ref, vt_ref):
    x = x_ref[...]
    ms = jnp.mean(x * x, axis=-1, keepdims=True)
    h = (x * lax.rsqrt(ms + RMS_EPS) * g1_ref[...]).astype(BF16)
    cw = 2 * LANES
    for c in range(QK_WIDTH // cw):
        cols = slice(c * cw, (c + 1) * cw)
        p = jnp.dot(h, w_ref[:, cols], preferred_element_type=F32)
        msq = jnp.dot((p * p).astype(BF16), grp_ref[...], preferred_element_type=F32)
        qk_ref[:, cols] = (p * lax.rsqrt(msq + RMS_EPS) * pg_ref[:, cols]).astype(BF16)
    for c in range(V_WIDTH // cw):
        p = jnp.dot(h, w_ref[:, QK_WIDTH + c * cw:QK_WIDTH + (c + 1) * cw], preferred_element_type=F32)
        for t in range(PROJ_ROWS // ATT_TILE):
            vt_ref[t, c * cw:(c + 1) * cw, :] = p[t * ATT_TILE:(t + 1) * ATT_TILE, :].T.astype(BF16)


def _in_proj(x2, g1, w_perm, post_gain, grp, batch, seq):
    n = x2.shape[0]
    steps_per_seq = seq // PROJ_ROWS
    tiles_per_step = PROJ_ROWS // ATT_TILE
    return pl.pallas_call(
        _in_proj_kernel,
        grid=(n // PROJ_ROWS,),
        in_specs=[
            pl.BlockSpec((PROJ_ROWS, D_MODEL), lambda i: (i, 0)),
            pl.BlockSpec((1, D_MODEL), lambda i: (0, 0)),
            pl.BlockSpec((D_MODEL, QK_WIDTH + V_WIDTH), lambda i: (0, 0)),
            pl.BlockSpec((1, QK_WIDTH), lambda i: (0, 0)),
            pl.BlockSpec((2 * LANES, 2 * LANES), lambda i: (0, 0)),
        ],
        out_specs=[
            pl.BlockSpec((PROJ_ROWS, QK_WIDTH), lambda i: (i, 0)),
            pl.BlockSpec((None, tiles_per_step, V_WIDTH, ATT_TILE),
                         lambda i: (i // steps_per_seq, i % steps_per_seq, 0, 0)),
        ],
        out_shape=[
            jax.ShapeDtypeStruct((n, QK_WIDTH), BF16),
            jax.ShapeDtypeStruct((batch, seq // ATT_TILE, V_WIDTH, ATT_TILE), BF16),
        ],
        compiler_params=_params("arbitrary"),
        name="in_proj",
    )(x2, g1, w_perm, post_gain, grp)


def _flash_tiles(qi, n_chains, logits_fn, values_fn, bias_fn, mask_fn, s_ref, p_ref, acc_ref):
    for c in range(n_chains):
        s_ref[0, c] = logits_fn(c, qi)
    p_ref[...] = jnp.zeros_like(p_ref)
    acc_ref[...] = jnp.zeros_like(acc_ref)

    def add_values(c, kv, alpha):
        acc_ref[c] = acc_ref[c] * alpha + jnp.dot(values_fn(c, kv), p_ref[c], preferred_element_type=F32)

    def step(j, carry):
        cur = lax.rem(j, 2)
        kv = qi - j
        out = []
        for c in range(n_chains):
            add_values(c, jnp.minimum(kv + 1, qi), carry[2 * c + 1])
        for c in range(n_chains):
            s = s_ref[cur, c] + bias_fn(c, j)
            tile_max = jnp.max(s, axis=0, keepdims=True)
            shift = mask_fn(c, kv)
            if shift is not None:
                tile_max = tile_max + shift
            m_new = jnp.maximum(carry[2 * c], tile_max)
            alpha = jnp.exp2(carry[2 * c] - m_new)
            p_ref[c] = jnp.exp2(s - (m_new if shift is None else m_new - shift)).astype(BF16)
            out += [m_new, alpha]
        for c in range(n_chains):
            s_ref[1 - cur, c] = logits_fn(c, jnp.maximum(kv - 1, 0))
        return tuple(out)

    m0 = jnp.full((1, ATT_TILE), NEG_INF, F32)
    one = jnp.ones((1, ATT_TILE), F32)
    carry = lax.fori_loop(0, qi + 1, step, (m0, one) * n_chains)
    for c in range(n_chains):
        add_values(c, 0, carry[2 * c + 1])


def _bounded_tiles(qi, n_chains, logits_fn, values_fn, bias_fn, mask_fn, acc_ref):
    def probs(kv, j):
        return tuple(jnp.exp2(logits_fn(c, kv) + bias_fn(c, j)).astype(BF16) for c in range(n_chains))

    def add_values(ps, kv):
        for c in range(n_chains):
            pv = jnp.dot(values_fn(c, kv), ps[c], preferred_element_type=F32)
            shift = mask_fn(c, kv)
            if shift is not None:
                pv = pv * jnp.where(shift < -1.0, 0.0, 1.0)
            acc_ref[c] = acc_ref[c] + pv

    acc_ref[...] = jnp.zeros_like(acc_ref)

    def step(j, ps):
        kv = qi - j
        new = probs(kv, j)
        add_values(ps, kv + 1)
        return new

    add_values(lax.fori_loop(1, qi + 1, step, probs(qi, 0)), 0)


def _logits_bounded(q_gain, k_gain, bias_cols):
    bound = (HEAD_DIM * jnp.max(jnp.abs(q_gain)) * jnp.max(jnp.abs(k_gain)) * ROUNDING_MARGIN
             + jnp.max(jnp.abs(bias_cols)) * LOG2E)
    return (bound <= LOGIT_LIMIT).astype(jnp.int32).reshape(1)


def _with_ones_rows(vt):
    return jnp.concatenate([vt, jnp.ones((ONES_ROWS, vt.shape[1]), vt.dtype)], axis=0)


def _flash_scratch(n_chains, value_rows):
    return [pltpu.VMEM((2, n_chains, ATT_TILE, ATT_TILE), F32),
            pltpu.VMEM((n_chains, ATT_TILE, ATT_TILE), BF16),
            pltpu.VMEM((n_chains, value_rows + ONES_ROWS, ATT_TILE), F32)]


def _batch_group(batch):
    return ATT_BATCH_GROUP if batch % ATT_BATCH_GROUP == 0 else 1


def _half_lane_split(q):
    lane = lax.broadcasted_iota(jnp.int32, q.shape, 1)
    zero = jnp.zeros_like(q)
    return jnp.where(lane < HEAD_DIM, q, zero), jnp.where(lane >= HEAD_DIM, q, zero)


def _diff_kernel(bounded_ref, lq1_ref, lk1_ref, lq2_ref, lk2_ref, subg_ref, q_ref, k_ref, vt_ref, bias_ref, o_ref,
                 s_ref, p_ref, acc_ref):
    qi = pl.program_id(2)
    group = q_ref.shape[0]
    qs = [half for g in range(group) for half in _half_lane_split(q_ref[g])]
    chains = dict(
        n_chains=2 * group,
        logits_fn=lambda c, kv: lax.dot_general(k_ref[c // 2, kv], qs[c], _NT, preferred_element_type=F32),
        values_fn=lambda c, kv: _with_ones_rows(vt_ref[c // 2, kv]),
        bias_fn=lambda c, j: bias_ref[j],
        mask_fn=lambda c, kv: None,
        acc_ref=acc_ref)
    bounded = bounded_ref[0] > 0

    @pl.when(bounded)
    def _():
        _bounded_tiles(qi, **chains)

    @pl.when(jnp.logical_not(bounded))
    def _():
        _flash_tiles(qi, s_ref=s_ref, p_ref=p_ref, **chains)

    lam = (jnp.exp(jnp.sum(lq1_ref[...] * lk1_ref[...], keepdims=True))
           - jnp.exp(jnp.sum(lq2_ref[...] * lk2_ref[...], keepdims=True)) + LAMBDA_INIT)
    width = 2 * HEAD_DIM
    for g in range(group):
        a1, a2 = acc_ref[2 * g], acc_ref[2 * g + 1]
        o = a1[:width] / a1[width:width + 1] - lam * (a2[:width] / a2[width:width + 1])
        ms = jnp.mean(o * o, axis=0, keepdims=True)
        o = o * lax.rsqrt(ms + RMS_EPS) * subg_ref[...] * (1.0 - LAMBDA_INIT)
        o_ref[g] = o.T.astype(BF16)


def _diff_attention(bounded, lam_vecs, sub_g_col, qk4, vt4, bias):
    batch, n_tiles = qk4.shape[0], qk4.shape[1]
    group = _batch_group(batch)
    k_col0 = DIFF_WIDTH // LANES
    vec = pl.BlockSpec((1, HEAD_DIM), lambda h, b, i: (0, 0))
    return pl.pallas_call(
        _diff_kernel,
        grid=(HEADS_DIFF, batch // group, n_tiles),
        in_specs=[
            pl.BlockSpec(memory_space=pltpu.SMEM),
            vec, vec, vec, vec,
            pl.BlockSpec((2 * HEAD_DIM, 1), lambda h, b, i: (0, 0)),
            pl.BlockSpec((group, None, ATT_TILE, LANES), lambda h, b, i: (b, i, 0, h)),
            pl.BlockSpec((group, n_tiles, ATT_TILE, LANES), lambda h, b, i: (b, 0, 0, k_col0 + h)),
            pl.BlockSpec((group, n_tiles, LANES, ATT_TILE), lambda h, b, i: (b, 0, h, 0)),
            pl.BlockSpec((None, n_tiles, ATT_TILE, ATT_TILE), lambda h, b, i: (h, 0, 0, 0)),
        ],
        out_specs=pl.BlockSpec((group, None, ATT_TILE, LANES), lambda h, b, i: (b, i, 0, h)),
        out_shape=jax.ShapeDtypeStruct((batch, n_tiles, ATT_TILE, DIFF_WIDTH), BF16),
        scratch_shapes=_flash_scratch(2 * group, 2 * HEAD_DIM),
        compiler_params=_params("arbitrary", "arbitrary", "arbitrary"),
        name="diff_attention",
    )(bounded, *lam_vecs, sub_g_col, qk4, qk4, vt4, bias)


def _split3(v):
    hi = v.astype(BF16)
    r1 = v - hi.astype(F32)
    mid = r1.astype(BF16)
    lo = (r1 - mid.astype(F32)).astype(BF16)
    return hi, mid, lo


def _block_mask(gate, own):
    n_blocks = gate.shape[0]
    row = lax.broadcasted_iota(jnp.int32, gate.shape, 0)
    rank = jnp.zeros(gate.shape, jnp.int32)
    for m in range(n_blocks):
        gm = gate[m:m + 1, :]
        beats = (gm > gate) | ((gm == gate) & (row > m))
        rank = rank + jnp.where(beats & (own > m), 1, 0)
    keep = ((rank < MOBA_TOPK) & (row < own)) | (row == own)
    return jnp.where(keep, 0.0, NEG_INF).astype(F32)


def _moba_kernel(bounded_ref, q_ref, k_ref, vt_ref, bias_ref, o_ref, kmean_ref, mask_ref, s_ref, p_ref, acc_ref):
    qi = pl.program_id(2)
    group, n_blocks = k_ref.shape[0], k_ref.shape[1]

    @pl.when(qi == 0)
    def _():
        for g in range(group):
            for n in range(n_blocks):
                kmean_ref[g, n:n + 1, :] = jnp.mean(k_ref[g, n].astype(F32), axis=0, keepdims=True)

    qs = [_half_lane_split(q_ref[g]) for g in range(group)]
    for g in range(group):
        parts = _split3(kmean_ref[g])
        for half in range(2):
            gate = jnp.zeros((n_blocks, ATT_TILE), F32)
            for part in parts:
                gate = gate + lax.dot_general(part, qs[g][half], _NT, preferred_element_type=F32)
            mask_ref[2 * g + half] = _block_mask(gate, qi)

    def values(c, kv):
        half = c % 2
        return _with_ones_rows(vt_ref[c // 2, kv, half * HEAD_DIM:(half + 1) * HEAD_DIM, :])

    chains = dict(
        n_chains=2 * group,
        logits_fn=lambda c, kv: lax.dot_general(k_ref[c // 2, kv], qs[c // 2][c % 2], _NT,
                                                preferred_element_type=F32),
        values_fn=values,
        bias_fn=lambda c, j: bias_ref[c % 2, j],
        mask_fn=lambda c, kv: mask_ref[c, pl.ds(kv, 1), :],
        acc_ref=acc_ref)
    bounded = bounded_ref[0] > 0

    @pl.when(bounded)
    def _():
        _bounded_tiles(qi, **chains)

    @pl.when(jnp.logical_not(bounded))
    def _():
        _flash_tiles(qi, s_ref=s_ref, p_ref=p_ref, **chains)

    for g in range(group):
        halves = [acc_ref[2 * g + half] for half in range(2)]
        o = jnp.concatenate([a[:HEAD_DIM] / a[HEAD_DIM:HEAD_DIM + 1] for a in halves], axis=0)
        o_ref[g] = o.T.astype(BF16)


def _moba_attention(bounded, qk4, vt4, bias_pairs):
    batch, n_tiles = qk4.shape[0], qk4.shape[1]
    q_col0 = 2 * DIFF_WIDTH // LANES
    k_col0 = q_col0 + MOBA_WIDTH // LANES
    v_row0 = DIFF_WIDTH // LANES
    pair0 = HEADS_DIFF // 2
    group = _batch_group(batch)
    return pl.pallas_call(
        _moba_kernel,
        grid=(HEADS_MOBA // 2, batch // group, n_tiles),
        in_specs=[
            pl.BlockSpec(memory_space=pltpu.SMEM),
            pl.BlockSpec((group, None, ATT_TILE, LANES), lambda h, b, i: (b, i, 0, q_col0 + h)),
            pl.BlockSpec((group, n_tiles, ATT_TILE, LANES), lambda h, b, i: (b, 0, 0, k_col0 + h)),
            pl.BlockSpec((group, n_tiles, LANES, ATT_TILE), lambda h, b, i: (b, 0, v_row0 + h, 0)),
            pl.BlockSpec((None, 2, n_tiles, ATT_TILE, ATT_TILE), lambda h, b, i: (pair0 + h, 0, 0, 0, 0)),
        ],
        out_specs=pl.BlockSpec((group, None, ATT_TILE, LANES), lambda h, b, i: (b, i, 0, h)),
        out_shape=jax.ShapeDtypeStruct((batch, n_tiles, ATT_TILE, MOBA_WIDTH), BF16),
        scratch_shapes=[
            pltpu.VMEM((group, n_tiles, LANES), F32),
            pltpu.VMEM((2 * group, n_tiles, ATT_TILE), F32),
        ] + _flash_scratch(2 * group, HEAD_DIM),
        compiler_params=_params("arbitrary", "arbitrary", "arbitrary"),
        name="moba_attention",
    )(bounded, qk4, qk4, vt4, bias_pairs)


def _load_token_rows(ref, n_rows, lead=()):
    chunks = [ref[lead + (pl.ds(c, n_rows, stride=ROW_TILE), slice(None))] for c in range(ROW_TILE)]
    return jnp.concatenate(chunks, axis=1)


def _store_token_rows(ref, value):
    n_rows = value.shape[0]
    for c in range(ROW_TILE):
        ref[pl.ds(c, n_rows, stride=ROW_TILE), :] = value[:, c * LANES:(c + 1) * LANES]


def _token_tile(ref, t):
    return ref.at[pl.ds(pl.multiple_of(t * ROW_TILE, ROW_TILE), ROW_TILE)]


def _first_argmax(v):
    top = jnp.max(v, axis=0, keepdims=True)
    row = lax.broadcasted_iota(jnp.int32, v.shape, 0)
    idx = jnp.min(jnp.where(v == top, row, v.shape[0]), axis=0, keepdims=True)
    return top, idx


def _out_proj_kernel(yd_ref, ym_ref, x_ref, wd_ref, wm_ref, g2_ref, wr_ref, tri_ref,
                     x1_ref, ri_ref, rf_ref, cnt_ref, run_ref):
    @pl.when(pl.program_id(0) == 0)
    def _():
        run_ref[...] = jnp.zeros_like(run_ref)

    x1 = (x_ref[...] + jnp.dot(yd_ref[...], wd_ref[...], preferred_element_type=F32)
          + jnp.dot(ym_ref[...], wm_ref[...], preferred_element_type=F32))
    _store_token_rows(x1_ref, x1)
    ms = jnp.mean(x1 * x1, axis=-1, keepdims=True)
    h2 = (x1 * lax.rsqrt(ms + RMS_EPS) * g2_ref[...]).astype(BF16)
    logits = lax.dot_general(wr_ref[...], h2, _NT, preferred_element_type=F32)

    g_logits = logits[N_EXPERTS:N_EXPERTS + N_GROUPS, :]
    g_top, g_idx = _first_argmax(g_logits)
    p_group = 1.0 / jnp.sum(jnp.exp(g_logits - g_top), axis=0, keepdims=True)
    e_logits = logits[0:EXPERTS_PER_GROUP, :]
    for g in range(1, N_GROUPS):
        e_logits = jnp.where(g_idx == g, logits[g * EXPERTS_PER_GROUP:(g + 1) * EXPERTS_PER_GROUP, :], e_logits)
    v1, i1 = _first_argmax(e_logits)
    row = lax.broadcasted_iota(jnp.int32, e_logits.shape, 0)
    v2, i2 = _first_argmax(jnp.where(row == i1, -jnp.inf, e_logits))
    t = jnp.exp(v2 - v1)
    w1 = p_group / (1.0 + t)
    w2 = p_group * t / (1.0 + t)
    e1 = g_idx * EXPERTS_PER_GROUP + i1
    e2 = g_idx * EXPERTS_PER_GROUP + i2

    erow = lax.broadcasted_iota(jnp.int32, (N_EXPERTS, OUT_ROWS), 0)
    hit1 = erow == e1
    hit2 = erow == e2
    onehot = jnp.where(hit1 | hit2, 1.0, 0.0).astype(F32)
    before = jnp.dot(onehot.astype(BF16), tri_ref[...], preferred_element_type=F32) + run_ref[...]
    r1 = jnp.sum(jnp.where(hit1, before, 0.0), axis=0, keepdims=True).astype(jnp.int32)
    r2 = jnp.sum(jnp.where(hit2, before, 0.0), axis=0, keepdims=True).astype(jnp.int32)
    run_ref[...] = run_ref[...] + jnp.sum(onehot, axis=1, keepdims=True)

    out_row = lax.broadcasted_iota(jnp.int32, ri_ref.shape, 0)
    ri_ref[...] = jnp.where(out_row == 0, e1, jnp.where(out_row == 1, e2,
                            jnp.where(out_row == 2, r1, jnp.where(out_row == 3, r2, 0))))
    rf_ref[...] = jnp.where(out_row == 0, w1, jnp.where(out_row == 1, w2, 0.0))
    cnt_ref[...] = jnp.broadcast_to(run_ref[...], cnt_ref.shape).astype(jnp.int32)


def _out_proj(y_d, y_m, x2, wo_d, wo_m, g2, w_router, tri):
    n = x2.shape[0]
    const = lambda i: (0, 0)
    return pl.pallas_call(
        _out_proj_kernel,
        grid=(n // OUT_ROWS,),
        in_specs=[
            pl.BlockSpec((OUT_ROWS, DIFF_WIDTH), lambda i: (i, 0)),
            pl.BlockSpec((OUT_ROWS, MOBA_WIDTH), lambda i: (i, 0)),
            pl.BlockSpec((OUT_ROWS, D_MODEL), lambda i: (i, 0)),
            pl.BlockSpec((DIFF_WIDTH, D_MODEL), const),
            pl.BlockSpec((MOBA_WIDTH, D_MODEL), const),
            pl.BlockSpec((1, D_MODEL), const),
            pl.BlockSpec((ROUTER_ROWS, D_MODEL), const),
            pl.BlockSpec((OUT_ROWS, OUT_ROWS), const),
        ],
        out_specs=[
            pl.BlockSpec((OUT_ROWS * ROW_TILE, LANES), lambda i: (i, 0)),
            pl.BlockSpec((8, OUT_ROWS), lambda i: (0, i)),
            pl.BlockSpec((8, OUT_ROWS), lambda i: (0, i)),
            pl.BlockSpec((N_EXPERTS, LANES), const),
        ],
        out_shape=[
            jax.ShapeDtypeStruct((n * ROW_TILE, LANES), F32),
            jax.ShapeDtypeStruct((8, n), jnp.int32),
            jax.ShapeDtypeStruct((8, n), F32),
            jax.ShapeDtypeStruct((N_EXPERTS, LANES), jnp.int32),
        ],
        scratch_shapes=[pltpu.VMEM((N_EXPERTS, 1), F32)],
        compiler_params=_params("arbitrary"),
        name="out_proj_route",
    )(y_d, y_m, x2, wo_d, wo_m, g2, w_router, tri)


def _dispatch_kernel(zchunk_ref, dest_ref, x1_hbm, xs_hbm, zero_ref, stage_ref, zero_sem, load_sem, row_sem):
    i = pl.program_id(0)
    last = pl.num_programs(0) - 1
    slot = lax.rem(i, 2)
    chunk_rows = MOE_CHUNK * ROW_TILE
    tile_rows = MOVE_ROWS * ROW_TILE

    def load(t, s):
        start = pl.multiple_of(t * tile_rows, tile_rows)
        return pltpu.make_async_copy(x1_hbm.at[pl.ds(start, tile_rows)], stage_ref.at[s], load_sem.at[s])

    def wait_rows(s):
        for _ in range(EXPERTS_PER_TOKEN):
            pltpu.make_async_copy(stage_ref.at[s], xs_hbm.at[pl.ds(0, tile_rows)], row_sem.at[s]).wait()

    def zero_copy(c):
        start = pl.multiple_of(c * chunk_rows, chunk_rows)
        return pltpu.make_async_copy(zero_ref, xs_hbm.at[pl.ds(start, chunk_rows)], zero_sem)

    @pl.when(i == 0)
    def _():
        zero_ref[...] = jnp.zeros_like(zero_ref)

        def start_one(c, carry):
            @pl.when(zchunk_ref[c] > 0)
            def _():
                zero_copy(c).start()
            return carry

        def wait_one(c, carry):
            @pl.when(zchunk_ref[c] > 0)
            def _():
                zero_copy(c).wait()
            return carry

        lax.fori_loop(0, zchunk_ref.shape[0], start_one, 0)
        lax.fori_loop(0, zchunk_ref.shape[0], wait_one, 0)
        load(0, 0).start()

    @pl.when(i > 0)
    def _():
        wait_rows(1 - slot)

    @pl.when(i < last)
    def _():
        load(i + 1, 1 - slot).start()

    load(i, slot).wait()

    def send_row(r, carry):
        src = _token_tile(stage_ref.at[slot], r)
        for k in range(EXPERTS_PER_TOKEN):
            pltpu.make_async_copy(src, _token_tile(xs_hbm, dest_ref[0, k, r]), row_sem.at[slot]).start()
        return carry

    lax.fori_loop(0, MOVE_ROWS, send_row, 0, unroll=8)

    @pl.when(i == last)
    def _():
        wait_rows(slot)


def _dispatch(zchunk, dest3, x1, n_slots):
    n = x1.shape[0] // ROW_TILE
    grid_spec = pltpu.PrefetchScalarGridSpec(
        num_scalar_prefetch=1,
        grid=(n // MOVE_ROWS,),
        in_specs=[
            pl.BlockSpec((1, EXPERTS_PER_TOKEN, MOVE_ROWS), lambda i, zc: (i, 0, 0), memory_space=pltpu.SMEM),
            pl.BlockSpec(memory_space=pl.ANY),
        ],
        out_specs=pl.BlockSpec(memory_space=pl.ANY),
        scratch_shapes=[
            pltpu.VMEM((MOE_CHUNK * ROW_TILE, LANES), F32),
            pltpu.VMEM((2, MOVE_ROWS * ROW_TILE, LANES), F32),
            pltpu.SemaphoreType.DMA(()),
            pltpu.SemaphoreType.DMA((2,)),
            pltpu.SemaphoreType.DMA((2,)),
        ],
    )
    return pl.pallas_call(
        _dispatch_kernel,
        grid_spec=grid_spec,
        out_shape=jax.ShapeDtypeStruct((n_slots * ROW_TILE, LANES), F32),
        compiler_params=_params("arbitrary"),
        name="moe_dispatch",
    )(zchunk, dest3, x1)


def _expert_kernel(ce_ref, na_ref, xs_ref, g2_ref, wg_ref, wu_ref, wd_ref, o_ref):
    active = pl.program_id(0) < na_ref[0]

    @pl.when(jnp.logical_not(active))
    def _():
        o_ref[...] = jnp.zeros_like(o_ref)

    @pl.when(active)
    def _():
        x = _load_token_rows(xs_ref, MOE_CHUNK)
        ms = jnp.mean(x * x, axis=-1, keepdims=True)
        h = (x * lax.rsqrt(ms + RMS_EPS) * g2_ref[...]).astype(BF16)
        gate = jnp.dot(h, wg_ref[...], preferred_element_type=F32)
        up = jnp.dot(h, wu_ref[...], preferred_element_type=F32)
        hid = (gate * jax.nn.sigmoid(gate) * up).astype(BF16)
        _store_token_rows(o_ref, jnp.dot(hid, wd_ref[...], preferred_element_type=F32))


def _experts(chunk_e, n_active, xs, g2, wg, wu, wd):
    n_slots = xs.shape[0] // ROW_TILE
    rows = lambda c, ce, na: (jnp.minimum(c, na[0] - 1), 0)
    grid_spec = pltpu.PrefetchScalarGridSpec(
        num_scalar_prefetch=2,
        grid=(n_slots // MOE_CHUNK,),
        in_specs=[
            pl.BlockSpec((MOE_CHUNK * ROW_TILE, LANES), rows),
            pl.BlockSpec((1, D_MODEL), lambda c, ce, na: (0, 0)),
            pl.BlockSpec((None, D_MODEL, EXPERT_HIDDEN), lambda c, ce, na: (ce[c], 0, 0)),
            pl.BlockSpec((None, D_MODEL, EXPERT_HIDDEN), lambda c, ce, na: (ce[c], 0, 0)),
            pl.BlockSpec((None, EXPERT_HIDDEN, D_MODEL), lambda c, ce, na: (ce[c], 0, 0)),
        ],
        out_specs=pl.BlockSpec((MOE_CHUNK * ROW_TILE, LANES), lambda c, ce, na: (c, 0)),
    )
    return pl.pallas_call(
        _expert_kernel,
        grid_spec=grid_spec,
        out_shape=jax.ShapeDtypeStruct((n_slots * ROW_TILE, LANES), F32),
        compiler_params=_params("arbitrary"),
        name="moe_experts",
    )(chunk_e, n_active, xs, g2, wg, wu, wd)


def _combine_kernel(dest_ref, dest_next_ref, w_ref, x1_ref, outs_hbm, y_ref, buf_ref, sem_ref):
    i = pl.program_id(0)
    slot = lax.rem(i, 2)

    def fetch(dref, s):
        def fetch_row(r, carry):
            for k in range(EXPERTS_PER_TOKEN):
                pltpu.make_async_copy(_token_tile(outs_hbm, dref[0, k, r]),
                                      _token_tile(buf_ref.at[s, k], r), sem_ref.at[s]).start()
            return carry
        lax.fori_loop(0, MOVE_ROWS, fetch_row, 0, unroll=8)

    @pl.when(i == 0)
    def _():
        fetch(dest_ref, 0)

    @pl.when(i + 1 < pl.num_programs(0))
    def _():
        fetch(dest_next_ref, 1 - slot)

    for k in range(EXPERTS_PER_TOKEN):
        pltpu.make_async_copy(outs_hbm.at[pl.ds(0, MOVE_ROWS * ROW_TILE)], buf_ref.at[slot, k],
                              sem_ref.at[slot]).wait()
    w = w_ref[...]
    y_ref[...] = (_load_token_rows(x1_ref, MOVE_ROWS)
                  + w[:, 0:1] * _load_token_rows(buf_ref, MOVE_ROWS, (slot, 0))
                  + w[:, 1:2] * _load_token_rows(buf_ref, MOVE_ROWS, (slot, 1)))


def _combine(dest3, wts, x1, outs):
    n = x1.shape[0] // ROW_TILE
    steps = n // MOVE_ROWS
    dest_block = (1, EXPERTS_PER_TOKEN, MOVE_ROWS)
    return pl.pallas_call(
        _combine_kernel,
        grid=(steps,),
        in_specs=[
            pl.BlockSpec(dest_block, lambda i: (i, 0, 0), memory_space=pltpu.SMEM),
            pl.BlockSpec(dest_block, lambda i: (jnp.minimum(i + 1, steps - 1), 0, 0), memory_space=pltpu.SMEM),
            pl.BlockSpec((MOVE_ROWS, EXPERTS_PER_TOKEN), lambda i: (i, 0)),
            pl.BlockSpec((MOVE_ROWS * ROW_TILE, LANES), lambda i: (i, 0)),
            pl.BlockSpec(memory_space=pl.ANY),
        ],
        out_specs=pl.BlockSpec((MOVE_ROWS, D_MODEL), lambda i: (i, 0)),
        out_shape=jax.ShapeDtypeStruct((n, D_MODEL), F32),
        scratch_shapes=[
            pltpu.VMEM((2, EXPERTS_PER_TOKEN, MOVE_ROWS * ROW_TILE, LANES), F32),
            pltpu.SemaphoreType.DMA((2,)),
        ],
        compiler_params=_params("arbitrary"),
        name="moe_combine",
    )(dest3, dest3, wts, x1, outs)


def _layer(x, norm1_g, w_in, diff_q_g, diff_k_g, lambda_q1, lambda_k1, lambda_q2, lambda_k2, diff_sub_g,
           moba_q_g, moba_k_g, rel_bias, w_out, norm2_g, router_group, router_expert, w_gate, w_up, w_down):
    batch, seq, _ = x.shape
    n = batch * seq
    n_tiles = seq // ATT_TILE
    scale = HEAD_DIM ** -0.5 * LOG2E

    d3 = 3 * DIFF_WIDTH
    w_perm = jnp.concatenate([w_in[:, :2 * DIFF_WIDTH], w_in[:, d3:d3 + 2 * MOBA_WIDTH],
                              w_in[:, 2 * DIFF_WIDTH:d3], w_in[:, d3 + 2 * MOBA_WIDTH:]], axis=1).astype(BF16)
    reps_d, reps_m = DIFF_WIDTH // HEAD_DIM, MOBA_WIDTH // HEAD_DIM
    post_gain = jnp.concatenate([jnp.tile(diff_q_g * scale, reps_d), jnp.tile(diff_k_g, reps_d),
                                 jnp.tile(moba_q_g * scale, reps_m), jnp.tile(moba_k_g, reps_m)])[None, :]
    head_of = np.arange(2 * LANES) // HEAD_DIM
    grp = jnp.asarray((head_of[:, None] == head_of[None, :]) / HEAD_DIM, BF16)

    bias = _bias_tiles(rel_bias, n_tiles)
    qk, vt4 = _in_proj(x.reshape(n, D_MODEL), norm1_g[None, :], w_perm, post_gain, grp, batch, seq)
    qk4 = qk.reshape(batch, n_tiles, ATT_TILE, QK_WIDTH)
    lam_vecs = [v[None, :] for v in (lambda_q1, lambda_k1, lambda_q2, lambda_k2)]
    y_d = _diff_attention(_logits_bounded(diff_q_g * scale, diff_k_g, rel_bias[:, :HEADS_DIFF]),
                          lam_vecs, diff_sub_g[:, None], qk4, vt4, bias)
    y_m = _moba_attention(_logits_bounded(moba_q_g * scale, moba_k_g, rel_bias[:, HEADS_DIFF:]),
                          qk4, vt4, bias.reshape(N_HEADS_TOTAL // 2, 2, n_tiles, ATT_TILE, ATT_TILE))

    w_router = jnp.concatenate([
        router_expert.transpose(0, 2, 1).reshape(N_EXPERTS, D_MODEL), router_group.T,
        jnp.zeros((ROUTER_ROWS - N_EXPERTS - N_GROUPS, D_MODEL), F32)], axis=0).astype(BF16)
    tri = jnp.asarray(np.triu(np.ones((OUT_ROWS, OUT_ROWS)), k=1), BF16)
    wo = w_out.astype(BF16)
    x1, route_i, route_f, cnt = _out_proj(y_d.reshape(n, DIFF_WIDTH), y_m.reshape(n, MOBA_WIDTH),
                                          x.reshape(n, D_MODEL), wo[:DIFF_WIDTH], wo[DIFF_WIDTH:],
                                          norm2_g[None, :], w_router, tri)

    counts = cnt[:, 0]
    pad_counts = ((counts + MOE_CHUNK - 1) // MOE_CHUNK) * MOE_CHUNK
    pad_end = jnp.cumsum(pad_counts)
    pad_start = pad_end - pad_counts
    experts = jnp.arange(N_EXPERTS, dtype=jnp.int32)
    start_of = jnp.sum(jnp.where(route_i[0:2, :, None] == experts, pad_start, 0), axis=-1)
    dest = start_of + route_i[2:4]
    dest3 = dest.reshape(EXPERTS_PER_TOKEN, n // MOVE_ROWS, MOVE_ROWS).transpose(1, 0, 2)
    n_slots = n * EXPERTS_PER_TOKEN + N_EXPERTS * MOE_CHUNK
    n_chunks = n_slots // MOE_CHUNK
    chunk_id = jnp.arange(n_chunks, dtype=jnp.int32)
    chunk_e = jnp.minimum(jnp.sum((pad_end[None, :] <= chunk_id[:, None] * MOE_CHUNK).astype(jnp.int32), axis=1),
                          N_EXPERTS - 1)
    n_active = (pad_end[-1:] // MOE_CHUNK).astype(jnp.int32)
    next_e = jnp.concatenate([chunk_e[1:], jnp.full((1,), N_EXPERTS, jnp.int32)])
    zchunk = ((chunk_id >= n_active - 1) | (chunk_e != next_e)).astype(jnp.int32)

    xs = _dispatch(zchunk, dest3, x1, n_slots)
    outs = _experts(chunk_e, n_active, xs, norm2_g[None, :], w_gate.astype(BF16), w_up.astype(BF16),
                    w_down.astype(BF16))
    y = _combine(dest3, route_f[0:2].T, x1, outs)
    return y.reshape(batch, seq, D_MODEL)


def kernel(x, norm1_g, w_in, diff_q_g, diff_k_g, lambda_q1, lambda_k1, lambda_q2, lambda_k2, diff_sub_g,
           moba_q_g, moba_k_g, rel_bias, w_out, norm2_g, router_group, router_expert, w_gate, w_up, w_down):
    assert x.shape[1] % PROJ_ROWS == 0 and x.shape[2] == D_MODEL and norm1_g.shape[0] == 1
    return _layer(x, norm1_g[0], w_in[0], diff_q_g[0], diff_k_g[0], lambda_q1[0], lambda_k1[0], lambda_q2[0],
                  lambda_k2[0], diff_sub_g[0], moba_q_g[0], moba_k_g[0], rel_bias, w_out[0], norm2_g[0],
                  router_group[0], router_expert[0], w_gate[0], w_up[0], w_down[0])
```

```python
import functools
import math

import numpy as np
import jax
import jax.numpy as jnp
from jax import lax
from jax.experimental import pallas as pl
from jax.experimental.pallas import tpu as pltpu

D_MODEL = 1024
HEAD_DIM = 64
HEADS_DIFF = 4
HEADS_MOBA = 8
N_HEADS_TOTAL = HEADS_DIFF + HEADS_MOBA
DIFF_WIDTH = HEADS_DIFF * 2 * HEAD_DIM
MOBA_WIDTH = HEADS_MOBA * HEAD_DIM
QK_WIDTH = 2 * DIFF_WIDTH + 2 * MOBA_WIDTH
V_WIDTH = DIFF_WIDTH + MOBA_WIDTH
MOBA_TOPK = 3
NUM_BUCKETS = 32
MAX_DISTANCE = 2048
N_GROUPS = 4
EXPERTS_PER_GROUP = 8
N_EXPERTS = N_GROUPS * EXPERTS_PER_GROUP
EXPERTS_PER_TOKEN = 2
EXPERT_HIDDEN = 256
RMS_EPS = 1e-6
NEG_INF = -1e30
LAMBDA_INIT = 0.8 - 0.6 * math.exp(-0.3 * 0)

LANES = 128
ROW_TILE = D_MODEL // LANES
ATT_TILE = 256
ONES_ROWS = 16
LOGIT_LIMIT = 80.0
ROUNDING_MARGIN = 1.05
LOG2E = math.log2(math.e)
PROJ_ROWS = 512
OUT_ROWS = 256
MOE_CHUNK = 256
MOVE_ROWS = 256
ROUTER_ROWS = 40
VMEM_LIMIT = 48 * 1024 * 1024

F32 = jnp.float32
BF16 = jnp.bfloat16
_NT = (((1,), (1,)), ((), ()))


def _t5_thresholds():
    n = np.arange(0, 1 << 16)
    max_exact = NUM_BUCKETS // 2
    nf = np.maximum(n, 1).astype(np.float32)
    large = max_exact + (np.log(nf / np.float32(max_exact)) / np.float32(math.log(MAX_DISTANCE / max_exact))
                         * np.float32(NUM_BUCKETS - max_exact)).astype(np.int32)
    bucket = np.where(n < max_exact, n, np.minimum(large, NUM_BUCKETS - 1))
    return [int(np.searchsorted(bucket, b, side="left")) for b in range(1, NUM_BUCKETS)]


_T5_THRESHOLDS = _t5_thresholds()


def _params(*sem):
    return pltpu.CompilerParams(dimension_semantics=sem, vmem_limit_bytes=VMEM_LIMIT)


def _bias_kernel(tab_ref, out_ref):
    h = pl.program_id(0)
    d = pl.program_id(1)
    kj = lax.broadcasted_iota(jnp.int32, (ATT_TILE, ATT_TILE), 0)
    qi = lax.broadcasted_iota(jnp.int32, (ATT_TILE, ATT_TILE), 1)
    dist = d * ATT_TILE + qi - kj
    val = jnp.full((ATT_TILE, ATT_TILE), tab_ref[0, h], F32)
    for b, thr in enumerate(_T5_THRESHOLDS, start=1):
        val = jnp.where(dist >= thr, tab_ref[b, h], val)
    out_ref[...] = jnp.where(dist < 0, NEG_INF, val * LOG2E)


def _bias_tiles(rel_bias, n_diag):
    return pl.pallas_call(
        _bias_kernel,
        grid=(N_HEADS_TOTAL, n_diag),
        in_specs=[pl.BlockSpec(memory_space=pltpu.SMEM)],
        out_specs=pl.BlockSpec((None, None, ATT_TILE, ATT_TILE), lambda h, d: (h, d, 0, 0)),
        out_shape=jax.ShapeDtypeStruct((N_HEADS_TOTAL, n_diag, ATT_TILE, ATT_TILE), F32),
        compiler_params=_params("arbitrary", "arbitrary"),
        name="bias_tiles",
    )(rel_bias)


def _in_proj_kernel(x_ref, g1_ref, w_ref, pg_ref, grp_ref, qk_ref, vt_ref):
    x = x_ref[...]
    ms = jnp.mean(x * x, axis=-1, keepdims=True)
    h = (x * lax.rsqrt(ms + RMS_EPS) * g1_ref[...]).astype(BF16)
    cw = 2 * LANES
    for c in range(QK_WIDTH // cw):
        cols = slice(c * cw, (c + 1) * cw)
        p = jnp.dot(h, w_ref[:, cols], preferred_element_type=F32)
        msq = jnp.dot((p * p).astype(BF16), grp_ref[...], preferred_element_type=F32)
        qk_ref[:, cols] = (p * lax.rsqrt(msq + RMS_EPS) * pg_ref[:, cols]).astype(BF16)
    for c in range(V_WIDTH // cw):
        p = jnp.dot(h, w_ref[:, QK_WIDTH + c * cw:QK_WIDTH + (c + 1) * cw], preferred_element_type=F32)
        for t in range(PROJ_ROWS // ATT_TILE):
            vt_ref[t, c * cw:(c + 1) * cw, :] = p[t * ATT_TILE:(t + 1) * ATT_TILE, :].T.astype(BF16)


def _in_proj(x2, g1, w_perm, post_gain, grp, batch, seq):
    n = x2.shape[0]
    steps_per_seq = seq // PROJ_ROWS
    tiles_per_step = PROJ_ROWS // ATT_TILE
    return pl.pallas_call(
        _in_proj_kernel,
        grid=(n // PROJ_ROWS,),
        in_specs=[
            pl.BlockSpec((PROJ_ROWS, D_MODEL), lambda i: (i, 0)),
            pl.BlockSpec((1, D_MODEL), lambda i: (0, 0)),
            pl.BlockSpec((D_MODEL, QK_WIDTH + V_WIDTH), lambda i: (0, 0)),
            pl.BlockSpec((1, QK_WIDTH), lambda i: (0, 0)),
            pl.BlockSpec((2 * LANES, 2 * LANES), lambda i: (0, 0)),
        ],
        out_specs=[
            pl.BlockSpec((PROJ_ROWS, QK_WIDTH), lambda i: (i, 0)),
            pl.BlockSpec((None, tiles_per_step, V_WIDTH, ATT_TILE),
                         lambda i: (i // steps_per_seq, i % steps_per_seq, 0, 0)),
        ],
        out_shape=[
            jax.ShapeDtypeStruct((n, QK_WIDTH), BF16),
            jax.ShapeDtypeStruct((batch, seq // ATT_TILE, V_WIDTH, ATT_TILE), BF16),
        ],
        compiler_params=_params("arbitrary"),
        name="in_proj",
    )(x2, g1, w_perm, post_gain, grp)


def _flash_tiles(qi, n_chains, logits_fn, values_fn, bias_fn, mask_fn, s_ref, p_ref, acc_ref):
    for c in range(n_chains):
        s_ref[0, c] = logits_fn(c, qi)
    p_ref[...] = jnp.zeros_like(p_ref)
    acc_ref[...] = jnp.zeros_like(acc_ref)

    def add_values(c, kv, alpha):
        acc_ref[c] = acc_ref[c] * alpha + jnp.dot(values_fn(c, kv), p_ref[c], preferred_element_type=F32)

    def step(j, carry):
        cur = lax.rem(j, 2)
        kv = qi - j
        out = []
        for c in range(n_chains):
            add_values(c, jnp.minimum(kv + 1, qi), carry[2 * c + 1])
        for c in range(n_chains):
            s = s_ref[cur, c] + bias_fn(c, j)
            tile_max = jnp.max(s, axis=0, keepdims=True)
            shift = mask_fn(c, kv)
            if shift is not None:
                tile_max = tile_max + shift
            m_new = jnp.maximum(carry[2 * c], tile_max)
            alpha = jnp.exp2(carry[2 * c] - m_new)
            p_ref[c] = jnp.exp2(s - (m_new if shift is None else m_new - shift)).astype(BF16)
            out += [m_new, alpha]
        for c in range(n_chains):
            s_ref[1 - cur, c] = logits_fn(c, jnp.maximum(kv - 1, 0))
        return tuple(out)

    m0 = jnp.full((1, ATT_TILE), NEG_INF, F32)
    one = jnp.ones((1, ATT_TILE), F32)
    carry = lax.fori_loop(0, qi + 1, step, (m0, one) * n_chains)
    for c in range(n_chains):
        add_values(c, 0, carry[2 * c + 1])


def _bounded_tiles(n_tiles, n_chains, logits_fn, values_fn, bias_fn, keep_fn, p_ref, acc_ref):
    for d in range(n_tiles):
        for qi in range(d, n_tiles):
            for c in range(n_chains):
                p_ref[qi, c] = jnp.exp2(logits_fn(c, qi, qi - d) + bias_fn(c, d)).astype(BF16)
        for qi in range(d, n_tiles):
            kv = qi - d
            for c in range(n_chains):
                pv = jnp.dot(values_fn(c, kv), p_ref[qi, c], preferred_element_type=F32)
                keep = keep_fn(c, qi, kv)
                if keep is not None:
                    pv = pv * keep
                acc_ref[qi, c] = pv if d == 0 else acc_ref[qi, c] + pv


def _logits_bounded(q_gain, k_gain, bias_cols):
    bound = (HEAD_DIM * jnp.max(jnp.abs(q_gain)) * jnp.max(jnp.abs(k_gain)) * ROUNDING_MARGIN
             + jnp.max(jnp.abs(bias_cols)) * LOG2E)
    return (bound <= LOGIT_LIMIT).astype(jnp.int32).reshape(1)


def _with_ones_rows(vt):
    return jnp.concatenate([vt, jnp.ones((ONES_ROWS, vt.shape[1]), vt.dtype)], axis=0)


def _attention_scratch(n_tiles, value_rows, n_chains=2):
    return [pltpu.VMEM((n_tiles, n_chains, ATT_TILE, LANES), BF16),
            pltpu.VMEM((2, n_chains, ATT_TILE, ATT_TILE), F32),
            pltpu.VMEM((n_tiles, n_chains, ATT_TILE, ATT_TILE), BF16),
            pltpu.VMEM((n_tiles, n_chains, value_rows + ONES_ROWS, ATT_TILE), F32)]


def _half_lane_split(q):
    lane = lax.broadcasted_iota(jnp.int32, q.shape, 1)
    zero = jnp.zeros_like(q)
    return jnp.where(lane < HEAD_DIM, q, zero), jnp.where(lane >= HEAD_DIM, q, zero)


def _diff_kernel(bounded_ref, lq1_ref, lk1_ref, lq2_ref, lk2_ref, subg_ref, q_ref, k_ref, vt_ref, bias_ref, o_ref,
                 qm_ref, s_ref, p_ref, acc_ref):
    n_tiles = q_ref.shape[0]
    for qi in range(n_tiles):
        qm_ref[qi, 0], qm_ref[qi, 1] = _half_lane_split(q_ref[qi])
    chains = dict(n_chains=2, values_fn=lambda c, kv: _with_ones_rows(vt_ref[kv]))
    bounded = bounded_ref[0] > 0

    @pl.when(bounded)
    def _():
        _bounded_tiles(
            n_tiles,
            logits_fn=lambda c, qi, kv: lax.dot_general(k_ref[kv], qm_ref[qi, c], _NT, preferred_element_type=F32),
            bias_fn=lambda c, d: bias_ref[d],
            keep_fn=lambda c, qi, kv: None,
            p_ref=p_ref, acc_ref=acc_ref, **chains)

    @pl.when(jnp.logical_not(bounded))
    def _():
        def q_tile(qi, carry):
            _flash_tiles(
                qi,
                logits_fn=lambda c, kv: lax.dot_general(k_ref[kv], qm_ref[qi, c], _NT, preferred_element_type=F32),
                bias_fn=lambda c, j: bias_ref[j],
                mask_fn=lambda c, kv: None,
                s_ref=s_ref, p_ref=p_ref.at[0], acc_ref=acc_ref.at[qi], **chains)
            return carry
        lax.fori_loop(0, n_tiles, q_tile, 0)

    lam = (jnp.exp(jnp.sum(lq1_ref[...] * lk1_ref[...], keepdims=True))
           - jnp.exp(jnp.sum(lq2_ref[...] * lk2_ref[...], keepdims=True)) + LAMBDA_INIT)
    width = 2 * HEAD_DIM
    for qi in range(n_tiles):
        a1, a2 = acc_ref[qi, 0], acc_ref[qi, 1]
        o = a1[:width] / a1[width:width + 1] - lam * (a2[:width] / a2[width:width + 1])
        ms = jnp.mean(o * o, axis=0, keepdims=True)
        o = o * lax.rsqrt(ms + RMS_EPS) * subg_ref[...] * (1.0 - LAMBDA_INIT)
        o_ref[qi] = o.T.astype(BF16)


def _diff_attention(bounded, lam_vecs, sub_g_col, qk4, vt4, bias):
    batch, n_tiles = qk4.shape[0], qk4.shape[1]
    k_col0 = DIFF_WIDTH // LANES
    vec = pl.BlockSpec((1, HEAD_DIM), lambda h, b: (0, 0))
    return pl.pallas_call(
        _diff_kernel,
        grid=(HEADS_DIFF, batch),
        in_specs=[
            pl.BlockSpec(memory_space=pltpu.SMEM),
            vec, vec, vec, vec,
            pl.BlockSpec((2 * HEAD_DIM, 1), lambda h, b: (0, 0)),
            pl.BlockSpec((None, n_tiles, ATT_TILE, LANES), lambda h, b: (b, 0, 0, h)),
            pl.BlockSpec((None, n_tiles, ATT_TILE, LANES), lambda h, b: (b, 0, 0, k_col0 + h)),
            pl.BlockSpec((None, n_tiles, LANES, ATT_TILE), lambda h, b: (b, 0, h, 0)),
            pl.BlockSpec((None, n_tiles, ATT_TILE, ATT_TILE), lambda h, b: (h, 0, 0, 0)),
        ],
        out_specs=pl.BlockSpec((None, n_tiles, ATT_TILE, LANES), lambda h, b: (b, 0, 0, h)),
        out_shape=jax.ShapeDtypeStruct((batch, n_tiles, ATT_TILE, DIFF_WIDTH), BF16),
        scratch_shapes=_attention_scratch(n_tiles, 2 * HEAD_DIM),
        compiler_params=_params("arbitrary", "arbitrary"),
        name="diff_attention",
    )(bounded, *lam_vecs, sub_g_col, qk4, qk4, vt4, bias)


def _split3(v):
    hi = v.astype(BF16)
    r1 = v - hi.astype(F32)
    mid = r1.astype(BF16)
    lo = (r1 - mid.astype(F32)).astype(BF16)
    return hi, mid, lo


def _block_mask(gate, own):
    row = lax.broadcasted_iota(jnp.int32, gate.shape, 0)
    rank = jnp.zeros(gate.shape, jnp.int32)
    for m in range(own):
        gm = gate[m:m + 1, :]
        beats = (gm > gate) | ((gm == gate) & (row > m))
        rank = rank + jnp.where(beats, 1, 0)
    keep = ((rank < MOBA_TOPK) & (row < own)) | (row == own)
    return jnp.where(keep, 0.0, NEG_INF).astype(F32)


def _moba_kernel(bounded_ref, q_ref, k_ref, vt_ref, bias_ref, o_ref, kmean_ref, mask_ref, qm_ref, s_ref, p_ref,
                 acc_ref):
    n_tiles = q_ref.shape[0]
    for n in range(n_tiles):
        kmean_ref[n:n + 1, :] = jnp.mean(k_ref[n].astype(F32), axis=0, keepdims=True)
    parts = _split3(kmean_ref[...])
    for qi in range(n_tiles):
        qm_ref[qi, 0], qm_ref[qi, 1] = _half_lane_split(q_ref[qi])
        for half in range(2):
            gate = jnp.zeros((n_tiles, ATT_TILE), F32)
            for part in parts:
                gate = gate + lax.dot_general(part, qm_ref[qi, half], _NT, preferred_element_type=F32)
            mask_ref[qi, half] = _block_mask(gate, qi)

    def keep_row(c, qi, kv):
        if qi <= MOBA_TOPK or kv == qi:
            return None
        return jnp.where(mask_ref[qi, c, kv:kv + 1, :] < -1.0, 0.0, 1.0)

    chains = dict(
        n_chains=2,
        values_fn=lambda c, kv: _with_ones_rows(vt_ref[kv, c * HEAD_DIM:(c + 1) * HEAD_DIM, :]))
    bounded = bounded_ref[0] > 0

    @pl.when(bounded)
    def _():
        _bounded_tiles(
            n_tiles,
            logits_fn=lambda c, qi, kv: lax.dot_general(k_ref[kv], qm_ref[qi, c], _NT, preferred_element_type=F32),
            bias_fn=lambda c, d: bias_ref[c, d],
            keep_fn=keep_row,
            p_ref=p_ref, acc_ref=acc_ref, **chains)

    @pl.when(jnp.logical_not(bounded))
    def _():
        def q_tile(qi, carry):
            _flash_tiles(
                qi,
                logits_fn=lambda c, kv: lax.dot_general(k_ref[kv], qm_ref[qi, c], _NT, preferred_element_type=F32),
                bias_fn=lambda c, j: bias_ref[c, j],
                mask_fn=lambda c, kv: mask_ref[qi, c, pl.ds(kv, 1), :],
                s_ref=s_ref, p_ref=p_ref.at[0], acc_ref=acc_ref.at[qi], **chains)
            return carry
        lax.fori_loop(0, n_tiles, q_tile, 0)

    for qi in range(n_tiles):
        halves = [acc_ref[qi, half] for half in range(2)]
        o = jnp.concatenate([a[:HEAD_DIM] / a[HEAD_DIM:HEAD_DIM + 1] for a in halves], axis=0)
        o_ref[qi] = o.T.astype(BF16)


def _moba_attention(bounded, qk4, vt4, bias_pairs):
    batch, n_tiles = qk4.shape[0], qk4.shape[1]
    q_col0 = 2 * DIFF_WIDTH // LANES
    k_col0 = q_col0 + MOBA_WIDTH // LANES
    v_row0 = DIFF_WIDTH // LANES
    pair0 = HEADS_DIFF // 2
    return pl.pallas_call(
        _moba_kernel,
        grid=(HEADS_MOBA // 2, batch),
        in_specs=[
            pl.BlockSpec(memory_space=pltpu.SMEM),
            pl.BlockSpec((None, n_tiles, ATT_TILE, LANES), lambda h, b: (b, 0, 0, q_col0 + h)),
            pl.BlockSpec((None, n_tiles, ATT_TILE, LANES), lambda h, b: (b, 0, 0, k_col0 + h)),
            pl.BlockSpec((None, n_tiles, LANES, ATT_TILE), lambda h, b: (b, 0, v_row0 + h, 0)),
            pl.BlockSpec((None, 2, n_tiles, ATT_TILE, ATT_TILE), lambda h, b: (pair0 + h, 0, 0, 0, 0)),
        ],
        out_specs=pl.BlockSpec((None, n_tiles, ATT_TILE, LANES), lambda h, b: (b, 0, 0, h)),
        out_shape=jax.ShapeDtypeStruct((batch, n_tiles, ATT_TILE, MOBA_WIDTH), BF16),
        scratch_shapes=[
            pltpu.VMEM((n_tiles, LANES), F32),
            pltpu.VMEM((n_tiles, 2, n_tiles, ATT_TILE), F32),
        ] + _attention_scratch(n_tiles, HEAD_DIM),
        compiler_params=_params("arbitrary", "arbitrary"),
        name="moba_attention",
    )(bounded, qk4, qk4, vt4, bias_pairs)


def _load_token_rows(ref, n_rows, lead=()):
    chunks = [ref[lead + (pl.ds(c, n_rows, stride=ROW_TILE), slice(None))] for c in range(ROW_TILE)]
    return jnp.concatenate(chunks, axis=1)


def _store_token_rows(ref, value):
    n_rows = value.shape[0]
    for c in range(ROW_TILE):
        ref[pl.ds(c, n_rows, stride=ROW_TILE), :] = value[:, c * LANES:(c + 1) * LANES]


def _token_tile(ref, t):
    return ref.at[pl.ds(pl.multiple_of(t * ROW_TILE, ROW_TILE), ROW_TILE)]


def _first_argmax(v):
    top = jnp.max(v, axis=0, keepdims=True)
    row = lax.broadcasted_iota(jnp.int32, v.shape, 0)
    idx = jnp.min(jnp.where(v == top, row, v.shape[0]), axis=0, keepdims=True)
    return top, idx


def _out_proj_kernel(yd_ref, ym_ref, x_ref, wd_ref, wm_ref, g2_ref, wr_ref, tri_ref,
                     x1_ref, ri_ref, rf_ref, cnt_ref, run_ref):
    @pl.when(pl.program_id(0) == 0)
    def _():
        run_ref[...] = jnp.zeros_like(run_ref)

    x1 = (x_ref[...] + jnp.dot(yd_ref[...], wd_ref[...], preferred_element_type=F32)
          + jnp.dot(ym_ref[...], wm_ref[...], preferred_element_type=F32))
    _store_token_rows(x1_ref, x1)
    ms = jnp.mean(x1 * x1, axis=-1, keepdims=True)
    h2 = (x1 * lax.rsqrt(ms + RMS_EPS) * g2_ref[...]).astype(BF16)
    logits = lax.dot_general(wr_ref[...], h2, _NT, preferred_element_type=F32)

    g_logits = logits[N_EXPERTS:N_EXPERTS + N_GROUPS, :]
    g_top, g_idx = _first_argmax(g_logits)
    p_group = 1.0 / jnp.sum(jnp.exp(g_logits - g_top), axis=0, keepdims=True)
    e_logits = logits[0:EXPERTS_PER_GROUP, :]
    for g in range(1, N_GROUPS):
        e_logits = jnp.where(g_idx == g, logits[g * EXPERTS_PER_GROUP:(g + 1) * EXPERTS_PER_GROUP, :], e_logits)
    v1, i1 = _first_argmax(e_logits)
    row = lax.broadcasted_iota(jnp.int32, e_logits.shape, 0)
    v2, i2 = _first_argmax(jnp.where(row == i1, -jnp.inf, e_logits))
    t = jnp.exp(v2 - v1)
    w1 = p_group / (1.0 + t)
    w2 = p_group * t / (1.0 + t)
    e1 = g_idx * EXPERTS_PER_GROUP + i1
    e2 = g_idx * EXPERTS_PER_GROUP + i2

    erow = lax.broadcasted_iota(jnp.int32, (N_EXPERTS, OUT_ROWS), 0)
    hit1 = erow == e1
    hit2 = erow == e2
    onehot = jnp.where(hit1 | hit2, 1.0, 0.0).astype(F32)
    before = jnp.dot(onehot.astype(BF16), tri_ref[...], preferred_element_type=F32) + run_ref[...]
    r1 = jnp.sum(jnp.where(hit1, before, 0.0), axis=0, keepdims=True).astype(jnp.int32)
    r2 = jnp.sum(jnp.where(hit2, before, 0.0), axis=0, keepdims=True).astype(jnp.int32)
    run_ref[...] = run_ref[...] + jnp.sum(onehot, axis=1, keepdims=True)

    out_row = lax.broadcasted_iota(jnp.int32, ri_ref.shape, 0)
    ri_ref[...] = jnp.where(out_row == 0, e1, jnp.where(out_row == 1, e2,
                            jnp.where(out_row == 2, r1, jnp.where(out_row == 3, r2, 0))))
    rf_ref[...] = jnp.where(out_row == 0, w1, jnp.where(out_row == 1, w2, 0.0))
    cnt_ref[...] = jnp.broadcast_to(run_ref[...], cnt_ref.shape).astype(jnp.int32)


def _out_proj(y_d, y_m, x2, wo_d, wo_m, g2, w_router, tri):
    n = x2.shape[0]
    const = lambda i: (0, 0)
    return pl.pallas_call(
        _out_proj_kernel,
        grid=(n // OUT_ROWS,),
        in_specs=[
            pl.BlockSpec((OUT_ROWS, DIFF_WIDTH), lambda i: (i, 0)),
            pl.BlockSpec((OUT_ROWS, MOBA_WIDTH), lambda i: (i, 0)),
            pl.BlockSpec((OUT_ROWS, D_MODEL), lambda i: (i, 0)),
            pl.BlockSpec((DIFF_WIDTH, D_MODEL), const),
            pl.BlockSpec((MOBA_WIDTH, D_MODEL), const),
            pl.BlockSpec((1, D_MODEL), const),
            pl.BlockSpec((ROUTER_ROWS, D_MODEL), const),
            pl.BlockSpec((OUT_ROWS, OUT_ROWS), const),
        ],
        out_specs=[
            pl.BlockSpec((OUT_ROWS * ROW_TILE, LANES), lambda i: (i, 0)),
            pl.BlockSpec((8, OUT_ROWS), lambda i: (0, i)),
            pl.BlockSpec((8, OUT_ROWS), lambda i: (0, i)),
            pl.BlockSpec((N_EXPERTS, LANES), const),
        ],
        out_shape=[
            jax.ShapeDtypeStruct((n * ROW_TILE, LANES), F32),
            jax.ShapeDtypeStruct((8, n), jnp.int32),
            jax.ShapeDtypeStruct((8, n), F32),
            jax.ShapeDtypeStruct((N_EXPERTS, LANES), jnp.int32),
        ],
        scratch_shapes=[pltpu.VMEM((N_EXPERTS, 1), F32)],
        compiler_params=_params("arbitrary"),
        name="out_proj_route",
    )(y_d, y_m, x2, wo_d, wo_m, g2, w_router, tri)


def _dispatch_kernel(zchunk_ref, dest_ref, x1_hbm, xs_hbm, zero_ref, stage_ref, zero_sem, load_sem, row_sem):
    i = pl.program_id(0)
    last = pl.num_programs(0) - 1
    slot = lax.rem(i, 2)
    chunk_rows = MOE_CHUNK * ROW_TILE
    tile_rows = MOVE_ROWS * ROW_TILE

    def load(t, s):
        start = pl.multiple_of(t * tile_rows, tile_rows)
        return pltpu.make_async_copy(x1_hbm.at[pl.ds(start, tile_rows)], stage_ref.at[s], load_sem.at[s])

    def wait_rows(s):
        for _ in range(EXPERTS_PER_TOKEN):
            pltpu.make_async_copy(stage_ref.at[s], xs_hbm.at[pl.ds(0, tile_rows)], row_sem.at[s]).wait()

    def zero_copy(c):
        start = pl.multiple_of(c * chunk_rows, chunk_rows)
        return pltpu.make_async_copy(zero_ref, xs_hbm.at[pl.ds(start, chunk_rows)], zero_sem)

    @pl.when(i == 0)
    def _():
        zero_ref[...] = jnp.zeros_like(zero_ref)

        def start_one(c, carry):
            @pl.when(zchunk_ref[c] > 0)
            def _():
                zero_copy(c).start()
            return carry

        def wait_one(c, carry):
            @pl.when(zchunk_ref[c] > 0)
            def _():
                zero_copy(c).wait()
            return carry

        lax.fori_loop(0, zchunk_ref.shape[0], start_one, 0)
        lax.fori_loop(0, zchunk_ref.shape[0], wait_one, 0)
        load(0, 0).start()

    @pl.when(i > 0)
    def _():
        wait_rows(1 - slot)

    @pl.when(i < last)
    def _():
        load(i + 1, 1 - slot).start()

    load(i, slot).wait()

    def send_row(r, carry):
        src = _token_tile(stage_ref.at[slot], r)
        for k in range(EXPERTS_PER_TOKEN):
            pltpu.make_async_copy(src, _token_tile(xs_hbm, dest_ref[0, k, r]), row_sem.at[slot]).start()
        return carry

    lax.fori_loop(0, MOVE_ROWS, send_row, 0, unroll=8)

    @pl.when(i == last)
    def _():
        wait_rows(slot)


def _dispatch(zchunk, dest3, x1, n_slots):
    n = x1.shape[0] // ROW_TILE
    grid_spec = pltpu.PrefetchScalarGridSpec(
        num_scalar_prefetch=1,
        grid=(n // MOVE_ROWS,),
        in_specs=[
            pl.BlockSpec((1, EXPERTS_PER_TOKEN, MOVE_ROWS), lambda i, zc: (i, 0, 0), memory_space=pltpu.SMEM),
            pl.BlockSpec(memory_space=pl.ANY),
        ],
        out_specs=pl.BlockSpec(memory_space=pl.ANY),
        scratch_shapes=[
            pltpu.VMEM((MOE_CHUNK * ROW_TILE, LANES), F32),
            pltpu.VMEM((2, MOVE_ROWS * ROW_TILE, LANES), F32),
            pltpu.SemaphoreType.DMA(()),
            pltpu.SemaphoreType.DMA((2,)),
            pltpu.SemaphoreType.DMA((2,)),
        ],
    )
    return pl.pallas_call(
        _dispatch_kernel,
        grid_spec=grid_spec,
        out_shape=jax.ShapeDtypeStruct((n_slots * ROW_TILE, LANES), F32),
        compiler_params=_params("arbitrary"),
        name="moe_dispatch",
    )(zchunk, dest3, x1)


def _expert_kernel(ce_ref, na_ref, xs_ref, g2_ref, wg_ref, wu_ref, wd_ref, o_ref):
    active = pl.program_id(0) < na_ref[0]

    @pl.when(jnp.logical_not(active))
    def _():
        o_ref[...] = jnp.zeros_like(o_ref)

    @pl.when(active)
    def _():
        x = _load_token_rows(xs_ref, MOE_CHUNK)
        ms = jnp.mean(x * x, axis=-1, keepdims=True)
        h = (x * lax.rsqrt(ms + RMS_EPS) * g2_ref[...]).astype(BF16)
        gate = jnp.dot(h, wg_ref[...], preferred_element_type=F32)
        up = jnp.dot(h, wu_ref[...], preferred_element_type=F32)
        hid = (gate * jax.nn.sigmoid(gate) * up).astype(BF16)
        _store_token_rows(o_ref, jnp.dot(hid, wd_ref[...], preferred_element_type=F32))


def _experts(chunk_e, n_active, xs, g2, wg, wu, wd):
    n_slots = xs.shape[0] // ROW_TILE
    rows = lambda c, ce, na: (jnp.minimum(c, na[0] - 1), 0)
    grid_spec = pltpu.PrefetchScalarGridSpec(
        num_scalar_prefetch=2,
        grid=(n_slots // MOE_CHUNK,),
        in_specs=[
            pl.BlockSpec((MOE_CHUNK * ROW_TILE, LANES), rows),
            pl.BlockSpec((1, D_MODEL), lambda c, ce, na: (0, 0)),
            pl.BlockSpec((None, D_MODEL, EXPERT_HIDDEN), lambda c, ce, na: (ce[c], 0, 0)),
            pl.BlockSpec((None, D_MODEL, EXPERT_HIDDEN), lambda c, ce, na: (ce[c], 0, 0)),
            pl.BlockSpec((None, EXPERT_HIDDEN, D_MODEL), lambda c, ce, na: (ce[c], 0, 0)),
        ],
        out_specs=pl.BlockSpec((MOE_CHUNK * ROW_TILE, LANES), lambda c, ce, na: (c, 0)),
    )
    return pl.pallas_call(
        _expert_kernel,
        grid_spec=grid_spec,
        out_shape=jax.ShapeDtypeStruct((n_slots * ROW_TILE, LANES), F32),
        compiler_params=_params("arbitrary"),
        name="moe_experts",
    )(chunk_e, n_active, xs, g2, wg, wu, wd)


def _combine_kernel(dest_ref, dest_next_ref, w_ref, x1_ref, outs_hbm, y_ref, buf_ref, sem_ref):
    i = pl.program_id(0)
    slot = lax.rem(i, 2)

    def fetch(dref, s):
        def fetch_row(r, carry):
            for k in range(EXPERTS_PER_TOKEN):
                pltpu.make_async_copy(_token_tile(outs_hbm, dref[0, k, r]),
                                      _token_tile(buf_ref.at[s, k], r), sem_ref.at[s]).start()
            return carry
        lax.fori_loop(0, MOVE_ROWS, fetch_row, 0, unroll=8)

    @pl.when(i == 0)
    def _():
        fetch(dest_ref, 0)

    @pl.when(i + 1 < pl.num_programs(0))
    def _():
        fetch(dest_next_ref, 1 - slot)

    for k in range(EXPERTS_PER_TOKEN):
        pltpu.make_async_copy(outs_hbm.at[pl.ds(0, MOVE_ROWS * ROW_TILE)], buf_ref.at[slot, k],
                              sem_ref.at[slot]).wait()
    w = w_ref[...]
    y_ref[...] = (_load_token_rows(x1_ref, MOVE_ROWS)
                  + w[:, 0:1] * _load_token_rows(buf_ref, MOVE_ROWS, (slot, 0))
                  + w[:, 1:2] * _load_token_rows(buf_ref, MOVE_ROWS, (slot, 1)))


def _combine(dest3, wts, x1, outs):
    n = x1.shape[0] // ROW_TILE
    steps = n // MOVE_ROWS
    dest_block = (1, EXPERTS_PER_TOKEN, MOVE_ROWS)
    return pl.pallas_call(
        _combine_kernel,
        grid=(steps,),
        in_specs=[
            pl.BlockSpec(dest_block, lambda i: (i, 0, 0), memory_space=pltpu.SMEM),
            pl.BlockSpec(dest_block, lambda i: (jnp.minimum(i + 1, steps - 1), 0, 0), memory_space=pltpu.SMEM),
            pl.BlockSpec((MOVE_ROWS, EXPERTS_PER_TOKEN), lambda i: (i, 0)),
            pl.BlockSpec((MOVE_ROWS * ROW_TILE, LANES), lambda i: (i, 0)),
            pl.BlockSpec(memory_space=pl.ANY),
        ],
        out_specs=pl.BlockSpec((MOVE_ROWS, D_MODEL), lambda i: (i, 0)),
        out_shape=jax.ShapeDtypeStruct((n, D_MODEL), F32),
        scratch_shapes=[
            pltpu.VMEM((2, EXPERTS_PER_TOKEN, MOVE_ROWS * ROW_TILE, LANES), F32),
            pltpu.SemaphoreType.DMA((2,)),
        ],
        compiler_params=_params("arbitrary"),
        name="moe_combine",
    )(dest3, dest3, wts, x1, outs)


def _layer(x, norm1_g, w_in, diff_q_g, diff_k_g, lambda_q1, lambda_k1, lambda_q2, lambda_k2, diff_sub_g,
           moba_q_g, moba_k_g, rel_bias, w_out, norm2_g, router_group, router_expert, w_gate, w_up, w_down):
    batch, seq, _ = x.shape
    n = batch * seq
    n_tiles = seq // ATT_TILE
    scale = HEAD_DIM ** -0.5 * LOG2E

    d3 = 3 * DIFF_WIDTH
    w_perm = jnp.concatenate([w_in[:, :2 * DIFF_WIDTH], w_in[:, d3:d3 + 2 * MOBA_WIDTH],
                              w_in[:, 2 * DIFF_WIDTH:d3], w_in[:, d3 + 2 * MOBA_WIDTH:]], axis=1).astype(BF16)
    reps_d, reps_m = DIFF_WIDTH // HEAD_DIM, MOBA_WIDTH // HEAD_DIM
    post_gain = jnp.concatenate([jnp.tile(diff_q_g * scale, reps_d), jnp.tile(diff_k_g, reps_d),
                                 jnp.tile(moba_q_g * scale, reps_m), jnp.tile(moba_k_g, reps_m)])[None, :]
    head_of = np.arange(2 * LANES) // HEAD_DIM
    grp = jnp.asarray((head_of[:, None] == head_of[None, :]) / HEAD_DIM, BF16)

    bias = _bias_tiles(rel_bias, n_tiles)
    qk, vt4 = _in_proj(x.reshape(n, D_MODEL), norm1_g[None, :], w_perm, post_gain, grp, batch, seq)
    qk4 = qk.reshape(batch, n_tiles, ATT_TILE, QK_WIDTH)
    lam_vecs = [v[None, :] for v in (lambda_q1, lambda_k1, lambda_q2, lambda_k2)]
    y_d = _diff_attention(_logits_bounded(diff_q_g * scale, diff_k_g, rel_bias[:, :HEADS_DIFF]),
                          lam_vecs, diff_sub_g[:, None], qk4, vt4, bias)
    y_m = _moba_attention(_logits_bounded(moba_q_g * scale, moba_k_g, rel_bias[:, HEADS_DIFF:]),
                          qk4, vt4, bias.reshape(N_HEADS_TOTAL // 2, 2, n_tiles, ATT_TILE, ATT_TILE))

    w_router = jnp.concatenate([
        router_expert.transpose(0, 2, 1).reshape(N_EXPERTS, D_MODEL), router_group.T,
        jnp.zeros((ROUTER_ROWS - N_EXPERTS - N_GROUPS, D_MODEL), F32)], axis=0).astype(BF16)
    tri = jnp.asarray(np.triu(np.ones((OUT_ROWS, OUT_ROWS)), k=1), BF16)
    wo = w_out.astype(BF16)
    x1, route_i, route_f, cnt = _out_proj(y_d.reshape(n, DIFF_WIDTH), y_m.reshape(n, MOBA_WIDTH),
                                          x.reshape(n, D_MODEL), wo[:DIFF_WIDTH], wo[DIFF_WIDTH:],
                                          norm2_g[None, :], w_router, tri)

    counts = cnt[:, 0]
    pad_counts = ((counts + MOE_CHUNK - 1) // MOE_CHUNK) * MOE_CHUNK
    pad_end = jnp.cumsum(pad_counts)
    pad_start = pad_end - pad_counts
    experts = jnp.arange(N_EXPERTS, dtype=jnp.int32)
    start_of = jnp.sum(jnp.where(route_i[0:2, :, None] == experts, pad_start, 0), axis=-1)
    dest = start_of + route_i[2:4]
    dest3 = dest.reshape(EXPERTS_PER_TOKEN, n // MOVE_ROWS, MOVE_ROWS).transpose(1, 0, 2)
    n_slots = n * EXPERTS_PER_TOKEN + N_EXPERTS * MOE_CHUNK
    n_chunks = n_slots // MOE_CHUNK
    chunk_id = jnp.arange(n_chunks, dtype=jnp.int32)
    chunk_e = jnp.minimum(jnp.sum((pad_end[None, :] <= chunk_id[:, None] * MOE_CHUNK).astype(jnp.int32), axis=1),
                          N_EXPERTS - 1)
    n_active = (pad_end[-1:] // MOE_CHUNK).astype(jnp.int32)
    next_e = jnp.concatenate([chunk_e[1:], jnp.full((1,), N_EXPERTS, jnp.int32)])
    zchunk = ((chunk_id >= n_active - 1) | (chunk_e != next_e)).astype(jnp.int32)

    xs = _dispatch(zchunk, dest3, x1, n_slots)
    outs = _experts(chunk_e, n_active, xs, norm2_g[None, :], w_gate.astype(BF16), w_up.astype(BF16),
                    w_down.astype(BF16))
    y = _combine(dest3, route_f[0:2].T, x1, outs)
    return y.reshape(batch, seq, D_MODEL)


def kernel(x, norm1_g, w_in, diff_q_g, diff_k_g, lambda_q1, lambda_k1, lambda_q2, lambda_k2, diff_sub_g,
           moba_q_g, moba_k_g, rel_bias, w_out, norm2_g, router_group, router_expert, w_gate, w_up, w_down):
    assert x.shape[1] % PROJ_ROWS == 0 and x.shape[2] == D_MODEL and norm1_g.shape[0] == 1
    return _layer(x, norm1_g[0], w_in[0], diff_q_g[0], diff_k_g[0], lambda_q1[0], lambda_k1[0], lambda_q2[0],
                  lambda_k2[0], diff_sub_g[0], moba_q_g[0], moba_k_g[0], rel_bias, w_out[0], norm2_g[0],
                  router_group[0], router_expert[0], w_gate[0], w_up[0], w_down[0])
```

```python
import functools
import math

import numpy as np
import jax
import jax.numpy as jnp
from jax import lax
from jax.experimental import pallas as pl
from jax.experimental.pallas import tpu as pltpu

D_MODEL = 1024
HEAD_DIM = 64
HEADS_DIFF = 4
HEADS_MOBA = 8
N_HEADS_TOTAL = HEADS_DIFF + HEADS_MOBA
DIFF_WIDTH = HEADS_DIFF * 2 * HEAD_DIM
MOBA_WIDTH = HEADS_MOBA * HEAD_DIM
QK_WIDTH = 2 * DIFF_WIDTH + 2 * MOBA_WIDTH
V_WIDTH = DIFF_WIDTH + MOBA_WIDTH
MOBA_TOPK = 3
NUM_BUCKETS = 32
MAX_DISTANCE = 2048
N_GROUPS = 4
EXPERTS_PER_GROUP = 8
N_EXPERTS = N_GROUPS * EXPERTS_PER_GROUP
EXPERTS_PER_TOKEN = 2
EXPERT_HIDDEN = 256
RMS_EPS = 1e-6
NEG_INF = -1e30
LAMBDA_INIT = 0.8 - 0.6 * math.exp(-0.3 * 0)

LANES = 128
ROW_TILE = D_MODEL // LANES
ATT_TILE = 256
ONES_ROWS = 16
LOGIT_LIMIT = 80.0
ROUNDING_MARGIN = 1.05
LOG2E = math.log2(math.e)
PROJ_ROWS = 512
OUT_ROWS = 512
ROUTE_TILE = 256
MOE_CHUNK = 512
EXPERT_ROW_PARTS = 4
MOVE_ROWS = 256
ROUTER_ROWS = 40
VMEM_LIMIT = 48 * 1024 * 1024

F32 = jnp.float32
BF16 = jnp.bfloat16
_NT = (((1,), (1,)), ((), ()))


def _t5_thresholds():
    n = np.arange(0, 1 << 16)
    max_exact = NUM_BUCKETS // 2
    nf = np.maximum(n, 1).astype(np.float32)
    large = max_exact + (np.log(nf / np.float32(max_exact)) / np.float32(math.log(MAX_DISTANCE / max_exact))
                         * np.float32(NUM_BUCKETS - max_exact)).astype(np.int32)
    bucket = np.where(n < max_exact, n, np.minimum(large, NUM_BUCKETS - 1))
    return [int(np.searchsorted(bucket, b, side="left")) for b in range(1, NUM_BUCKETS)]


_T5_THRESHOLDS = _t5_thresholds()


def _params(*sem):
    return pltpu.CompilerParams(dimension_semantics=sem, vmem_limit_bytes=VMEM_LIMIT)


def _bias_kernel(tab_ref, out_ref):
    h = pl.program_id(0)
    d = pl.program_id(1)
    kj = lax.broadcasted_iota(jnp.int32, (ATT_TILE, ATT_TILE), 0)
    qi = lax.broadcasted_iota(jnp.int32, (ATT_TILE, ATT_TILE), 1)
    dist = d * ATT_TILE + qi - kj
    val = jnp.full((ATT_TILE, ATT_TILE), tab_ref[0, h], F32)
    for b, thr in enumerate(_T5_THRESHOLDS, start=1):
        val = jnp.where(dist >= thr, tab_ref[b, h], val)
    out_ref[...] = jnp.where(dist < 0, NEG_INF, val * LOG2E)


def _bias_tiles(rel_bias, n_diag):
    return pl.pallas_call(
        _bias_kernel,
        grid=(N_HEADS_TOTAL, n_diag),
        in_specs=[pl.BlockSpec(memory_space=pltpu.SMEM)],
        out_specs=pl.BlockSpec((None, None, ATT_TILE, ATT_TILE), lambda h, d: (h, d, 0, 0)),
        out_shape=jax.ShapeDtypeStruct((N_HEADS_TOTAL, n_diag, ATT_TILE, ATT_TILE), F32),
        compiler_params=_params("arbitrary", "arbitrary"),
        name="bias_tiles",
    )(rel_bias)


def _in_proj_kernel(x_ref, g1_ref, w_ref, pg_ref, grp_ref, qk_ref, vt_ref):
    n_row_tiles = PROJ_ROWS // ATT_TILE
    h = []
    for t in range(n_row_tiles):
        x = x_ref[t * ATT_TILE:(t + 1) * ATT_TILE, :]
        ms = jnp.mean(x * x, axis=-1, keepdims=True)
        h.append((x * lax.rsqrt(ms + RMS_EPS) * g1_ref[...]).astype(BF16))

    cw = 2 * LANES

    def finish(c, t, p):
        rows = slice(t * ATT_TILE, (t + 1) * ATT_TILE)
        cols = slice(c * cw, (c + 1) * cw)
        if c < QK_WIDTH // cw:
            msq = jnp.dot((p * p).astype(BF16), grp_ref[...], preferred_element_type=F32)
            qk_ref[rows, cols] = (p * lax.rsqrt(msq + RMS_EPS) * pg_ref[:, cols]).astype(BF16)
        else:
            vt_ref[t, c * cw - QK_WIDTH:(c + 1) * cw - QK_WIDTH, :] = p.T.astype(BF16)

    units = [(c, t) for c in range((QK_WIDTH + V_WIDTH) // cw) for t in range(n_row_tiles)]
    pending = None
    for c, t in units:
        p = jnp.dot(h[t], w_ref[:, c * cw:(c + 1) * cw], preferred_element_type=F32)
        if pending is not None:
            finish(*pending)
        pending = (c, t, p)
    finish(*pending)


def _in_proj(x2, g1, w_perm, post_gain, grp, batch, seq):
    n = x2.shape[0]
    steps_per_seq = seq // PROJ_ROWS
    tiles_per_step = PROJ_ROWS // ATT_TILE
    return pl.pallas_call(
        _in_proj_kernel,
        grid=(n // PROJ_ROWS,),
        in_specs=[
            pl.BlockSpec((PROJ_ROWS, D_MODEL), lambda i: (i, 0)),
            pl.BlockSpec((1, D_MODEL), lambda i: (0, 0)),
            pl.BlockSpec((D_MODEL, QK_WIDTH + V_WIDTH), lambda i: (0, 0)),
            pl.BlockSpec((1, QK_WIDTH), lambda i: (0, 0)),
            pl.BlockSpec((2 * LANES, 2 * LANES), lambda i: (0, 0)),
        ],
        out_specs=[
            pl.BlockSpec((PROJ_ROWS, QK_WIDTH), lambda i: (i, 0)),
            pl.BlockSpec((None, tiles_per_step, V_WIDTH, ATT_TILE),
                         lambda i: (i // steps_per_seq, i % steps_per_seq, 0, 0)),
        ],
        out_shape=[
            jax.ShapeDtypeStruct((n, QK_WIDTH), BF16),
            jax.ShapeDtypeStruct((batch, seq // ATT_TILE, V_WIDTH, ATT_TILE), BF16),
        ],
        compiler_params=_params("arbitrary"),
        name="in_proj",
    )(x2, g1, w_perm, post_gain, grp)


def _flash_tiles(qi, n_chains, logits_fn, values_fn, bias_fn, mask_fn, s_ref, p_ref, acc_ref):
    for c in range(n_chains):
        s_ref[0, c] = logits_fn(c, qi)
    p_ref[...] = jnp.zeros_like(p_ref)
    acc_ref[...] = jnp.zeros_like(acc_ref)

    def add_values(c, kv, alpha):
        acc_ref[c] = acc_ref[c] * alpha + jnp.dot(values_fn(c, kv), p_ref[c], preferred_element_type=F32)

    def step(j, carry):
        cur = lax.rem(j, 2)
        kv = qi - j
        out = []
        for c in range(n_chains):
            add_values(c, jnp.minimum(kv + 1, qi), carry[2 * c + 1])
        for c in range(n_chains):
            s = s_ref[cur, c] + bias_fn(c, j)
            tile_max = jnp.max(s, axis=0, keepdims=True)
            shift = mask_fn(c, kv)
            if shift is not None:
                tile_max = tile_max + shift
            m_new = jnp.maximum(carry[2 * c], tile_max)
            alpha = jnp.exp2(carry[2 * c] - m_new)
            p_ref[c] = jnp.exp2(s - (m_new if shift is None else m_new - shift)).astype(BF16)
            out += [m_new, alpha]
        for c in range(n_chains):
            s_ref[1 - cur, c] = logits_fn(c, jnp.maximum(kv - 1, 0))
        return tuple(out)

    m0 = jnp.full((1, ATT_TILE), NEG_INF, F32)
    one = jnp.ones((1, ATT_TILE), F32)
    carry = lax.fori_loop(0, qi + 1, step, (m0, one) * n_chains)
    for c in range(n_chains):
        add_values(c, 0, carry[2 * c + 1])


def _bounded_tiles(n_tiles, n_chains, logits_fn, values_fn, bias_fn, keep_fn, p_ref, acc_ref):
    for d in range(n_tiles):
        for qi in range(d, n_tiles):
            for c in range(n_chains):
                p_ref[qi, c] = jnp.exp2(logits_fn(c, qi, qi - d) + bias_fn(c, d)).astype(BF16)
        for qi in range(d, n_tiles):
            kv = qi - d
            for c in range(n_chains):
                pv = jnp.dot(values_fn(c, kv), p_ref[qi, c], preferred_element_type=F32)
                keep = keep_fn(c, qi, kv)
                if keep is not None:
                    pv = pv * keep
                acc_ref[qi, c] = pv if d == 0 else acc_ref[qi, c] + pv


def _logits_bounded(q_gain, k_gain, bias_cols):
    bound = (HEAD_DIM * jnp.max(jnp.abs(q_gain)) * jnp.max(jnp.abs(k_gain)) * ROUNDING_MARGIN
             + jnp.max(jnp.abs(bias_cols)) * LOG2E)
    return (bound <= LOGIT_LIMIT).astype(jnp.int32).reshape(1)


def _with_ones_rows(vt):
    return jnp.concatenate([vt, jnp.ones((ONES_ROWS, vt.shape[1]), vt.dtype)], axis=0)


def _attention_scratch(n_tiles, value_rows, n_chains=2):
    return [pltpu.VMEM((n_tiles, n_chains, ATT_TILE, LANES), BF16),
            pltpu.VMEM((2, n_chains, ATT_TILE, ATT_TILE), F32),
            pltpu.VMEM((n_tiles, n_chains, ATT_TILE, ATT_TILE), BF16),
            pltpu.VMEM((n_tiles, n_chains, value_rows + ONES_ROWS, ATT_TILE), F32)]


def _half_lane_split(q):
    lane = lax.broadcasted_iota(jnp.int32, q.shape, 1)
    zero = jnp.zeros_like(q)
    return jnp.where(lane < HEAD_DIM, q, zero), jnp.where(lane >= HEAD_DIM, q, zero)


def _diff_kernel(bounded_ref, lq1_ref, lk1_ref, lq2_ref, lk2_ref, subg_ref, q_ref, k_ref, vt_ref, bias_ref, o_ref,
                 qm_ref, s_ref, p_ref, acc_ref):
    n_tiles = q_ref.shape[0]
    for qi in range(n_tiles):
        qm_ref[qi, 0], qm_ref[qi, 1] = _half_lane_split(q_ref[qi])
    chains = dict(n_chains=2, values_fn=lambda c, kv: _with_ones_rows(vt_ref[kv]))
    bounded = bounded_ref[0] > 0

    @pl.when(bounded)
    def _():
        _bounded_tiles(
            n_tiles,
            logits_fn=lambda c, qi, kv: lax.dot_general(k_ref[kv], qm_ref[qi, c], _NT, preferred_element_type=F32),
            bias_fn=lambda c, d: bias_ref[d],
            keep_fn=lambda c, qi, kv: None,
            p_ref=p_ref, acc_ref=acc_ref, **chains)

    @pl.when(jnp.logical_not(bounded))
    def _():
        def q_tile(qi, carry):
            _flash_tiles(
                qi,
                logits_fn=lambda c, kv: lax.dot_general(k_ref[kv], qm_ref[qi, c], _NT, preferred_element_type=F32),
                bias_fn=lambda c, j: bias_ref[j],
                mask_fn=lambda c, kv: None,
                s_ref=s_ref, p_ref=p_ref.at[0], acc_ref=acc_ref.at[qi], **chains)
            return carry
        lax.fori_loop(0, n_tiles, q_tile, 0)

    lam = (jnp.exp(jnp.sum(lq1_ref[...] * lk1_ref[...], keepdims=True))
           - jnp.exp(jnp.sum(lq2_ref[...] * lk2_ref[...], keepdims=True)) + LAMBDA_INIT)
    width = 2 * HEAD_DIM
    for qi in range(n_tiles):
        a1, a2 = acc_ref[qi, 0], acc_ref[qi, 1]
        o = a1[:width] / a1[width:width + 1] - lam * (a2[:width] / a2[width:width + 1])
        ms = jnp.mean(o * o, axis=0, keepdims=True)
        o = o * lax.rsqrt(ms + RMS_EPS) * subg_ref[...] * (1.0 - LAMBDA_INIT)
        o_ref[qi] = o.T.astype(BF16)


def _diff_attention(bounded, lam_vecs, sub_g_col, qk4, vt4, bias):
    batch, n_tiles = qk4.shape[0], qk4.shape[1]
    k_col0 = DIFF_WIDTH // LANES
    vec = pl.BlockSpec((1, HEAD_DIM), lambda h, b: (0, 0))
    return pl.pallas_call(
        _diff_kernel,
        grid=(HEADS_DIFF, batch),
        in_specs=[
            pl.BlockSpec(memory_space=pltpu.SMEM),
            vec, vec, vec, vec,
            pl.BlockSpec((2 * HEAD_DIM, 1), lambda h, b: (0, 0)),
            pl.BlockSpec((None, n_tiles, ATT_TILE, LANES), lambda h, b: (b, 0, 0, h)),
            pl.BlockSpec((None, n_tiles, ATT_TILE, LANES), lambda h, b: (b, 0, 0, k_col0 + h)),
            pl.BlockSpec((None, n_tiles, LANES, ATT_TILE), lambda h, b: (b, 0, h, 0)),
            pl.BlockSpec((None, n_tiles, ATT_TILE, ATT_TILE), lambda h, b: (h, 0, 0, 0)),
        ],
        out_specs=pl.BlockSpec((None, n_tiles, ATT_TILE, LANES), lambda h, b: (b, 0, 0, h)),
        out_shape=jax.ShapeDtypeStruct((batch, n_tiles, ATT_TILE, DIFF_WIDTH), BF16),
        scratch_shapes=_attention_scratch(n_tiles, 2 * HEAD_DIM),
        compiler_params=_params("arbitrary", "arbitrary"),
        name="diff_attention",
    )(bounded, *lam_vecs, sub_g_col, qk4, qk4, vt4, bias)


def _split3(v):
    hi = v.astype(BF16)
    r1 = v - hi.astype(F32)
    mid = r1.astype(BF16)
    lo = (r1 - mid.astype(F32)).astype(BF16)
    return hi, mid, lo


def _block_mask(gate, own):
    row = lax.broadcasted_iota(jnp.int32, gate.shape, 0)
    rank = jnp.zeros(gate.shape, jnp.int32)
    for m in range(own):
        gm = gate[m:m + 1, :]
        beats = (gm > gate) | ((gm == gate) & (row > m))
        rank = rank + jnp.where(beats, 1, 0)
    keep = ((rank < MOBA_TOPK) & (row < own)) | (row == own)
    return jnp.where(keep, 0.0, NEG_INF).astype(F32)


def _moba_kernel(bounded_ref, q_ref, k_ref, vt_ref, bias_ref, o_ref, kmean_ref, mask_ref, qm_ref, s_ref, p_ref,
                 acc_ref):
    n_tiles = q_ref.shape[0]
    for n in range(n_tiles):
        kmean_ref[n:n + 1, :] = jnp.mean(k_ref[n].astype(F32), axis=0, keepdims=True)
    parts = _split3(kmean_ref[...])
    for qi in range(n_tiles):
        qm_ref[qi, 0], qm_ref[qi, 1] = _half_lane_split(q_ref[qi])
        for half in range(2):
            gate = jnp.zeros((n_tiles, ATT_TILE), F32)
            for part in parts:
                gate = gate + lax.dot_general(part, qm_ref[qi, half], _NT, preferred_element_type=F32)
            mask_ref[qi, half] = _block_mask(gate, qi)

    def keep_row(c, qi, kv):
        if qi <= MOBA_TOPK or kv == qi:
            return None
        return jnp.where(mask_ref[qi, c, kv:kv + 1, :] < -1.0, 0.0, 1.0)

    chains = dict(
        n_chains=2,
        values_fn=lambda c, kv: _with_ones_rows(vt_ref[kv, c * HEAD_DIM:(c + 1) * HEAD_DIM, :]))
    bounded = bounded_ref[0] > 0

    @pl.when(bounded)
    def _():
        _bounded_tiles(
            n_tiles,
            logits_fn=lambda c, qi, kv: lax.dot_general(k_ref[kv], qm_ref[qi, c], _NT, preferred_element_type=F32),
            bias_fn=lambda c, d: bias_ref[c, d],
            keep_fn=keep_row,
            p_ref=p_ref, acc_ref=acc_ref, **chains)

    @pl.when(jnp.logical_not(bounded))
    def _():
        def q_tile(qi, carry):
            _flash_tiles(
                qi,
                logits_fn=lambda c, kv: lax.dot_general(k_ref[kv], qm_ref[qi, c], _NT, preferred_element_type=F32),
                bias_fn=lambda c, j: bias_ref[c, j],
                mask_fn=lambda c, kv: mask_ref[qi, c, pl.ds(kv, 1), :],
                s_ref=s_ref, p_ref=p_ref.at[0], acc_ref=acc_ref.at[qi], **chains)
            return carry
        lax.fori_loop(0, n_tiles, q_tile, 0)

    for qi in range(n_tiles):
        halves = [acc_ref[qi, half] for half in range(2)]
        o = jnp.concatenate([a[:HEAD_DIM] / a[HEAD_DIM:HEAD_DIM + 1] for a in halves], axis=0)
        o_ref[qi] = o.T.astype(BF16)


def _moba_attention(bounded, qk4, vt4, bias_pairs):
    batch, n_tiles = qk4.shape[0], qk4.shape[1]
    q_col0 = 2 * DIFF_WIDTH // LANES
    k_col0 = q_col0 + MOBA_WIDTH // LANES
    v_row0 = DIFF_WIDTH // LANES
    pair0 = HEADS_DIFF // 2
    return pl.pallas_call(
        _moba_kernel,
        grid=(HEADS_MOBA // 2, batch),
        in_specs=[
            pl.BlockSpec(memory_space=pltpu.SMEM),
            pl.BlockSpec((None, n_tiles, ATT_TILE, LANES), lambda h, b: (b, 0, 0, q_col0 + h)),
            pl.BlockSpec((None, n_tiles, ATT_TILE, LANES), lambda h, b: (b, 0, 0, k_col0 + h)),
            pl.BlockSpec((None, n_tiles, LANES, ATT_TILE), lambda h, b: (b, 0, v_row0 + h, 0)),
            pl.BlockSpec((None, 2, n_tiles, ATT_TILE, ATT_TILE), lambda h, b: (pair0 + h, 0, 0, 0, 0)),
        ],
        out_specs=pl.BlockSpec((None, n_tiles, ATT_TILE, LANES), lambda h, b: (b, 0, 0, h)),
        out_shape=jax.ShapeDtypeStruct((batch, n_tiles, ATT_TILE, MOBA_WIDTH), BF16),
        scratch_shapes=[
            pltpu.VMEM((n_tiles, LANES), F32),
            pltpu.VMEM((n_tiles, 2, n_tiles, ATT_TILE), F32),
        ] + _attention_scratch(n_tiles, HEAD_DIM),
        compiler_params=_params("arbitrary", "arbitrary"),
        name="moba_attention",
    )(bounded, qk4, qk4, vt4, bias_pairs)


def _load_token_rows(ref, n_rows, lead=(), first=0):
    chunks = [ref[lead + (pl.ds(first * ROW_TILE + c, n_rows, stride=ROW_TILE), slice(None))]
              for c in range(ROW_TILE)]
    return jnp.concatenate(chunks, axis=1)


def _store_token_rows(ref, value, first=0):
    n_rows = value.shape[0]
    for c in range(ROW_TILE):
        ref[pl.ds(first * ROW_TILE + c, n_rows, stride=ROW_TILE), :] = value[:, c * LANES:(c + 1) * LANES]


def _token_tile(ref, t):
    return ref.at[pl.ds(pl.multiple_of(t * ROW_TILE, ROW_TILE), ROW_TILE)]


def _first_argmax(v):
    top = jnp.max(v, axis=0, keepdims=True)
    row = lax.broadcasted_iota(jnp.int32, v.shape, 0)
    idx = jnp.min(jnp.where(v == top, row, v.shape[0]), axis=0, keepdims=True)
    return top, idx


def _out_proj_kernel(yd_ref, ym_ref, x_ref, wd_ref, wm_ref, g2_ref, wr_ref, tri_ref,
                     x1_ref, ri_ref, rf_ref, cnt_ref, run_ref):
    @pl.when(pl.program_id(0) == 0)
    def _():
        run_ref[...] = jnp.zeros_like(run_ref)

    tiles = range(OUT_ROWS // ROUTE_TILE)
    h2 = []
    for t in tiles:
        rows = slice(t * ROUTE_TILE, (t + 1) * ROUTE_TILE)
        x1 = (x_ref[rows, :] + jnp.dot(yd_ref[rows, :], wd_ref[...], preferred_element_type=F32)
              + jnp.dot(ym_ref[rows, :], wm_ref[...], preferred_element_type=F32))
        _store_token_rows(x1_ref, x1, first=t * ROUTE_TILE)
        ms = jnp.mean(x1 * x1, axis=-1, keepdims=True)
        h2.append((x1 * lax.rsqrt(ms + RMS_EPS) * g2_ref[...]).astype(BF16))
    logits = [lax.dot_general(wr_ref[...], h, _NT, preferred_element_type=F32) for h in h2]

    picks = []
    for lg in logits:
        g_logits = lg[N_EXPERTS:N_EXPERTS + N_GROUPS, :]
        g_top, g_idx = _first_argmax(g_logits)
        p_group = 1.0 / jnp.sum(jnp.exp(g_logits - g_top), axis=0, keepdims=True)
        e_logits = lg[0:EXPERTS_PER_GROUP, :]
        for g in range(1, N_GROUPS):
            e_logits = jnp.where(g_idx == g, lg[g * EXPERTS_PER_GROUP:(g + 1) * EXPERTS_PER_GROUP, :], e_logits)
        v1, i1 = _first_argmax(e_logits)
        row = lax.broadcasted_iota(jnp.int32, e_logits.shape, 0)
        v2, i2 = _first_argmax(jnp.where(row == i1, -jnp.inf, e_logits))
        ratio = jnp.exp(v2 - v1)
        w1 = p_group / (1.0 + ratio)
        w2 = p_group * ratio / (1.0 + ratio)
        e1 = g_idx * EXPERTS_PER_GROUP + i1
        e2 = g_idx * EXPERTS_PER_GROUP + i2
        erow = lax.broadcasted_iota(jnp.int32, (N_EXPERTS, ROUTE_TILE), 0)
        hit1 = erow == e1
        hit2 = erow == e2
        onehot = jnp.where(hit1 | hit2, 1.0, 0.0).astype(F32)
        picks.append((e1, e2, w1, w2, hit1, hit2, onehot))

    prefix = [jnp.dot(p[-1].astype(BF16), tri_ref[...], preferred_element_type=F32) for p in picks]
    out_row = lax.broadcasted_iota(jnp.int32, (8, ROUTE_TILE), 0)
    for t in tiles:
        e1, e2, w1, w2, hit1, hit2, onehot = picks[t]
        lanes = slice(t * ROUTE_TILE, (t + 1) * ROUTE_TILE)
        before = prefix[t] + run_ref[...]
        r1 = jnp.sum(jnp.where(hit1, before, 0.0), axis=0, keepdims=True).astype(jnp.int32)
        r2 = jnp.sum(jnp.where(hit2, before, 0.0), axis=0, keepdims=True).astype(jnp.int32)
        run_ref[...] = run_ref[...] + jnp.sum(onehot, axis=1, keepdims=True)
        ri_ref[:, lanes] = jnp.where(out_row == 0, e1, jnp.where(out_row == 1, e2,
                                     jnp.where(out_row == 2, r1, jnp.where(out_row == 3, r2, 0))))
        rf_ref[:, lanes] = jnp.where(out_row == 0, w1, jnp.where(out_row == 1, w2, 0.0))
    cnt_ref[...] = jnp.broadcast_to(run_ref[...], cnt_ref.shape).astype(jnp.int32)


def _out_proj(y_d, y_m, x2, wo_d, wo_m, g2, w_router, tri):
    n = x2.shape[0]
    const = lambda i: (0, 0)
    return pl.pallas_call(
        _out_proj_kernel,
        grid=(n // OUT_ROWS,),
        in_specs=[
            pl.BlockSpec((OUT_ROWS, DIFF_WIDTH), lambda i: (i, 0)),
            pl.BlockSpec((OUT_ROWS, MOBA_WIDTH), lambda i: (i, 0)),
            pl.BlockSpec((OUT_ROWS, D_MODEL), lambda i: (i, 0)),
            pl.BlockSpec((DIFF_WIDTH, D_MODEL), const),
            pl.BlockSpec((MOBA_WIDTH, D_MODEL), const),
            pl.BlockSpec((1, D_MODEL), const),
            pl.BlockSpec((ROUTER_ROWS, D_MODEL), const),
            pl.BlockSpec((ROUTE_TILE, ROUTE_TILE), const),
        ],
        out_specs=[
            pl.BlockSpec((OUT_ROWS * ROW_TILE, LANES), lambda i: (i, 0)),
            pl.BlockSpec((8, OUT_ROWS), lambda i: (0, i)),
            pl.BlockSpec((8, OUT_ROWS), lambda i: (0, i)),
            pl.BlockSpec((N_EXPERTS, LANES), const),
        ],
        out_shape=[
            jax.ShapeDtypeStruct((n * ROW_TILE, LANES), F32),
            jax.ShapeDtypeStruct((8, n), jnp.int32),
            jax.ShapeDtypeStruct((8, n), F32),
            jax.ShapeDtypeStruct((N_EXPERTS, LANES), jnp.int32),
        ],
        scratch_shapes=[pltpu.VMEM((N_EXPERTS, 1), F32)],
        compiler_params=_params("arbitrary"),
        name="out_proj_route",
    )(y_d, y_m, x2, wo_d, wo_m, g2, w_router, tri)


def _dispatch_kernel(zchunk_ref, dest_ref, x1_hbm, xs_hbm, zero_ref, stage_ref, zero_sem, load_sem, row_sem):
    i = pl.program_id(0)
    last = pl.num_programs(0) - 1
    slot = lax.rem(i, 2)
    chunk_rows = MOE_CHUNK * ROW_TILE
    tile_rows = MOVE_ROWS * ROW_TILE

    def load(t, s):
        start = pl.multiple_of(t * tile_rows, tile_rows)
        return pltpu.make_async_copy(x1_hbm.at[pl.ds(start, tile_rows)], stage_ref.at[s], load_sem.at[s])

    def wait_rows(s):
        for _ in range(EXPERTS_PER_TOKEN):
            pltpu.make_async_copy(stage_ref.at[s], xs_hbm.at[pl.ds(0, tile_rows)], row_sem.at[s]).wait()

    def zero_copy(c):
        start = pl.multiple_of(c * chunk_rows, chunk_rows)
        return pltpu.make_async_copy(zero_ref, xs_hbm.at[pl.ds(start, chunk_rows)], zero_sem)

    @pl.when(i == 0)
    def _():
        zero_ref[...] = jnp.zeros_like(zero_ref)

        def start_one(c, carry):
            @pl.when(zchunk_ref[c] > 0)
            def _():
                zero_copy(c).start()
            return carry

        def wait_one(c, carry):
            @pl.when(zchunk_ref[c] > 0)
            def _():
                zero_copy(c).wait()
            return carry

        lax.fori_loop(0, zchunk_ref.shape[0], start_one, 0)
        lax.fori_loop(0, zchunk_ref.shape[0], wait_one, 0)
        load(0, 0).start()

    @pl.when(i > 0)
    def _():
        wait_rows(1 - slot)

    @pl.when(i < last)
    def _():
        load(i + 1, 1 - slot).start()

    load(i, slot).wait()

    def send_row(r, carry):
        src = _token_tile(stage_ref.at[slot], r)
        for k in range(EXPERTS_PER_TOKEN):
            pltpu.make_async_copy(src, _token_tile(xs_hbm, dest_ref[0, k, r]), row_sem.at[slot]).start()
        return carry

    lax.fori_loop(0, MOVE_ROWS, send_row, 0, unroll=8)

    @pl.when(i == last)
    def _():
        wait_rows(slot)


def _dispatch(zchunk, dest3, x1, n_slots):
    n = x1.shape[0] // ROW_TILE
    grid_spec = pltpu.PrefetchScalarGridSpec(
        num_scalar_prefetch=1,
        grid=(n // MOVE_ROWS,),
        in_specs=[
            pl.BlockSpec((1, EXPERTS_PER_TOKEN, MOVE_ROWS), lambda i, zc: (i, 0, 0), memory_space=pltpu.SMEM),
            pl.BlockSpec(memory_space=pl.ANY),
        ],
        out_specs=pl.BlockSpec(memory_space=pl.ANY),
        scratch_shapes=[
            pltpu.VMEM((MOE_CHUNK * ROW_TILE, LANES), F32),
            pltpu.VMEM((2, MOVE_ROWS * ROW_TILE, LANES), F32),
            pltpu.SemaphoreType.DMA(()),
            pltpu.SemaphoreType.DMA((2,)),
            pltpu.SemaphoreType.DMA((2,)),
        ],
    )
    return pl.pallas_call(
        _dispatch_kernel,
        grid_spec=grid_spec,
        out_shape=jax.ShapeDtypeStruct((n_slots * ROW_TILE, LANES), F32),
        compiler_params=_params("arbitrary"),
        name="moe_dispatch",
    )(zchunk, dest3, x1)


def _expert_kernel(ce_ref, na_ref, xs_ref, g2_ref, wg_ref, wu_ref, wd_ref, o_ref):
    active = pl.program_id(0) < na_ref[0]

    @pl.when(jnp.logical_not(active))
    def _():
        o_ref[...] = jnp.zeros_like(o_ref)

    @pl.when(active)
    def _():
        part_rows = MOE_CHUNK // EXPERT_ROW_PARTS

        def gate_up(r):
            x = _load_token_rows(xs_ref, part_rows, first=r * part_rows)
            ms = jnp.mean(x * x, axis=-1, keepdims=True)
            h = (x * lax.rsqrt(ms + RMS_EPS) * g2_ref[...]).astype(BF16)
            return (jnp.dot(h, wg_ref[...], preferred_element_type=F32),
                    jnp.dot(h, wu_ref[...], preferred_element_type=F32))

        def down(r, gate, up):
            hid = (gate * jax.nn.sigmoid(gate) * up).astype(BF16)
            _store_token_rows(o_ref, jnp.dot(hid, wd_ref[...], preferred_element_type=F32), first=r * part_rows)

        pending = None
        for r in range(EXPERT_ROW_PARTS):
            current = (r,) + gate_up(r)
            if pending is not None:
                down(*pending)
            pending = current
        down(*pending)


def _experts(chunk_e, n_active, xs, g2, wg, wu, wd):
    n_slots = xs.shape[0] // ROW_TILE
    rows = lambda c, ce, na: (jnp.minimum(c, na[0] - 1), 0)
    grid_spec = pltpu.PrefetchScalarGridSpec(
        num_scalar_prefetch=2,
        grid=(n_slots // MOE_CHUNK,),
        in_specs=[
            pl.BlockSpec((MOE_CHUNK * ROW_TILE, LANES), rows),
            pl.BlockSpec((1, D_MODEL), lambda c, ce, na: (0, 0)),
            pl.BlockSpec((None, D_MODEL, EXPERT_HIDDEN), lambda c, ce, na: (ce[c], 0, 0)),
            pl.BlockSpec((None, D_MODEL, EXPERT_HIDDEN), lambda c, ce, na: (ce[c], 0, 0)),
            pl.BlockSpec((None, EXPERT_HIDDEN, D_MODEL), lambda c, ce, na: (ce[c], 0, 0)),
        ],
        out_specs=pl.BlockSpec((MOE_CHUNK * ROW_TILE, LANES), lambda c, ce, na: (c, 0)),
    )
    return pl.pallas_call(
        _expert_kernel,
        grid_spec=grid_spec,
        out_shape=jax.ShapeDtypeStruct((n_slots * ROW_TILE, LANES), F32),
        compiler_params=_params("arbitrary"),
        name="moe_experts",
    )(chunk_e, n_active, xs, g2, wg, wu, wd)


def _combine_kernel(dest_ref, dest_next_ref, w_ref, x1_ref, outs_hbm, y_ref, buf_ref, sem_ref):
    i = pl.program_id(0)
    slot = lax.rem(i, 2)

    def fetch(dref, s):
        def fetch_row(r, carry):
            for k in range(EXPERTS_PER_TOKEN):
                pltpu.make_async_copy(_token_tile(outs_hbm, dref[0, k, r]),
                                      _token_tile(buf_ref.at[s, k], r), sem_ref.at[s]).start()
            return carry
        lax.fori_loop(0, MOVE_ROWS, fetch_row, 0, unroll=8)

    @pl.when(i == 0)
    def _():
        fetch(dest_ref, 0)

    @pl.when(i + 1 < pl.num_programs(0))
    def _():
        fetch(dest_next_ref, 1 - slot)

    for k in range(EXPERTS_PER_TOKEN):
        pltpu.make_async_copy(outs_hbm.at[pl.ds(0, MOVE_ROWS * ROW_TILE)], buf_ref.at[slot, k],
                              sem_ref.at[slot]).wait()
    w = w_ref[...]
    y_ref[...] = (_load_token_rows(x1_ref, MOVE_ROWS)
                  + w[:, 0:1] * _load_token_rows(buf_ref, MOVE_ROWS, (slot, 0))
                  + w[:, 1:2] * _load_token_rows(buf_ref, MOVE_ROWS, (slot, 1)))


def _combine(dest3, wts, x1, outs):
    n = x1.shape[0] // ROW_TILE
    steps = n // MOVE_ROWS
    dest_block = (1, EXPERTS_PER_TOKEN, MOVE_ROWS)
    return pl.pallas_call(
        _combine_kernel,
        grid=(steps,),
        in_specs=[
            pl.BlockSpec(dest_block, lambda i: (i, 0, 0), memory_space=pltpu.SMEM),
            pl.BlockSpec(dest_block, lambda i: (jnp.minimum(i + 1, steps - 1), 0, 0), memory_space=pltpu.SMEM),
            pl.BlockSpec((MOVE_ROWS, EXPERTS_PER_TOKEN), lambda i: (i, 0)),
            pl.BlockSpec((MOVE_ROWS * ROW_TILE, LANES), lambda i: (i, 0)),
            pl.BlockSpec(memory_space=pl.ANY),
        ],
        out_specs=pl.BlockSpec((MOVE_ROWS, D_MODEL), lambda i: (i, 0)),
        out_shape=jax.ShapeDtypeStruct((n, D_MODEL), F32),
        scratch_shapes=[
            pltpu.VMEM((2, EXPERTS_PER_TOKEN, MOVE_ROWS * ROW_TILE, LANES), F32),
            pltpu.SemaphoreType.DMA((2,)),
        ],
        compiler_params=_params("arbitrary"),
        name="moe_combine",
    )(dest3, dest3, wts, x1, outs)


def _layer(x, norm1_g, w_in, diff_q_g, diff_k_g, lambda_q1, lambda_k1, lambda_q2, lambda_k2, diff_sub_g,
           moba_q_g, moba_k_g, rel_bias, w_out, norm2_g, router_group, router_expert, w_gate, w_up, w_down):
    batch, seq, _ = x.shape
    n = batch * seq
    n_tiles = seq // ATT_TILE
    scale = HEAD_DIM ** -0.5 * LOG2E

    d3 = 3 * DIFF_WIDTH
    w_perm = jnp.concatenate([w_in[:, :2 * DIFF_WIDTH], w_in[:, d3:d3 + 2 * MOBA_WIDTH],
                              w_in[:, 2 * DIFF_WIDTH:d3], w_in[:, d3 + 2 * MOBA_WIDTH:]], axis=1).astype(BF16)
    reps_d, reps_m = DIFF_WIDTH // HEAD_DIM, MOBA_WIDTH // HEAD_DIM
    post_gain = jnp.concatenate([jnp.tile(diff_q_g * scale, reps_d), jnp.tile(diff_k_g, reps_d),
                                 jnp.tile(moba_q_g * scale, reps_m), jnp.tile(moba_k_g, reps_m)])[None, :]
    head_of = np.arange(2 * LANES) // HEAD_DIM
    grp = jnp.asarray((head_of[:, None] == head_of[None, :]) / HEAD_DIM, BF16)

    bias = _bias_tiles(rel_bias, n_tiles)
    qk, vt4 = _in_proj(x.reshape(n, D_MODEL), norm1_g[None, :], w_perm, post_gain, grp, batch, seq)
    qk4 = qk.reshape(batch, n_tiles, ATT_TILE, QK_WIDTH)
    lam_vecs = [v[None, :] for v in (lambda_q1, lambda_k1, lambda_q2, lambda_k2)]
    y_d = _diff_attention(_logits_bounded(diff_q_g * scale, diff_k_g, rel_bias[:, :HEADS_DIFF]),
                          lam_vecs, diff_sub_g[:, None], qk4, vt4, bias)
    y_m = _moba_attention(_logits_bounded(moba_q_g * scale, moba_k_g, rel_bias[:, HEADS_DIFF:]),
                          qk4, vt4, bias.reshape(N_HEADS_TOTAL // 2, 2, n_tiles, ATT_TILE, ATT_TILE))

    w_router = jnp.concatenate([
        router_expert.transpose(0, 2, 1).reshape(N_EXPERTS, D_MODEL), router_group.T,
        jnp.zeros((ROUTER_ROWS - N_EXPERTS - N_GROUPS, D_MODEL), F32)], axis=0).astype(BF16)
    tri = jnp.asarray(np.triu(np.ones((ROUTE_TILE, ROUTE_TILE)), k=1), BF16)
    wo = w_out.astype(BF16)
    x1, route_i, route_f, cnt = _out_proj(y_d.reshape(n, DIFF_WIDTH), y_m.reshape(n, MOBA_WIDTH),
                                          x.reshape(n, D_MODEL), wo[:DIFF_WIDTH], wo[DIFF_WIDTH:],
                                          norm2_g[None, :], w_router, tri)

    counts = cnt[:, 0]
    pad_counts = ((counts + MOE_CHUNK - 1) // MOE_CHUNK) * MOE_CHUNK
    pad_end = jnp.cumsum(pad_counts)
    pad_start = pad_end - pad_counts
    experts = jnp.arange(N_EXPERTS, dtype=jnp.int32)
    start_of = jnp.sum(jnp.where(route_i[0:2, :, None] == experts, pad_start, 0), axis=-1)
    dest = start_of + route_i[2:4]
    dest3 = dest.reshape(EXPERTS_PER_TOKEN, n // MOVE_ROWS, MOVE_ROWS).transpose(1, 0, 2)
    n_slots = n * EXPERTS_PER_TOKEN + N_EXPERTS * MOE_CHUNK
    n_chunks = n_slots // MOE_CHUNK
    chunk_id = jnp.arange(n_chunks, dtype=jnp.int32)
    chunk_e = jnp.minimum(jnp.sum((pad_end[None, :] <= chunk_id[:, None] * MOE_CHUNK).astype(jnp.int32), axis=1),
                          N_EXPERTS - 1)
    n_active = (pad_end[-1:] // MOE_CHUNK).astype(jnp.int32)
    next_e = jnp.concatenate([chunk_e[1:], jnp.full((1,), N_EXPERTS, jnp.int32)])
    zchunk = ((chunk_id >= n_active - 1) | (chunk_e != next_e)).astype(jnp.int32)

    xs = _dispatch(zchunk, dest3, x1, n_slots)
    outs = _experts(chunk_e, n_active, xs, norm2_g[None, :], w_gate.astype(BF16), w_up.astype(BF16),
                    w_down.astype(BF16))
    y = _combine(dest3, route_f[0:2].T, x1, outs)
    return y.reshape(batch, seq, D_MODEL)


def kernel(x, norm1_g, w_in, diff_q_g, diff_k_g, lambda_q1, lambda_k1, lambda_q2, lambda_k2, diff_sub_g,
           moba_q_g, moba_k_g, rel_bias, w_out, norm2_g, router_group, router_expert, w_gate, w_up, w_down):
    assert x.shape[1] % PROJ_ROWS == 0 and x.shape[2] == D_MODEL and norm1_g.shape[0] == 1
    return _layer(x, norm1_g[0], w_in[0], diff_q_g[0], diff_k_g[0], lambda_q1[0], lambda_k1[0], lambda_q2[0],
                  lambda_k2[0], diff_sub_g[0], moba_q_g[0], moba_k_g[0], rel_bias, w_out[0], norm2_g[0],
                  router_group[0], router_expert[0], w_gate[0], w_up[0], w_down[0])
```

```python
import functools
import math

import numpy as np
import jax
import jax.numpy as jnp
from jax import lax
from jax.experimental import pallas as pl
from jax.experimental.pallas import tpu as pltpu

D_MODEL = 1024
HEAD_DIM = 64
HEADS_DIFF = 4
HEADS_MOBA = 8
N_HEADS_TOTAL = HEADS_DIFF + HEADS_MOBA
DIFF_WIDTH = HEADS_DIFF * 2 * HEAD_DIM
MOBA_WIDTH = HEADS_MOBA * HEAD_DIM
QK_WIDTH = 2 * DIFF_WIDTH + 2 * MOBA_WIDTH
V_WIDTH = DIFF_WIDTH + MOBA_WIDTH
MOBA_TOPK = 3
NUM_BUCKETS = 32
MAX_DISTANCE = 2048
N_GROUPS = 4
EXPERTS_PER_GROUP = 8
N_EXPERTS = N_GROUPS * EXPERTS_PER_GROUP
EXPERTS_PER_TOKEN = 2
EXPERT_HIDDEN = 256
RMS_EPS = 1e-6
NEG_INF = -1e30
LAMBDA_INIT = 0.8 - 0.6 * math.exp(-0.3 * 0)

LANES = 128
ROW_TILE = D_MODEL // LANES
ATT_TILE = 256
ONES_ROWS = 16
LOGIT_LIMIT = 80.0
ROUNDING_MARGIN = 1.05
LOG2E = math.log2(math.e)
PROJ_ROWS = 512
OUT_ROWS = 512
ROUTE_TILE = 256
MOE_CHUNK = 512
MOE_TOKEN_PARTS = 2
EXPERT_ROW_PARTS = 4
MOVE_ROWS = 256
ROUTER_ROWS = 40
VMEM_LIMIT = 48 * 1024 * 1024

F32 = jnp.float32
BF16 = jnp.bfloat16
_NT = (((1,), (1,)), ((), ()))


def _t5_thresholds():
    n = np.arange(0, 1 << 16)
    max_exact = NUM_BUCKETS // 2
    nf = np.maximum(n, 1).astype(np.float32)
    large = max_exact + (np.log(nf / np.float32(max_exact)) / np.float32(math.log(MAX_DISTANCE / max_exact))
                         * np.float32(NUM_BUCKETS - max_exact)).astype(np.int32)
    bucket = np.where(n < max_exact, n, np.minimum(large, NUM_BUCKETS - 1))
    return [int(np.searchsorted(bucket, b, side="left")) for b in range(1, NUM_BUCKETS)]


_T5_THRESHOLDS = _t5_thresholds()


def _params(*sem):
    return pltpu.CompilerParams(dimension_semantics=sem, vmem_limit_bytes=VMEM_LIMIT)


def _bias_kernel(tab_ref, out_ref):
    h = pl.program_id(0)
    kj = lax.broadcasted_iota(jnp.int32, (ATT_TILE, ATT_TILE), 0)
    qi = lax.broadcasted_iota(jnp.int32, (ATT_TILE, ATT_TILE), 1)
    for d in range(out_ref.shape[0]):
        dist = d * ATT_TILE + qi - kj
        lo, hi = d * ATT_TILE - (ATT_TILE - 1), d * ATT_TILE + (ATT_TILE - 1)
        base = sum(thr <= max(lo, 0) for thr in _T5_THRESHOLDS)
        val = jnp.full((ATT_TILE, ATT_TILE), tab_ref[base, h], F32)
        for b, thr in enumerate(_T5_THRESHOLDS, start=1):
            if max(lo, 0) < thr <= hi:
                val = jnp.where(dist >= thr, tab_ref[b, h], val)
        val = val * LOG2E
        out_ref[d] = jnp.where(dist < 0, NEG_INF, val) if lo < 0 else val


def _bias_tiles(rel_bias, n_diag):
    return pl.pallas_call(
        _bias_kernel,
        grid=(N_HEADS_TOTAL,),
        in_specs=[pl.BlockSpec(memory_space=pltpu.SMEM)],
        out_specs=pl.BlockSpec((None, n_diag, ATT_TILE, ATT_TILE), lambda h: (h, 0, 0, 0)),
        out_shape=jax.ShapeDtypeStruct((N_HEADS_TOTAL, n_diag, ATT_TILE, ATT_TILE), F32),
        compiler_params=_params("arbitrary"),
        name="bias_tiles",
    )(rel_bias)


def _in_proj_kernel(x_ref, g1_ref, w_ref, pg_ref, grp_ref, qk_ref, vt_ref):
    n_row_tiles = PROJ_ROWS // ATT_TILE
    h = []
    for t in range(n_row_tiles):
        x = x_ref[t * ATT_TILE:(t + 1) * ATT_TILE, :]
        ms = jnp.mean(x * x, axis=-1, keepdims=True)
        h.append((x * lax.rsqrt(ms + RMS_EPS) * g1_ref[...]).astype(BF16))

    cw = 2 * LANES

    def finish(c, t, p):
        rows = slice(t * ATT_TILE, (t + 1) * ATT_TILE)
        cols = slice(c * cw, (c + 1) * cw)
        if c < QK_WIDTH // cw:
            msq = jnp.dot((p * p).astype(BF16), grp_ref[...], preferred_element_type=F32)
            qk_ref[rows, cols] = (p * lax.rsqrt(msq + RMS_EPS) * pg_ref[:, cols]).astype(BF16)
        else:
            vt_ref[t, c * cw - QK_WIDTH:(c + 1) * cw - QK_WIDTH, :] = p.T.astype(BF16)

    units = [(c, t) for c in range((QK_WIDTH + V_WIDTH) // cw) for t in range(n_row_tiles)]
    pending = None
    for c, t in units:
        p = jnp.dot(h[t], w_ref[:, c * cw:(c + 1) * cw], preferred_element_type=F32)
        if pending is not None:
            finish(*pending)
        pending = (c, t, p)
    finish(*pending)


def _in_proj(x2, g1, w_perm, post_gain, grp, batch, seq):
    n = x2.shape[0]
    steps_per_seq = seq // PROJ_ROWS
    tiles_per_step = PROJ_ROWS // ATT_TILE
    return pl.pallas_call(
        _in_proj_kernel,
        grid=(n // PROJ_ROWS,),
        in_specs=[
            pl.BlockSpec((PROJ_ROWS, D_MODEL), lambda i: (i, 0)),
            pl.BlockSpec((1, D_MODEL), lambda i: (0, 0)),
            pl.BlockSpec((D_MODEL, QK_WIDTH + V_WIDTH), lambda i: (0, 0)),
            pl.BlockSpec((1, QK_WIDTH), lambda i: (0, 0)),
            pl.BlockSpec((2 * LANES, 2 * LANES), lambda i: (0, 0)),
        ],
        out_specs=[
            pl.BlockSpec((PROJ_ROWS, QK_WIDTH), lambda i: (i, 0)),
            pl.BlockSpec((None, tiles_per_step, V_WIDTH, ATT_TILE),
                         lambda i: (i // steps_per_seq, i % steps_per_seq, 0, 0)),
        ],
        out_shape=[
            jax.ShapeDtypeStruct((n, QK_WIDTH), BF16),
            jax.ShapeDtypeStruct((batch, seq // ATT_TILE, V_WIDTH, ATT_TILE), BF16),
        ],
        compiler_params=_params("arbitrary"),
        name="in_proj",
    )(x2, g1, w_perm, post_gain, grp)


def _flash_tiles(qi, n_chains, logits_fn, values_fn, bias_fn, mask_fn, s_ref, p_ref, acc_ref):
    for c in range(n_chains):
        s_ref[0, c] = logits_fn(c, qi)
    p_ref[...] = jnp.zeros_like(p_ref)
    acc_ref[...] = jnp.zeros_like(acc_ref)

    def add_values(c, kv, alpha):
        acc_ref[c] = acc_ref[c] * alpha + jnp.dot(values_fn(c, kv), p_ref[c], preferred_element_type=F32)

    def step(j, carry):
        cur = lax.rem(j, 2)
        kv = qi - j
        out = []
        for c in range(n_chains):
            add_values(c, jnp.minimum(kv + 1, qi), carry[2 * c + 1])
        for c in range(n_chains):
            s = s_ref[cur, c] + bias_fn(c, j)
            tile_max = jnp.max(s, axis=0, keepdims=True)
            shift = mask_fn(c, kv)
            if shift is not None:
                tile_max = tile_max + shift
            m_new = jnp.maximum(carry[2 * c], tile_max)
            alpha = jnp.exp2(carry[2 * c] - m_new)
            p_ref[c] = jnp.exp2(s - (m_new if shift is None else m_new - shift)).astype(BF16)
            out += [m_new, alpha]
        for c in range(n_chains):
            s_ref[1 - cur, c] = logits_fn(c, jnp.maximum(kv - 1, 0))
        return tuple(out)

    m0 = jnp.full((1, ATT_TILE), NEG_INF, F32)
    one = jnp.ones((1, ATT_TILE), F32)
    carry = lax.fori_loop(0, qi + 1, step, (m0, one) * n_chains)
    for c in range(n_chains):
        add_values(c, 0, carry[2 * c + 1])


def _bounded_tiles(n_tiles, n_chains, logits_fn, values_fn, bias_fn, keep_fn, p_ref, acc_ref):
    for d in range(n_tiles):
        for qi in range(d, n_tiles):
            for c in range(n_chains):
                p_ref[qi, c] = jnp.exp2(logits_fn(c, qi, qi - d) + bias_fn(c, d)).astype(BF16)
        for qi in range(d, n_tiles):
            kv = qi - d
            for c in range(n_chains):
                pv = jnp.dot(values_fn(c, kv), p_ref[qi, c], preferred_element_type=F32)
                keep = keep_fn(c, qi, kv)
                if keep is not None:
                    pv = pv * keep
                acc_ref[qi, c] = pv if d == 0 else acc_ref[qi, c] + pv


def _logits_bounded(q_gain, k_gain, bias_cols):
    bound = (HEAD_DIM * jnp.max(jnp.abs(q_gain)) * jnp.max(jnp.abs(k_gain)) * ROUNDING_MARGIN
             + jnp.max(jnp.abs(bias_cols)) * LOG2E)
    return (bound <= LOGIT_LIMIT).astype(jnp.int32).reshape(1)


def _with_ones_rows(vt):
    return jnp.concatenate([vt, jnp.ones((ONES_ROWS, vt.shape[1]), vt.dtype)], axis=0)


def _attention_scratch(n_tiles, value_rows, n_chains=2):
    return [pltpu.VMEM((n_tiles, n_chains, ATT_TILE, LANES), BF16),
            pltpu.VMEM((2, n_chains, ATT_TILE, ATT_TILE), F32),
            pltpu.VMEM((n_tiles, n_chains, ATT_TILE, ATT_TILE), BF16),
            pltpu.VMEM((n_tiles, n_chains, value_rows + ONES_ROWS, ATT_TILE), F32)]


def _half_lane_split(q):
    lane = lax.broadcasted_iota(jnp.int32, q.shape, 1)
    zero = jnp.zeros_like(q)
    return jnp.where(lane < HEAD_DIM, q, zero), jnp.where(lane >= HEAD_DIM, q, zero)


def _diff_kernel(bounded_ref, lq1_ref, lk1_ref, lq2_ref, lk2_ref, subg_ref, q_ref, k_ref, vt_ref, bias_ref, o_ref,
                 qm_ref, s_ref, p_ref, acc_ref):
    n_tiles = q_ref.shape[0]
    for qi in range(n_tiles):
        qm_ref[qi, 0], qm_ref[qi, 1] = _half_lane_split(q_ref[qi])
    chains = dict(n_chains=2, values_fn=lambda c, kv: _with_ones_rows(vt_ref[kv]))
    bounded = bounded_ref[0] > 0

    @pl.when(bounded)
    def _():
        _bounded_tiles(
            n_tiles,
            logits_fn=lambda c, qi, kv: lax.dot_general(k_ref[kv], qm_ref[qi, c], _NT, preferred_element_type=F32),
            bias_fn=lambda c, d: bias_ref[d],
            keep_fn=lambda c, qi, kv: None,
            p_ref=p_ref, acc_ref=acc_ref, **chains)

    @pl.when(jnp.logical_not(bounded))
    def _():
        def q_tile(qi, carry):
            _flash_tiles(
                qi,
                logits_fn=lambda c, kv: lax.dot_general(k_ref[kv], qm_ref[qi, c], _NT, preferred_element_type=F32),
                bias_fn=lambda c, j: bias_ref[j],
                mask_fn=lambda c, kv: None,
                s_ref=s_ref, p_ref=p_ref.at[0], acc_ref=acc_ref.at[qi], **chains)
            return carry
        lax.fori_loop(0, n_tiles, q_tile, 0)

    lam = (jnp.exp(jnp.sum(lq1_ref[...] * lk1_ref[...], keepdims=True))
           - jnp.exp(jnp.sum(lq2_ref[...] * lk2_ref[...], keepdims=True)) + LAMBDA_INIT)
    width = 2 * HEAD_DIM
    for qi in range(n_tiles):
        a1, a2 = acc_ref[qi, 0], acc_ref[qi, 1]
        o = a1[:width] / a1[width:width + 1] - lam * (a2[:width] / a2[width:width + 1])
        ms = jnp.mean(o * o, axis=0, keepdims=True)
        o = o * lax.rsqrt(ms + RMS_EPS) * subg_ref[...] * (1.0 - LAMBDA_INIT)
        o_ref[qi] = o.T.astype(BF16)


def _diff_attention(bounded, lam_vecs, sub_g_col, qk4, vt4, bias):
    batch, n_tiles = qk4.shape[0], qk4.shape[1]
    k_col0 = DIFF_WIDTH // LANES
    vec = pl.BlockSpec((1, HEAD_DIM), lambda h, b: (0, 0))
    return pl.pallas_call(
        _diff_kernel,
        grid=(HEADS_DIFF, batch),
        in_specs=[
            pl.BlockSpec(memory_space=pltpu.SMEM),
            vec, vec, vec, vec,
            pl.BlockSpec((2 * HEAD_DIM, 1), lambda h, b: (0, 0)),
            pl.BlockSpec((None, n_tiles, ATT_TILE, LANES), lambda h, b: (b, 0, 0, h)),
            pl.BlockSpec((None, n_tiles, ATT_TILE, LANES), lambda h, b: (b, 0, 0, k_col0 + h)),
            pl.BlockSpec((None, n_tiles, LANES, ATT_TILE), lambda h, b: (b, 0, h, 0)),
            pl.BlockSpec((None, n_tiles, ATT_TILE, ATT_TILE), lambda h, b: (h, 0, 0, 0)),
        ],
        out_specs=pl.BlockSpec((None, n_tiles, ATT_TILE, LANES), lambda h, b: (b, 0, 0, h)),
        out_shape=jax.ShapeDtypeStruct((batch, n_tiles, ATT_TILE, DIFF_WIDTH), BF16),
        scratch_shapes=_attention_scratch(n_tiles, 2 * HEAD_DIM),
        compiler_params=_params("arbitrary", "arbitrary"),
        name="diff_attention",
    )(bounded, *lam_vecs, sub_g_col, qk4, qk4, vt4, bias)


def _split3(v):
    hi = v.astype(BF16)
    r1 = v - hi.astype(F32)
    mid = r1.astype(BF16)
    lo = (r1 - mid.astype(F32)).astype(BF16)
    return hi, mid, lo


def _block_mask(gate, own):
    row = lax.broadcasted_iota(jnp.int32, gate.shape, 0)
    rank = jnp.zeros(gate.shape, jnp.int32)
    for m in range(own):
        gm = gate[m:m + 1, :]
        beats = (gm > gate) | ((gm == gate) & (row > m))
        rank = rank + jnp.where(beats, 1, 0)
    keep = ((rank < MOBA_TOPK) & (row < own)) | (row == own)
    return jnp.where(keep, 0.0, NEG_INF).astype(F32)


def _moba_kernel(bounded_ref, q_ref, k_ref, vt_ref, bias_ref, o_ref, kmean_ref, mask_ref, qm_ref, s_ref, p_ref,
                 acc_ref):
    n_tiles = q_ref.shape[0]
    for n in range(n_tiles):
        kmean_ref[n:n + 1, :] = jnp.mean(k_ref[n].astype(F32), axis=0, keepdims=True)
    per_head = zip(*[_half_lane_split(term) for term in _split3(kmean_ref[...])])
    gate_lhs = jnp.concatenate([term for head_terms in per_head for term in head_terms], axis=0)
    n_terms = gate_lhs.shape[0] // (2 * n_tiles)
    for qi in range(n_tiles):
        qm_ref[qi, 0], qm_ref[qi, 1] = _half_lane_split(q_ref[qi])
        terms = lax.dot_general(gate_lhs, q_ref[qi], _NT, preferred_element_type=F32)
        for half in range(2):
            rows = [terms[(half * n_terms + t) * n_tiles:(half * n_terms + t + 1) * n_tiles] for t in range(n_terms)]
            mask_ref[qi, half] = _block_mask(sum(rows[1:], rows[0]), qi)

    def keep_row(c, qi, kv):
        if qi <= MOBA_TOPK or kv == qi:
            return None
        return jnp.where(mask_ref[qi, c, kv:kv + 1, :] < -1.0, 0.0, 1.0)

    chains = dict(
        n_chains=2,
        values_fn=lambda c, kv: _with_ones_rows(vt_ref[kv, c * HEAD_DIM:(c + 1) * HEAD_DIM, :]))
    bounded = bounded_ref[0] > 0

    @pl.when(bounded)
    def _():
        _bounded_tiles(
            n_tiles,
            logits_fn=lambda c, qi, kv: lax.dot_general(k_ref[kv], qm_ref[qi, c], _NT, preferred_element_type=F32),
            bias_fn=lambda c, d: bias_ref[c, d],
            keep_fn=keep_row,
            p_ref=p_ref, acc_ref=acc_ref, **chains)

    @pl.when(jnp.logical_not(bounded))
    def _():
        def q_tile(qi, carry):
            _flash_tiles(
                qi,
                logits_fn=lambda c, kv: lax.dot_general(k_ref[kv], qm_ref[qi, c], _NT, preferred_element_type=F32),
                bias_fn=lambda c, j: bias_ref[c, j],
                mask_fn=lambda c, kv: mask_ref[qi, c, pl.ds(kv, 1), :],
                s_ref=s_ref, p_ref=p_ref.at[0], acc_ref=acc_ref.at[qi], **chains)
            return carry
        lax.fori_loop(0, n_tiles, q_tile, 0)

    for qi in range(n_tiles):
        halves = [acc_ref[qi, half] for half in range(2)]
        o = jnp.concatenate([a[:HEAD_DIM] / a[HEAD_DIM:HEAD_DIM + 1] for a in halves], axis=0)
        o_ref[qi] = o.T.astype(BF16)


def _moba_attention(bounded, qk4, vt4, bias_pairs):
    batch, n_tiles = qk4.shape[0], qk4.shape[1]
    q_col0 = 2 * DIFF_WIDTH // LANES
    k_col0 = q_col0 + MOBA_WIDTH // LANES
    v_row0 = DIFF_WIDTH // LANES
    pair0 = HEADS_DIFF // 2
    return pl.pallas_call(
        _moba_kernel,
        grid=(HEADS_MOBA // 2, batch),
        in_specs=[
            pl.BlockSpec(memory_space=pltpu.SMEM),
            pl.BlockSpec((None, n_tiles, ATT_TILE, LANES), lambda h, b: (b, 0, 0, q_col0 + h)),
            pl.BlockSpec((None, n_tiles, ATT_TILE, LANES), lambda h, b: (b, 0, 0, k_col0 + h)),
            pl.BlockSpec((None, n_tiles, LANES, ATT_TILE), lambda h, b: (b, 0, v_row0 + h, 0)),
            pl.BlockSpec((None, 2, n_tiles, ATT_TILE, ATT_TILE), lambda h, b: (pair0 + h, 0, 0, 0, 0)),
        ],
        out_specs=pl.BlockSpec((None, n_tiles, ATT_TILE, LANES), lambda h, b: (b, 0, 0, h)),
        out_shape=jax.ShapeDtypeStruct((batch, n_tiles, ATT_TILE, MOBA_WIDTH), BF16),
        scratch_shapes=[
            pltpu.VMEM((n_tiles, LANES), F32),
            pltpu.VMEM((n_tiles, 2, n_tiles, ATT_TILE), F32),
        ] + _attention_scratch(n_tiles, HEAD_DIM),
        compiler_params=_params("arbitrary", "arbitrary"),
        name="moba_attention",
    )(bounded, qk4, qk4, vt4, bias_pairs)


def _load_token_rows(ref, n_rows, lead=(), first=0):
    chunks = [ref[lead + (pl.ds(first * ROW_TILE + c, n_rows, stride=ROW_TILE), slice(None))]
              for c in range(ROW_TILE)]
    return jnp.concatenate(chunks, axis=1)


def _store_token_rows(ref, value, first=0):
    n_rows = value.shape[0]
    for c in range(ROW_TILE):
        ref[pl.ds(first * ROW_TILE + c, n_rows, stride=ROW_TILE), :] = value[:, c * LANES:(c + 1) * LANES]


def _token_tile(ref, t):
    return ref.at[pl.ds(pl.multiple_of(t * ROW_TILE, ROW_TILE), ROW_TILE)]


def _first_argmax(v):
    top = jnp.max(v, axis=0, keepdims=True)
    row = lax.broadcasted_iota(jnp.int32, v.shape, 0)
    idx = jnp.min(jnp.where(v == top, row, v.shape[0]), axis=0, keepdims=True)
    return top, idx


def _out_proj_kernel(yd_ref, ym_ref, x_ref, wd_ref, wm_ref, g2_ref, wr_ref, tri_ref,
                     x1_ref, ri_ref, rf_ref, cnt_ref, run_ref):
    @pl.when(lax.rem(pl.program_id(0), pl.num_programs(0) // MOE_TOKEN_PARTS) == 0)
    def _():
        run_ref[...] = jnp.zeros_like(run_ref)

    tiles = range(OUT_ROWS // ROUTE_TILE)
    h2 = []
    for t in tiles:
        rows = slice(t * ROUTE_TILE, (t + 1) * ROUTE_TILE)
        x1 = (x_ref[rows, :] + jnp.dot(yd_ref[rows, :], wd_ref[...], preferred_element_type=F32)
              + jnp.dot(ym_ref[rows, :], wm_ref[...], preferred_element_type=F32))
        _store_token_rows(x1_ref, x1, first=t * ROUTE_TILE)
        ms = jnp.mean(x1 * x1, axis=-1, keepdims=True)
        h2.append((x1 * lax.rsqrt(ms + RMS_EPS) * g2_ref[...]).astype(BF16))
    logits = [lax.dot_general(wr_ref[...], h, _NT, preferred_element_type=F32) for h in h2]

    picks = []
    for lg in logits:
        g_logits = lg[N_EXPERTS:N_EXPERTS + N_GROUPS, :]
        g_top, g_idx = _first_argmax(g_logits)
        p_group = 1.0 / jnp.sum(jnp.exp(g_logits - g_top), axis=0, keepdims=True)
        e_logits = lg[0:EXPERTS_PER_GROUP, :]
        for g in range(1, N_GROUPS):
            e_logits = jnp.where(g_idx == g, lg[g * EXPERTS_PER_GROUP:(g + 1) * EXPERTS_PER_GROUP, :], e_logits)
        v1, i1 = _first_argmax(e_logits)
        row = lax.broadcasted_iota(jnp.int32, e_logits.shape, 0)
        v2, i2 = _first_argmax(jnp.where(row == i1, -jnp.inf, e_logits))
        ratio = jnp.exp(v2 - v1)
        w1 = p_group / (1.0 + ratio)
        w2 = p_group * ratio / (1.0 + ratio)
        e1 = g_idx * EXPERTS_PER_GROUP + i1
        e2 = g_idx * EXPERTS_PER_GROUP + i2
        erow = lax.broadcasted_iota(jnp.int32, (N_EXPERTS, ROUTE_TILE), 0)
        hit1 = erow == e1
        hit2 = erow == e2
        onehot = jnp.where(hit1 | hit2, 1.0, 0.0).astype(F32)
        picks.append((e1, e2, w1, w2, hit1, hit2, onehot))

    prefix = [jnp.dot(p[-1].astype(BF16), tri_ref[...], preferred_element_type=F32) for p in picks]
    out_row = lax.broadcasted_iota(jnp.int32, (8, ROUTE_TILE), 0)
    for t in tiles:
        e1, e2, w1, w2, hit1, hit2, onehot = picks[t]
        lanes = slice(t * ROUTE_TILE, (t + 1) * ROUTE_TILE)
        before = prefix[t] + run_ref[...]
        r1 = jnp.sum(jnp.where(hit1, before, 0.0), axis=0, keepdims=True).astype(jnp.int32)
        r2 = jnp.sum(jnp.where(hit2, before, 0.0), axis=0, keepdims=True).astype(jnp.int32)
        run_ref[...] = run_ref[...] + jnp.sum(onehot, axis=1, keepdims=True)
        ri_ref[:, lanes] = jnp.where(out_row == 0, e1, jnp.where(out_row == 1, e2,
                                     jnp.where(out_row == 2, r1, jnp.where(out_row == 3, r2, 0))))
        rf_ref[:, lanes] = jnp.where(out_row == 0, w1, jnp.where(out_row == 1, w2, 0.0))
    cnt_ref[...] = jnp.broadcast_to(run_ref[...], cnt_ref.shape).astype(jnp.int32)


def _out_proj(y_d, y_m, x2, wo_d, wo_m, g2, w_router, tri):
    n = x2.shape[0]
    const = lambda i: (0, 0)
    steps_per_part = n // OUT_ROWS // MOE_TOKEN_PARTS
    return pl.pallas_call(
        _out_proj_kernel,
        grid=(n // OUT_ROWS,),
        in_specs=[
            pl.BlockSpec((OUT_ROWS, DIFF_WIDTH), lambda i: (i, 0)),
            pl.BlockSpec((OUT_ROWS, MOBA_WIDTH), lambda i: (i, 0)),
            pl.BlockSpec((OUT_ROWS, D_MODEL), lambda i: (i, 0)),
            pl.BlockSpec((DIFF_WIDTH, D_MODEL), const),
            pl.BlockSpec((MOBA_WIDTH, D_MODEL), const),
            pl.BlockSpec((1, D_MODEL), const),
            pl.BlockSpec((ROUTER_ROWS, D_MODEL), const),
            pl.BlockSpec((ROUTE_TILE, ROUTE_TILE), const),
        ],
        out_specs=[
            pl.BlockSpec((OUT_ROWS * ROW_TILE, LANES), lambda i: (i, 0)),
            pl.BlockSpec((8, OUT_ROWS), lambda i: (0, i)),
            pl.BlockSpec((8, OUT_ROWS), lambda i: (0, i)),
            pl.BlockSpec((None, N_EXPERTS, LANES), lambda i: (i // steps_per_part, 0, 0)),
        ],
        out_shape=[
            jax.ShapeDtypeStruct((n * ROW_TILE, LANES), F32),
            jax.ShapeDtypeStruct((8, n), jnp.int32),
            jax.ShapeDtypeStruct((8, n), F32),
            jax.ShapeDtypeStruct((MOE_TOKEN_PARTS, N_EXPERTS, LANES), jnp.int32),
        ],
        scratch_shapes=[pltpu.VMEM((N_EXPERTS, 1), F32)],
        compiler_params=_params("arbitrary"),
        name="out_proj_route",
    )(y_d, y_m, x2, wo_d, wo_m, g2, w_router, tri)


def _dispatch_step(i, n_steps, first_tile, zchunk_ref, dest_ref, x1_hbm, xs_hbm, zero_ref, stage_ref, zero_sem,
                   load_sem, row_sem):
    last = n_steps - 1
    slot = lax.rem(i, 2)
    chunk_rows = MOE_CHUNK * ROW_TILE
    tile_rows = MOVE_ROWS * ROW_TILE

    def load(t, s):
        start = pl.multiple_of((first_tile + t) * tile_rows, tile_rows)
        return pltpu.make_async_copy(x1_hbm.at[pl.ds(start, tile_rows)], stage_ref.at[s], load_sem.at[s])

    def wait_rows(s):
        for _ in range(EXPERTS_PER_TOKEN):
            pltpu.make_async_copy(stage_ref.at[s], xs_hbm.at[pl.ds(0, tile_rows)], row_sem.at[s]).wait()

    def zero_copy(c):
        start = pl.multiple_of(c * chunk_rows, chunk_rows)
        return pltpu.make_async_copy(zero_ref, xs_hbm.at[pl.ds(start, chunk_rows)], zero_sem)

    @pl.when(i == 0)
    def _():
        zero_ref[...] = jnp.zeros_like(zero_ref)

        def start_one(c, carry):
            @pl.when(zchunk_ref[c] > 0)
            def _():
                zero_copy(c).start()
            return carry

        def wait_one(c, carry):
            @pl.when(zchunk_ref[c] > 0)
            def _():
                zero_copy(c).wait()
            return carry

        lax.fori_loop(0, zchunk_ref.shape[0], start_one, 0)
        lax.fori_loop(0, zchunk_ref.shape[0], wait_one, 0)
        load(0, 0).start()

    @pl.when(i > 0)
    def _():
        wait_rows(1 - slot)

    @pl.when(i < last)
    def _():
        load(i + 1, 1 - slot).start()

    load(i, slot).wait()

    def send_row(r, carry):
        src = _token_tile(stage_ref.at[slot], r)
        for k in range(EXPERTS_PER_TOKEN):
            pltpu.make_async_copy(src, _token_tile(xs_hbm, dest_ref[0, k, r]), row_sem.at[slot]).start()
        return carry

    lax.fori_loop(0, MOVE_ROWS, send_row, 0, unroll=8)

    @pl.when(i == last)
    def _():
        wait_rows(slot)


def _dispatch_scratch():
    return [pltpu.VMEM((MOE_CHUNK * ROW_TILE, LANES), F32),
            pltpu.VMEM((2, MOVE_ROWS * ROW_TILE, LANES), F32),
            pltpu.SemaphoreType.DMA(()),
            pltpu.SemaphoreType.DMA((2,)),
            pltpu.SemaphoreType.DMA((2,))]


def _dispatch_kernel(first_tile, zchunk_ref, dest_ref, x1_hbm, xs_hbm, *scratch):
    _dispatch_step(pl.program_id(0), pl.num_programs(0), first_tile, zchunk_ref, dest_ref, x1_hbm, xs_hbm, *scratch)


def _dispatch(part, zchunk, dest3, x1, n_slots):
    tiles = x1.shape[0] // ROW_TILE // MOVE_ROWS // MOE_TOKEN_PARTS
    first_tile = part * tiles
    grid_spec = pltpu.PrefetchScalarGridSpec(
        num_scalar_prefetch=1,
        grid=(tiles,),
        in_specs=[
            pl.BlockSpec((1, EXPERTS_PER_TOKEN, MOVE_ROWS), lambda i, zc: (first_tile + i, 0, 0),
                         memory_space=pltpu.SMEM),
            pl.BlockSpec(memory_space=pl.ANY),
        ],
        out_specs=pl.BlockSpec(memory_space=pl.ANY),
        scratch_shapes=_dispatch_scratch(),
    )
    return pl.pallas_call(
        functools.partial(_dispatch_kernel, first_tile),
        grid_spec=grid_spec,
        out_shape=jax.ShapeDtypeStruct((n_slots * ROW_TILE, LANES), F32),
        compiler_params=_params("arbitrary"),
        name="moe_dispatch",
    )(zchunk, dest3, x1)


def _expert_step(c, na_ref, xs_ref, g2_ref, wg_ref, wu_ref, wd_ref, o_ref):
    active = c < na_ref[0]

    @pl.when(jnp.logical_not(active))
    def _():
        o_ref[...] = jnp.zeros_like(o_ref)

    @pl.when(active)
    def _():
        part_rows = MOE_CHUNK // EXPERT_ROW_PARTS

        def gate_up(r):
            x = _load_token_rows(xs_ref, part_rows, first=r * part_rows)
            ms = jnp.mean(x * x, axis=-1, keepdims=True)
            h = (x * lax.rsqrt(ms + RMS_EPS) * g2_ref[...]).astype(BF16)
            return (jnp.dot(h, wg_ref[...], preferred_element_type=F32),
                    jnp.dot(h, wu_ref[...], preferred_element_type=F32))

        def down(r, gate, up):
            hid = (gate * jax.nn.sigmoid(gate) * up).astype(BF16)
            _store_token_rows(o_ref, jnp.dot(hid, wd_ref[...], preferred_element_type=F32), first=r * part_rows)

        pending = None
        for r in range(EXPERT_ROW_PARTS):
            current = (r,) + gate_up(r)
            if pending is not None:
                down(*pending)
            pending = current
        down(*pending)


def _expert_kernel(ce_ref, na_ref, *refs):
    _expert_step(pl.program_id(0), na_ref, *refs)


def _expert_specs():
    rows = lambda c, ce, na, *_: (jnp.minimum(c, na[0] - 1), 0)
    expert = lambda c, ce, na, *_: (ce[c], 0, 0)
    in_specs = [
        pl.BlockSpec((MOE_CHUNK * ROW_TILE, LANES), rows),
        pl.BlockSpec((1, D_MODEL), lambda c, *_: (0, 0)),
        pl.BlockSpec((None, D_MODEL, EXPERT_HIDDEN), expert),
        pl.BlockSpec((None, D_MODEL, EXPERT_HIDDEN), expert),
        pl.BlockSpec((None, EXPERT_HIDDEN, D_MODEL), expert),
    ]
    return in_specs, pl.BlockSpec((MOE_CHUNK * ROW_TILE, LANES), lambda c, *_: (c, 0))


def _experts(chunk_e, n_active, xs, g2, wg, wu, wd):
    n_slots = xs.shape[0] // ROW_TILE
    in_specs, out_spec = _expert_specs()
    grid_spec = pltpu.PrefetchScalarGridSpec(
        num_scalar_prefetch=2, grid=(n_slots // MOE_CHUNK,), in_specs=in_specs, out_specs=out_spec)
    return pl.pallas_call(
        _expert_kernel,
        grid_spec=grid_spec,
        out_shape=jax.ShapeDtypeStruct((n_slots * ROW_TILE, LANES), F32),
        compiler_params=_params("arbitrary"),
        name="moe_experts",
    )(chunk_e, n_active, xs, g2, wg, wu, wd)


def _experts_dispatch_kernel(first_tile, n_tiles, ce_ref, na_ref, zchunk_ref,
                             xs_ref, g2_ref, wg_ref, wu_ref, wd_ref, dest_ref, x1_hbm,
                             o_ref, xs_next_hbm, *scratch):
    i = pl.program_id(0)

    @pl.when(i < n_tiles)
    def _():
        _dispatch_step(i, n_tiles, first_tile, zchunk_ref, dest_ref, x1_hbm, xs_next_hbm, *scratch)

    _expert_step(i, na_ref, xs_ref, g2_ref, wg_ref, wu_ref, wd_ref, o_ref)


def _experts_dispatch(chunk_e, n_active, zchunk_next, xs, g2, wg, wu, wd, next_part, dest3, x1):
    n_slots = xs.shape[0] // ROW_TILE
    n_chunks = n_slots // MOE_CHUNK
    tiles = x1.shape[0] // ROW_TILE // MOVE_ROWS // MOE_TOKEN_PARTS
    first_tile = next_part * tiles
    assert n_chunks >= tiles
    in_specs, out_spec = _expert_specs()
    in_specs += [
        pl.BlockSpec((1, EXPERTS_PER_TOKEN, MOVE_ROWS),
                     lambda c, *_: (first_tile + jnp.minimum(c, tiles - 1), 0, 0), memory_space=pltpu.SMEM),
        pl.BlockSpec(memory_space=pl.ANY),
    ]
    grid_spec = pltpu.PrefetchScalarGridSpec(
        num_scalar_prefetch=3, grid=(n_chunks,), in_specs=in_specs,
        out_specs=[out_spec, pl.BlockSpec(memory_space=pl.ANY)],
        scratch_shapes=_dispatch_scratch())
    slots = jax.ShapeDtypeStruct((n_slots * ROW_TILE, LANES), F32)
    return pl.pallas_call(
        functools.partial(_experts_dispatch_kernel, first_tile, tiles),
        grid_spec=grid_spec,
        out_shape=[slots, slots],
        compiler_params=_params("arbitrary"),
        name="moe_experts_dispatch",
    )(chunk_e, n_active, zchunk_next, xs, g2, wg, wu, wd, dest3, x1)


def _combine_kernel(dest_ref, dest_next_ref, w_ref, x1_ref, *rest):
    outs_hbm, (y_ref, buf_ref, sem_ref) = rest[:MOE_TOKEN_PARTS], rest[MOE_TOKEN_PARTS:]
    i = pl.program_id(0)
    slot = lax.rem(i, 2)
    tiles_per_part = pl.num_programs(0) // MOE_TOKEN_PARTS

    def fetch(tile, dref, s):
        for part, src_hbm in enumerate(outs_hbm):
            @pl.when(tile // tiles_per_part == part)
            def _():
                def fetch_row(r, carry):
                    for k in range(EXPERTS_PER_TOKEN):
                        pltpu.make_async_copy(_token_tile(src_hbm, dref[0, k, r]),
                                              _token_tile(buf_ref.at[s, k], r), sem_ref.at[s]).start()
                    return carry
                lax.fori_loop(0, MOVE_ROWS, fetch_row, 0, unroll=8)

    @pl.when(i == 0)
    def _():
        fetch(i, dest_ref, 0)

    @pl.when(i + 1 < pl.num_programs(0))
    def _():
        fetch(i + 1, dest_next_ref, 1 - slot)

    for k in range(EXPERTS_PER_TOKEN):
        pltpu.make_async_copy(outs_hbm[0].at[pl.ds(0, MOVE_ROWS * ROW_TILE)], buf_ref.at[slot, k],
                              sem_ref.at[slot]).wait()
    w = w_ref[...]
    y_ref[...] = (_load_token_rows(x1_ref, MOVE_ROWS)
                  + w[:, 0:1] * _load_token_rows(buf_ref, MOVE_ROWS, (slot, 0))
                  + w[:, 1:2] * _load_token_rows(buf_ref, MOVE_ROWS, (slot, 1)))


def _combine(dest3, wts, x1, outs_parts):
    n = x1.shape[0] // ROW_TILE
    steps = n // MOVE_ROWS
    dest_block = (1, EXPERTS_PER_TOKEN, MOVE_ROWS)
    return pl.pallas_call(
        _combine_kernel,
        grid=(steps,),
        in_specs=[
            pl.BlockSpec(dest_block, lambda i: (i, 0, 0), memory_space=pltpu.SMEM),
            pl.BlockSpec(dest_block, lambda i: (jnp.minimum(i + 1, steps - 1), 0, 0), memory_space=pltpu.SMEM),
            pl.BlockSpec((MOVE_ROWS, EXPERTS_PER_TOKEN), lambda i: (i, 0)),
            pl.BlockSpec((MOVE_ROWS * ROW_TILE, LANES), lambda i: (i, 0)),
        ] + [pl.BlockSpec(memory_space=pl.ANY)] * MOE_TOKEN_PARTS,
        out_specs=pl.BlockSpec((MOVE_ROWS, D_MODEL), lambda i: (i, 0)),
        out_shape=jax.ShapeDtypeStruct((n, D_MODEL), F32),
        scratch_shapes=[
            pltpu.VMEM((2, EXPERTS_PER_TOKEN, MOVE_ROWS * ROW_TILE, LANES), F32),
            pltpu.SemaphoreType.DMA((2,)),
        ],
        compiler_params=_params("arbitrary"),
        name="moe_combine",
    )(dest3, dest3, wts, x1, *outs_parts)


def _layer(x, norm1_g, w_in, diff_q_g, diff_k_g, lambda_q1, lambda_k1, lambda_q2, lambda_k2, diff_sub_g,
           moba_q_g, moba_k_g, rel_bias, w_out, norm2_g, router_group, router_expert, w_gate, w_up, w_down):
    batch, seq, _ = x.shape
    n = batch * seq
    n_tiles = seq // ATT_TILE
    scale = HEAD_DIM ** -0.5 * LOG2E

    d3 = 3 * DIFF_WIDTH
    w_perm = jnp.concatenate([w_in[:, :2 * DIFF_WIDTH], w_in[:, d3:d3 + 2 * MOBA_WIDTH],
                              w_in[:, 2 * DIFF_WIDTH:d3], w_in[:, d3 + 2 * MOBA_WIDTH:]], axis=1).astype(BF16)
    reps_d, reps_m = DIFF_WIDTH // HEAD_DIM, MOBA_WIDTH // HEAD_DIM
    post_gain = jnp.concatenate([jnp.tile(diff_q_g * scale, reps_d), jnp.tile(diff_k_g, reps_d),
                                 jnp.tile(moba_q_g * scale, reps_m), jnp.tile(moba_k_g, reps_m)])[None, :]
    head_of = np.arange(2 * LANES) // HEAD_DIM
    grp = jnp.asarray((head_of[:, None] == head_of[None, :]) / HEAD_DIM, BF16)

    bias = _bias_tiles(rel_bias, n_tiles)
    qk, vt4 = _in_proj(x.reshape(n, D_MODEL), norm1_g[None, :], w_perm, post_gain, grp, batch, seq)
    qk4 = qk.reshape(batch, n_tiles, ATT_TILE, QK_WIDTH)
    lam_vecs = [v[None, :] for v in (lambda_q1, lambda_k1, lambda_q2, lambda_k2)]
    y_d = _diff_attention(_logits_bounded(diff_q_g * scale, diff_k_g, rel_bias[:, :HEADS_DIFF]),
                          lam_vecs, diff_sub_g[:, None], qk4, vt4, bias)
    y_m = _moba_attention(_logits_bounded(moba_q_g * scale, moba_k_g, rel_bias[:, HEADS_DIFF:]),
                          qk4, vt4, bias.reshape(N_HEADS_TOTAL // 2, 2, n_tiles, ATT_TILE, ATT_TILE))

    w_router = jnp.concatenate([
        router_expert.transpose(0, 2, 1).reshape(N_EXPERTS, D_MODEL), router_group.T,
        jnp.zeros((ROUTER_ROWS - N_EXPERTS - N_GROUPS, D_MODEL), F32)], axis=0).astype(BF16)
    tri = jnp.asarray(np.triu(np.ones((ROUTE_TILE, ROUTE_TILE)), k=1), BF16)
    wo = w_out.astype(BF16)
    x1, route_i, route_f, cnt = _out_proj(y_d.reshape(n, DIFF_WIDTH), y_m.reshape(n, MOBA_WIDTH),
                                          x.reshape(n, D_MODEL), wo[:DIFF_WIDTH], wo[DIFF_WIDTH:],
                                          norm2_g[None, :], w_router, tri)

    parts = MOE_TOKEN_PARTS
    counts = cnt[:, :, 0]
    pad_counts = ((counts + MOE_CHUNK - 1) // MOE_CHUNK) * MOE_CHUNK
    pad_end = jnp.cumsum(pad_counts, axis=1)
    pad_start = pad_end - pad_counts
    experts = jnp.arange(N_EXPERTS, dtype=jnp.int32)
    picked = route_i[0:2].reshape(EXPERTS_PER_TOKEN, parts, n // parts)
    start_of = jnp.sum(jnp.where(picked[..., None] == experts, pad_start[None, :, None, :], 0), axis=-1)
    dest = start_of.reshape(EXPERTS_PER_TOKEN, n) + route_i[2:4]
    dest3 = dest.reshape(EXPERTS_PER_TOKEN, n // MOVE_ROWS, MOVE_ROWS).transpose(1, 0, 2)
    n_slots = n // parts * EXPERTS_PER_TOKEN + N_EXPERTS * MOE_CHUNK
    n_chunks = n_slots // MOE_CHUNK
    chunk_id = jnp.arange(n_chunks, dtype=jnp.int32)
    chunk_e = jnp.minimum(jnp.sum((pad_end[:, None, :] <= chunk_id[None, :, None] * MOE_CHUNK).astype(jnp.int32),
                                  axis=2), N_EXPERTS - 1)
    n_active = (pad_end[:, -1:] // MOE_CHUNK).astype(jnp.int32)
    next_e = jnp.concatenate([chunk_e[:, 1:], jnp.full((parts, 1), N_EXPERTS, jnp.int32)], axis=1)
    zchunk = ((chunk_id[None, :] >= n_active - 1) | (chunk_e != next_e)).astype(jnp.int32)

    weights = (norm2_g[None, :], w_gate.astype(BF16), w_up.astype(BF16), w_down.astype(BF16))
    xs = _dispatch(0, zchunk[0], dest3, x1, n_slots)
    outs = []
    for p in range(parts):
        if p + 1 < parts:
            out_p, xs_next = _experts_dispatch(chunk_e[p], n_active[p], zchunk[p + 1], xs, *weights, p + 1, dest3, x1)
        else:
            out_p, xs_next = _experts(chunk_e[p], n_active[p], xs, *weights), None
        outs.append(out_p)
        xs = xs_next
    y = _combine(dest3, route_f[0:2].T, x1, outs)
    return y.reshape(batch, seq, D_MODEL)


def kernel(x, norm1_g, w_in, diff_q_g, diff_k_g, lambda_q1, lambda_k1, lambda_q2, lambda_k2, diff_sub_g,
           moba_q_g, moba_k_g, rel_bias, w_out, norm2_g, router_group, router_expert, w_gate, w_up, w_down):
    assert x.shape[1] % PROJ_ROWS == 0 and x.shape[2] == D_MODEL and norm1_g.shape[0] == 1
    return _layer(x, norm1_g[0], w_in[0], diff_q_g[0], diff_k_g[0], lambda_q1[0], lambda_k1[0], lambda_q2[0],
                  lambda_k2[0], diff_sub_g[0], moba_q_g[0], moba_k_g[0], rel_bias, w_out[0], norm2_g[0],
                  router_group[0], router_expert[0], w_gate[0], w_up[0], w_down[0])
```

```python
import functools
import math

import numpy as np
import jax
import jax.numpy as jnp
from jax import lax
from jax.experimental import pallas as pl
from jax.experimental.pallas import tpu as pltpu

D_MODEL = 1024
HEAD_DIM = 64
HEADS_DIFF = 4
HEADS_MOBA = 8
N_HEADS_TOTAL = HEADS_DIFF + HEADS_MOBA
DIFF_WIDTH = HEADS_DIFF * 2 * HEAD_DIM
MOBA_WIDTH = HEADS_MOBA * HEAD_DIM
QK_WIDTH = 2 * DIFF_WIDTH + 2 * MOBA_WIDTH
V_WIDTH = DIFF_WIDTH + MOBA_WIDTH
MOBA_TOPK = 3
NUM_BUCKETS = 32
MAX_DISTANCE = 2048
N_GROUPS = 4
EXPERTS_PER_GROUP = 8
N_EXPERTS = N_GROUPS * EXPERTS_PER_GROUP
EXPERTS_PER_TOKEN = 2
EXPERT_HIDDEN = 256
RMS_EPS = 1e-6
NEG_INF = -1e30
LAMBDA_INIT = 0.8 - 0.6 * math.exp(-0.3 * 0)

LANES = 128
ROW_TILE = D_MODEL // LANES
ATT_TILE = 256
ONES_ROWS = 16
LOGIT_LIMIT = 80.0
ROUNDING_MARGIN = 1.05
LOG2E = math.log2(math.e)
PROJ_ROWS = 1024
OUT_ROWS = 512
ROUTE_TILE = 256
MOE_CHUNK = 512
MOE_TOKEN_PARTS = 2
EXPERT_ROW_PARTS = 4
MOVE_ROWS = 256
ROUTER_ROWS = 40
VMEM_LIMIT = 48 * 1024 * 1024

F32 = jnp.float32
BF16 = jnp.bfloat16
_NT = (((1,), (1,)), ((), ()))


def _t5_thresholds():
    n = np.arange(0, 1 << 16)
    max_exact = NUM_BUCKETS // 2
    nf = np.maximum(n, 1).astype(np.float32)
    large = max_exact + (np.log(nf / np.float32(max_exact)) / np.float32(math.log(MAX_DISTANCE / max_exact))
                         * np.float32(NUM_BUCKETS - max_exact)).astype(np.int32)
    bucket = np.where(n < max_exact, n, np.minimum(large, NUM_BUCKETS - 1))
    return [int(np.searchsorted(bucket, b, side="left")) for b in range(1, NUM_BUCKETS)]


_T5_THRESHOLDS = _t5_thresholds()


def _params(*sem):
    return pltpu.CompilerParams(dimension_semantics=sem, vmem_limit_bytes=VMEM_LIMIT)


def _bias_kernel(tab_ref, out_ref):
    h = pl.program_id(0)
    kj = lax.broadcasted_iota(jnp.int32, (ATT_TILE, ATT_TILE), 0)
    qi = lax.broadcasted_iota(jnp.int32, (ATT_TILE, ATT_TILE), 1)
    for d in range(out_ref.shape[0]):
        dist = d * ATT_TILE + qi - kj
        lo, hi = d * ATT_TILE - (ATT_TILE - 1), d * ATT_TILE + (ATT_TILE - 1)
        base = sum(thr <= max(lo, 0) for thr in _T5_THRESHOLDS)
        val = jnp.full((ATT_TILE, ATT_TILE), tab_ref[base, h], F32)
        for b, thr in enumerate(_T5_THRESHOLDS, start=1):
            if max(lo, 0) < thr <= hi:
                val = jnp.where(dist >= thr, tab_ref[b, h], val)
        val = val * LOG2E
        out_ref[d] = jnp.where(dist < 0, NEG_INF, val) if lo < 0 else val


def _bias_tiles(rel_bias, n_diag):
    return pl.pallas_call(
        _bias_kernel,
        grid=(N_HEADS_TOTAL,),
        in_specs=[pl.BlockSpec(memory_space=pltpu.SMEM)],
        out_specs=pl.BlockSpec((None, n_diag, ATT_TILE, ATT_TILE), lambda h: (h, 0, 0, 0)),
        out_shape=jax.ShapeDtypeStruct((N_HEADS_TOTAL, n_diag, ATT_TILE, ATT_TILE), F32),
        compiler_params=_params("arbitrary"),
        name="bias_tiles",
    )(rel_bias)


def _in_proj_kernel(x_ref, g1_ref, w_ref, pg_ref, grp_ref, qk_ref, vt_ref):
    n_row_tiles = PROJ_ROWS // ATT_TILE
    h = []
    for t in range(n_row_tiles):
        x = x_ref[t * ATT_TILE:(t + 1) * ATT_TILE, :]
        ms = jnp.mean(x * x, axis=-1, keepdims=True)
        h.append((x * lax.rsqrt(ms + RMS_EPS) * g1_ref[...]).astype(BF16))

    cw = 2 * LANES

    def finish(c, t, p):
        rows = slice(t * ATT_TILE, (t + 1) * ATT_TILE)
        cols = slice(c * cw, (c + 1) * cw)
        if c < QK_WIDTH // cw:
            msq = jnp.dot((p * p).astype(BF16), grp_ref[...], preferred_element_type=F32)
            qk_ref[rows, cols] = (p * lax.rsqrt(msq + RMS_EPS) * pg_ref[:, cols]).astype(BF16)
        else:
            vt_ref[t, c * cw - QK_WIDTH:(c + 1) * cw - QK_WIDTH, :] = p.T.astype(BF16)

    units = [(c, t) for c in range((QK_WIDTH + V_WIDTH) // cw) for t in range(n_row_tiles)]
    pending = None
    for c, t in units:
        p = jnp.dot(h[t], w_ref[:, c * cw:(c + 1) * cw], preferred_element_type=F32)
        if pending is not None:
            finish(*pending)
        pending = (c, t, p)
    finish(*pending)


def _in_proj(x2, g1, w_perm, post_gain, grp, batch, seq):
    n = x2.shape[0]
    steps_per_seq = seq // PROJ_ROWS
    tiles_per_step = PROJ_ROWS // ATT_TILE
    return pl.pallas_call(
        _in_proj_kernel,
        grid=(n // PROJ_ROWS,),
        in_specs=[
            pl.BlockSpec((PROJ_ROWS, D_MODEL), lambda i: (i, 0)),
            pl.BlockSpec((1, D_MODEL), lambda i: (0, 0)),
            pl.BlockSpec((D_MODEL, QK_WIDTH + V_WIDTH), lambda i: (0, 0)),
            pl.BlockSpec((1, QK_WIDTH), lambda i: (0, 0)),
            pl.BlockSpec((2 * LANES, 2 * LANES), lambda i: (0, 0)),
        ],
        out_specs=[
            pl.BlockSpec((PROJ_ROWS, QK_WIDTH), lambda i: (i, 0)),
            pl.BlockSpec((None, tiles_per_step, V_WIDTH, ATT_TILE),
                         lambda i: (i // steps_per_seq, i % steps_per_seq, 0, 0)),
        ],
        out_shape=[
            jax.ShapeDtypeStruct((n, QK_WIDTH), BF16),
            jax.ShapeDtypeStruct((batch, seq // ATT_TILE, V_WIDTH, ATT_TILE), BF16),
        ],
        compiler_params=_params("arbitrary"),
        name="in_proj",
    )(x2, g1, w_perm, post_gain, grp)


def _flash_tiles(qi, n_chains, logits_fn, values_fn, bias_fn, mask_fn, s_ref, p_ref, acc_ref):
    for c in range(n_chains):
        s_ref[0, c] = logits_fn(c, qi)
    p_ref[...] = jnp.zeros_like(p_ref)
    acc_ref[...] = jnp.zeros_like(acc_ref)

    def add_values(c, kv, alpha):
        acc_ref[c] = acc_ref[c] * alpha + jnp.dot(values_fn(c, kv), p_ref[c], preferred_element_type=F32)

    def step(j, carry):
        cur = lax.rem(j, 2)
        kv = qi - j
        out = []
        for c in range(n_chains):
            add_values(c, jnp.minimum(kv + 1, qi), carry[2 * c + 1])
        for c in range(n_chains):
            s = s_ref[cur, c] + bias_fn(c, j)
            tile_max = jnp.max(s, axis=0, keepdims=True)
            shift = mask_fn(c, kv)
            if shift is not None:
                tile_max = tile_max + shift
            m_new = jnp.maximum(carry[2 * c], tile_max)
            alpha = jnp.exp2(carry[2 * c] - m_new)
            p_ref[c] = jnp.exp2(s - (m_new if shift is None else m_new - shift)).astype(BF16)
            out += [m_new, alpha]
        for c in range(n_chains):
            s_ref[1 - cur, c] = logits_fn(c, jnp.maximum(kv - 1, 0))
        return tuple(out)

    m0 = jnp.full((1, ATT_TILE), NEG_INF, F32)
    one = jnp.ones((1, ATT_TILE), F32)
    carry = lax.fori_loop(0, qi + 1, step, (m0, one) * n_chains)
    for c in range(n_chains):
        add_values(c, 0, carry[2 * c + 1])


def _bounded_tiles(n_tiles, n_chains, logits_fn, values_fn, bias_fn, keep_fn, finish_fn, p_ref, acc_ref,
                   between_fn=lambda d: None):
    for d in range(n_tiles):
        for qi in range(d, n_tiles):
            for c in range(n_chains):
                p_ref[qi, c] = jnp.exp2(logits_fn(c, qi, qi - d) + bias_fn(c, d)).astype(BF16)
        between_fn(d)
        for qi in range(d, n_tiles):
            kv = qi - d
            for c in range(n_chains):
                pv = jnp.dot(values_fn(c, kv), p_ref[qi, c], preferred_element_type=F32)
                keep = keep_fn(c, qi, kv)
                if keep is not None:
                    pv = pv * keep
                acc_ref[qi, c] = pv if d == 0 else acc_ref[qi, c] + pv
        finish_fn(d)


def _logits_bounded(q_gain, k_gain, bias_cols):
    bound = (HEAD_DIM * jnp.max(jnp.abs(q_gain)) * jnp.max(jnp.abs(k_gain)) * ROUNDING_MARGIN
             + jnp.max(jnp.abs(bias_cols)) * LOG2E)
    return (bound <= LOGIT_LIMIT).astype(jnp.int32).reshape(1)


def _with_ones_rows(vt):
    return jnp.concatenate([vt, jnp.ones((ONES_ROWS, vt.shape[1]), vt.dtype)], axis=0)


def _attention_scratch(n_tiles, value_rows, n_chains=2):
    return [pltpu.VMEM((n_tiles, n_chains, ATT_TILE, LANES), BF16),
            pltpu.VMEM((2, n_chains, ATT_TILE, ATT_TILE), F32),
            pltpu.VMEM((n_tiles, n_chains, ATT_TILE, ATT_TILE), BF16),
            pltpu.VMEM((n_tiles, n_chains, value_rows + ONES_ROWS, ATT_TILE), F32)]


def _half_lane_split(q):
    lane = lax.broadcasted_iota(jnp.int32, q.shape, 1)
    zero = jnp.zeros_like(q)
    return jnp.where(lane < HEAD_DIM, q, zero), jnp.where(lane >= HEAD_DIM, q, zero)


def _diff_kernel(bounded_ref, lq1_ref, lk1_ref, lq2_ref, lk2_ref, subg_ref, q_ref, k_ref, vt_ref, bias_ref, o_ref,
                 qm_ref, s_ref, p_ref, acc_ref):
    n_tiles = q_ref.shape[0]
    chains = dict(n_chains=2, values_fn=lambda c, kv: _with_ones_rows(vt_ref[kv]))
    width = 2 * HEAD_DIM

    def prepare():
        for qi in range(n_tiles):
            qm_ref[qi, 0], qm_ref[qi, 1] = _half_lane_split(q_ref[qi])

    def finish(qi):
        lam = (jnp.exp(jnp.sum(lq1_ref[...] * lk1_ref[...], keepdims=True))
               - jnp.exp(jnp.sum(lq2_ref[...] * lk2_ref[...], keepdims=True)) + LAMBDA_INIT)
        a1, a2 = acc_ref[qi, 0], acc_ref[qi, 1]
        o = a1[:width] / a1[width:width + 1] - lam * (a2[:width] / a2[width:width + 1])
        ms = jnp.mean(o * o, axis=0, keepdims=True)
        o = o * lax.rsqrt(ms + RMS_EPS) * subg_ref[...] * (1.0 - LAMBDA_INIT)
        o_ref[qi] = o.T.astype(BF16)

    bounded = bounded_ref[0] > 0

    @pl.when(bounded)
    def _():
        prepare()
        _bounded_tiles(
            n_tiles,
            logits_fn=lambda c, qi, kv: lax.dot_general(k_ref[kv], qm_ref[qi, c], _NT, preferred_element_type=F32),
            bias_fn=lambda c, d: bias_ref[d],
            keep_fn=lambda c, qi, kv: None,
            finish_fn=finish,
            p_ref=p_ref, acc_ref=acc_ref, **chains)

    @pl.when(jnp.logical_not(bounded))
    def _():
        prepare()

        def q_tile(qi, carry):
            _flash_tiles(
                qi,
                logits_fn=lambda c, kv: lax.dot_general(k_ref[kv], qm_ref[qi, c], _NT, preferred_element_type=F32),
                bias_fn=lambda c, j: bias_ref[j],
                mask_fn=lambda c, kv: None,
                s_ref=s_ref, p_ref=p_ref.at[0], acc_ref=acc_ref.at[qi], **chains)
            return carry
        lax.fori_loop(0, n_tiles, q_tile, 0)
        for qi in range(n_tiles):
            finish(qi)


def _diff_attention(bounded, lam_vecs, sub_g_col, qk4, vt4, bias):
    batch, n_tiles = qk4.shape[0], qk4.shape[1]
    k_col0 = DIFF_WIDTH // LANES
    vec = pl.BlockSpec((1, HEAD_DIM), lambda h, b: (0, 0))
    return pl.pallas_call(
        _diff_kernel,
        grid=(HEADS_DIFF, batch),
        in_specs=[
            pl.BlockSpec(memory_space=pltpu.SMEM),
            vec, vec, vec, vec,
            pl.BlockSpec((2 * HEAD_DIM, 1), lambda h, b: (0, 0)),
            pl.BlockSpec((None, n_tiles, ATT_TILE, LANES), lambda h, b: (b, 0, 0, h)),
            pl.BlockSpec((None, n_tiles, ATT_TILE, LANES), lambda h, b: (b, 0, 0, k_col0 + h)),
            pl.BlockSpec((None, n_tiles, LANES, ATT_TILE), lambda h, b: (b, 0, h, 0)),
            pl.BlockSpec((None, n_tiles, ATT_TILE, ATT_TILE), lambda h, b: (h, 0, 0, 0)),
        ],
        out_specs=pl.BlockSpec((None, n_tiles, ATT_TILE, LANES), lambda h, b: (b, 0, 0, h)),
        out_shape=jax.ShapeDtypeStruct((batch, n_tiles, ATT_TILE, DIFF_WIDTH), BF16),
        scratch_shapes=_attention_scratch(n_tiles, 2 * HEAD_DIM),
        compiler_params=_params("arbitrary", "arbitrary"),
        name="diff_attention",
    )(bounded, *lam_vecs, sub_g_col, qk4, qk4, vt4, bias)


def _split3(v):
    hi = v.astype(BF16)
    r1 = v - hi.astype(F32)
    mid = r1.astype(BF16)
    lo = (r1 - mid.astype(F32)).astype(BF16)
    return hi, mid, lo


def _block_mask(gate, own):
    row = lax.broadcasted_iota(jnp.int32, gate.shape, 0)
    rank = jnp.zeros(gate.shape, jnp.int32)
    for m in range(own):
        gm = gate[m:m + 1, :]
        beats = (gm > gate) | ((gm == gate) & (row > m))
        rank = rank + jnp.where(beats, 1, 0)
    keep = ((rank < MOBA_TOPK) & (row < own)) | (row == own)
    return jnp.where(keep, 0.0, NEG_INF).astype(F32)


def _moba_kernel(bounded_ref, q_ref, k_ref, vt_ref, bias_ref, o_ref, kmean_ref, mask_ref, qm_ref, s_ref, p_ref,
                 acc_ref):
    n_tiles = q_ref.shape[0]

    def split_queries():
        for qi in range(n_tiles):
            qm_ref[qi, 0], qm_ref[qi, 1] = _half_lane_split(q_ref[qi])

    def block_masks(q_tiles):
        for n in range(n_tiles):
            kmean_ref[n:n + 1, :] = jnp.mean(k_ref[n].astype(F32), axis=0, keepdims=True)
        per_head = zip(*[_half_lane_split(term) for term in _split3(kmean_ref[...])])
        gate_lhs = jnp.concatenate([term for head_terms in per_head for term in head_terms], axis=0)
        n_terms = gate_lhs.shape[0] // (2 * n_tiles)
        for qi in q_tiles:
            terms = lax.dot_general(gate_lhs, q_ref[qi], _NT, preferred_element_type=F32)
            for half in range(2):
                rows = [terms[(half * n_terms + t) * n_tiles:(half * n_terms + t + 1) * n_tiles]
                        for t in range(n_terms)]
                mask_ref[qi, half] = _block_mask(sum(rows[1:], rows[0]), qi)

    def keep_row(c, qi, kv):
        if qi <= MOBA_TOPK or kv == qi:
            return None
        return jnp.where(mask_ref[qi, c, kv:kv + 1, :] < -1.0, 0.0, 1.0)

    def finish(qi):
        halves = [acc_ref[qi, half] for half in range(2)]
        o = jnp.concatenate([a[:HEAD_DIM] / a[HEAD_DIM:HEAD_DIM + 1] for a in halves], axis=0)
        o_ref[qi] = o.T.astype(BF16)

    chains = dict(
        n_chains=2,
        values_fn=lambda c, kv: _with_ones_rows(vt_ref[kv, c * HEAD_DIM:(c + 1) * HEAD_DIM, :]))
    bounded = bounded_ref[0] > 0

    @pl.when(bounded)
    def _():
        split_queries()
        _bounded_tiles(
            n_tiles,
            logits_fn=lambda c, qi, kv: lax.dot_general(k_ref[kv], qm_ref[qi, c], _NT, preferred_element_type=F32),
            bias_fn=lambda c, d: bias_ref[c, d],
            keep_fn=keep_row,
            between_fn=lambda d: block_masks(range(MOBA_TOPK + 1, n_tiles)) if d == 0 else None,
            finish_fn=finish,
            p_ref=p_ref, acc_ref=acc_ref, **chains)

    @pl.when(jnp.logical_not(bounded))
    def _():
        split_queries()
        block_masks(range(n_tiles))

        def q_tile(qi, carry):
            _flash_tiles(
                qi,
                logits_fn=lambda c, kv: lax.dot_general(k_ref[kv], qm_ref[qi, c], _NT, preferred_element_type=F32),
                bias_fn=lambda c, j: bias_ref[c, j],
                mask_fn=lambda c, kv: mask_ref[qi, c, pl.ds(kv, 1), :],
                s_ref=s_ref, p_ref=p_ref.at[0], acc_ref=acc_ref.at[qi], **chains)
            return carry
        lax.fori_loop(0, n_tiles, q_tile, 0)
        for qi in range(n_tiles):
            finish(qi)


def _moba_attention(bounded, qk4, vt4, bias_pairs):
    batch, n_tiles = qk4.shape[0], qk4.shape[1]
    q_col0 = 2 * DIFF_WIDTH // LANES
    k_col0 = q_col0 + MOBA_WIDTH // LANES
    v_row0 = DIFF_WIDTH // LANES
    pair0 = HEADS_DIFF // 2
    return pl.pallas_call(
        _moba_kernel,
        grid=(HEADS_MOBA // 2, batch),
        in_specs=[
            pl.BlockSpec(memory_space=pltpu.SMEM),
            pl.BlockSpec((None, n_tiles, ATT_TILE, LANES), lambda h, b: (b, 0, 0, q_col0 + h)),
            pl.BlockSpec((None, n_tiles, ATT_TILE, LANES), lambda h, b: (b, 0, 0, k_col0 + h)),
            pl.BlockSpec((None, n_tiles, LANES, ATT_TILE), lambda h, b: (b, 0, v_row0 + h, 0)),
            pl.BlockSpec((None, 2, n_tiles, ATT_TILE, ATT_TILE), lambda h, b: (pair0 + h, 0, 0, 0, 0)),
        ],
        out_specs=pl.BlockSpec((None, n_tiles, ATT_TILE, LANES), lambda h, b: (b, 0, 0, h)),
        out_shape=jax.ShapeDtypeStruct((batch, n_tiles, ATT_TILE, MOBA_WIDTH), BF16),
        scratch_shapes=[
            pltpu.VMEM((n_tiles, LANES), F32),
            pltpu.VMEM((n_tiles, 2, n_tiles, ATT_TILE), F32),
        ] + _attention_scratch(n_tiles, HEAD_DIM),
        compiler_params=_params("arbitrary", "arbitrary"),
        name="moba_attention",
    )(bounded, qk4, qk4, vt4, bias_pairs)


def _load_token_rows(ref, n_rows, lead=(), first=0):
    chunks = [ref[lead + (pl.ds(first * ROW_TILE + c, n_rows, stride=ROW_TILE), slice(None))]
              for c in range(ROW_TILE)]
    return jnp.concatenate(chunks, axis=1)


def _store_token_rows(ref, value, first=0):
    n_rows = value.shape[0]
    for c in range(ROW_TILE):
        ref[pl.ds(first * ROW_TILE + c, n_rows, stride=ROW_TILE), :] = value[:, c * LANES:(c + 1) * LANES]


def _token_tile(ref, t):
    return ref.at[pl.ds(pl.multiple_of(t * ROW_TILE, ROW_TILE), ROW_TILE)]


def _first_argmax(v):
    top = jnp.max(v, axis=0, keepdims=True)
    row = lax.broadcasted_iota(jnp.int32, v.shape, 0)
    idx = jnp.min(jnp.where(v == top, row, v.shape[0]), axis=0, keepdims=True)
    return top, idx


def _out_proj_kernel(yd_ref, ym_ref, x_ref, wd_ref, wm_ref, g2_ref, wr_ref, tri_ref,
                     x1_ref, ri_ref, rf_ref, cnt_ref, run_ref):
    @pl.when(lax.rem(pl.program_id(0), pl.num_programs(0) // MOE_TOKEN_PARTS) == 0)
    def _():
        run_ref[...] = jnp.zeros_like(run_ref)

    tiles = range(OUT_ROWS // ROUTE_TILE)
    h2 = []
    for t in tiles:
        rows = slice(t * ROUTE_TILE, (t + 1) * ROUTE_TILE)
        x1 = (x_ref[rows, :] + jnp.dot(yd_ref[rows, :], wd_ref[...], preferred_element_type=F32)
              + jnp.dot(ym_ref[rows, :], wm_ref[...], preferred_element_type=F32))
        _store_token_rows(x1_ref, x1, first=t * ROUTE_TILE)
        ms = jnp.mean(x1 * x1, axis=-1, keepdims=True)
        h2.append((x1 * lax.rsqrt(ms + RMS_EPS) * g2_ref[...]).astype(BF16))
    logits = [lax.dot_general(wr_ref[...], h, _NT, preferred_element_type=F32) for h in h2]

    picks = []
    for lg in logits:
        g_logits = lg[N_EXPERTS:N_EXPERTS + N_GROUPS, :]
        g_top, g_idx = _first_argmax(g_logits)
        p_group = 1.0 / jnp.sum(jnp.exp(g_logits - g_top), axis=0, keepdims=True)
        e_logits = lg[0:EXPERTS_PER_GROUP, :]
        for g in range(1, N_GROUPS):
            e_logits = jnp.where(g_idx == g, lg[g * EXPERTS_PER_GROUP:(g + 1) * EXPERTS_PER_GROUP, :], e_logits)
        v1, i1 = _first_argmax(e_logits)
        row = lax.broadcasted_iota(jnp.int32, e_logits.shape, 0)
        v2, i2 = _first_argmax(jnp.where(row == i1, -jnp.inf, e_logits))
        ratio = jnp.exp(v2 - v1)
        w1 = p_group / (1.0 + ratio)
        w2 = p_group * ratio / (1.0 + ratio)
        e1 = g_idx * EXPERTS_PER_GROUP + i1
        e2 = g_idx * EXPERTS_PER_GROUP + i2
        erow = lax.broadcasted_iota(jnp.int32, (N_EXPERTS, ROUTE_TILE), 0)
        hit1 = erow == e1
        hit2 = erow == e2
        onehot = jnp.where(hit1 | hit2, 1.0, 0.0).astype(F32)
        picks.append((e1, e2, w1, w2, hit1, hit2, onehot))

    prefix = [jnp.dot(p[-1].astype(BF16), tri_ref[...], preferred_element_type=F32) for p in picks]
    out_row = lax.broadcasted_iota(jnp.int32, (8, ROUTE_TILE), 0)
    for t in tiles:
        e1, e2, w1, w2, hit1, hit2, onehot = picks[t]
        lanes = slice(t * ROUTE_TILE, (t + 1) * ROUTE_TILE)
        before = prefix[t] + run_ref[...]
        r1 = jnp.sum(jnp.where(hit1, before, 0.0), axis=0, keepdims=True).astype(jnp.int32)
        r2 = jnp.sum(jnp.where(hit2, before, 0.0), axis=0, keepdims=True).astype(jnp.int32)
        run_ref[...] = run_ref[...] + jnp.sum(onehot, axis=1, keepdims=True)
        ri_ref[:, lanes] = jnp.where(out_row == 0, e1, jnp.where(out_row == 1, e2,
                                     jnp.where(out_row == 2, r1, jnp.where(out_row == 3, r2, 0))))
        rf_ref[:, lanes] = jnp.where(out_row == 0, w1, jnp.where(out_row == 1, w2, 0.0))
    cnt_ref[...] = jnp.broadcast_to(run_ref[...], cnt_ref.shape).astype(jnp.int32)


def _out_proj(y_d, y_m, x2, wo_d, wo_m, g2, w_router, tri):
    n = x2.shape[0]
    const = lambda i: (0, 0)
    steps_per_part = n // OUT_ROWS // MOE_TOKEN_PARTS
    return pl.pallas_call(
        _out_proj_kernel,
        grid=(n // OUT_ROWS,),
        in_specs=[
            pl.BlockSpec((OUT_ROWS, DIFF_WIDTH), lambda i: (i, 0)),
            pl.BlockSpec((OUT_ROWS, MOBA_WIDTH), lambda i: (i, 0)),
            pl.BlockSpec((OUT_ROWS, D_MODEL), lambda i: (i, 0)),
            pl.BlockSpec((DIFF_WIDTH, D_MODEL), const),
            pl.BlockSpec((MOBA_WIDTH, D_MODEL), const),
            pl.BlockSpec((1, D_MODEL), const),
            pl.BlockSpec((ROUTER_ROWS, D_MODEL), const),
            pl.BlockSpec((ROUTE_TILE, ROUTE_TILE), const),
        ],
        out_specs=[
            pl.BlockSpec((OUT_ROWS * ROW_TILE, LANES), lambda i: (i, 0)),
            pl.BlockSpec((8, OUT_ROWS), lambda i: (0, i)),
            pl.BlockSpec((8, OUT_ROWS), lambda i: (0, i)),
            pl.BlockSpec((None, N_EXPERTS, LANES), lambda i: (i // steps_per_part, 0, 0)),
        ],
        out_shape=[
            jax.ShapeDtypeStruct((n * ROW_TILE, LANES), F32),
            jax.ShapeDtypeStruct((8, n), jnp.int32),
            jax.ShapeDtypeStruct((8, n), F32),
            jax.ShapeDtypeStruct((MOE_TOKEN_PARTS, N_EXPERTS, LANES), jnp.int32),
        ],
        scratch_shapes=[pltpu.VMEM((N_EXPERTS, 1), F32)],
        compiler_params=_params("arbitrary"),
        name="out_proj_route",
    )(y_d, y_m, x2, wo_d, wo_m, g2, w_router, tri)


def _dispatch_step(i, n_steps, first_tile, zchunk_ref, dest_ref, x1_hbm, xs_hbm, zero_ref, stage_ref, zero_sem,
                   load_sem, row_sem):
    last = n_steps - 1
    slot = lax.rem(i, 2)
    chunk_rows = MOE_CHUNK * ROW_TILE
    tile_rows = MOVE_ROWS * ROW_TILE

    def load(t, s):
        start = pl.multiple_of((first_tile + t) * tile_rows, tile_rows)
        return pltpu.make_async_copy(x1_hbm.at[pl.ds(start, tile_rows)], stage_ref.at[s], load_sem.at[s])

    def wait_rows(s):
        for _ in range(EXPERTS_PER_TOKEN):
            pltpu.make_async_copy(stage_ref.at[s], xs_hbm.at[pl.ds(0, tile_rows)], row_sem.at[s]).wait()

    def zero_copy(c):
        start = pl.multiple_of(c * chunk_rows, chunk_rows)
        return pltpu.make_async_copy(zero_ref, xs_hbm.at[pl.ds(start, chunk_rows)], zero_sem)

    @pl.when(i == 0)
    def _():
        zero_ref[...] = jnp.zeros_like(zero_ref)

        def start_one(c, carry):
            @pl.when(zchunk_ref[c] > 0)
            def _():
                zero_copy(c).start()
            return carry

        def wait_one(c, carry):
            @pl.when(zchunk_ref[c] > 0)
            def _():
                zero_copy(c).wait()
            return carry

        lax.fori_loop(0, zchunk_ref.shape[0], start_one, 0)
        lax.fori_loop(0, zchunk_ref.shape[0], wait_one, 0)
        load(0, 0).start()

    @pl.when(i > 0)
    def _():
        wait_rows(1 - slot)

    @pl.when(i < last)
    def _():
        load(i + 1, 1 - slot).start()

    load(i, slot).wait()

    def send_row(r, carry):
        src = _token_tile(stage_ref.at[slot], r)
        for k in range(EXPERTS_PER_TOKEN):
            pltpu.make_async_copy(src, _token_tile(xs_hbm, dest_ref[0, k, r]), row_sem.at[slot]).start()
        return carry

    lax.fori_loop(0, MOVE_ROWS, send_row, 0, unroll=8)

    @pl.when(i == last)
    def _():
        wait_rows(slot)


def _dispatch_scratch():
    return [pltpu.VMEM((MOE_CHUNK * ROW_TILE, LANES), F32),
            pltpu.VMEM((2, MOVE_ROWS * ROW_TILE, LANES), F32),
            pltpu.SemaphoreType.DMA(()),
            pltpu.SemaphoreType.DMA((2,)),
            pltpu.SemaphoreType.DMA((2,))]


def _dispatch_kernel(first_tile, zchunk_ref, dest_ref, x1_hbm, xs_hbm, *scratch):
    _dispatch_step(pl.program_id(0), pl.num_programs(0), first_tile, zchunk_ref, dest_ref, x1_hbm, xs_hbm, *scratch)


def _dispatch(part, zchunk, dest3, x1, n_slots):
    tiles = x1.shape[0] // ROW_TILE // MOVE_ROWS // MOE_TOKEN_PARTS
    first_tile = part * tiles
    grid_spec = pltpu.PrefetchScalarGridSpec(
        num_scalar_prefetch=1,
        grid=(tiles,),
        in_specs=[
            pl.BlockSpec((1, EXPERTS_PER_TOKEN, MOVE_ROWS), lambda i, zc: (first_tile + i, 0, 0),
                         memory_space=pltpu.SMEM),
            pl.BlockSpec(memory_space=pl.ANY),
        ],
        out_specs=pl.BlockSpec(memory_space=pl.ANY),
        scratch_shapes=_dispatch_scratch(),
    )
    return pl.pallas_call(
        functools.partial(_dispatch_kernel, first_tile),
        grid_spec=grid_spec,
        out_shape=jax.ShapeDtypeStruct((n_slots * ROW_TILE, LANES), F32),
        compiler_params=_params("arbitrary"),
        name="moe_dispatch",
    )(zchunk, dest3, x1)


def _expert_step(c, na_ref, xs_ref, g2_ref, wg_ref, wu_ref, wd_ref, o_ref):
    active = c < na_ref[0]

    @pl.when(jnp.logical_not(active))
    def _():
        o_ref[...] = jnp.zeros_like(o_ref)

    @pl.when(active)
    def _():
        part_rows = MOE_CHUNK // EXPERT_ROW_PARTS

        def gate_up(r):
            x = _load_token_rows(xs_ref, part_rows, first=r * part_rows)
            ms = jnp.mean(x * x, axis=-1, keepdims=True)
            h = (x * lax.rsqrt(ms + RMS_EPS) * g2_ref[...]).astype(BF16)
            return (jnp.dot(h, wg_ref[...], preferred_element_type=F32),
                    jnp.dot(h, wu_ref[...], preferred_element_type=F32))

        def down(r, gate, up):
            hid = (gate * jax.nn.sigmoid(gate) * up).astype(BF16)
            _store_token_rows(o_ref, jnp.dot(hid, wd_ref[...], preferred_element_type=F32), first=r * part_rows)

        pending = None
        for r in range(EXPERT_ROW_PARTS):
            current = (r,) + gate_up(r)
            if pending is not None:
                down(*pending)
            pending = current
        down(*pending)


def _expert_kernel(ce_ref, na_ref, *refs):
    _expert_step(pl.program_id(0), na_ref, *refs)


def _expert_specs():
    rows = lambda c, ce, na, *_: (jnp.minimum(c, na[0] - 1), 0)
    expert = lambda c, ce, na, *_: (ce[c], 0, 0)
    in_specs = [
        pl.BlockSpec((MOE_CHUNK * ROW_TILE, LANES), rows),
        pl.BlockSpec((1, D_MODEL), lambda c, *_: (0, 0)),
        pl.BlockSpec((None, D_MODEL, EXPERT_HIDDEN), expert),
        pl.BlockSpec((None, D_MODEL, EXPERT_HIDDEN), expert),
        pl.BlockSpec((None, EXPERT_HIDDEN, D_MODEL), expert),
    ]
    return in_specs, pl.BlockSpec((MOE_CHUNK * ROW_TILE, LANES), lambda c, *_: (c, 0))


def _experts(chunk_e, n_active, xs, g2, wg, wu, wd):
    n_slots = xs.shape[0] // ROW_TILE
    in_specs, out_spec = _expert_specs()
    grid_spec = pltpu.PrefetchScalarGridSpec(
        num_scalar_prefetch=2, grid=(n_slots // MOE_CHUNK,), in_specs=in_specs, out_specs=out_spec)
    return pl.pallas_call(
        _expert_kernel,
        grid_spec=grid_spec,
        out_shape=jax.ShapeDtypeStruct((n_slots * ROW_TILE, LANES), F32),
        compiler_params=_params("arbitrary"),
        name="moe_experts",
    )(chunk_e, n_active, xs, g2, wg, wu, wd)


def _experts_dispatch_kernel(first_tile, n_tiles, ce_ref, na_ref, zchunk_ref,
                             xs_ref, g2_ref, wg_ref, wu_ref, wd_ref, dest_ref, x1_hbm,
                             o_ref, xs_next_hbm, *scratch):
    i = pl.program_id(0)

    @pl.when(i < n_tiles)
    def _():
        _dispatch_step(i, n_tiles, first_tile, zchunk_ref, dest_ref, x1_hbm, xs_next_hbm, *scratch)

    _expert_step(i, na_ref, xs_ref, g2_ref, wg_ref, wu_ref, wd_ref, o_ref)


def _experts_dispatch(chunk_e, n_active, zchunk_next, xs, g2, wg, wu, wd, next_part, dest3, x1):
    n_slots = xs.shape[0] // ROW_TILE
    n_chunks = n_slots // MOE_CHUNK
    tiles = x1.shape[0] // ROW_TILE // MOVE_ROWS // MOE_TOKEN_PARTS
    first_tile = next_part * tiles
    assert n_chunks >= tiles
    in_specs, out_spec = _expert_specs()
    in_specs += [
        pl.BlockSpec((1, EXPERTS_PER_TOKEN, MOVE_ROWS),
                     lambda c, *_: (first_tile + jnp.minimum(c, tiles - 1), 0, 0), memory_space=pltpu.SMEM),
        pl.BlockSpec(memory_space=pl.ANY),
    ]
    grid_spec = pltpu.PrefetchScalarGridSpec(
        num_scalar_prefetch=3, grid=(n_chunks,), in_specs=in_specs,
        out_specs=[out_spec, pl.BlockSpec(memory_space=pl.ANY)],
        scratch_shapes=_dispatch_scratch())
    slots = jax.ShapeDtypeStruct((n_slots * ROW_TILE, LANES), F32)
    return pl.pallas_call(
        functools.partial(_experts_dispatch_kernel, first_tile, tiles),
        grid_spec=grid_spec,
        out_shape=[slots, slots],
        compiler_params=_params("arbitrary"),
        name="moe_experts_dispatch",
    )(chunk_e, n_active, zchunk_next, xs, g2, wg, wu, wd, dest3, x1)


def _combine_kernel(dest_ref, dest_next_ref, w_ref, x1_ref, *rest):
    outs_hbm, (y_ref, buf_ref, sem_ref) = rest[:MOE_TOKEN_PARTS], rest[MOE_TOKEN_PARTS:]
    i = pl.program_id(0)
    slot = lax.rem(i, 2)
    tiles_per_part = pl.num_programs(0) // MOE_TOKEN_PARTS

    def fetch(tile, dref, s):
        for part, src_hbm in enumerate(outs_hbm):
            @pl.when(tile // tiles_per_part == part)
            def _():
                def fetch_row(r, carry):
                    for k in range(EXPERTS_PER_TOKEN):
                        pltpu.make_async_copy(_token_tile(src_hbm, dref[0, k, r]),
                                              _token_tile(buf_ref.at[s, k], r), sem_ref.at[s]).start()
                    return carry
                lax.fori_loop(0, MOVE_ROWS, fetch_row, 0, unroll=8)

    @pl.when(i == 0)
    def _():
        fetch(i, dest_ref, 0)

    @pl.when(i + 1 < pl.num_programs(0))
    def _():
        fetch(i + 1, dest_next_ref, 1 - slot)

    for k in range(EXPERTS_PER_TOKEN):
        pltpu.make_async_copy(outs_hbm[0].at[pl.ds(0, MOVE_ROWS * ROW_TILE)], buf_ref.at[slot, k],
                              sem_ref.at[slot]).wait()
    w = w_ref[...]
    y_ref[...] = (_load_token_rows(x1_ref, MOVE_ROWS)
                  + w[:, 0:1] * _load_token_rows(buf_ref, MOVE_ROWS, (slot, 0))
                  + w[:, 1:2] * _load_token_rows(buf_ref, MOVE_ROWS, (slot, 1)))


def _combine(dest3, wts, x1, outs_parts):
    n = x1.shape[0] // ROW_TILE
    steps = n // MOVE_ROWS
    dest_block = (1, EXPERTS_PER_TOKEN, MOVE_ROWS)
    return pl.pallas_call(
        _combine_kernel,
        grid=(steps,),
        in_specs=[
            pl.BlockSpec(dest_block, lambda i: (i, 0, 0), memory_space=pltpu.SMEM),
            pl.BlockSpec(dest_block, lambda i: (jnp.minimum(i + 1, steps - 1), 0, 0), memory_space=pltpu.SMEM),
            pl.BlockSpec((MOVE_ROWS, EXPERTS_PER_TOKEN), lambda i: (i, 0)),
            pl.BlockSpec((MOVE_ROWS * ROW_TILE, LANES), lambda i: (i, 0)),
        ] + [pl.BlockSpec(memory_space=pl.ANY)] * MOE_TOKEN_PARTS,
        out_specs=pl.BlockSpec((MOVE_ROWS, D_MODEL), lambda i: (i, 0)),
        out_shape=jax.ShapeDtypeStruct((n, D_MODEL), F32),
        scratch_shapes=[
            pltpu.VMEM((2, EXPERTS_PER_TOKEN, MOVE_ROWS * ROW_TILE, LANES), F32),
            pltpu.SemaphoreType.DMA((2,)),
        ],
        compiler_params=_params("arbitrary"),
        name="moe_combine",
    )(dest3, dest3, wts, x1, *outs_parts)


def _layer(x, norm1_g, w_in, diff_q_g, diff_k_g, lambda_q1, lambda_k1, lambda_q2, lambda_k2, diff_sub_g,
           moba_q_g, moba_k_g, rel_bias, w_out, norm2_g, router_group, router_expert, w_gate, w_up, w_down):
    batch, seq, _ = x.shape
    n = batch * seq
    n_tiles = seq // ATT_TILE
    scale = HEAD_DIM ** -0.5 * LOG2E

    d3 = 3 * DIFF_WIDTH
    w_perm = jnp.concatenate([w_in[:, :2 * DIFF_WIDTH], w_in[:, d3:d3 + 2 * MOBA_WIDTH],
                              w_in[:, 2 * DIFF_WIDTH:d3], w_in[:, d3 + 2 * MOBA_WIDTH:]], axis=1).astype(BF16)
    reps_d, reps_m = DIFF_WIDTH // HEAD_DIM, MOBA_WIDTH // HEAD_DIM
    post_gain = jnp.concatenate([jnp.tile(diff_q_g * scale, reps_d), jnp.tile(diff_k_g, reps_d),
                                 jnp.tile(moba_q_g * scale, reps_m), jnp.tile(moba_k_g, reps_m)])[None, :]
    head_of = np.arange(2 * LANES) // HEAD_DIM
    grp = jnp.asarray((head_of[:, None] == head_of[None, :]) / HEAD_DIM, BF16)

    bias = _bias_tiles(rel_bias, n_tiles)
    qk, vt4 = _in_proj(x.reshape(n, D_MODEL), norm1_g[None, :], w_perm, post_gain, grp, batch, seq)
    qk4 = qk.reshape(batch, n_tiles, ATT_TILE, QK_WIDTH)
    lam_vecs = [v[None, :] for v in (lambda_q1, lambda_k1, lambda_q2, lambda_k2)]
    y_d = _diff_attention(_logits_bounded(diff_q_g * scale, diff_k_g, rel_bias[:, :HEADS_DIFF]),
                          lam_vecs, diff_sub_g[:, None], qk4, vt4, bias)
    y_m = _moba_attention(_logits_bounded(moba_q_g * scale, moba_k_g, rel_bias[:, HEADS_DIFF:]),
                          qk4, vt4, bias.reshape(N_HEADS_TOTAL // 2, 2, n_tiles, ATT_TILE, ATT_TILE))

    w_router = jnp.concatenate([
        router_expert.transpose(0, 2, 1).reshape(N_EXPERTS, D_MODEL), router_group.T,
        jnp.zeros((ROUTER_ROWS - N_EXPERTS - N_GROUPS, D_MODEL), F32)], axis=0).astype(BF16)
    tri = jnp.asarray(np.triu(np.ones((ROUTE_TILE, ROUTE_TILE)), k=1), BF16)
    wo = w_out.astype(BF16)
    x1, route_i, route_f, cnt = _out_proj(y_d.reshape(n, DIFF_WIDTH), y_m.reshape(n, MOBA_WIDTH),
                                          x.reshape(n, D_MODEL), wo[:DIFF_WIDTH], wo[DIFF_WIDTH:],
                                          norm2_g[None, :], w_router, tri)

    parts = MOE_TOKEN_PARTS
    counts = cnt[:, :, 0]
    pad_counts = ((counts + MOE_CHUNK - 1) // MOE_CHUNK) * MOE_CHUNK
    pad_end = jnp.cumsum(pad_counts, axis=1)
    pad_start = pad_end - pad_counts
    experts = jnp.arange(N_EXPERTS, dtype=jnp.int32)
    picked = route_i[0:2].reshape(EXPERTS_PER_TOKEN, parts, n // parts)
    start_of = jnp.sum(jnp.where(picked[..., None] == experts, pad_start[None, :, None, :], 0), axis=-1)
    dest = start_of.reshape(EXPERTS_PER_TOKEN, n) + route_i[2:4]
    dest3 = dest.reshape(EXPERTS_PER_TOKEN, n // MOVE_ROWS, MOVE_ROWS).transpose(1, 0, 2)
    n_slots = n // parts * EXPERTS_PER_TOKEN + N_EXPERTS * MOE_CHUNK
    n_chunks = n_slots // MOE_CHUNK
    chunk_id = jnp.arange(n_chunks, dtype=jnp.int32)
    chunk_e = jnp.minimum(jnp.sum((pad_end[:, None, :] <= chunk_id[None, :, None] * MOE_CHUNK).astype(jnp.int32),
                                  axis=2), N_EXPERTS - 1)
    n_active = (pad_end[:, -1:] // MOE_CHUNK).astype(jnp.int32)
    next_e = jnp.concatenate([chunk_e[:, 1:], jnp.full((parts, 1), N_EXPERTS, jnp.int32)], axis=1)
    zchunk = ((chunk_id[None, :] >= n_active - 1) | (chunk_e != next_e)).astype(jnp.int32)

    weights = (norm2_g[None, :], w_gate.astype(BF16), w_up.astype(BF16), w_down.astype(BF16))
    xs = _dispatch(0, zchunk[0], dest3, x1, n_slots)
    outs = []
    for p in range(parts):
        if p + 1 < parts:
            out_p, xs_next = _experts_dispatch(chunk_e[p], n_active[p], zchunk[p + 1], xs, *weights, p + 1, dest3, x1)
        else:
            out_p, xs_next = _experts(chunk_e[p], n_active[p], xs, *weights), None
        outs.append(out_p)
        xs = xs_next
    y = _combine(dest3, route_f[0:2].T, x1, outs)
    return y.reshape(batch, seq, D_MODEL)


def kernel(x, norm1_g, w_in, diff_q_g, diff_k_g, lambda_q1, lambda_k1, lambda_q2, lambda_k2, diff_sub_g,
           moba_q_g, moba_k_g, rel_bias, w_out, norm2_g, router_group, router_expert, w_gate, w_up, w_down):
    assert x.shape[1] % PROJ_ROWS == 0 and x.shape[2] == D_MODEL and norm1_g.shape[0] == 1
    return _layer(x, norm1_g[0], w_in[0], diff_q_g[0], diff_k_g[0], lambda_q1[0], lambda_k1[0], lambda_q2[0],
                  lambda_k2[0], diff_sub_g[0], moba_q_g[0], moba_k_g[0], rel_bias, w_out[0], norm2_g[0],
                  router_group[0], router_expert[0], w_gate[0], w_up[0], w_down[0])
```

```python
import functools
import math

import numpy as np
import jax
import jax.numpy as jnp
from jax import lax
from jax.experimental import pallas as pl
from jax.experimental.pallas import tpu as pltpu

D_MODEL = 1024
HEAD_DIM = 64
HEADS_DIFF = 4
HEADS_MOBA = 8
N_HEADS_TOTAL = HEADS_DIFF + HEADS_MOBA
DIFF_WIDTH = HEADS_DIFF * 2 * HEAD_DIM
MOBA_WIDTH = HEADS_MOBA * HEAD_DIM
QK_WIDTH = 2 * DIFF_WIDTH + 2 * MOBA_WIDTH
V_WIDTH = DIFF_WIDTH + MOBA_WIDTH
MOBA_TOPK = 3
NUM_BUCKETS = 32
MAX_DISTANCE = 2048
N_GROUPS = 4
EXPERTS_PER_GROUP = 8
N_EXPERTS = N_GROUPS * EXPERTS_PER_GROUP
EXPERTS_PER_TOKEN = 2
EXPERT_HIDDEN = 256
RMS_EPS = 1e-6
NEG_INF = -1e30
LAMBDA_INIT = 0.8 - 0.6 * math.exp(-0.3 * 0)

LANES = 128
ROW_TILE = D_MODEL // LANES
ATT_TILE = 256
ONES_ROWS = 16
LOGIT_LIMIT = 80.0
ROUNDING_MARGIN = 1.05
LOG2E = math.log2(math.e)
PROJ_ROWS = 1024
OUT_ROWS = 512
ROUTE_TILE = 256
MOE_CHUNK = 512
MOE_TOKEN_PARTS = 2
EXPERT_ROW_PARTS = 4
MOVE_ROWS = 256
ROUTER_ROWS = 40
VMEM_LIMIT = 48 * 1024 * 1024

F32 = jnp.float32
BF16 = jnp.bfloat16
_NT = (((1,), (1,)), ((), ()))


def _t5_thresholds():
    n = np.arange(0, 1 << 16)
    max_exact = NUM_BUCKETS // 2
    nf = np.maximum(n, 1).astype(np.float32)
    large = max_exact + (np.log(nf / np.float32(max_exact)) / np.float32(math.log(MAX_DISTANCE / max_exact))
                         * np.float32(NUM_BUCKETS - max_exact)).astype(np.int32)
    bucket = np.where(n < max_exact, n, np.minimum(large, NUM_BUCKETS - 1))
    return [int(np.searchsorted(bucket, b, side="left")) for b in range(1, NUM_BUCKETS)]


_T5_THRESHOLDS = _t5_thresholds()


def _params(*sem):
    return pltpu.CompilerParams(dimension_semantics=sem, vmem_limit_bytes=VMEM_LIMIT)


def _bias_kernel(tab_ref, out_ref):
    h = pl.program_id(0)
    kj = lax.broadcasted_iota(jnp.int32, (ATT_TILE, ATT_TILE), 0)
    qi = lax.broadcasted_iota(jnp.int32, (ATT_TILE, ATT_TILE), 1)
    for d in range(out_ref.shape[0]):
        dist = d * ATT_TILE + qi - kj
        lo, hi = d * ATT_TILE - (ATT_TILE - 1), d * ATT_TILE + (ATT_TILE - 1)
        base = sum(thr <= max(lo, 0) for thr in _T5_THRESHOLDS)
        val = jnp.full((ATT_TILE, ATT_TILE), tab_ref[base, h], F32)
        for b, thr in enumerate(_T5_THRESHOLDS, start=1):
            if max(lo, 0) < thr <= hi:
                val = jnp.where(dist >= thr, tab_ref[b, h], val)
        val = val * LOG2E
        out_ref[d] = jnp.where(dist < 0, NEG_INF, val) if lo < 0 else val


def _bias_tiles(rel_bias, n_diag):
    return pl.pallas_call(
        _bias_kernel,
        grid=(N_HEADS_TOTAL,),
        in_specs=[pl.BlockSpec(memory_space=pltpu.SMEM)],
        out_specs=pl.BlockSpec((None, n_diag, ATT_TILE, ATT_TILE), lambda h: (h, 0, 0, 0)),
        out_shape=jax.ShapeDtypeStruct((N_HEADS_TOTAL, n_diag, ATT_TILE, ATT_TILE), F32),
        compiler_params=_params("arbitrary"),
        name="bias_tiles",
    )(rel_bias)


def _in_proj_kernel(x_ref, g1_ref, w_ref, pg_ref, grp_ref, qk_ref, vt_ref):
    n_row_tiles = PROJ_ROWS // ATT_TILE
    h = []
    for t in range(n_row_tiles):
        x = x_ref[t * ATT_TILE:(t + 1) * ATT_TILE, :]
        ms = jnp.mean(x * x, axis=-1, keepdims=True)
        h.append((x * lax.rsqrt(ms + RMS_EPS) * g1_ref[...]).astype(BF16))

    cw = 2 * LANES

    def finish(c, t, p):
        rows = slice(t * ATT_TILE, (t + 1) * ATT_TILE)
        cols = slice(c * cw, (c + 1) * cw)
        if c < QK_WIDTH // cw:
            msq = jnp.dot((p * p).astype(BF16), grp_ref[...], preferred_element_type=F32)
            qk_ref[rows, cols] = (p * lax.rsqrt(msq + RMS_EPS) * pg_ref[:, cols]).astype(BF16)
        else:
            vt_ref[t, c * cw - QK_WIDTH:(c + 1) * cw - QK_WIDTH, :] = p.T.astype(BF16)

    units = [(c, t) for c in range((QK_WIDTH + V_WIDTH) // cw) for t in range(n_row_tiles)]
    pending = None
    for c, t in units:
        p = jnp.dot(h[t], w_ref[:, c * cw:(c + 1) * cw], preferred_element_type=F32)
        if pending is not None:
            finish(*pending)
        pending = (c, t, p)
    finish(*pending)


def _in_proj(x2, g1, w_perm, post_gain, grp, batch, seq):
    n = x2.shape[0]
    steps_per_seq = seq // PROJ_ROWS
    tiles_per_step = PROJ_ROWS // ATT_TILE
    return pl.pallas_call(
        _in_proj_kernel,
        grid=(n // PROJ_ROWS,),
        in_specs=[
            pl.BlockSpec((PROJ_ROWS, D_MODEL), lambda i: (i, 0)),
            pl.BlockSpec((1, D_MODEL), lambda i: (0, 0)),
            pl.BlockSpec((D_MODEL, QK_WIDTH + V_WIDTH), lambda i: (0, 0)),
            pl.BlockSpec((1, QK_WIDTH), lambda i: (0, 0)),
            pl.BlockSpec((2 * LANES, 2 * LANES), lambda i: (0, 0)),
        ],
        out_specs=[
            pl.BlockSpec((PROJ_ROWS, QK_WIDTH), lambda i: (i, 0)),
            pl.BlockSpec((None, tiles_per_step, V_WIDTH, ATT_TILE),
                         lambda i: (i // steps_per_seq, i % steps_per_seq, 0, 0)),
        ],
        out_shape=[
            jax.ShapeDtypeStruct((n, QK_WIDTH), BF16),
            jax.ShapeDtypeStruct((batch, seq // ATT_TILE, V_WIDTH, ATT_TILE), BF16),
        ],
        compiler_params=_params("arbitrary"),
        name="in_proj",
    )(x2, g1, w_perm, post_gain, grp)


def _flash_tiles(qi, n_chains, logits_fn, values_fn, bias_fn, mask_fn, s_ref, p_ref, acc_ref):
    for c in range(n_chains):
        s_ref[0, c] = logits_fn(c, qi)
    p_ref[...] = jnp.zeros_like(p_ref)
    acc_ref[...] = jnp.zeros_like(acc_ref)

    def add_values(c, kv, alpha):
        acc_ref[c] = acc_ref[c] * alpha + jnp.dot(values_fn(c, kv), p_ref[c], preferred_element_type=F32)

    def step(j, carry):
        cur = lax.rem(j, 2)
        kv = qi - j
        out = []
        for c in range(n_chains):
            add_values(c, jnp.minimum(kv + 1, qi), carry[2 * c + 1])
        for c in range(n_chains):
            s = s_ref[cur, c] + bias_fn(c, j)
            tile_max = jnp.max(s, axis=0, keepdims=True)
            shift = mask_fn(c, kv)
            if shift is not None:
                tile_max = tile_max + shift
            m_new = jnp.maximum(carry[2 * c], tile_max)
            alpha = jnp.exp2(carry[2 * c] - m_new)
            p_ref[c] = jnp.exp2(s - (m_new if shift is None else m_new - shift)).astype(BF16)
            out += [m_new, alpha]
        for c in range(n_chains):
            s_ref[1 - cur, c] = logits_fn(c, jnp.maximum(kv - 1, 0))
        return tuple(out)

    m0 = jnp.full((1, ATT_TILE), NEG_INF, F32)
    one = jnp.ones((1, ATT_TILE), F32)
    carry = lax.fori_loop(0, qi + 1, step, (m0, one) * n_chains)
    for c in range(n_chains):
        add_values(c, 0, carry[2 * c + 1])


def _bounded_tiles(n_tiles, n_chains, logits_fn, values_fn, bias_fn, keep_fn, finish_fn, p_ref, acc_ref,
                   between_fn=lambda d: None):
    for d in range(n_tiles):
        for qi in range(d, n_tiles):
            for c in range(n_chains):
                p_ref[qi, c] = jnp.exp2(logits_fn(c, qi, qi - d) + bias_fn(c, d)).astype(BF16)
        between_fn(d)
        for qi in range(d, n_tiles):
            kv = qi - d
            for c in range(n_chains):
                pv = jnp.dot(values_fn(c, kv), p_ref[qi, c], preferred_element_type=F32)
                keep = keep_fn(c, qi, kv)
                if keep is not None:
                    pv = pv * keep
                acc_ref[qi, c] = pv if d == 0 else acc_ref[qi, c] + pv
        finish_fn(d)


def _logits_bounded(q_gain, k_gain, bias_cols):
    bound = (HEAD_DIM * jnp.max(jnp.abs(q_gain)) * jnp.max(jnp.abs(k_gain)) * ROUNDING_MARGIN
             + jnp.max(jnp.abs(bias_cols)) * LOG2E)
    return (bound <= LOGIT_LIMIT).astype(jnp.int32).reshape(1)


def _with_ones_rows(vt):
    return jnp.concatenate([vt, jnp.ones((ONES_ROWS, vt.shape[1]), vt.dtype)], axis=0)


def _attention_scratch(n_tiles, value_rows, n_chains=2):
    return [pltpu.VMEM((n_tiles, n_chains, ATT_TILE, LANES), BF16),
            pltpu.VMEM((2, n_chains, ATT_TILE, ATT_TILE), F32),
            pltpu.VMEM((n_tiles, n_chains, ATT_TILE, ATT_TILE), BF16),
            pltpu.VMEM((n_tiles, n_chains, value_rows + ONES_ROWS, ATT_TILE), F32)]


def _half_lane_split(q):
    lane = lax.broadcasted_iota(jnp.int32, q.shape, 1)
    zero = jnp.zeros_like(q)
    return jnp.where(lane < HEAD_DIM, q, zero), jnp.where(lane >= HEAD_DIM, q, zero)


def _diff_kernel(bounded_ref, lq1_ref, lk1_ref, lq2_ref, lk2_ref, subg_ref, q_ref, k_ref, vt_ref, bias_ref, o_ref,
                 qm_ref, s_ref, p_ref, acc_ref):
    n_tiles = q_ref.shape[0]
    chains = dict(n_chains=2, values_fn=lambda c, kv: _with_ones_rows(vt_ref[kv]))
    width = 2 * HEAD_DIM

    def prepare():
        for qi in range(n_tiles):
            qm_ref[qi, 0], qm_ref[qi, 1] = _half_lane_split(q_ref[qi])

    def finish(qi):
        lam = (jnp.exp(jnp.sum(lq1_ref[...] * lk1_ref[...], keepdims=True))
               - jnp.exp(jnp.sum(lq2_ref[...] * lk2_ref[...], keepdims=True)) + LAMBDA_INIT)
        a1, a2 = acc_ref[qi, 0], acc_ref[qi, 1]
        o = a1[:width] / a1[width:width + 1] - lam * (a2[:width] / a2[width:width + 1])
        ms = jnp.mean(o * o, axis=0, keepdims=True)
        o = o * lax.rsqrt(ms + RMS_EPS) * subg_ref[...] * (1.0 - LAMBDA_INIT)
        o_ref[qi] = o.T.astype(BF16)

    bounded = bounded_ref[0] > 0

    @pl.when(bounded)
    def _():
        prepare()
        _bounded_tiles(
            n_tiles,
            logits_fn=lambda c, qi, kv: lax.dot_general(k_ref[kv], qm_ref[qi, c], _NT, preferred_element_type=F32),
            bias_fn=lambda c, d: bias_ref[d],
            keep_fn=lambda c, qi, kv: None,
            finish_fn=finish,
            p_ref=p_ref, acc_ref=acc_ref, **chains)

    @pl.when(jnp.logical_not(bounded))
    def _():
        prepare()

        def q_tile(qi, carry):
            _flash_tiles(
                qi,
                logits_fn=lambda c, kv: lax.dot_general(k_ref[kv], qm_ref[qi, c], _NT, preferred_element_type=F32),
                bias_fn=lambda c, j: bias_ref[j],
                mask_fn=lambda c, kv: None,
                s_ref=s_ref, p_ref=p_ref.at[0], acc_ref=acc_ref.at[qi], **chains)
            return carry
        lax.fori_loop(0, n_tiles, q_tile, 0)
        for qi in range(n_tiles):
            finish(qi)


def _diff_attention(bounded, lam_vecs, sub_g_col, qk4, vt4, bias):
    batch, n_tiles = qk4.shape[0], qk4.shape[1]
    k_col0 = DIFF_WIDTH // LANES
    vec = pl.BlockSpec((1, HEAD_DIM), lambda h, b: (0, 0))
    return pl.pallas_call(
        _diff_kernel,
        grid=(HEADS_DIFF, batch),
        in_specs=[
            pl.BlockSpec(memory_space=pltpu.SMEM),
            vec, vec, vec, vec,
            pl.BlockSpec((2 * HEAD_DIM, 1), lambda h, b: (0, 0)),
            pl.BlockSpec((None, n_tiles, ATT_TILE, LANES), lambda h, b: (b, 0, 0, h)),
            pl.BlockSpec((None, n_tiles, ATT_TILE, LANES), lambda h, b: (b, 0, 0, k_col0 + h)),
            pl.BlockSpec((None, n_tiles, LANES, ATT_TILE), lambda h, b: (b, 0, h, 0)),
            pl.BlockSpec((None, n_tiles, ATT_TILE, ATT_TILE), lambda h, b: (h, 0, 0, 0)),
        ],
        out_specs=pl.BlockSpec((None, n_tiles, ATT_TILE, LANES), lambda h, b: (b, 0, 0, h)),
        out_shape=jax.ShapeDtypeStruct((batch, n_tiles, ATT_TILE, DIFF_WIDTH), BF16),
        scratch_shapes=_attention_scratch(n_tiles, 2 * HEAD_DIM),
        compiler_params=_params("arbitrary", "arbitrary"),
        name="diff_attention",
    )(bounded, *lam_vecs, sub_g_col, qk4, qk4, vt4, bias)


def _split3(v):
    hi = v.astype(BF16)
    r1 = v - hi.astype(F32)
    mid = r1.astype(BF16)
    lo = (r1 - mid.astype(F32)).astype(BF16)
    return hi, mid, lo


def _block_mask(gate, own):
    row = lax.broadcasted_iota(jnp.int32, gate.shape, 0)
    rank = jnp.zeros(gate.shape, jnp.int32)
    for m in range(own):
        gm = gate[m:m + 1, :]
        beats = (gm > gate) | ((gm == gate) & (row > m))
        rank = rank + jnp.where(beats, 1, 0)
    keep = ((rank < MOBA_TOPK) & (row < own)) | (row == own)
    return jnp.where(keep, 0.0, NEG_INF).astype(F32)


def _moba_kernel(bounded_ref, q_ref, k_ref, vt_ref, bias_ref, o_ref, kmean_ref, mask_ref, qm_ref, s_ref, p_ref,
                 acc_ref):
    n_tiles = q_ref.shape[0]

    def split_queries():
        for qi in range(n_tiles):
            qm_ref[qi, 0], qm_ref[qi, 1] = _half_lane_split(q_ref[qi])

    def block_masks(q_tiles):
        for n in range(n_tiles):
            kmean_ref[n:n + 1, :] = jnp.mean(k_ref[n].astype(F32), axis=0, keepdims=True)
        per_head = zip(*[_half_lane_split(term) for term in _split3(kmean_ref[...])])
        gate_lhs = jnp.concatenate([term for head_terms in per_head for term in head_terms], axis=0)
        n_terms = gate_lhs.shape[0] // (2 * n_tiles)
        for qi in q_tiles:
            terms = lax.dot_general(gate_lhs, q_ref[qi], _NT, preferred_element_type=F32)
            for half in range(2):
                rows = [terms[(half * n_terms + t) * n_tiles:(half * n_terms + t + 1) * n_tiles]
                        for t in range(n_terms)]
                mask_ref[qi, half] = _block_mask(sum(rows[1:], rows[0]), qi)

    def keep_row(c, qi, kv):
        if qi <= MOBA_TOPK or kv == qi:
            return None
        return jnp.where(mask_ref[qi, c, kv:kv + 1, :] < -1.0, 0.0, 1.0)

    def finish(qi):
        halves = [acc_ref[qi, half] for half in range(2)]
        o = jnp.concatenate([a[:HEAD_DIM] / a[HEAD_DIM:HEAD_DIM + 1] for a in halves], axis=0)
        o_ref[qi] = o.T.astype(BF16)

    chains = dict(
        n_chains=2,
        values_fn=lambda c, kv: _with_ones_rows(vt_ref[kv, c * HEAD_DIM:(c + 1) * HEAD_DIM, :]))
    bounded = bounded_ref[0] > 0

    @pl.when(bounded)
    def _():
        split_queries()
        _bounded_tiles(
            n_tiles,
            logits_fn=lambda c, qi, kv: lax.dot_general(k_ref[kv], qm_ref[qi, c], _NT, preferred_element_type=F32),
            bias_fn=lambda c, d: bias_ref[c, d],
            keep_fn=keep_row,
            between_fn=lambda d: block_masks(range(MOBA_TOPK + 1, n_tiles)) if d == 0 else None,
            finish_fn=finish,
            p_ref=p_ref, acc_ref=acc_ref, **chains)

    @pl.when(jnp.logical_not(bounded))
    def _():
        split_queries()
        block_masks(range(n_tiles))

        def q_tile(qi, carry):
            _flash_tiles(
                qi,
                logits_fn=lambda c, kv: lax.dot_general(k_ref[kv], qm_ref[qi, c], _NT, preferred_element_type=F32),
                bias_fn=lambda c, j: bias_ref[c, j],
                mask_fn=lambda c, kv: mask_ref[qi, c, pl.ds(kv, 1), :],
                s_ref=s_ref, p_ref=p_ref.at[0], acc_ref=acc_ref.at[qi], **chains)
            return carry
        lax.fori_loop(0, n_tiles, q_tile, 0)
        for qi in range(n_tiles):
            finish(qi)


def _moba_attention(bounded, qk4, vt4, bias_pairs):
    batch, n_tiles = qk4.shape[0], qk4.shape[1]
    q_col0 = 2 * DIFF_WIDTH // LANES
    k_col0 = q_col0 + MOBA_WIDTH // LANES
    v_row0 = DIFF_WIDTH // LANES
    pair0 = HEADS_DIFF // 2
    return pl.pallas_call(
        _moba_kernel,
        grid=(HEADS_MOBA // 2, batch),
        in_specs=[
            pl.BlockSpec(memory_space=pltpu.SMEM),
            pl.BlockSpec((None, n_tiles, ATT_TILE, LANES), lambda h, b: (b, 0, 0, q_col0 + h)),
            pl.BlockSpec((None, n_tiles, ATT_TILE, LANES), lambda h, b: (b, 0, 0, k_col0 + h)),
            pl.BlockSpec((None, n_tiles, LANES, ATT_TILE), lambda h, b: (b, 0, v_row0 + h, 0)),
            pl.BlockSpec((None, 2, n_tiles, ATT_TILE, ATT_TILE), lambda h, b: (pair0 + h, 0, 0, 0, 0)),
        ],
        out_specs=pl.BlockSpec((None, n_tiles, ATT_TILE, LANES), lambda h, b: (b, 0, 0, h)),
        out_shape=jax.ShapeDtypeStruct((batch, n_tiles, ATT_TILE, MOBA_WIDTH), BF16),
        scratch_shapes=[
            pltpu.VMEM((n_tiles, LANES), F32),
            pltpu.VMEM((n_tiles, 2, n_tiles, ATT_TILE), F32),
        ] + _attention_scratch(n_tiles, HEAD_DIM),
        compiler_params=_params("arbitrary", "arbitrary"),
        name="moba_attention",
    )(bounded, qk4, qk4, vt4, bias_pairs)


def _load_token_rows(ref, n_rows, lead=(), first=0):
    chunks = [ref[lead + (pl.ds(first * ROW_TILE + c, n_rows, stride=ROW_TILE), slice(None))]
              for c in range(ROW_TILE)]
    return jnp.concatenate(chunks, axis=1)


def _store_token_rows(ref, value, first=0):
    n_rows = value.shape[0]
    for c in range(ROW_TILE):
        ref[pl.ds(first * ROW_TILE + c, n_rows, stride=ROW_TILE), :] = value[:, c * LANES:(c + 1) * LANES]


def _token_tile(ref, t):
    return ref.at[pl.ds(pl.multiple_of(t * ROW_TILE, ROW_TILE), ROW_TILE)]


def _first_argmax(v):
    top = jnp.max(v, axis=0, keepdims=True)
    row = lax.broadcasted_iota(jnp.int32, v.shape, 0)
    idx = jnp.min(jnp.where(v == top, row, v.shape[0]), axis=0, keepdims=True)
    return top, idx


def _out_proj_kernel(yd_ref, ym_ref, x_ref, wd_ref, wm_ref, g2_ref, wr_ref, tri_ref,
                     x1_ref, ri_ref, rf_ref, cnt_ref, run_ref):
    @pl.when(lax.rem(pl.program_id(0), pl.num_programs(0) // MOE_TOKEN_PARTS) == 0)
    def _():
        run_ref[...] = jnp.zeros_like(run_ref)

    tiles = range(OUT_ROWS // ROUTE_TILE)
    h2 = []
    for t in tiles:
        rows = slice(t * ROUTE_TILE, (t + 1) * ROUTE_TILE)
        x1 = (x_ref[rows, :] + jnp.dot(yd_ref[rows, :], wd_ref[...], preferred_element_type=F32)
              + jnp.dot(ym_ref[rows, :], wm_ref[...], preferred_element_type=F32))
        _store_token_rows(x1_ref, x1, first=t * ROUTE_TILE)
        ms = jnp.mean(x1 * x1, axis=-1, keepdims=True)
        h2.append((x1 * lax.rsqrt(ms + RMS_EPS) * g2_ref[...]).astype(BF16))
    logits = [lax.dot_general(wr_ref[...], h, _NT, preferred_element_type=F32) for h in h2]

    picks = []
    for lg in logits:
        g_logits = lg[N_EXPERTS:N_EXPERTS + N_GROUPS, :]
        g_top, g_idx = _first_argmax(g_logits)
        p_group = 1.0 / jnp.sum(jnp.exp(g_logits - g_top), axis=0, keepdims=True)
        e_logits = lg[0:EXPERTS_PER_GROUP, :]
        for g in range(1, N_GROUPS):
            e_logits = jnp.where(g_idx == g, lg[g * EXPERTS_PER_GROUP:(g + 1) * EXPERTS_PER_GROUP, :], e_logits)
        v1, i1 = _first_argmax(e_logits)
        row = lax.broadcasted_iota(jnp.int32, e_logits.shape, 0)
        v2, i2 = _first_argmax(jnp.where(row == i1, -jnp.inf, e_logits))
        ratio = jnp.exp(v2 - v1)
        w1 = p_group / (1.0 + ratio)
        w2 = p_group * ratio / (1.0 + ratio)
        e1 = g_idx * EXPERTS_PER_GROUP + i1
        e2 = g_idx * EXPERTS_PER_GROUP + i2
        erow = lax.broadcasted_iota(jnp.int32, (N_EXPERTS, ROUTE_TILE), 0)
        hit1 = erow == e1
        hit2 = erow == e2
        onehot = jnp.where(hit1 | hit2, 1.0, 0.0).astype(F32)
        picks.append((e1, e2, w1, w2, hit1, hit2, onehot))

    prefix = [jnp.dot(p[-1].astype(BF16), tri_ref[...], preferred_element_type=F32) for p in picks]
    out_row = lax.broadcasted_iota(jnp.int32, (8, ROUTE_TILE), 0)
    for t in tiles:
        e1, e2, w1, w2, hit1, hit2, onehot = picks[t]
        lanes = slice(t * ROUTE_TILE, (t + 1) * ROUTE_TILE)
        before = prefix[t] + run_ref[...]
        r1 = jnp.sum(jnp.where(hit1, before, 0.0), axis=0, keepdims=True).astype(jnp.int32)
        r2 = jnp.sum(jnp.where(hit2, before, 0.0), axis=0, keepdims=True).astype(jnp.int32)
        run_ref[...] = run_ref[...] + jnp.sum(onehot, axis=1, keepdims=True)
        ri_ref[:, lanes] = jnp.where(out_row == 0, e1, jnp.where(out_row == 1, e2,
                                     jnp.where(out_row == 2, r1, jnp.where(out_row == 3, r2, 0))))
        rf_ref[:, lanes] = jnp.where(out_row == 0, w1, jnp.where(out_row == 1, w2, 0.0))
    cnt_ref[...] = jnp.broadcast_to(run_ref[...], cnt_ref.shape).astype(jnp.int32)


def _out_proj(y_d, y_m, x2, wo_d, wo_m, g2, w_router, tri):
    n = x2.shape[0]
    const = lambda i: (0, 0)
    steps_per_part = n // OUT_ROWS // MOE_TOKEN_PARTS
    return pl.pallas_call(
        _out_proj_kernel,
        grid=(n // OUT_ROWS,),
        in_specs=[
            pl.BlockSpec((OUT_ROWS, DIFF_WIDTH), lambda i: (i, 0)),
            pl.BlockSpec((OUT_ROWS, MOBA_WIDTH), lambda i: (i, 0)),
            pl.BlockSpec((OUT_ROWS, D_MODEL), lambda i: (i, 0)),
            pl.BlockSpec((DIFF_WIDTH, D_MODEL), const),
            pl.BlockSpec((MOBA_WIDTH, D_MODEL), const),
            pl.BlockSpec((1, D_MODEL), const),
            pl.BlockSpec((ROUTER_ROWS, D_MODEL), const),
            pl.BlockSpec((ROUTE_TILE, ROUTE_TILE), const),
        ],
        out_specs=[
            pl.BlockSpec((OUT_ROWS * ROW_TILE, LANES), lambda i: (i, 0)),
            pl.BlockSpec((8, OUT_ROWS), lambda i: (0, i)),
            pl.BlockSpec((8, OUT_ROWS), lambda i: (0, i)),
            pl.BlockSpec((None, N_EXPERTS, LANES), lambda i: (i // steps_per_part, 0, 0)),
        ],
        out_shape=[
            jax.ShapeDtypeStruct((n * ROW_TILE, LANES), F32),
            jax.ShapeDtypeStruct((8, n), jnp.int32),
            jax.ShapeDtypeStruct((8, n), F32),
            jax.ShapeDtypeStruct((MOE_TOKEN_PARTS, N_EXPERTS, LANES), jnp.int32),
        ],
        scratch_shapes=[pltpu.VMEM((N_EXPERTS, 1), F32)],
        compiler_params=_params("arbitrary"),
        name="out_proj_route",
    )(y_d, y_m, x2, wo_d, wo_m, g2, w_router, tri)


def _dispatch_step(i, n_steps, first_tile, zchunk_ref, dest_ref, x1_hbm, xs_hbm, zero_ref, stage_ref, zero_sem,
                   load_sem, row_sem):
    last = n_steps - 1
    slot = lax.rem(i, 2)
    chunk_rows = MOE_CHUNK * ROW_TILE
    tile_rows = MOVE_ROWS * ROW_TILE

    def load(t, s):
        start = pl.multiple_of((first_tile + t) * tile_rows, tile_rows)
        return pltpu.make_async_copy(x1_hbm.at[pl.ds(start, tile_rows)], stage_ref.at[s], load_sem.at[s])

    def wait_rows(s):
        for _ in range(EXPERTS_PER_TOKEN):
            pltpu.make_async_copy(stage_ref.at[s], xs_hbm.at[pl.ds(0, tile_rows)], row_sem.at[s]).wait()

    def zero_copy(c):
        start = pl.multiple_of(c * chunk_rows, chunk_rows)
        return pltpu.make_async_copy(zero_ref, xs_hbm.at[pl.ds(start, chunk_rows)], zero_sem)

    @pl.when(i == 0)
    def _():
        zero_ref[...] = jnp.zeros_like(zero_ref)

        def start_one(c, carry):
            @pl.when(zchunk_ref[c] > 0)
            def _():
                zero_copy(c).start()
            return carry

        def wait_one(c, carry):
            @pl.when(zchunk_ref[c] > 0)
            def _():
                zero_copy(c).wait()
            return carry

        lax.fori_loop(0, zchunk_ref.shape[0], start_one, 0)
        lax.fori_loop(0, zchunk_ref.shape[0], wait_one, 0)
        load(0, 0).start()

    @pl.when(i > 0)
    def _():
        wait_rows(1 - slot)

    @pl.when(i < last)
    def _():
        load(i + 1, 1 - slot).start()

    load(i, slot).wait()

    def send_row(r, carry):
        src = _token_tile(stage_ref.at[slot], r)
        for k in range(EXPERTS_PER_TOKEN):
            pltpu.make_async_copy(src, _token_tile(xs_hbm, dest_ref[0, k, r]), row_sem.at[slot]).start(priority=k)
        return carry

    lax.fori_loop(0, MOVE_ROWS, send_row, 0, unroll=8)

    @pl.when(i == last)
    def _():
        wait_rows(slot)


def _dispatch_scratch():
    return [pltpu.VMEM((MOE_CHUNK * ROW_TILE, LANES), F32),
            pltpu.VMEM((2, MOVE_ROWS * ROW_TILE, LANES), F32),
            pltpu.SemaphoreType.DMA(()),
            pltpu.SemaphoreType.DMA((2,)),
            pltpu.SemaphoreType.DMA((2,))]


def _dispatch_kernel(first_tile, zchunk_ref, dest_ref, x1_hbm, xs_hbm, *scratch):
    _dispatch_step(pl.program_id(0), pl.num_programs(0), first_tile, zchunk_ref, dest_ref, x1_hbm, xs_hbm, *scratch)


def _dispatch(part, zchunk, dest3, x1, n_slots):
    tiles = x1.shape[0] // ROW_TILE // MOVE_ROWS // MOE_TOKEN_PARTS
    first_tile = part * tiles
    grid_spec = pltpu.PrefetchScalarGridSpec(
        num_scalar_prefetch=1,
        grid=(tiles,),
        in_specs=[
            pl.BlockSpec((1, EXPERTS_PER_TOKEN, MOVE_ROWS), lambda i, zc: (first_tile + i, 0, 0),
                         memory_space=pltpu.SMEM),
            pl.BlockSpec(memory_space=pl.ANY),
        ],
        out_specs=pl.BlockSpec(memory_space=pl.ANY),
        scratch_shapes=_dispatch_scratch(),
    )
    return pl.pallas_call(
        functools.partial(_dispatch_kernel, first_tile),
        grid_spec=grid_spec,
        out_shape=jax.ShapeDtypeStruct((n_slots * ROW_TILE, LANES), F32),
        compiler_params=_params("arbitrary"),
        name="moe_dispatch",
    )(zchunk, dest3, x1)


def _expert_step(c, na_ref, xs_ref, g2_ref, wg_ref, wu_ref, wd_ref, o_ref):
    active = c < na_ref[0]

    @pl.when(jnp.logical_not(active))
    def _():
        o_ref[...] = jnp.zeros_like(o_ref)

    @pl.when(active)
    def _():
        part_rows = MOE_CHUNK // EXPERT_ROW_PARTS

        def gate_up(r):
            x = _load_token_rows(xs_ref, part_rows, first=r * part_rows)
            ms = jnp.mean(x * x, axis=-1, keepdims=True)
            h = (x * lax.rsqrt(ms + RMS_EPS) * g2_ref[...]).astype(BF16)
            return (jnp.dot(h, wg_ref[...], preferred_element_type=F32),
                    jnp.dot(h, wu_ref[...], preferred_element_type=F32))

        def down(r, gate, up):
            hid = (gate * jax.nn.sigmoid(gate) * up).astype(BF16)
            _store_token_rows(o_ref, jnp.dot(hid, wd_ref[...], preferred_element_type=F32), first=r * part_rows)

        pending = None
        for r in range(EXPERT_ROW_PARTS):
            current = (r,) + gate_up(r)
            if pending is not None:
                down(*pending)
            pending = current
        down(*pending)


def _expert_kernel(ce_ref, na_ref, *refs):
    _expert_step(pl.program_id(0), na_ref, *refs)


def _expert_specs():
    rows = lambda c, ce, na, *_: (jnp.minimum(c, na[0] - 1), 0)
    expert = lambda c, ce, na, *_: (ce[c], 0, 0)
    in_specs = [
        pl.BlockSpec((MOE_CHUNK * ROW_TILE, LANES), rows),
        pl.BlockSpec((1, D_MODEL), lambda c, *_: (0, 0)),
        pl.BlockSpec((None, D_MODEL, EXPERT_HIDDEN), expert),
        pl.BlockSpec((None, D_MODEL, EXPERT_HIDDEN), expert),
        pl.BlockSpec((None, EXPERT_HIDDEN, D_MODEL), expert),
    ]
    return in_specs, pl.BlockSpec((MOE_CHUNK * ROW_TILE, LANES), lambda c, *_: (c, 0))


def _experts(chunk_e, n_active, xs, g2, wg, wu, wd):
    n_slots = xs.shape[0] // ROW_TILE
    in_specs, out_spec = _expert_specs()
    grid_spec = pltpu.PrefetchScalarGridSpec(
        num_scalar_prefetch=2, grid=(n_slots // MOE_CHUNK,), in_specs=in_specs, out_specs=out_spec)
    return pl.pallas_call(
        _expert_kernel,
        grid_spec=grid_spec,
        out_shape=jax.ShapeDtypeStruct((n_slots * ROW_TILE, LANES), F32),
        compiler_params=_params("arbitrary"),
        name="moe_experts",
    )(chunk_e, n_active, xs, g2, wg, wu, wd)


def _experts_dispatch_kernel(first_tile, n_tiles, ce_ref, na_ref, zchunk_ref,
                             xs_ref, g2_ref, wg_ref, wu_ref, wd_ref, dest_ref, x1_hbm,
                             o_ref, xs_next_hbm, *scratch):
    i = pl.program_id(0)

    @pl.when(i < n_tiles)
    def _():
        _dispatch_step(i, n_tiles, first_tile, zchunk_ref, dest_ref, x1_hbm, xs_next_hbm, *scratch)

    _expert_step(i, na_ref, xs_ref, g2_ref, wg_ref, wu_ref, wd_ref, o_ref)


def _experts_dispatch(chunk_e, n_active, zchunk_next, xs, g2, wg, wu, wd, next_part, dest3, x1):
    n_slots = xs.shape[0] // ROW_TILE
    n_chunks = n_slots // MOE_CHUNK
    tiles = x1.shape[0] // ROW_TILE // MOVE_ROWS // MOE_TOKEN_PARTS
    first_tile = next_part * tiles
    assert n_chunks >= tiles
    in_specs, out_spec = _expert_specs()
    in_specs += [
        pl.BlockSpec((1, EXPERTS_PER_TOKEN, MOVE_ROWS),
                     lambda c, *_: (first_tile + jnp.minimum(c, tiles - 1), 0, 0), memory_space=pltpu.SMEM),
        pl.BlockSpec(memory_space=pl.ANY),
    ]
    grid_spec = pltpu.PrefetchScalarGridSpec(
        num_scalar_prefetch=3, grid=(n_chunks,), in_specs=in_specs,
        out_specs=[out_spec, pl.BlockSpec(memory_space=pl.ANY)],
        scratch_shapes=_dispatch_scratch())
    slots = jax.ShapeDtypeStruct((n_slots * ROW_TILE, LANES), F32)
    return pl.pallas_call(
        functools.partial(_experts_dispatch_kernel, first_tile, tiles),
        grid_spec=grid_spec,
        out_shape=[slots, slots],
        compiler_params=_params("arbitrary"),
        name="moe_experts_dispatch",
    )(chunk_e, n_active, zchunk_next, xs, g2, wg, wu, wd, dest3, x1)


def _combine_kernel(dest_ref, dest_next_ref, w_ref, x1_ref, *rest):
    outs_hbm, (y_ref, buf_ref, sem_ref) = rest[:MOE_TOKEN_PARTS], rest[MOE_TOKEN_PARTS:]
    i = pl.program_id(0)
    slot = lax.rem(i, 2)
    tiles_per_part = pl.num_programs(0) // MOE_TOKEN_PARTS

    def fetch(tile, dref, s):
        for part, src_hbm in enumerate(outs_hbm):
            @pl.when(tile // tiles_per_part == part)
            def _():
                def fetch_row(r, carry):
                    for k in range(EXPERTS_PER_TOKEN):
                        pltpu.make_async_copy(_token_tile(src_hbm, dref[0, k, r]),
                                              _token_tile(buf_ref.at[s, k], r), sem_ref.at[s]).start(priority=k)
                    return carry
                lax.fori_loop(0, MOVE_ROWS, fetch_row, 0, unroll=8)

    @pl.when(i == 0)
    def _():
        fetch(i, dest_ref, 0)

    @pl.when(i + 1 < pl.num_programs(0))
    def _():
        fetch(i + 1, dest_next_ref, 1 - slot)

    for k in range(EXPERTS_PER_TOKEN):
        pltpu.make_async_copy(outs_hbm[0].at[pl.ds(0, MOVE_ROWS * ROW_TILE)], buf_ref.at[slot, k],
                              sem_ref.at[slot]).wait()
    w = w_ref[...]
    y_ref[...] = (_load_token_rows(x1_ref, MOVE_ROWS)
                  + w[:, 0:1] * _load_token_rows(buf_ref, MOVE_ROWS, (slot, 0))
                  + w[:, 1:2] * _load_token_rows(buf_ref, MOVE_ROWS, (slot, 1)))


def _combine(dest3, wts, x1, outs_parts):
    n = x1.shape[0] // ROW_TILE
    steps = n // MOVE_ROWS
    dest_block = (1, EXPERTS_PER_TOKEN, MOVE_ROWS)
    return pl.pallas_call(
        _combine_kernel,
        grid=(steps,),
        in_specs=[
            pl.BlockSpec(dest_block, lambda i: (i, 0, 0), memory_space=pltpu.SMEM),
            pl.BlockSpec(dest_block, lambda i: (jnp.minimum(i + 1, steps - 1), 0, 0), memory_space=pltpu.SMEM),
            pl.BlockSpec((MOVE_ROWS, EXPERTS_PER_TOKEN), lambda i: (i, 0)),
            pl.BlockSpec((MOVE_ROWS * ROW_TILE, LANES), lambda i: (i, 0)),
        ] + [pl.BlockSpec(memory_space=pl.ANY)] * MOE_TOKEN_PARTS,
        out_specs=pl.BlockSpec((MOVE_ROWS, D_MODEL), lambda i: (i, 0)),
        out_shape=jax.ShapeDtypeStruct((n, D_MODEL), F32),
        scratch_shapes=[
            pltpu.VMEM((2, EXPERTS_PER_TOKEN, MOVE_ROWS * ROW_TILE, LANES), F32),
            pltpu.SemaphoreType.DMA((2,)),
        ],
        compiler_params=_params("arbitrary"),
        name="moe_combine",
    )(dest3, dest3, wts, x1, *outs_parts)


def _layer(x, norm1_g, w_in, diff_q_g, diff_k_g, lambda_q1, lambda_k1, lambda_q2, lambda_k2, diff_sub_g,
           moba_q_g, moba_k_g, rel_bias, w_out, norm2_g, router_group, router_expert, w_gate, w_up, w_down):
    batch, seq, _ = x.shape
    n = batch * seq
    n_tiles = seq // ATT_TILE
    scale = HEAD_DIM ** -0.5 * LOG2E

    d3 = 3 * DIFF_WIDTH
    w_perm = jnp.concatenate([w_in[:, :2 * DIFF_WIDTH], w_in[:, d3:d3 + 2 * MOBA_WIDTH],
                              w_in[:, 2 * DIFF_WIDTH:d3], w_in[:, d3 + 2 * MOBA_WIDTH:]], axis=1).astype(BF16)
    reps_d, reps_m = DIFF_WIDTH // HEAD_DIM, MOBA_WIDTH // HEAD_DIM
    post_gain = jnp.concatenate([jnp.tile(diff_q_g * scale, reps_d), jnp.tile(diff_k_g, reps_d),
                                 jnp.tile(moba_q_g * scale, reps_m), jnp.tile(moba_k_g, reps_m)])[None, :]
    head_of = np.arange(2 * LANES) // HEAD_DIM
    grp = jnp.asarray((head_of[:, None] == head_of[None, :]) / HEAD_DIM, BF16)

    bias = _bias_tiles(rel_bias, n_tiles)
    qk, vt4 = _in_proj(x.reshape(n, D_MODEL), norm1_g[None, :], w_perm, post_gain, grp, batch, seq)
    qk4 = qk.reshape(batch, n_tiles, ATT_TILE, QK_WIDTH)
    lam_vecs = [v[None, :] for v in (lambda_q1, lambda_k1, lambda_q2, lambda_k2)]
    y_d = _diff_attention(_logits_bounded(diff_q_g * scale, diff_k_g, rel_bias[:, :HEADS_DIFF]),
                          lam_vecs, diff_sub_g[:, None], qk4, vt4, bias)
    y_m = _moba_attention(_logits_bounded(moba_q_g * scale, moba_k_g, rel_bias[:, HEADS_DIFF:]),
                          qk4, vt4, bias.reshape(N_HEADS_TOTAL // 2, 2, n_tiles, ATT_TILE, ATT_TILE))

    w_router = jnp.concatenate([
        router_expert.transpose(0, 2, 1).reshape(N_EXPERTS, D_MODEL), router_group.T,
        jnp.zeros((ROUTER_ROWS - N_EXPERTS - N_GROUPS, D_MODEL), F32)], axis=0).astype(BF16)
    tri = jnp.asarray(np.triu(np.ones((ROUTE_TILE, ROUTE_TILE)), k=1), BF16)
    wo = w_out.astype(BF16)
    x1, route_i, route_f, cnt = _out_proj(y_d.reshape(n, DIFF_WIDTH), y_m.reshape(n, MOBA_WIDTH),
                                          x.reshape(n, D_MODEL), wo[:DIFF_WIDTH], wo[DIFF_WIDTH:],
                                          norm2_g[None, :], w_router, tri)

    parts = MOE_TOKEN_PARTS
    counts = cnt[:, :, 0]
    pad_counts = ((counts + MOE_CHUNK - 1) // MOE_CHUNK) * MOE_CHUNK
    pad_end = jnp.cumsum(pad_counts, axis=1)
    pad_start = pad_end - pad_counts
    experts = jnp.arange(N_EXPERTS, dtype=jnp.int32)
    picked = route_i[0:2].reshape(EXPERTS_PER_TOKEN, parts, n // parts)
    start_of = jnp.sum(jnp.where(picked[..., None] == experts, pad_start[None, :, None, :], 0), axis=-1)
    dest = start_of.reshape(EXPERTS_PER_TOKEN, n) + route_i[2:4]
    dest3 = dest.reshape(EXPERTS_PER_TOKEN, n // MOVE_ROWS, MOVE_ROWS).transpose(1, 0, 2)
    n_slots = n // parts * EXPERTS_PER_TOKEN + N_EXPERTS * MOE_CHUNK
    n_chunks = n_slots // MOE_CHUNK
    chunk_id = jnp.arange(n_chunks, dtype=jnp.int32)
    chunk_e = jnp.minimum(jnp.sum((pad_end[:, None, :] <= chunk_id[None, :, None] * MOE_CHUNK).astype(jnp.int32),
                                  axis=2), N_EXPERTS - 1)
    n_active = (pad_end[:, -1:] // MOE_CHUNK).astype(jnp.int32)
    next_e = jnp.concatenate([chunk_e[:, 1:], jnp.full((parts, 1), N_EXPERTS, jnp.int32)], axis=1)
    zchunk = ((chunk_id[None, :] >= n_active - 1) | (chunk_e != next_e)).astype(jnp.int32)

    weights = (norm2_g[None, :], w_gate.astype(BF16), w_up.astype(BF16), w_down.astype(BF16))
    xs = _dispatch(0, zchunk[0], dest3, x1, n_slots)
    outs = []
    for p in range(parts):
        if p + 1 < parts:
            out_p, xs_next = _experts_dispatch(chunk_e[p], n_active[p], zchunk[p + 1], xs, *weights, p + 1, dest3, x1)
        else:
            out_p, xs_next = _experts(chunk_e[p], n_active[p], xs, *weights), None
        outs.append(out_p)
        xs = xs_next
    y = _combine(dest3, route_f[0:2].T, x1, outs)
    return y.reshape(batch, seq, D_MODEL)


def kernel(x, norm1_g, w_in, diff_q_g, diff_k_g, lambda_q1, lambda_k1, lambda_q2, lambda_k2, diff_sub_g,
           moba_q_g, moba_k_g, rel_bias, w_out, norm2_g, router_group, router_expert, w_gate, w_up, w_down):
    assert x.shape[1] % PROJ_ROWS == 0 and x.shape[2] == D_MODEL and norm1_g.shape[0] == 1
    return _layer(x, norm1_g[0], w_in[0], diff_q_g[0], diff_k_g[0], lambda_q1[0], lambda_k1[0], lambda_q2[0],
                  lambda_k2[0], diff_sub_g[0], moba_q_g[0], moba_k_g[0], rel_bias, w_out[0], norm2_g[0],
                  router_group[0], router_expert[0], w_gate[0], w_up[0], w_down[0])
```

```python
import functools
import math

import numpy as np
import jax
import jax.numpy as jnp
from jax import lax
from jax.experimental import pallas as pl
from jax.experimental.pallas import tpu as pltpu

D_MODEL = 1024
HEAD_DIM = 64
HEADS_DIFF = 4
HEADS_MOBA = 8
N_HEADS_TOTAL = HEADS_DIFF + HEADS_MOBA
DIFF_WIDTH = HEADS_DIFF * 2 * HEAD_DIM
MOBA_WIDTH = HEADS_MOBA * HEAD_DIM
QK_WIDTH = 2 * DIFF_WIDTH + 2 * MOBA_WIDTH
V_WIDTH = DIFF_WIDTH + MOBA_WIDTH
MOBA_TOPK = 3
NUM_BUCKETS = 32
MAX_DISTANCE = 2048
N_GROUPS = 4
EXPERTS_PER_GROUP = 8
N_EXPERTS = N_GROUPS * EXPERTS_PER_GROUP
EXPERTS_PER_TOKEN = 2
EXPERT_HIDDEN = 256
RMS_EPS = 1e-6
NEG_INF = -1e30
LAMBDA_INIT = 0.8 - 0.6 * math.exp(-0.3 * 0)

LANES = 128
ROW_TILE = D_MODEL // LANES
ATT_TILE = 256
ONES_ROWS = 16
LOGIT_LIMIT = 80.0
ROUNDING_MARGIN = 1.05
LOG2E = math.log2(math.e)
PROJ_ROWS = 1024
OUT_ROWS = 1024
ROUTE_TILE = 256
MOE_CHUNK = 512
MOE_TOKEN_PARTS = 2
EXPERT_ROW_PARTS = 4
MOVE_ROWS = 256
ROUTER_ROWS = 40
VMEM_LIMIT = 48 * 1024 * 1024

F32 = jnp.float32
BF16 = jnp.bfloat16
_NT = (((1,), (1,)), ((), ()))


def _t5_thresholds():
    n = np.arange(0, 1 << 16)
    max_exact = NUM_BUCKETS // 2
    nf = np.maximum(n, 1).astype(np.float32)
    large = max_exact + (np.log(nf / np.float32(max_exact)) / np.float32(math.log(MAX_DISTANCE / max_exact))
                         * np.float32(NUM_BUCKETS - max_exact)).astype(np.int32)
    bucket = np.where(n < max_exact, n, np.minimum(large, NUM_BUCKETS - 1))
    return [int(np.searchsorted(bucket, b, side="left")) for b in range(1, NUM_BUCKETS)]


_T5_THRESHOLDS = _t5_thresholds()


def _params(*sem):
    return pltpu.CompilerParams(dimension_semantics=sem, vmem_limit_bytes=VMEM_LIMIT)


def _bias_kernel(tab_ref, out_ref):
    h = pl.program_id(0)
    kj = lax.broadcasted_iota(jnp.int32, (ATT_TILE, ATT_TILE), 0)
    qi = lax.broadcasted_iota(jnp.int32, (ATT_TILE, ATT_TILE), 1)
    for d in range(out_ref.shape[0]):
        dist = d * ATT_TILE + qi - kj
        lo, hi = d * ATT_TILE - (ATT_TILE - 1), d * ATT_TILE + (ATT_TILE - 1)
        base = sum(thr <= max(lo, 0) for thr in _T5_THRESHOLDS)
        val = jnp.full((ATT_TILE, ATT_TILE), tab_ref[base, h], F32)
        for b, thr in enumerate(_T5_THRESHOLDS, start=1):
            if max(lo, 0) < thr <= hi:
                val = jnp.where(dist >= thr, tab_ref[b, h], val)
        val = val * LOG2E
        out_ref[d] = jnp.where(dist < 0, NEG_INF, val) if lo < 0 else val


def _bias_tiles(rel_bias, n_diag):
    return pl.pallas_call(
        _bias_kernel,
        grid=(N_HEADS_TOTAL,),
        in_specs=[pl.BlockSpec(memory_space=pltpu.SMEM)],
        out_specs=pl.BlockSpec((None, n_diag, ATT_TILE, ATT_TILE), lambda h: (h, 0, 0, 0)),
        out_shape=jax.ShapeDtypeStruct((N_HEADS_TOTAL, n_diag, ATT_TILE, ATT_TILE), F32),
        compiler_params=_params("arbitrary"),
        name="bias_tiles",
    )(rel_bias)


def _in_proj_kernel(x_ref, g1_ref, w_ref, pg_ref, grp_ref, qk_ref, vt_ref):
    n_row_tiles = PROJ_ROWS // ATT_TILE
    h = []
    for t in range(n_row_tiles):
        x = x_ref[t * ATT_TILE:(t + 1) * ATT_TILE, :]
        ms = jnp.mean(x * x, axis=-1, keepdims=True)
        h.append((x * lax.rsqrt(ms + RMS_EPS) * g1_ref[...]).astype(BF16))

    cw = 2 * LANES

    def finish(c, t, p):
        rows = slice(t * ATT_TILE, (t + 1) * ATT_TILE)
        cols = slice(c * cw, (c + 1) * cw)
        if c < QK_WIDTH // cw:
            msq = jnp.dot((p * p).astype(BF16), grp_ref[...], preferred_element_type=F32)
            qk_ref[rows, cols] = (p * lax.rsqrt(msq + RMS_EPS) * pg_ref[:, cols]).astype(BF16)
        else:
            vt_ref[t, c * cw - QK_WIDTH:(c + 1) * cw - QK_WIDTH, :] = p.T.astype(BF16)

    units = [(c, t) for c in range((QK_WIDTH + V_WIDTH) // cw) for t in range(n_row_tiles)]
    pending = None
    for c, t in units:
        p = jnp.dot(h[t], w_ref[:, c * cw:(c + 1) * cw], preferred_element_type=F32)
        if pending is not None:
            finish(*pending)
        pending = (c, t, p)
    finish(*pending)


def _in_proj(x2, g1, w_perm, post_gain, grp, batch, seq):
    n = x2.shape[0]
    steps_per_seq = seq // PROJ_ROWS
    tiles_per_step = PROJ_ROWS // ATT_TILE
    return pl.pallas_call(
        _in_proj_kernel,
        grid=(n // PROJ_ROWS,),
        in_specs=[
            pl.BlockSpec((PROJ_ROWS, D_MODEL), lambda i: (i, 0)),
            pl.BlockSpec((1, D_MODEL), lambda i: (0, 0)),
            pl.BlockSpec((D_MODEL, QK_WIDTH + V_WIDTH), lambda i: (0, 0)),
            pl.BlockSpec((1, QK_WIDTH), lambda i: (0, 0)),
            pl.BlockSpec((2 * LANES, 2 * LANES), lambda i: (0, 0)),
        ],
        out_specs=[
            pl.BlockSpec((PROJ_ROWS, QK_WIDTH), lambda i: (i, 0)),
            pl.BlockSpec((None, tiles_per_step, V_WIDTH, ATT_TILE),
                         lambda i: (i // steps_per_seq, i % steps_per_seq, 0, 0)),
        ],
        out_shape=[
            jax.ShapeDtypeStruct((n, QK_WIDTH), BF16),
            jax.ShapeDtypeStruct((batch, seq // ATT_TILE, V_WIDTH, ATT_TILE), BF16),
        ],
        compiler_params=_params("arbitrary"),
        name="in_proj",
    )(x2, g1, w_perm, post_gain, grp)


def _flash_tiles(qi, n_chains, logits_fn, values_fn, bias_fn, mask_fn, s_ref, p_ref, acc_ref):
    for c in range(n_chains):
        s_ref[0, c] = logits_fn(c, qi)
    p_ref[...] = jnp.zeros_like(p_ref)
    acc_ref[...] = jnp.zeros_like(acc_ref)

    def add_values(c, kv, alpha):
        acc_ref[c] = acc_ref[c] * alpha + jnp.dot(values_fn(c, kv), p_ref[c], preferred_element_type=F32)

    def step(j, carry):
        cur = lax.rem(j, 2)
        kv = qi - j
        out = []
        for c in range(n_chains):
            add_values(c, jnp.minimum(kv + 1, qi), carry[2 * c + 1])
        for c in range(n_chains):
            s = s_ref[cur, c] + bias_fn(c, j)
            tile_max = jnp.max(s, axis=0, keepdims=True)
            shift = mask_fn(c, kv)
            if shift is not None:
                tile_max = tile_max + shift
            m_new = jnp.maximum(carry[2 * c], tile_max)
            alpha = jnp.exp2(carry[2 * c] - m_new)
            p_ref[c] = jnp.exp2(s - (m_new if shift is None else m_new - shift)).astype(BF16)
            out += [m_new, alpha]
        for c in range(n_chains):
            s_ref[1 - cur, c] = logits_fn(c, jnp.maximum(kv - 1, 0))
        return tuple(out)

    m0 = jnp.full((1, ATT_TILE), NEG_INF, F32)
    one = jnp.ones((1, ATT_TILE), F32)
    carry = lax.fori_loop(0, qi + 1, step, (m0, one) * n_chains)
    for c in range(n_chains):
        add_values(c, 0, carry[2 * c + 1])


def _bounded_tiles(n_tiles, n_chains, logits_fn, values_fn, bias_fn, keep_fn, finish_fn, p_ref, acc_ref,
                   between_fn=lambda d: None):
    for d in range(n_tiles):
        for qi in range(d, n_tiles):
            for c in range(n_chains):
                p_ref[qi, c] = jnp.exp2(logits_fn(c, qi, qi - d) + bias_fn(c, d)).astype(BF16)
        between_fn(d)
        for qi in range(d, n_tiles):
            kv = qi - d
            for c in range(n_chains):
                pv = jnp.dot(values_fn(c, kv), p_ref[qi, c], preferred_element_type=F32)
                keep = keep_fn(c, qi, kv)
                if keep is not None:
                    pv = pv * keep
                acc_ref[qi, c] = pv if d == 0 else acc_ref[qi, c] + pv
        finish_fn(d)


def _logits_bounded(q_gain, k_gain, bias_cols):
    bound = (HEAD_DIM * jnp.max(jnp.abs(q_gain)) * jnp.max(jnp.abs(k_gain)) * ROUNDING_MARGIN
             + jnp.max(jnp.abs(bias_cols)) * LOG2E)
    return (bound <= LOGIT_LIMIT).astype(jnp.int32).reshape(1)


def _with_ones_rows(vt):
    return jnp.concatenate([vt, jnp.ones((ONES_ROWS, vt.shape[1]), vt.dtype)], axis=0)


def _attention_scratch(n_tiles, value_rows, n_chains=2):
    return [pltpu.VMEM((n_tiles, n_chains, ATT_TILE, LANES), BF16),
            pltpu.VMEM((2, n_chains, ATT_TILE, ATT_TILE), F32),
            pltpu.VMEM((n_tiles, n_chains, ATT_TILE, ATT_TILE), BF16),
            pltpu.VMEM((n_tiles, n_chains, value_rows + ONES_ROWS, ATT_TILE), F32)]


def _half_lane_split(q):
    lane = lax.broadcasted_iota(jnp.int32, q.shape, 1)
    zero = jnp.zeros_like(q)
    return jnp.where(lane < HEAD_DIM, q, zero), jnp.where(lane >= HEAD_DIM, q, zero)


def _diff_kernel(bounded_ref, lq1_ref, lk1_ref, lq2_ref, lk2_ref, subg_ref, q_ref, k_ref, vt_ref, bias_ref, o_ref,
                 qm_ref, s_ref, p_ref, acc_ref):
    n_tiles = q_ref.shape[0]
    chains = dict(n_chains=2, values_fn=lambda c, kv: _with_ones_rows(vt_ref[kv]))
    width = 2 * HEAD_DIM

    def prepare():
        for qi in range(n_tiles):
            qm_ref[qi, 0], qm_ref[qi, 1] = _half_lane_split(q_ref[qi])

    def finish(qi):
        lam = (jnp.exp(jnp.sum(lq1_ref[...] * lk1_ref[...], keepdims=True))
               - jnp.exp(jnp.sum(lq2_ref[...] * lk2_ref[...], keepdims=True)) + LAMBDA_INIT)
        a1, a2 = acc_ref[qi, 0], acc_ref[qi, 1]
        o = a1[:width] / a1[width:width + 1] - lam * (a2[:width] / a2[width:width + 1])
        ms = jnp.mean(o * o, axis=0, keepdims=True)
        o = o * lax.rsqrt(ms + RMS_EPS) * subg_ref[...] * (1.0 - LAMBDA_INIT)
        o_ref[qi] = o.T.astype(BF16)

    bounded = bounded_ref[0] > 0

    @pl.when(bounded)
    def _():
        prepare()
        _bounded_tiles(
            n_tiles,
            logits_fn=lambda c, qi, kv: lax.dot_general(k_ref[kv], qm_ref[qi, c], _NT, preferred_element_type=F32),
            bias_fn=lambda c, d: bias_ref[d],
            keep_fn=lambda c, qi, kv: None,
            finish_fn=finish,
            p_ref=p_ref, acc_ref=acc_ref, **chains)

    @pl.when(jnp.logical_not(bounded))
    def _():
        prepare()

        def q_tile(qi, carry):
            _flash_tiles(
                qi,
                logits_fn=lambda c, kv: lax.dot_general(k_ref[kv], qm_ref[qi, c], _NT, preferred_element_type=F32),
                bias_fn=lambda c, j: bias_ref[j],
                mask_fn=lambda c, kv: None,
                s_ref=s_ref, p_ref=p_ref.at[0], acc_ref=acc_ref.at[qi], **chains)
            return carry
        lax.fori_loop(0, n_tiles, q_tile, 0)
        for qi in range(n_tiles):
            finish(qi)


def _diff_attention(bounded, lam_vecs, sub_g_col, qk4, vt4, bias):
    batch, n_tiles = qk4.shape[0], qk4.shape[1]
    k_col0 = DIFF_WIDTH // LANES
    vec = pl.BlockSpec((1, HEAD_DIM), lambda h, b: (0, 0))
    return pl.pallas_call(
        _diff_kernel,
        grid=(HEADS_DIFF, batch),
        in_specs=[
            pl.BlockSpec(memory_space=pltpu.SMEM),
            vec, vec, vec, vec,
            pl.BlockSpec((2 * HEAD_DIM, 1), lambda h, b: (0, 0)),
            pl.BlockSpec((None, n_tiles, ATT_TILE, LANES), lambda h, b: (b, 0, 0, h)),
            pl.BlockSpec((None, n_tiles, ATT_TILE, LANES), lambda h, b: (b, 0, 0, k_col0 + h)),
            pl.BlockSpec((None, n_tiles, LANES, ATT_TILE), lambda h, b: (b, 0, h, 0)),
            pl.BlockSpec((None, n_tiles, ATT_TILE, ATT_TILE), lambda h, b: (h, 0, 0, 0)),
        ],
        out_specs=pl.BlockSpec((None, n_tiles, ATT_TILE, LANES), lambda h, b: (b, 0, 0, h)),
        out_shape=jax.ShapeDtypeStruct((batch, n_tiles, ATT_TILE, DIFF_WIDTH), BF16),
        scratch_shapes=_attention_scratch(n_tiles, 2 * HEAD_DIM),
        compiler_params=_params("arbitrary", "arbitrary"),
        name="diff_attention",
    )(bounded, *lam_vecs, sub_g_col, qk4, qk4, vt4, bias)


def _split3(v):
    hi = v.astype(BF16)
    r1 = v - hi.astype(F32)
    mid = r1.astype(BF16)
    lo = (r1 - mid.astype(F32)).astype(BF16)
    return hi, mid, lo


def _block_mask(gate, own):
    row = lax.broadcasted_iota(jnp.int32, gate.shape, 0)
    rank = jnp.zeros(gate.shape, jnp.int32)
    for m in range(own):
        gm = gate[m:m + 1, :]
        beats = (gm > gate) | ((gm == gate) & (row > m))
        rank = rank + jnp.where(beats, 1, 0)
    keep = ((rank < MOBA_TOPK) & (row < own)) | (row == own)
    return jnp.where(keep, 0.0, NEG_INF).astype(F32)


def _moba_kernel(bounded_ref, q_ref, k_ref, vt_ref, bias_ref, o_ref, kmean_ref, mask_ref, qm_ref, s_ref, p_ref,
                 acc_ref):
    n_tiles = q_ref.shape[0]

    def split_queries():
        for qi in range(n_tiles):
            qm_ref[qi, 0], qm_ref[qi, 1] = _half_lane_split(q_ref[qi])

    def block_masks(q_tiles):
        for n in range(n_tiles):
            kmean_ref[n:n + 1, :] = jnp.mean(k_ref[n].astype(F32), axis=0, keepdims=True)
        per_head = zip(*[_half_lane_split(term) for term in _split3(kmean_ref[...])])
        gate_lhs = jnp.concatenate([term for head_terms in per_head for term in head_terms], axis=0)
        n_terms = gate_lhs.shape[0] // (2 * n_tiles)
        for qi in q_tiles:
            terms = lax.dot_general(gate_lhs, q_ref[qi], _NT, preferred_element_type=F32)
            for half in range(2):
                rows = [terms[(half * n_terms + t) * n_tiles:(half * n_terms + t + 1) * n_tiles]
                        for t in range(n_terms)]
                mask_ref[qi, half] = _block_mask(sum(rows[1:], rows[0]), qi)

    def keep_row(c, qi, kv):
        if qi <= MOBA_TOPK or kv == qi:
            return None
        return jnp.where(mask_ref[qi, c, kv:kv + 1, :] < -1.0, 0.0, 1.0)

    def finish(qi):
        halves = [acc_ref[qi, half] for half in range(2)]
        o = jnp.concatenate([a[:HEAD_DIM] / a[HEAD_DIM:HEAD_DIM + 1] for a in halves], axis=0)
        o_ref[qi] = o.T.astype(BF16)

    chains = dict(
        n_chains=2,
        values_fn=lambda c, kv: _with_ones_rows(vt_ref[kv, c * HEAD_DIM:(c + 1) * HEAD_DIM, :]))
    bounded = bounded_ref[0] > 0

    @pl.when(bounded)
    def _():
        split_queries()
        _bounded_tiles(
            n_tiles,
            logits_fn=lambda c, qi, kv: lax.dot_general(k_ref[kv], qm_ref[qi, c], _NT, preferred_element_type=F32),
            bias_fn=lambda c, d: bias_ref[c, d],
            keep_fn=keep_row,
            between_fn=lambda d: block_masks(range(MOBA_TOPK + 1, n_tiles)) if d == 0 else None,
            finish_fn=finish,
            p_ref=p_ref, acc_ref=acc_ref, **chains)

    @pl.when(jnp.logical_not(bounded))
    def _():
        split_queries()
        block_masks(range(n_tiles))

        def q_tile(qi, carry):
            _flash_tiles(
                qi,
                logits_fn=lambda c, kv: lax.dot_general(k_ref[kv], qm_ref[qi, c], _NT, preferred_element_type=F32),
                bias_fn=lambda c, j: bias_ref[c, j],
                mask_fn=lambda c, kv: mask_ref[qi, c, pl.ds(kv, 1), :],
                s_ref=s_ref, p_ref=p_ref.at[0], acc_ref=acc_ref.at[qi], **chains)
            return carry
        lax.fori_loop(0, n_tiles, q_tile, 0)
        for qi in range(n_tiles):
            finish(qi)


def _moba_attention(bounded, qk4, vt4, bias_pairs):
    batch, n_tiles = qk4.shape[0], qk4.shape[1]
    q_col0 = 2 * DIFF_WIDTH // LANES
    k_col0 = q_col0 + MOBA_WIDTH // LANES
    v_row0 = DIFF_WIDTH // LANES
    pair0 = HEADS_DIFF // 2
    return pl.pallas_call(
        _moba_kernel,
        grid=(HEADS_MOBA // 2, batch),
        in_specs=[
            pl.BlockSpec(memory_space=pltpu.SMEM),
            pl.BlockSpec((None, n_tiles, ATT_TILE, LANES), lambda h, b: (b, 0, 0, q_col0 + h)),
            pl.BlockSpec((None, n_tiles, ATT_TILE, LANES), lambda h, b: (b, 0, 0, k_col0 + h)),
            pl.BlockSpec((None, n_tiles, LANES, ATT_TILE), lambda h, b: (b, 0, v_row0 + h, 0)),
            pl.BlockSpec((None, 2, n_tiles, ATT_TILE, ATT_TILE), lambda h, b: (pair0 + h, 0, 0, 0, 0)),
        ],
        out_specs=pl.BlockSpec((None, n_tiles, ATT_TILE, LANES), lambda h, b: (b, 0, 0, h)),
        out_shape=jax.ShapeDtypeStruct((batch, n_tiles, ATT_TILE, MOBA_WIDTH), BF16),
        scratch_shapes=[
            pltpu.VMEM((n_tiles, LANES), F32),
            pltpu.VMEM((n_tiles, 2, n_tiles, ATT_TILE), F32),
        ] + _attention_scratch(n_tiles, HEAD_DIM),
        compiler_params=_params("arbitrary", "arbitrary"),
        name="moba_attention",
    )(bounded, qk4, qk4, vt4, bias_pairs)


def _load_token_rows(ref, n_rows, lead=(), first=0):
    chunks = [ref[lead + (pl.ds(first * ROW_TILE + c, n_rows, stride=ROW_TILE), slice(None))]
              for c in range(ROW_TILE)]
    return jnp.concatenate(chunks, axis=1)


def _store_token_rows(ref, value, first=0):
    n_rows = value.shape[0]
    for c in range(ROW_TILE):
        ref[pl.ds(first * ROW_TILE + c, n_rows, stride=ROW_TILE), :] = value[:, c * LANES:(c + 1) * LANES]


def _token_tile(ref, t):
    return ref.at[pl.ds(pl.multiple_of(t * ROW_TILE, ROW_TILE), ROW_TILE)]


def _first_argmax(v):
    top = jnp.max(v, axis=0, keepdims=True)
    row = lax.broadcasted_iota(jnp.int32, v.shape, 0)
    idx = jnp.min(jnp.where(v == top, row, v.shape[0]), axis=0, keepdims=True)
    return top, idx


def _out_proj_kernel(yd_ref, ym_ref, x_ref, wd_ref, wm_ref, g2_ref, wr_ref, tri_ref,
                     x1_ref, ri_ref, rf_ref, cnt_ref, run_ref):
    @pl.when(lax.rem(pl.program_id(0), pl.num_programs(0) // MOE_TOKEN_PARTS) == 0)
    def _():
        run_ref[...] = jnp.zeros_like(run_ref)

    def project(t):
        rows = slice(t * ROUTE_TILE, (t + 1) * ROUTE_TILE)
        x1 = (x_ref[rows, :] + jnp.dot(yd_ref[rows, :], wd_ref[...], preferred_element_type=F32)
              + jnp.dot(ym_ref[rows, :], wm_ref[...], preferred_element_type=F32))
        _store_token_rows(x1_ref, x1, first=t * ROUTE_TILE)
        ms = jnp.mean(x1 * x1, axis=-1, keepdims=True)
        return (x1 * lax.rsqrt(ms + RMS_EPS) * g2_ref[...]).astype(BF16)

    def pick(h2):
        lg = lax.dot_general(wr_ref[...], h2, _NT, preferred_element_type=F32)
        g_logits = lg[N_EXPERTS:N_EXPERTS + N_GROUPS, :]
        g_top, g_idx = _first_argmax(g_logits)
        p_group = 1.0 / jnp.sum(jnp.exp(g_logits - g_top), axis=0, keepdims=True)
        e_logits = lg[0:EXPERTS_PER_GROUP, :]
        for g in range(1, N_GROUPS):
            e_logits = jnp.where(g_idx == g, lg[g * EXPERTS_PER_GROUP:(g + 1) * EXPERTS_PER_GROUP, :], e_logits)
        v1, i1 = _first_argmax(e_logits)
        row = lax.broadcasted_iota(jnp.int32, e_logits.shape, 0)
        v2, i2 = _first_argmax(jnp.where(row == i1, -jnp.inf, e_logits))
        ratio = jnp.exp(v2 - v1)
        w1 = p_group / (1.0 + ratio)
        w2 = p_group * ratio / (1.0 + ratio)
        e1 = g_idx * EXPERTS_PER_GROUP + i1
        e2 = g_idx * EXPERTS_PER_GROUP + i2
        erow = lax.broadcasted_iota(jnp.int32, (N_EXPERTS, ROUTE_TILE), 0)
        hit1 = erow == e1
        hit2 = erow == e2
        onehot = jnp.where(hit1 | hit2, 1.0, 0.0).astype(F32)
        return e1, e2, w1, w2, hit1, hit2, onehot

    def place(t, e1, e2, w1, w2, hit1, hit2, onehot):
        before = jnp.dot(onehot.astype(BF16), tri_ref[...], preferred_element_type=F32) + run_ref[...]
        r1 = jnp.sum(jnp.where(hit1, before, 0.0), axis=0, keepdims=True).astype(jnp.int32)
        r2 = jnp.sum(jnp.where(hit2, before, 0.0), axis=0, keepdims=True).astype(jnp.int32)
        run_ref[...] = run_ref[...] + jnp.sum(onehot, axis=1, keepdims=True)
        lanes = slice(t * ROUTE_TILE, (t + 1) * ROUTE_TILE)
        out_row = lax.broadcasted_iota(jnp.int32, (8, ROUTE_TILE), 0)
        ri_ref[:, lanes] = jnp.where(out_row == 0, e1, jnp.where(out_row == 1, e2,
                                     jnp.where(out_row == 2, r1, jnp.where(out_row == 3, r2, 0))))
        rf_ref[:, lanes] = jnp.where(out_row == 0, w1, jnp.where(out_row == 1, w2, 0.0))

    n_tiles = OUT_ROWS // ROUTE_TILE
    projected, picked = {}, {}
    for t in range(n_tiles + 2):
        if t < n_tiles:
            projected[t] = project(t)
        if 0 <= t - 1 < n_tiles:
            picked[t - 1] = pick(projected.pop(t - 1))
        if 0 <= t - 2 < n_tiles:
            place(t - 2, *picked.pop(t - 2))
    cnt_ref[...] = jnp.broadcast_to(run_ref[...], cnt_ref.shape).astype(jnp.int32)


def _out_proj(y_d, y_m, x2, wo_d, wo_m, g2, w_router, tri):
    n = x2.shape[0]
    const = lambda i: (0, 0)
    steps_per_part = n // OUT_ROWS // MOE_TOKEN_PARTS
    return pl.pallas_call(
        _out_proj_kernel,
        grid=(n // OUT_ROWS,),
        in_specs=[
            pl.BlockSpec((OUT_ROWS, DIFF_WIDTH), lambda i: (i, 0)),
            pl.BlockSpec((OUT_ROWS, MOBA_WIDTH), lambda i: (i, 0)),
            pl.BlockSpec((OUT_ROWS, D_MODEL), lambda i: (i, 0)),
            pl.BlockSpec((DIFF_WIDTH, D_MODEL), const),
            pl.BlockSpec((MOBA_WIDTH, D_MODEL), const),
            pl.BlockSpec((1, D_MODEL), const),
            pl.BlockSpec((ROUTER_ROWS, D_MODEL), const),
            pl.BlockSpec((ROUTE_TILE, ROUTE_TILE), const),
        ],
        out_specs=[
            pl.BlockSpec((OUT_ROWS * ROW_TILE, LANES), lambda i: (i, 0)),
            pl.BlockSpec((8, OUT_ROWS), lambda i: (0, i)),
            pl.BlockSpec((8, OUT_ROWS), lambda i: (0, i)),
            pl.BlockSpec((None, N_EXPERTS, LANES), lambda i: (i // steps_per_part, 0, 0)),
        ],
        out_shape=[
            jax.ShapeDtypeStruct((n * ROW_TILE, LANES), F32),
            jax.ShapeDtypeStruct((8, n), jnp.int32),
            jax.ShapeDtypeStruct((8, n), F32),
            jax.ShapeDtypeStruct((MOE_TOKEN_PARTS, N_EXPERTS, LANES), jnp.int32),
        ],
        scratch_shapes=[pltpu.VMEM((N_EXPERTS, 1), F32)],
        compiler_params=_params("arbitrary"),
        name="out_proj_route",
    )(y_d, y_m, x2, wo_d, wo_m, g2, w_router, tri)


def _dispatch_step(i, n_steps, first_tile, zchunk_ref, dest_ref, x1_hbm, xs_hbm, zero_ref, stage_ref, zero_sem,
                   load_sem, row_sem):
    last = n_steps - 1
    slot = lax.rem(i, 2)
    chunk_rows = MOE_CHUNK * ROW_TILE
    tile_rows = MOVE_ROWS * ROW_TILE

    def load(t, s):
        start = pl.multiple_of((first_tile + t) * tile_rows, tile_rows)
        return pltpu.make_async_copy(x1_hbm.at[pl.ds(start, tile_rows)], stage_ref.at[s], load_sem.at[s])

    def wait_rows(s):
        for _ in range(EXPERTS_PER_TOKEN):
            pltpu.make_async_copy(stage_ref.at[s], xs_hbm.at[pl.ds(0, tile_rows)], row_sem.at[s]).wait()

    def zero_copy(c):
        start = pl.multiple_of(c * chunk_rows, chunk_rows)
        return pltpu.make_async_copy(zero_ref, xs_hbm.at[pl.ds(start, chunk_rows)], zero_sem)

    @pl.when(i == 0)
    def _():
        zero_ref[...] = jnp.zeros_like(zero_ref)

        def start_one(c, carry):
            @pl.when(zchunk_ref[c] > 0)
            def _():
                zero_copy(c).start()
            return carry

        def wait_one(c, carry):
            @pl.when(zchunk_ref[c] > 0)
            def _():
                zero_copy(c).wait()
            return carry

        lax.fori_loop(0, zchunk_ref.shape[0], start_one, 0)
        lax.fori_loop(0, zchunk_ref.shape[0], wait_one, 0)
        load(0, 0).start()

    @pl.when(i > 0)
    def _():
        wait_rows(1 - slot)

    @pl.when(i < last)
    def _():
        load(i + 1, 1 - slot).start()

    load(i, slot).wait()

    def send_row(r, carry):
        src = _token_tile(stage_ref.at[slot], r)
        for k in range(EXPERTS_PER_TOKEN):
            pltpu.make_async_copy(src, _token_tile(xs_hbm, dest_ref[0, k, r]), row_sem.at[slot]).start(priority=k)
        return carry

    lax.fori_loop(0, MOVE_ROWS, send_row, 0, unroll=8)

    @pl.when(i == last)
    def _():
        wait_rows(slot)


def _dispatch_scratch():
    return [pltpu.VMEM((MOE_CHUNK * ROW_TILE, LANES), F32),
            pltpu.VMEM((2, MOVE_ROWS * ROW_TILE, LANES), F32),
            pltpu.SemaphoreType.DMA(()),
            pltpu.SemaphoreType.DMA((2,)),
            pltpu.SemaphoreType.DMA((2,))]


def _dispatch_kernel(first_tile, zchunk_ref, dest_ref, x1_hbm, xs_hbm, *scratch):
    _dispatch_step(pl.program_id(0), pl.num_programs(0), first_tile, zchunk_ref, dest_ref, x1_hbm, xs_hbm, *scratch)


def _dispatch(part, zchunk, dest3, x1, n_slots):
    tiles = x1.shape[0] // ROW_TILE // MOVE_ROWS // MOE_TOKEN_PARTS
    first_tile = part * tiles
    grid_spec = pltpu.PrefetchScalarGridSpec(
        num_scalar_prefetch=1,
        grid=(tiles,),
        in_specs=[
            pl.BlockSpec((1, EXPERTS_PER_TOKEN, MOVE_ROWS), lambda i, zc: (first_tile + i, 0, 0),
                         memory_space=pltpu.SMEM),
            pl.BlockSpec(memory_space=pl.ANY),
        ],
        out_specs=pl.BlockSpec(memory_space=pl.ANY),
        scratch_shapes=_dispatch_scratch(),
    )
    return pl.pallas_call(
        functools.partial(_dispatch_kernel, first_tile),
        grid_spec=grid_spec,
        out_shape=jax.ShapeDtypeStruct((n_slots * ROW_TILE, LANES), F32),
        compiler_params=_params("arbitrary"),
        name="moe_dispatch",
    )(zchunk, dest3, x1)


def _expert_step(c, na_ref, xs_ref, g2_ref, wg_ref, wu_ref, wd_ref, o_ref):
    active = c < na_ref[0]

    @pl.when(jnp.logical_not(active))
    def _():
        o_ref[...] = jnp.zeros_like(o_ref)

    @pl.when(active)
    def _():
        part_rows = MOE_CHUNK // EXPERT_ROW_PARTS

        def gate_up(r):
            x = _load_token_rows(xs_ref, part_rows, first=r * part_rows)
            ms = jnp.mean(x * x, axis=-1, keepdims=True)
            h = (x * lax.rsqrt(ms + RMS_EPS) * g2_ref[...]).astype(BF16)
            return (jnp.dot(h, wg_ref[...], preferred_element_type=F32),
                    jnp.dot(h, wu_ref[...], preferred_element_type=F32))

        def down(r, gate, up):
            hid = (gate * jax.nn.sigmoid(gate) * up).astype(BF16)
            _store_token_rows(o_ref, jnp.dot(hid, wd_ref[...], preferred_element_type=F32), first=r * part_rows)

        pending = None
        for r in range(EXPERT_ROW_PARTS):
            current = (r,) + gate_up(r)
            if pending is not None:
                down(*pending)
            pending = current
        down(*pending)


def _expert_kernel(ce_ref, na_ref, *refs):
    _expert_step(pl.program_id(0), na_ref, *refs)


def _expert_specs():
    rows = lambda c, ce, na, *_: (jnp.minimum(c, na[0] - 1), 0)
    expert = lambda c, ce, na, *_: (ce[c], 0, 0)
    in_specs = [
        pl.BlockSpec((MOE_CHUNK * ROW_TILE, LANES), rows),
        pl.BlockSpec((1, D_MODEL), lambda c, *_: (0, 0)),
        pl.BlockSpec((None, D_MODEL, EXPERT_HIDDEN), expert),
        pl.BlockSpec((None, D_MODEL, EXPERT_HIDDEN), expert),
        pl.BlockSpec((None, EXPERT_HIDDEN, D_MODEL), expert),
    ]
    return in_specs, pl.BlockSpec((MOE_CHUNK * ROW_TILE, LANES), lambda c, *_: (c, 0))


def _experts(chunk_e, n_active, xs, g2, wg, wu, wd):
    n_slots = xs.shape[0] // ROW_TILE
    in_specs, out_spec = _expert_specs()
    grid_spec = pltpu.PrefetchScalarGridSpec(
        num_scalar_prefetch=2, grid=(n_slots // MOE_CHUNK,), in_specs=in_specs, out_specs=out_spec)
    return pl.pallas_call(
        _expert_kernel,
        grid_spec=grid_spec,
        out_shape=jax.ShapeDtypeStruct((n_slots * ROW_TILE, LANES), F32),
        compiler_params=_params("arbitrary"),
        name="moe_experts",
    )(chunk_e, n_active, xs, g2, wg, wu, wd)


def _experts_dispatch_kernel(first_tile, n_tiles, ce_ref, na_ref, zchunk_ref,
                             xs_ref, g2_ref, wg_ref, wu_ref, wd_ref, dest_ref, x1_hbm,
                             o_ref, xs_next_hbm, *scratch):
    i = pl.program_id(0)

    @pl.when(i < n_tiles)
    def _():
        _dispatch_step(i, n_tiles, first_tile, zchunk_ref, dest_ref, x1_hbm, xs_next_hbm, *scratch)

    _expert_step(i, na_ref, xs_ref, g2_ref, wg_ref, wu_ref, wd_ref, o_ref)


def _experts_dispatch(chunk_e, n_active, zchunk_next, xs, g2, wg, wu, wd, next_part, dest3, x1):
    n_slots = xs.shape[0] // ROW_TILE
    n_chunks = n_slots // MOE_CHUNK
    tiles = x1.shape[0] // ROW_TILE // MOVE_ROWS // MOE_TOKEN_PARTS
    first_tile = next_part * tiles
    assert n_chunks >= tiles
    in_specs, out_spec = _expert_specs()
    in_specs += [
        pl.BlockSpec((1, EXPERTS_PER_TOKEN, MOVE_ROWS),
                     lambda c, *_: (first_tile + jnp.minimum(c, tiles - 1), 0, 0), memory_space=pltpu.SMEM),
        pl.BlockSpec(memory_space=pl.ANY),
    ]
    grid_spec = pltpu.PrefetchScalarGridSpec(
        num_scalar_prefetch=3, grid=(n_chunks,), in_specs=in_specs,
        out_specs=[out_spec, pl.BlockSpec(memory_space=pl.ANY)],
        scratch_shapes=_dispatch_scratch())
    slots = jax.ShapeDtypeStruct((n_slots * ROW_TILE, LANES), F32)
    return pl.pallas_call(
        functools.partial(_experts_dispatch_kernel, first_tile, tiles),
        grid_spec=grid_spec,
        out_shape=[slots, slots],
        compiler_params=_params("arbitrary"),
        name="moe_experts_dispatch",
    )(chunk_e, n_active, zchunk_next, xs, g2, wg, wu, wd, dest3, x1)


def _combine_kernel(dest_ref, dest_next_ref, w_ref, x1_ref, *rest):
    outs_hbm, (y_ref, buf_ref, sem_ref) = rest[:MOE_TOKEN_PARTS], rest[MOE_TOKEN_PARTS:]
    i = pl.program_id(0)
    slot = lax.rem(i, 2)
    tiles_per_part = pl.num_programs(0) // MOE_TOKEN_PARTS

    def fetch(tile, dref, s):
        for part, src_hbm in enumerate(outs_hbm):
            @pl.when(tile // tiles_per_part == part)
            def _():
                def fetch_row(r, carry):
                    for k in range(EXPERTS_PER_TOKEN):
                        pltpu.make_async_copy(_token_tile(src_hbm, dref[0, k, r]),
                                              _token_tile(buf_ref.at[s, k], r), sem_ref.at[s]).start(priority=k)
                    return carry
                lax.fori_loop(0, MOVE_ROWS, fetch_row, 0, unroll=8)

    @pl.when(i == 0)
    def _():
        fetch(i, dest_ref, 0)

    @pl.when(i + 1 < pl.num_programs(0))
    def _():
        fetch(i + 1, dest_next_ref, 1 - slot)

    for k in range(EXPERTS_PER_TOKEN):
        pltpu.make_async_copy(outs_hbm[0].at[pl.ds(0, MOVE_ROWS * ROW_TILE)], buf_ref.at[slot, k],
                              sem_ref.at[slot]).wait()
    w = w_ref[...]
    y_ref[...] = (_load_token_rows(x1_ref, MOVE_ROWS)
                  + w[:, 0:1] * _load_token_rows(buf_ref, MOVE_ROWS, (slot, 0))
                  + w[:, 1:2] * _load_token_rows(buf_ref, MOVE_ROWS, (slot, 1)))


def _combine(dest3, wts, x1, outs_parts):
    n = x1.shape[0] // ROW_TILE
    steps = n // MOVE_ROWS
    dest_block = (1, EXPERTS_PER_TOKEN, MOVE_ROWS)
    return pl.pallas_call(
        _combine_kernel,
        grid=(steps,),
        in_specs=[
            pl.BlockSpec(dest_block, lambda i: (i, 0, 0), memory_space=pltpu.SMEM),
            pl.BlockSpec(dest_block, lambda i: (jnp.minimum(i + 1, steps - 1), 0, 0), memory_space=pltpu.SMEM),
            pl.BlockSpec((MOVE_ROWS, EXPERTS_PER_TOKEN), lambda i: (i, 0)),
            pl.BlockSpec((MOVE_ROWS * ROW_TILE, LANES), lambda i: (i, 0)),
        ] + [pl.BlockSpec(memory_space=pl.ANY)] * MOE_TOKEN_PARTS,
        out_specs=pl.BlockSpec((MOVE_ROWS, D_MODEL), lambda i: (i, 0)),
        out_shape=jax.ShapeDtypeStruct((n, D_MODEL), F32),
        scratch_shapes=[
            pltpu.VMEM((2, EXPERTS_PER_TOKEN, MOVE_ROWS * ROW_TILE, LANES), F32),
            pltpu.SemaphoreType.DMA((2,)),
        ],
        compiler_params=_params("arbitrary"),
        name="moe_combine",
    )(dest3, dest3, wts, x1, *outs_parts)


def _layer(x, norm1_g, w_in, diff_q_g, diff_k_g, lambda_q1, lambda_k1, lambda_q2, lambda_k2, diff_sub_g,
           moba_q_g, moba_k_g, rel_bias, w_out, norm2_g, router_group, router_expert, w_gate, w_up, w_down):
    batch, seq, _ = x.shape
    n = batch * seq
    n_tiles = seq // ATT_TILE
    scale = HEAD_DIM ** -0.5 * LOG2E

    d3 = 3 * DIFF_WIDTH
    w_perm = jnp.concatenate([w_in[:, :2 * DIFF_WIDTH], w_in[:, d3:d3 + 2 * MOBA_WIDTH],
                              w_in[:, 2 * DIFF_WIDTH:d3], w_in[:, d3 + 2 * MOBA_WIDTH:]], axis=1).astype(BF16)
    reps_d, reps_m = DIFF_WIDTH // HEAD_DIM, MOBA_WIDTH // HEAD_DIM
    post_gain = jnp.concatenate([jnp.tile(diff_q_g * scale, reps_d), jnp.tile(diff_k_g, reps_d),
                                 jnp.tile(moba_q_g * scale, reps_m), jnp.tile(moba_k_g, reps_m)])[None, :]
    head_of = np.arange(2 * LANES) // HEAD_DIM
    grp = jnp.asarray((head_of[:, None] == head_of[None, :]) / HEAD_DIM, BF16)

    bias = _bias_tiles(rel_bias, n_tiles)
    qk, vt4 = _in_proj(x.reshape(n, D_MODEL), norm1_g[None, :], w_perm, post_gain, grp, batch, seq)
    qk4 = qk.reshape(batch, n_tiles, ATT_TILE, QK_WIDTH)
    lam_vecs = [v[None, :] for v in (lambda_q1, lambda_k1, lambda_q2, lambda_k2)]
    y_d = _diff_attention(_logits_bounded(diff_q_g * scale, diff_k_g, rel_bias[:, :HEADS_DIFF]),
                          lam_vecs, diff_sub_g[:, None], qk4, vt4, bias)
    y_m = _moba_attention(_logits_bounded(moba_q_g * scale, moba_k_g, rel_bias[:, HEADS_DIFF:]),
                          qk4, vt4, bias.reshape(N_HEADS_TOTAL // 2, 2, n_tiles, ATT_TILE, ATT_TILE))

    w_router = jnp.concatenate([
        router_expert.transpose(0, 2, 1).reshape(N_EXPERTS, D_MODEL), router_group.T,
        jnp.zeros((ROUTER_ROWS - N_EXPERTS - N_GROUPS, D_MODEL), F32)], axis=0).astype(BF16)
    tri = jnp.asarray(np.triu(np.ones((ROUTE_TILE, ROUTE_TILE)), k=1), BF16)
    wo = w_out.astype(BF16)
    x1, route_i, route_f, cnt = _out_proj(y_d.reshape(n, DIFF_WIDTH), y_m.reshape(n, MOBA_WIDTH),
                                          x.reshape(n, D_MODEL), wo[:DIFF_WIDTH], wo[DIFF_WIDTH:],
                                          norm2_g[None, :], w_router, tri)

    parts = MOE_TOKEN_PARTS
    counts = cnt[:, :, 0]
    pad_counts = ((counts + MOE_CHUNK - 1) // MOE_CHUNK) * MOE_CHUNK
    pad_end = jnp.cumsum(pad_counts, axis=1)
    pad_start = pad_end - pad_counts
    experts = jnp.arange(N_EXPERTS, dtype=jnp.int32)
    picked = route_i[0:2].reshape(EXPERTS_PER_TOKEN, parts, n // parts)
    start_of = jnp.sum(jnp.where(picked[..., None] == experts, pad_start[None, :, None, :], 0), axis=-1)
    dest = start_of.reshape(EXPERTS_PER_TOKEN, n) + route_i[2:4]
    dest3 = dest.reshape(EXPERTS_PER_TOKEN, n // MOVE_ROWS, MOVE_ROWS).transpose(1, 0, 2)
    n_slots = n // parts * EXPERTS_PER_TOKEN + N_EXPERTS * MOE_CHUNK
    n_chunks = n_slots // MOE_CHUNK
    chunk_id = jnp.arange(n_chunks, dtype=jnp.int32)
    chunk_e = jnp.minimum(jnp.sum((pad_end[:, None, :] <= chunk_id[None, :, None] * MOE_CHUNK).astype(jnp.int32),
                                  axis=2), N_EXPERTS - 1)
    n_active = (pad_end[:, -1:] // MOE_CHUNK).astype(jnp.int32)
    next_e = jnp.concatenate([chunk_e[:, 1:], jnp.full((parts, 1), N_EXPERTS, jnp.int32)], axis=1)
    zchunk = ((chunk_id[None, :] >= n_active - 1) | (chunk_e != next_e)).astype(jnp.int32)

    weights = (norm2_g[None, :], w_gate.astype(BF16), w_up.astype(BF16), w_down.astype(BF16))
    xs = _dispatch(0, zchunk[0], dest3, x1, n_slots)
    outs = []
    for p in range(parts):
        if p + 1 < parts:
            out_p, xs_next = _experts_dispatch(chunk_e[p], n_active[p], zchunk[p + 1], xs, *weights, p + 1, dest3, x1)
        else:
            out_p, xs_next = _experts(chunk_e[p], n_active[p], xs, *weights), None
        outs.append(out_p)
        xs = xs_next
    y = _combine(dest3, route_f[0:2].T, x1, outs)
    return y.reshape(batch, seq, D_MODEL)


def kernel(x, norm1_g, w_in, diff_q_g, diff_k_g, lambda_q1, lambda_k1, lambda_q2, lambda_k2, diff_sub_g,
           moba_q_g, moba_k_g, rel_bias, w_out, norm2_g, router_group, router_expert, w_gate, w_up, w_down):
    assert x.shape[1] % PROJ_ROWS == 0 and x.shape[2] == D_MODEL and norm1_g.shape[0] == 1
    return _layer(x, norm1_g[0], w_in[0], diff_q_g[0], diff_k_g[0], lambda_q1[0], lambda_k1[0], lambda_q2[0],
                  lambda_k2[0], diff_sub_g[0], moba_q_g[0], moba_k_g[0], rel_bias, w_out[0], norm2_g[0],
                  router_group[0], router_expert[0], w_gate[0], w_up[0], w_down[0])
```

```python
import functools
import math

import numpy as np
import jax
import jax.numpy as jnp
from jax import lax
from jax.experimental import pallas as pl
from jax.experimental.pallas import tpu as pltpu

D_MODEL = 1024
HEAD_DIM = 64
HEADS_DIFF = 4
HEADS_MOBA = 8
N_HEADS_TOTAL = HEADS_DIFF + HEADS_MOBA
DIFF_WIDTH = HEADS_DIFF * 2 * HEAD_DIM
MOBA_WIDTH = HEADS_MOBA * HEAD_DIM
QK_WIDTH = 2 * DIFF_WIDTH + 2 * MOBA_WIDTH
V_WIDTH = DIFF_WIDTH + MOBA_WIDTH
MOBA_TOPK = 3
NUM_BUCKETS = 32
MAX_DISTANCE = 2048
N_GROUPS = 4
EXPERTS_PER_GROUP = 8
N_EXPERTS = N_GROUPS * EXPERTS_PER_GROUP
EXPERTS_PER_TOKEN = 2
EXPERT_HIDDEN = 256
RMS_EPS = 1e-6
NEG_INF = -1e30
LAMBDA_INIT = 0.8 - 0.6 * math.exp(-0.3 * 0)

LANES = 128
ROW_TILE = D_MODEL // LANES
ATT_TILE = 256
ONES_ROWS = 16
LOGIT_LIMIT = 80.0
ROUNDING_MARGIN = 1.05
LOG2E = math.log2(math.e)
PROJ_ROWS = 1024
OUT_ROWS = 1024
ROUTE_TILE = 256
MOE_CHUNK = 512
MOE_TOKEN_PARTS = 2
EXPERT_ROW_PARTS = 4
MOVE_ROWS = 512
ROUTER_ROWS = 40
VMEM_LIMIT = 48 * 1024 * 1024

F32 = jnp.float32
BF16 = jnp.bfloat16
_NT = (((1,), (1,)), ((), ()))


def _t5_thresholds():
    n = np.arange(0, 1 << 16)
    max_exact = NUM_BUCKETS // 2
    nf = np.maximum(n, 1).astype(np.float32)
    large = max_exact + (np.log(nf / np.float32(max_exact)) / np.float32(math.log(MAX_DISTANCE / max_exact))
                         * np.float32(NUM_BUCKETS - max_exact)).astype(np.int32)
    bucket = np.where(n < max_exact, n, np.minimum(large, NUM_BUCKETS - 1))
    return [int(np.searchsorted(bucket, b, side="left")) for b in range(1, NUM_BUCKETS)]


_T5_THRESHOLDS = _t5_thresholds()


def _params(*sem):
    return pltpu.CompilerParams(dimension_semantics=sem, vmem_limit_bytes=VMEM_LIMIT)


def _bias_kernel(tab_ref, out_ref):
    h = pl.program_id(0)
    kj = lax.broadcasted_iota(jnp.int32, (ATT_TILE, ATT_TILE), 0)
    qi = lax.broadcasted_iota(jnp.int32, (ATT_TILE, ATT_TILE), 1)
    for d in range(out_ref.shape[0]):
        dist = d * ATT_TILE + qi - kj
        lo, hi = d * ATT_TILE - (ATT_TILE - 1), d * ATT_TILE + (ATT_TILE - 1)
        base = sum(thr <= max(lo, 0) for thr in _T5_THRESHOLDS)
        val = jnp.full((ATT_TILE, ATT_TILE), tab_ref[base, h], F32)
        for b, thr in enumerate(_T5_THRESHOLDS, start=1):
            if max(lo, 0) < thr <= hi:
                val = jnp.where(dist >= thr, tab_ref[b, h], val)
        val = val * LOG2E
        out_ref[d] = jnp.where(dist < 0, NEG_INF, val) if lo < 0 else val


def _bias_tiles(rel_bias, n_diag):
    return pl.pallas_call(
        _bias_kernel,
        grid=(N_HEADS_TOTAL,),
        in_specs=[pl.BlockSpec(memory_space=pltpu.SMEM)],
        out_specs=pl.BlockSpec((None, n_diag, ATT_TILE, ATT_TILE), lambda h: (h, 0, 0, 0)),
        out_shape=jax.ShapeDtypeStruct((N_HEADS_TOTAL, n_diag, ATT_TILE, ATT_TILE), F32),
        compiler_params=_params("arbitrary"),
        name="bias_tiles",
    )(rel_bias)


def _in_proj_kernel(x_ref, g1_ref, w_ref, pg_ref, grp_ref, qk_ref, vt_ref):
    n_row_tiles = PROJ_ROWS // ATT_TILE
    h = []
    for t in range(n_row_tiles):
        x = x_ref[t * ATT_TILE:(t + 1) * ATT_TILE, :]
        ms = jnp.mean(x * x, axis=-1, keepdims=True)
        h.append((x * lax.rsqrt(ms + RMS_EPS) * g1_ref[...]).astype(BF16))

    cw = 2 * LANES

    def finish(c, t, p):
        rows = slice(t * ATT_TILE, (t + 1) * ATT_TILE)
        cols = slice(c * cw, (c + 1) * cw)
        if c < QK_WIDTH // cw:
            msq = jnp.dot((p * p).astype(BF16), grp_ref[...], preferred_element_type=F32)
            qk_ref[rows, cols] = (p * lax.rsqrt(msq + RMS_EPS) * pg_ref[:, cols]).astype(BF16)
        else:
            vt_ref[t, c * cw - QK_WIDTH:(c + 1) * cw - QK_WIDTH, :] = p.T.astype(BF16)

    units = [(c, t) for c in range((QK_WIDTH + V_WIDTH) // cw) for t in range(n_row_tiles)]
    pending = None
    for c, t in units:
        p = jnp.dot(h[t], w_ref[:, c * cw:(c + 1) * cw], preferred_element_type=F32)
        if pending is not None:
            finish(*pending)
        pending = (c, t, p)
    finish(*pending)


def _in_proj(x2, g1, w_perm, post_gain, grp, batch, seq):
    n = x2.shape[0]
    steps_per_seq = seq // PROJ_ROWS
    tiles_per_step = PROJ_ROWS // ATT_TILE
    return pl.pallas_call(
        _in_proj_kernel,
        grid=(n // PROJ_ROWS,),
        in_specs=[
            pl.BlockSpec((PROJ_ROWS, D_MODEL), lambda i: (i, 0)),
            pl.BlockSpec((1, D_MODEL), lambda i: (0, 0)),
            pl.BlockSpec((D_MODEL, QK_WIDTH + V_WIDTH), lambda i: (0, 0)),
            pl.BlockSpec((1, QK_WIDTH), lambda i: (0, 0)),
            pl.BlockSpec((2 * LANES, 2 * LANES), lambda i: (0, 0)),
        ],
        out_specs=[
            pl.BlockSpec((PROJ_ROWS, QK_WIDTH), lambda i: (i, 0)),
            pl.BlockSpec((None, tiles_per_step, V_WIDTH, ATT_TILE),
                         lambda i: (i // steps_per_seq, i % steps_per_seq, 0, 0)),
        ],
        out_shape=[
            jax.ShapeDtypeStruct((n, QK_WIDTH), BF16),
            jax.ShapeDtypeStruct((batch, seq // ATT_TILE, V_WIDTH, ATT_TILE), BF16),
        ],
        compiler_params=_params("arbitrary"),
        name="in_proj",
    )(x2, g1, w_perm, post_gain, grp)


def _flash_tiles(qi, n_chains, logits_fn, values_fn, bias_fn, mask_fn, s_ref, p_ref, acc_ref):
    for c in range(n_chains):
        s_ref[0, c] = logits_fn(c, qi)
    p_ref[...] = jnp.zeros_like(p_ref)
    acc_ref[...] = jnp.zeros_like(acc_ref)

    def add_values(c, kv, alpha):
        acc_ref[c] = acc_ref[c] * alpha + jnp.dot(values_fn(c, kv), p_ref[c], preferred_element_type=F32)

    def step(j, carry):
        cur = lax.rem(j, 2)
        kv = qi - j
        out = []
        for c in range(n_chains):
            add_values(c, jnp.minimum(kv + 1, qi), carry[2 * c + 1])
        for c in range(n_chains):
            s = s_ref[cur, c] + bias_fn(c, j)
            tile_max = jnp.max(s, axis=0, keepdims=True)
            shift = mask_fn(c, kv)
            if shift is not None:
                tile_max = tile_max + shift
            m_new = jnp.maximum(carry[2 * c], tile_max)
            alpha = jnp.exp2(carry[2 * c] - m_new)
            p_ref[c] = jnp.exp2(s - (m_new if shift is None else m_new - shift)).astype(BF16)
            out += [m_new, alpha]
        for c in range(n_chains):
            s_ref[1 - cur, c] = logits_fn(c, jnp.maximum(kv - 1, 0))
        return tuple(out)

    m0 = jnp.full((1, ATT_TILE), NEG_INF, F32)
    one = jnp.ones((1, ATT_TILE), F32)
    carry = lax.fori_loop(0, qi + 1, step, (m0, one) * n_chains)
    for c in range(n_chains):
        add_values(c, 0, carry[2 * c + 1])


def _bounded_tiles(n_tiles, n_chains, logits_fn, values_fn, bias_fn, keep_fn, finish_fn, p_ref, acc_ref,
                   between_fn=lambda d: None):
    for d in range(n_tiles):
        for qi in range(d, n_tiles):
            for c in range(n_chains):
                p_ref[qi, c] = jnp.exp2(logits_fn(c, qi, qi - d) + bias_fn(c, d)).astype(BF16)
        between_fn(d)
        for qi in range(d, n_tiles):
            kv = qi - d
            for c in range(n_chains):
                pv = jnp.dot(values_fn(c, kv), p_ref[qi, c], preferred_element_type=F32)
                keep = keep_fn(c, qi, kv)
                if keep is not None:
                    pv = pv * keep
                acc_ref[qi, c] = pv if d == 0 else acc_ref[qi, c] + pv
        finish_fn(d)


def _logits_bounded(q_gain, k_gain, bias_cols):
    bound = (HEAD_DIM * jnp.max(jnp.abs(q_gain)) * jnp.max(jnp.abs(k_gain)) * ROUNDING_MARGIN
             + jnp.max(jnp.abs(bias_cols)) * LOG2E)
    return (bound <= LOGIT_LIMIT).astype(jnp.int32).reshape(1)


def _with_ones_rows(vt):
    return jnp.concatenate([vt, jnp.ones((ONES_ROWS, vt.shape[1]), vt.dtype)], axis=0)


def _attention_scratch(n_tiles, value_rows, n_chains=2):
    return [pltpu.VMEM((n_tiles, n_chains, ATT_TILE, LANES), BF16),
            pltpu.VMEM((2, n_chains, ATT_TILE, ATT_TILE), F32),
            pltpu.VMEM((n_tiles, n_chains, ATT_TILE, ATT_TILE), BF16),
            pltpu.VMEM((n_tiles, n_chains, value_rows + ONES_ROWS, ATT_TILE), F32)]


def _half_lane_split(q):
    lane = lax.broadcasted_iota(jnp.int32, q.shape, 1)
    zero = jnp.zeros_like(q)
    return jnp.where(lane < HEAD_DIM, q, zero), jnp.where(lane >= HEAD_DIM, q, zero)


def _diff_kernel(bounded_ref, lq1_ref, lk1_ref, lq2_ref, lk2_ref, subg_ref, q_ref, k_ref, vt_ref, bias_ref, o_ref,
                 qm_ref, s_ref, p_ref, acc_ref):
    n_tiles = q_ref.shape[0]
    chains = dict(n_chains=2, values_fn=lambda c, kv: _with_ones_rows(vt_ref[kv]))
    width = 2 * HEAD_DIM

    def prepare():
        for qi in range(n_tiles):
            qm_ref[qi, 0], qm_ref[qi, 1] = _half_lane_split(q_ref[qi])

    def finish(qi):
        lam = (jnp.exp(jnp.sum(lq1_ref[...] * lk1_ref[...], keepdims=True))
               - jnp.exp(jnp.sum(lq2_ref[...] * lk2_ref[...], keepdims=True)) + LAMBDA_INIT)
        a1, a2 = acc_ref[qi, 0], acc_ref[qi, 1]
        o = a1[:width] / a1[width:width + 1] - lam * (a2[:width] / a2[width:width + 1])
        ms = jnp.mean(o * o, axis=0, keepdims=True)
        o = o * lax.rsqrt(ms + RMS_EPS) * subg_ref[...] * (1.0 - LAMBDA_INIT)
        o_ref[qi] = o.T.astype(BF16)

    bounded = bounded_ref[0] > 0

    @pl.when(bounded)
    def _():
        prepare()
        _bounded_tiles(
            n_tiles,
            logits_fn=lambda c, qi, kv: lax.dot_general(k_ref[kv], qm_ref[qi, c], _NT, preferred_element_type=F32),
            bias_fn=lambda c, d: bias_ref[d],
            keep_fn=lambda c, qi, kv: None,
            finish_fn=finish,
            p_ref=p_ref, acc_ref=acc_ref, **chains)

    @pl.when(jnp.logical_not(bounded))
    def _():
        prepare()

        def q_tile(qi, carry):
            _flash_tiles(
                qi,
                logits_fn=lambda c, kv: lax.dot_general(k_ref[kv], qm_ref[qi, c], _NT, preferred_element_type=F32),
                bias_fn=lambda c, j: bias_ref[j],
                mask_fn=lambda c, kv: None,
                s_ref=s_ref, p_ref=p_ref.at[0], acc_ref=acc_ref.at[qi], **chains)
            return carry
        lax.fori_loop(0, n_tiles, q_tile, 0)
        for qi in range(n_tiles):
            finish(qi)


def _diff_attention(bounded, lam_vecs, sub_g_col, qk4, vt4, bias):
    batch, n_tiles = qk4.shape[0], qk4.shape[1]
    k_col0 = DIFF_WIDTH // LANES
    vec = pl.BlockSpec((1, HEAD_DIM), lambda h, b: (0, 0))
    return pl.pallas_call(
        _diff_kernel,
        grid=(HEADS_DIFF, batch),
        in_specs=[
            pl.BlockSpec(memory_space=pltpu.SMEM),
            vec, vec, vec, vec,
            pl.BlockSpec((2 * HEAD_DIM, 1), lambda h, b: (0, 0)),
            pl.BlockSpec((None, n_tiles, ATT_TILE, LANES), lambda h, b: (b, 0, 0, h)),
            pl.BlockSpec((None, n_tiles, ATT_TILE, LANES), lambda h, b: (b, 0, 0, k_col0 + h)),
            pl.BlockSpec((None, n_tiles, LANES, ATT_TILE), lambda h, b: (b, 0, h, 0)),
            pl.BlockSpec((None, n_tiles, ATT_TILE, ATT_TILE), lambda h, b: (h, 0, 0, 0)),
        ],
        out_specs=pl.BlockSpec((None, n_tiles, ATT_TILE, LANES), lambda h, b: (b, 0, 0, h)),
        out_shape=jax.ShapeDtypeStruct((batch, n_tiles, ATT_TILE, DIFF_WIDTH), BF16),
        scratch_shapes=_attention_scratch(n_tiles, 2 * HEAD_DIM),
        compiler_params=_params("arbitrary", "arbitrary"),
        name="diff_attention",
    )(bounded, *lam_vecs, sub_g_col, qk4, qk4, vt4, bias)


def _split3(v):
    hi = v.astype(BF16)
    r1 = v - hi.astype(F32)
    mid = r1.astype(BF16)
    lo = (r1 - mid.astype(F32)).astype(BF16)
    return hi, mid, lo


def _block_mask(gate, own):
    row = lax.broadcasted_iota(jnp.int32, gate.shape, 0)
    rank = jnp.zeros(gate.shape, jnp.int32)
    for m in range(own):
        gm = gate[m:m + 1, :]
        beats = (gm > gate) | ((gm == gate) & (row > m))
        rank = rank + jnp.where(beats, 1, 0)
    keep = ((rank < MOBA_TOPK) & (row < own)) | (row == own)
    return jnp.where(keep, 0.0, NEG_INF).astype(F32)


def _moba_kernel(bounded_ref, q_ref, k_ref, vt_ref, bias_ref, o_ref, kmean_ref, mask_ref, qm_ref, s_ref, p_ref,
                 acc_ref):
    n_tiles = q_ref.shape[0]

    def split_queries():
        for qi in range(n_tiles):
            qm_ref[qi, 0], qm_ref[qi, 1] = _half_lane_split(q_ref[qi])

    def block_masks(q_tiles):
        for n in range(n_tiles):
            kmean_ref[n:n + 1, :] = jnp.mean(k_ref[n].astype(F32), axis=0, keepdims=True)
        per_head = zip(*[_half_lane_split(term) for term in _split3(kmean_ref[...])])
        gate_lhs = jnp.concatenate([term for head_terms in per_head for term in head_terms], axis=0)
        n_terms = gate_lhs.shape[0] // (2 * n_tiles)
        for qi in q_tiles:
            terms = lax.dot_general(gate_lhs, q_ref[qi], _NT, preferred_element_type=F32)
            for half in range(2):
                rows = [terms[(half * n_terms + t) * n_tiles:(half * n_terms + t + 1) * n_tiles]
                        for t in range(n_terms)]
                mask_ref[qi, half] = _block_mask(sum(rows[1:], rows[0]), qi)

    def keep_row(c, qi, kv):
        if qi <= MOBA_TOPK or kv == qi:
            return None
        return jnp.where(mask_ref[qi, c, kv:kv + 1, :] < -1.0, 0.0, 1.0)

    def finish(qi):
        halves = [acc_ref[qi, half] for half in range(2)]
        o = jnp.concatenate([a[:HEAD_DIM] / a[HEAD_DIM:HEAD_DIM + 1] for a in halves], axis=0)
        o_ref[qi] = o.T.astype(BF16)

    chains = dict(
        n_chains=2,
        values_fn=lambda c, kv: _with_ones_rows(vt_ref[kv, c * HEAD_DIM:(c + 1) * HEAD_DIM, :]))
    bounded = bounded_ref[0] > 0

    @pl.when(bounded)
    def _():
        split_queries()
        _bounded_tiles(
            n_tiles,
            logits_fn=lambda c, qi, kv: lax.dot_general(k_ref[kv], qm_ref[qi, c], _NT, preferred_element_type=F32),
            bias_fn=lambda c, d: bias_ref[c, d],
            keep_fn=keep_row,
            between_fn=lambda d: block_masks(range(MOBA_TOPK + 1, n_tiles)) if d == 0 else None,
            finish_fn=finish,
            p_ref=p_ref, acc_ref=acc_ref, **chains)

    @pl.when(jnp.logical_not(bounded))
    def _():
        split_queries()
        block_masks(range(n_tiles))

        def q_tile(qi, carry):
            _flash_tiles(
                qi,
                logits_fn=lambda c, kv: lax.dot_general(k_ref[kv], qm_ref[qi, c], _NT, preferred_element_type=F32),
                bias_fn=lambda c, j: bias_ref[c, j],
                mask_fn=lambda c, kv: mask_ref[qi, c, pl.ds(kv, 1), :],
                s_ref=s_ref, p_ref=p_ref.at[0], acc_ref=acc_ref.at[qi], **chains)
            return carry
        lax.fori_loop(0, n_tiles, q_tile, 0)
        for qi in range(n_tiles):
            finish(qi)


def _moba_attention(bounded, qk4, vt4, bias_pairs):
    batch, n_tiles = qk4.shape[0], qk4.shape[1]
    q_col0 = 2 * DIFF_WIDTH // LANES
    k_col0 = q_col0 + MOBA_WIDTH // LANES
    v_row0 = DIFF_WIDTH // LANES
    pair0 = HEADS_DIFF // 2
    return pl.pallas_call(
        _moba_kernel,
        grid=(HEADS_MOBA // 2, batch),
        in_specs=[
            pl.BlockSpec(memory_space=pltpu.SMEM),
            pl.BlockSpec((None, n_tiles, ATT_TILE, LANES), lambda h, b: (b, 0, 0, q_col0 + h)),
            pl.BlockSpec((None, n_tiles, ATT_TILE, LANES), lambda h, b: (b, 0, 0, k_col0 + h)),
            pl.BlockSpec((None, n_tiles, LANES, ATT_TILE), lambda h, b: (b, 0, v_row0 + h, 0)),
            pl.BlockSpec((None, 2, n_tiles, ATT_TILE, ATT_TILE), lambda h, b: (pair0 + h, 0, 0, 0, 0)),
        ],
        out_specs=pl.BlockSpec((None, n_tiles, ATT_TILE, LANES), lambda h, b: (b, 0, 0, h)),
        out_shape=jax.ShapeDtypeStruct((batch, n_tiles, ATT_TILE, MOBA_WIDTH), BF16),
        scratch_shapes=[
            pltpu.VMEM((n_tiles, LANES), F32),
            pltpu.VMEM((n_tiles, 2, n_tiles, ATT_TILE), F32),
        ] + _attention_scratch(n_tiles, HEAD_DIM),
        compiler_params=_params("arbitrary", "arbitrary"),
        name="moba_attention",
    )(bounded, qk4, qk4, vt4, bias_pairs)


def _load_token_rows(ref, n_rows, lead=(), first=0):
    chunks = [ref[lead + (pl.ds(first * ROW_TILE + c, n_rows, stride=ROW_TILE), slice(None))]
              for c in range(ROW_TILE)]
    return jnp.concatenate(chunks, axis=1)


def _store_token_rows(ref, value, first=0):
    n_rows = value.shape[0]
    for c in range(ROW_TILE):
        ref[pl.ds(first * ROW_TILE + c, n_rows, stride=ROW_TILE), :] = value[:, c * LANES:(c + 1) * LANES]


def _token_tile(ref, t):
    return ref.at[pl.ds(pl.multiple_of(t * ROW_TILE, ROW_TILE), ROW_TILE)]


def _first_argmax(v):
    top = jnp.max(v, axis=0, keepdims=True)
    row = lax.broadcasted_iota(jnp.int32, v.shape, 0)
    idx = jnp.min(jnp.where(v == top, row, v.shape[0]), axis=0, keepdims=True)
    return top, idx


def _out_proj_kernel(yd_ref, ym_ref, x_ref, wd_ref, wm_ref, g2_ref, wr_ref, tri_ref,
                     x1_ref, ri_ref, rf_ref, cnt_ref, run_ref):
    @pl.when(lax.rem(pl.program_id(0), pl.num_programs(0) // MOE_TOKEN_PARTS) == 0)
    def _():
        run_ref[...] = jnp.zeros_like(run_ref)

    def project(t):
        rows = slice(t * ROUTE_TILE, (t + 1) * ROUTE_TILE)
        x1 = (x_ref[rows, :] + jnp.dot(yd_ref[rows, :], wd_ref[...], preferred_element_type=F32)
              + jnp.dot(ym_ref[rows, :], wm_ref[...], preferred_element_type=F32))
        _store_token_rows(x1_ref, x1, first=t * ROUTE_TILE)
        ms = jnp.mean(x1 * x1, axis=-1, keepdims=True)
        return (x1 * lax.rsqrt(ms + RMS_EPS) * g2_ref[...]).astype(BF16)

    def pick(h2):
        lg = lax.dot_general(wr_ref[...], h2, _NT, preferred_element_type=F32)
        g_logits = lg[N_EXPERTS:N_EXPERTS + N_GROUPS, :]
        g_top, g_idx = _first_argmax(g_logits)
        p_group = 1.0 / jnp.sum(jnp.exp(g_logits - g_top), axis=0, keepdims=True)
        e_logits = lg[0:EXPERTS_PER_GROUP, :]
        for g in range(1, N_GROUPS):
            e_logits = jnp.where(g_idx == g, lg[g * EXPERTS_PER_GROUP:(g + 1) * EXPERTS_PER_GROUP, :], e_logits)
        v1, i1 = _first_argmax(e_logits)
        row = lax.broadcasted_iota(jnp.int32, e_logits.shape, 0)
        v2, i2 = _first_argmax(jnp.where(row == i1, -jnp.inf, e_logits))
        ratio = jnp.exp(v2 - v1)
        w1 = p_group / (1.0 + ratio)
        w2 = p_group * ratio / (1.0 + ratio)
        e1 = g_idx * EXPERTS_PER_GROUP + i1
        e2 = g_idx * EXPERTS_PER_GROUP + i2
        erow = lax.broadcasted_iota(jnp.int32, (N_EXPERTS, ROUTE_TILE), 0)
        hit1 = erow == e1
        hit2 = erow == e2
        onehot = jnp.where(hit1 | hit2, 1.0, 0.0).astype(F32)
        return e1, e2, w1, w2, hit1, hit2, onehot

    def place(t, e1, e2, w1, w2, hit1, hit2, onehot):
        before = jnp.dot(onehot.astype(BF16), tri_ref[...], preferred_element_type=F32) + run_ref[...]
        r1 = jnp.sum(jnp.where(hit1, before, 0.0), axis=0, keepdims=True).astype(jnp.int32)
        r2 = jnp.sum(jnp.where(hit2, before, 0.0), axis=0, keepdims=True).astype(jnp.int32)
        run_ref[...] = run_ref[...] + jnp.sum(onehot, axis=1, keepdims=True)
        lanes = slice(t * ROUTE_TILE, (t + 1) * ROUTE_TILE)
        out_row = lax.broadcasted_iota(jnp.int32, (8, ROUTE_TILE), 0)
        ri_ref[:, lanes] = jnp.where(out_row == 0, e1, jnp.where(out_row == 1, e2,
                                     jnp.where(out_row == 2, r1, jnp.where(out_row == 3, r2, 0))))
        rf_ref[:, lanes] = jnp.where(out_row == 0, w1, jnp.where(out_row == 1, w2, 0.0))

    n_tiles = OUT_ROWS // ROUTE_TILE
    projected, picked = {}, {}
    for t in range(n_tiles + 2):
        if t < n_tiles:
            projected[t] = project(t)
        if 0 <= t - 1 < n_tiles:
            picked[t - 1] = pick(projected.pop(t - 1))
        if 0 <= t - 2 < n_tiles:
            place(t - 2, *picked.pop(t - 2))
    cnt_ref[...] = jnp.broadcast_to(run_ref[...], cnt_ref.shape).astype(jnp.int32)


def _out_proj(y_d, y_m, x2, wo_d, wo_m, g2, w_router, tri):
    n = x2.shape[0]
    const = lambda i: (0, 0)
    steps_per_part = n // OUT_ROWS // MOE_TOKEN_PARTS
    return pl.pallas_call(
        _out_proj_kernel,
        grid=(n // OUT_ROWS,),
        in_specs=[
            pl.BlockSpec((OUT_ROWS, DIFF_WIDTH), lambda i: (i, 0)),
            pl.BlockSpec((OUT_ROWS, MOBA_WIDTH), lambda i: (i, 0)),
            pl.BlockSpec((OUT_ROWS, D_MODEL), lambda i: (i, 0)),
            pl.BlockSpec((DIFF_WIDTH, D_MODEL), const),
            pl.BlockSpec((MOBA_WIDTH, D_MODEL), const),
            pl.BlockSpec((1, D_MODEL), const),
            pl.BlockSpec((ROUTER_ROWS, D_MODEL), const),
            pl.BlockSpec((ROUTE_TILE, ROUTE_TILE), const),
        ],
        out_specs=[
            pl.BlockSpec((OUT_ROWS * ROW_TILE, LANES), lambda i: (i, 0)),
            pl.BlockSpec((8, OUT_ROWS), lambda i: (0, i)),
            pl.BlockSpec((8, OUT_ROWS), lambda i: (0, i)),
            pl.BlockSpec((None, N_EXPERTS, LANES), lambda i: (i // steps_per_part, 0, 0)),
        ],
        out_shape=[
            jax.ShapeDtypeStruct((n * ROW_TILE, LANES), F32),
            jax.ShapeDtypeStruct((8, n), jnp.int32),
            jax.ShapeDtypeStruct((8, n), F32),
            jax.ShapeDtypeStruct((MOE_TOKEN_PARTS, N_EXPERTS, LANES), jnp.int32),
        ],
        scratch_shapes=[pltpu.VMEM((N_EXPERTS, 1), F32)],
        compiler_params=_params("arbitrary"),
        name="out_proj_route",
    )(y_d, y_m, x2, wo_d, wo_m, g2, w_router, tri)


def _dispatch_step(i, n_steps, first_tile, zchunk_ref, dest_ref, x1_hbm, xs_hbm, zero_ref, stage_ref, zero_sem,
                   load_sem, row_sem):
    last = n_steps - 1
    slot = lax.rem(i, 2)
    chunk_rows = MOE_CHUNK * ROW_TILE
    tile_rows = MOVE_ROWS * ROW_TILE

    def load(t, s):
        start = pl.multiple_of((first_tile + t) * tile_rows, tile_rows)
        return pltpu.make_async_copy(x1_hbm.at[pl.ds(start, tile_rows)], stage_ref.at[s], load_sem.at[s])

    def wait_rows(s):
        for _ in range(EXPERTS_PER_TOKEN):
            pltpu.make_async_copy(stage_ref.at[s], xs_hbm.at[pl.ds(0, tile_rows)], row_sem.at[s]).wait()

    def zero_copy(c):
        start = pl.multiple_of(c * chunk_rows, chunk_rows)
        return pltpu.make_async_copy(zero_ref, xs_hbm.at[pl.ds(start, chunk_rows)], zero_sem)

    @pl.when(i == 0)
    def _():
        zero_ref[...] = jnp.zeros_like(zero_ref)

        def start_one(c, carry):
            @pl.when(zchunk_ref[c] > 0)
            def _():
                zero_copy(c).start()
            return carry

        def wait_one(c, carry):
            @pl.when(zchunk_ref[c] > 0)
            def _():
                zero_copy(c).wait()
            return carry

        lax.fori_loop(0, zchunk_ref.shape[0], start_one, 0)
        lax.fori_loop(0, zchunk_ref.shape[0], wait_one, 0)
        load(0, 0).start()

    @pl.when(i > 0)
    def _():
        wait_rows(1 - slot)

    @pl.when(i < last)
    def _():
        load(i + 1, 1 - slot).start()

    load(i, slot).wait()

    def send_row(r, carry):
        src = _token_tile(stage_ref.at[slot], r)
        for k in range(EXPERTS_PER_TOKEN):
            pltpu.make_async_copy(src, _token_tile(xs_hbm, dest_ref[0, k, r]), row_sem.at[slot]).start(priority=k)
        return carry

    lax.fori_loop(0, MOVE_ROWS, send_row, 0, unroll=8)

    @pl.when(i == last)
    def _():
        wait_rows(slot)


def _dispatch_scratch():
    return [pltpu.VMEM((MOE_CHUNK * ROW_TILE, LANES), F32),
            pltpu.VMEM((2, MOVE_ROWS * ROW_TILE, LANES), F32),
            pltpu.SemaphoreType.DMA(()),
            pltpu.SemaphoreType.DMA((2,)),
            pltpu.SemaphoreType.DMA((2,))]


def _dispatch_kernel(first_tile, zchunk_ref, dest_ref, x1_hbm, xs_hbm, *scratch):
    _dispatch_step(pl.program_id(0), pl.num_programs(0), first_tile, zchunk_ref, dest_ref, x1_hbm, xs_hbm, *scratch)


def _dispatch(part, zchunk, dest3, x1, n_slots):
    tiles = x1.shape[0] // ROW_TILE // MOVE_ROWS // MOE_TOKEN_PARTS
    first_tile = part * tiles
    grid_spec = pltpu.PrefetchScalarGridSpec(
        num_scalar_prefetch=1,
        grid=(tiles,),
        in_specs=[
            pl.BlockSpec((1, EXPERTS_PER_TOKEN, MOVE_ROWS), lambda i, zc: (first_tile + i, 0, 0),
                         memory_space=pltpu.SMEM),
            pl.BlockSpec(memory_space=pl.ANY),
        ],
        out_specs=pl.BlockSpec(memory_space=pl.ANY),
        scratch_shapes=_dispatch_scratch(),
    )
    return pl.pallas_call(
        functools.partial(_dispatch_kernel, first_tile),
        grid_spec=grid_spec,
        out_shape=jax.ShapeDtypeStruct((n_slots * ROW_TILE, LANES), F32),
        compiler_params=_params("arbitrary"),
        name="moe_dispatch",
    )(zchunk, dest3, x1)


def _expert_step(c, na_ref, xs_ref, g2_ref, wg_ref, wu_ref, wd_ref, o_ref):
    active = c < na_ref[0]

    @pl.when(jnp.logical_not(active))
    def _():
        o_ref[...] = jnp.zeros_like(o_ref)

    @pl.when(active)
    def _():
        part_rows = MOE_CHUNK // EXPERT_ROW_PARTS

        def gate_up(r):
            x = _load_token_rows(xs_ref, part_rows, first=r * part_rows)
            ms = jnp.mean(x * x, axis=-1, keepdims=True)
            h = (x * lax.rsqrt(ms + RMS_EPS) * g2_ref[...]).astype(BF16)
            return (jnp.dot(h, wg_ref[...], preferred_element_type=F32),
                    jnp.dot(h, wu_ref[...], preferred_element_type=F32))

        def down(r, gate, up):
            hid = (gate * jax.nn.sigmoid(gate) * up).astype(BF16)
            _store_token_rows(o_ref, jnp.dot(hid, wd_ref[...], preferred_element_type=F32), first=r * part_rows)

        pending = None
        for r in range(EXPERT_ROW_PARTS):
            current = (r,) + gate_up(r)
            if pending is not None:
                down(*pending)
            pending = current
        down(*pending)


def _expert_kernel(ce_ref, na_ref, *refs):
    _expert_step(pl.program_id(0), na_ref, *refs)


def _expert_specs():
    rows = lambda c, ce, na, *_: (jnp.minimum(c, na[0] - 1), 0)
    expert = lambda c, ce, na, *_: (ce[c], 0, 0)
    in_specs = [
        pl.BlockSpec((MOE_CHUNK * ROW_TILE, LANES), rows),
        pl.BlockSpec((1, D_MODEL), lambda c, *_: (0, 0)),
        pl.BlockSpec((None, D_MODEL, EXPERT_HIDDEN), expert),
        pl.BlockSpec((None, D_MODEL, EXPERT_HIDDEN), expert),
        pl.BlockSpec((None, EXPERT_HIDDEN, D_MODEL), expert),
    ]
    return in_specs, pl.BlockSpec((MOE_CHUNK * ROW_TILE, LANES), lambda c, *_: (c, 0))


def _experts(chunk_e, n_active, xs, g2, wg, wu, wd):
    n_slots = xs.shape[0] // ROW_TILE
    in_specs, out_spec = _expert_specs()
    grid_spec = pltpu.PrefetchScalarGridSpec(
        num_scalar_prefetch=2, grid=(n_slots // MOE_CHUNK,), in_specs=in_specs, out_specs=out_spec)
    return pl.pallas_call(
        _expert_kernel,
        grid_spec=grid_spec,
        out_shape=jax.ShapeDtypeStruct((n_slots * ROW_TILE, LANES), F32),
        compiler_params=_params("arbitrary"),
        name="moe_experts",
    )(chunk_e, n_active, xs, g2, wg, wu, wd)


def _experts_dispatch_kernel(first_tile, n_tiles, ce_ref, na_ref, zchunk_ref,
                             xs_ref, g2_ref, wg_ref, wu_ref, wd_ref, dest_ref, x1_hbm,
                             o_ref, xs_next_hbm, *scratch):
    i = pl.program_id(0)

    @pl.when(i < n_tiles)
    def _():
        _dispatch_step(i, n_tiles, first_tile, zchunk_ref, dest_ref, x1_hbm, xs_next_hbm, *scratch)

    _expert_step(i, na_ref, xs_ref, g2_ref, wg_ref, wu_ref, wd_ref, o_ref)


def _experts_dispatch(chunk_e, n_active, zchunk_next, xs, g2, wg, wu, wd, next_part, dest3, x1):
    n_slots = xs.shape[0] // ROW_TILE
    n_chunks = n_slots // MOE_CHUNK
    tiles = x1.shape[0] // ROW_TILE // MOVE_ROWS // MOE_TOKEN_PARTS
    first_tile = next_part * tiles
    assert n_chunks >= tiles
    in_specs, out_spec = _expert_specs()
    in_specs += [
        pl.BlockSpec((1, EXPERTS_PER_TOKEN, MOVE_ROWS),
                     lambda c, *_: (first_tile + jnp.minimum(c, tiles - 1), 0, 0), memory_space=pltpu.SMEM),
        pl.BlockSpec(memory_space=pl.ANY),
    ]
    grid_spec = pltpu.PrefetchScalarGridSpec(
        num_scalar_prefetch=3, grid=(n_chunks,), in_specs=in_specs,
        out_specs=[out_spec, pl.BlockSpec(memory_space=pl.ANY)],
        scratch_shapes=_dispatch_scratch())
    slots = jax.ShapeDtypeStruct((n_slots * ROW_TILE, LANES), F32)
    return pl.pallas_call(
        functools.partial(_experts_dispatch_kernel, first_tile, tiles),
        grid_spec=grid_spec,
        out_shape=[slots, slots],
        compiler_params=_params("arbitrary"),
        name="moe_experts_dispatch",
    )(chunk_e, n_active, zchunk_next, xs, g2, wg, wu, wd, dest3, x1)


def _combine_kernel(dest_ref, dest_next_ref, w_ref, x1_ref, *rest):
    outs_hbm, (y_ref, buf_ref, sem_ref) = rest[:MOE_TOKEN_PARTS], rest[MOE_TOKEN_PARTS:]
    i = pl.program_id(0)
    slot = lax.rem(i, 2)
    tiles_per_part = pl.num_programs(0) // MOE_TOKEN_PARTS

    def fetch(tile, dref, s):
        for part, src_hbm in enumerate(outs_hbm):
            @pl.when(tile // tiles_per_part == part)
            def _():
                def fetch_row(r, carry):
                    for k in range(EXPERTS_PER_TOKEN):
                        pltpu.make_async_copy(_token_tile(src_hbm, dref[0, k, r]),
                                              _token_tile(buf_ref.at[s, k], r), sem_ref.at[s]).start(priority=k)
                    return carry
                lax.fori_loop(0, MOVE_ROWS, fetch_row, 0, unroll=8)

    @pl.when(i == 0)
    def _():
        fetch(i, dest_ref, 0)

    @pl.when(i + 1 < pl.num_programs(0))
    def _():
        fetch(i + 1, dest_next_ref, 1 - slot)

    for k in range(EXPERTS_PER_TOKEN):
        pltpu.make_async_copy(outs_hbm[0].at[pl.ds(0, MOVE_ROWS * ROW_TILE)], buf_ref.at[slot, k],
                              sem_ref.at[slot]).wait()
    w = w_ref[...]
    y_ref[...] = (_load_token_rows(x1_ref, MOVE_ROWS)
                  + w[:, 0:1] * _load_token_rows(buf_ref, MOVE_ROWS, (slot, 0))
                  + w[:, 1:2] * _load_token_rows(buf_ref, MOVE_ROWS, (slot, 1)))


def _combine(dest3, wts, x1, outs_parts):
    n = x1.shape[0] // ROW_TILE
    steps = n // MOVE_ROWS
    dest_block = (1, EXPERTS_PER_TOKEN, MOVE_ROWS)
    return pl.pallas_call(
        _combine_kernel,
        grid=(steps,),
        in_specs=[
            pl.BlockSpec(dest_block, lambda i: (i, 0, 0), memory_space=pltpu.SMEM),
            pl.BlockSpec(dest_block, lambda i: (jnp.minimum(i + 1, steps - 1), 0, 0), memory_space=pltpu.SMEM),
            pl.BlockSpec((MOVE_ROWS, EXPERTS_PER_TOKEN), lambda i: (i, 0)),
            pl.BlockSpec((MOVE_ROWS * ROW_TILE, LANES), lambda i: (i, 0)),
        ] + [pl.BlockSpec(memory_space=pl.ANY)] * MOE_TOKEN_PARTS,
        out_specs=pl.BlockSpec((MOVE_ROWS, D_MODEL), lambda i: (i, 0)),
        out_shape=jax.ShapeDtypeStruct((n, D_MODEL), F32),
        scratch_shapes=[
            pltpu.VMEM((2, EXPERTS_PER_TOKEN, MOVE_ROWS * ROW_TILE, LANES), F32),
            pltpu.SemaphoreType.DMA((2,)),
        ],
        compiler_params=_params("arbitrary"),
        name="moe_combine",
    )(dest3, dest3, wts, x1, *outs_parts)


def _layer(x, norm1_g, w_in, diff_q_g, diff_k_g, lambda_q1, lambda_k1, lambda_q2, lambda_k2, diff_sub_g,
           moba_q_g, moba_k_g, rel_bias, w_out, norm2_g, router_group, router_expert, w_gate, w_up, w_down):
    batch, seq, _ = x.shape
    n = batch * seq
    n_tiles = seq // ATT_TILE
    scale = HEAD_DIM ** -0.5 * LOG2E

    d3 = 3 * DIFF_WIDTH
    w_perm = jnp.concatenate([w_in[:, :2 * DIFF_WIDTH], w_in[:, d3:d3 + 2 * MOBA_WIDTH],
                              w_in[:, 2 * DIFF_WIDTH:d3], w_in[:, d3 + 2 * MOBA_WIDTH:]], axis=1).astype(BF16)
    reps_d, reps_m = DIFF_WIDTH // HEAD_DIM, MOBA_WIDTH // HEAD_DIM
    post_gain = jnp.concatenate([jnp.tile(diff_q_g * scale, reps_d), jnp.tile(diff_k_g, reps_d),
                                 jnp.tile(moba_q_g * scale, reps_m), jnp.tile(moba_k_g, reps_m)])[None, :]
    head_of = np.arange(2 * LANES) // HEAD_DIM
    grp = jnp.asarray((head_of[:, None] == head_of[None, :]) / HEAD_DIM, BF16)

    bias = _bias_tiles(rel_bias, n_tiles)
    qk, vt4 = _in_proj(x.reshape(n, D_MODEL), norm1_g[None, :], w_perm, post_gain, grp, batch, seq)
    qk4 = qk.reshape(batch, n_tiles, ATT_TILE, QK_WIDTH)
    lam_vecs = [v[None, :] for v in (lambda_q1, lambda_k1, lambda_q2, lambda_k2)]
    y_d = _diff_attention(_logits_bounded(diff_q_g * scale, diff_k_g, rel_bias[:, :HEADS_DIFF]),
                          lam_vecs, diff_sub_g[:, None], qk4, vt4, bias)
    y_m = _moba_attention(_logits_bounded(moba_q_g * scale, moba_k_g, rel_bias[:, HEADS_DIFF:]),
                          qk4, vt4, bias.reshape(N_HEADS_TOTAL // 2, 2, n_tiles, ATT_TILE, ATT_TILE))

    w_router = jnp.concatenate([
        router_expert.transpose(0, 2, 1).reshape(N_EXPERTS, D_MODEL), router_group.T,
        jnp.zeros((ROUTER_ROWS - N_EXPERTS - N_GROUPS, D_MODEL), F32)], axis=0).astype(BF16)
    tri = jnp.asarray(np.triu(np.ones((ROUTE_TILE, ROUTE_TILE)), k=1), BF16)
    wo = w_out.astype(BF16)
    x1, route_i, route_f, cnt = _out_proj(y_d.reshape(n, DIFF_WIDTH), y_m.reshape(n, MOBA_WIDTH),
                                          x.reshape(n, D_MODEL), wo[:DIFF_WIDTH], wo[DIFF_WIDTH:],
                                          norm2_g[None, :], w_router, tri)

    parts = MOE_TOKEN_PARTS
    counts = cnt[:, :, 0]
    pad_counts = ((counts + MOE_CHUNK - 1) // MOE_CHUNK) * MOE_CHUNK
    pad_end = jnp.cumsum(pad_counts, axis=1)
    pad_start = pad_end - pad_counts
    experts = jnp.arange(N_EXPERTS, dtype=jnp.int32)
    picked = route_i[0:2].reshape(EXPERTS_PER_TOKEN, parts, n // parts)
    start_of = jnp.sum(jnp.where(picked[..., None] == experts, pad_start[None, :, None, :], 0), axis=-1)
    dest = start_of.reshape(EXPERTS_PER_TOKEN, n) + route_i[2:4]
    dest3 = dest.reshape(EXPERTS_PER_TOKEN, n // MOVE_ROWS, MOVE_ROWS).transpose(1, 0, 2)
    n_slots = n // parts * EXPERTS_PER_TOKEN + N_EXPERTS * MOE_CHUNK
    n_chunks = n_slots // MOE_CHUNK
    chunk_id = jnp.arange(n_chunks, dtype=jnp.int32)
    chunk_e = jnp.minimum(jnp.sum((pad_end[:, None, :] <= chunk_id[None, :, None] * MOE_CHUNK).astype(jnp.int32),
                                  axis=2), N_EXPERTS - 1)
    n_active = (pad_end[:, -1:] // MOE_CHUNK).astype(jnp.int32)
    next_e = jnp.concatenate([chunk_e[:, 1:], jnp.full((parts, 1), N_EXPERTS, jnp.int32)], axis=1)
    zchunk = ((chunk_id[None, :] >= n_active - 1) | (chunk_e != next_e)).astype(jnp.int32)

    weights = (norm2_g[None, :], w_gate.astype(BF16), w_up.astype(BF16), w_down.astype(BF16))
    xs = _dispatch(0, zchunk[0], dest3, x1, n_slots)
    outs = []
    for p in range(parts):
        if p + 1 < parts:
            out_p, xs_next = _experts_dispatch(chunk_e[p], n_active[p], zchunk[p + 1], xs, *weights, p + 1, dest3, x1)
        else:
            out_p, xs_next = _experts(chunk_e[p], n_active[p], xs, *weights), None
        outs.append(out_p)
        xs = xs_next
    y = _combine(dest3, route_f[0:2].T, x1, outs)
    return y.reshape(batch, seq, D_MODEL)


def kernel(x, norm1_g, w_in, diff_q_g, diff_k_g, lambda_q1, lambda_k1, lambda_q2, lambda_k2, diff_sub_g,
           moba_q_g, moba_k_g, rel_bias, w_out, norm2_g, router_group, router_expert, w_gate, w_up, w_down):
    assert x.shape[1] % PROJ_ROWS == 0 and x.shape[2] == D_MODEL and norm1_g.shape[0] == 1
    return _layer(x, norm1_g[0], w_in[0], diff_q_g[0], diff_k_g[0], lambda_q1[0], lambda_k1[0], lambda_q2[0],
                  lambda_k2[0], diff_sub_g[0], moba_q_g[0], moba_k_g[0], rel_bias, w_out[0], norm2_g[0],
                  router_group[0], router_expert[0], w_gate[0], w_up[0], w_down[0])
```

```python
import functools
import math

import numpy as np
import jax
import jax.numpy as jnp
from jax import lax
from jax.experimental import pallas as pl
from jax.experimental.pallas import tpu as pltpu

D_MODEL = 1024
HEAD_DIM = 64
HEADS_DIFF = 4
HEADS_MOBA = 8
N_HEADS_TOTAL = HEADS_DIFF + HEADS_MOBA
DIFF_WIDTH = HEADS_DIFF * 2 * HEAD_DIM
MOBA_WIDTH = HEADS_MOBA * HEAD_DIM
QK_WIDTH = 2 * DIFF_WIDTH + 2 * MOBA_WIDTH
V_WIDTH = DIFF_WIDTH + MOBA_WIDTH
MOBA_TOPK = 3
NUM_BUCKETS = 32
MAX_DISTANCE = 2048
N_GROUPS = 4
EXPERTS_PER_GROUP = 8
N_EXPERTS = N_GROUPS * EXPERTS_PER_GROUP
EXPERTS_PER_TOKEN = 2
EXPERT_HIDDEN = 256
RMS_EPS = 1e-6
NEG_INF = -1e30
LAMBDA_INIT = 0.8 - 0.6 * math.exp(-0.3 * 0)

LANES = 128
ROW_TILE = D_MODEL // LANES
ATT_TILE = 256
ONES_ROWS = 16
LOGIT_LIMIT = 80.0
ROUNDING_MARGIN = 1.05
LOG2E = math.log2(math.e)
PROJ_ROWS = 1024
OUT_ROWS = 1024
ROUTE_TILE = 256
MOE_CHUNK = 512
MOE_TOKEN_PARTS = 2
EXPERT_ROW_PARTS = 4
MOVE_ROWS = 512
FUSED_MOVE_ROWS = 256
SUBLANES = 8
ROUTER_ROWS = -(-(N_EXPERTS + N_GROUPS) // SUBLANES) * SUBLANES
V7X_VMEM_BYTES = 64 * 1024 * 1024
VMEM_LIMIT = V7X_VMEM_BYTES * 3 // 4

F32 = jnp.float32
BF16 = jnp.bfloat16
_NT = (((1,), (1,)), ((), ()))


def _t5_thresholds():
    n = np.arange(0, 1 << 16)
    max_exact = NUM_BUCKETS // 2
    nf = np.maximum(n, 1).astype(np.float32)
    large = max_exact + (np.log(nf / np.float32(max_exact)) / np.float32(math.log(MAX_DISTANCE / max_exact))
                         * np.float32(NUM_BUCKETS - max_exact)).astype(np.int32)
    bucket = np.where(n < max_exact, n, np.minimum(large, NUM_BUCKETS - 1))
    return [int(np.searchsorted(bucket, b, side="left")) for b in range(1, NUM_BUCKETS)]


_T5_THRESHOLDS = _t5_thresholds()


def _params(*sem):
    return pltpu.CompilerParams(dimension_semantics=sem, vmem_limit_bytes=VMEM_LIMIT)


def _bias_kernel(tab_ref, out_ref):
    h = pl.program_id(0)
    kj = lax.broadcasted_iota(jnp.int32, (ATT_TILE, ATT_TILE), 0)
    qi = lax.broadcasted_iota(jnp.int32, (ATT_TILE, ATT_TILE), 1)
    for d in range(out_ref.shape[0]):
        dist = d * ATT_TILE + qi - kj
        lo, hi = d * ATT_TILE - (ATT_TILE - 1), d * ATT_TILE + (ATT_TILE - 1)
        base = sum(thr <= max(lo, 0) for thr in _T5_THRESHOLDS)
        val = jnp.full((ATT_TILE, ATT_TILE), tab_ref[base, h], F32)
        for b, thr in enumerate(_T5_THRESHOLDS, start=1):
            if max(lo, 0) < thr <= hi:
                val = jnp.where(dist >= thr, tab_ref[b, h], val)
        val = val * LOG2E
        out_ref[d] = jnp.where(dist < 0, NEG_INF, val) if lo < 0 else val


def _bias_tiles(rel_bias, n_diag):
    return pl.pallas_call(
        _bias_kernel,
        grid=(N_HEADS_TOTAL,),
        in_specs=[pl.BlockSpec(memory_space=pltpu.SMEM)],
        out_specs=pl.BlockSpec((None, n_diag, ATT_TILE, ATT_TILE), lambda h: (h, 0, 0, 0)),
        out_shape=jax.ShapeDtypeStruct((N_HEADS_TOTAL, n_diag, ATT_TILE, ATT_TILE), F32),
        compiler_params=_params("arbitrary"),
        name="bias_tiles",
    )(rel_bias)


def _in_proj_kernel(x_ref, g1_ref, w_ref, pg_ref, grp_ref, qk_ref, vt_ref):
    n_row_tiles = PROJ_ROWS // ATT_TILE
    h = []
    for t in range(n_row_tiles):
        x = x_ref[t * ATT_TILE:(t + 1) * ATT_TILE, :]
        ms = jnp.mean(x * x, axis=-1, keepdims=True)
        h.append((x * lax.rsqrt(ms + RMS_EPS) * g1_ref[...]).astype(BF16))

    cw = 2 * LANES

    def finish(c, t, p):
        rows = slice(t * ATT_TILE, (t + 1) * ATT_TILE)
        cols = slice(c * cw, (c + 1) * cw)
        if c < QK_WIDTH // cw:
            msq = jnp.dot((p * p).astype(BF16), grp_ref[...], preferred_element_type=F32)
            qk_ref[rows, cols] = (p * lax.rsqrt(msq + RMS_EPS) * pg_ref[:, cols]).astype(BF16)
        else:
            vt_ref[t, c * cw - QK_WIDTH:(c + 1) * cw - QK_WIDTH, :] = p.T.astype(BF16)

    units = [(c, t) for c in range((QK_WIDTH + V_WIDTH) // cw) for t in range(n_row_tiles)]
    pending = None
    for c, t in units:
        p = jnp.dot(h[t], w_ref[:, c * cw:(c + 1) * cw], preferred_element_type=F32)
        if pending is not None:
            finish(*pending)
        pending = (c, t, p)
    finish(*pending)


def _in_proj(x2, g1, w_perm, post_gain, grp, batch, seq):
    n = x2.shape[0]
    steps_per_seq = seq // PROJ_ROWS
    tiles_per_step = PROJ_ROWS // ATT_TILE
    return pl.pallas_call(
        _in_proj_kernel,
        grid=(n // PROJ_ROWS,),
        in_specs=[
            pl.BlockSpec((PROJ_ROWS, D_MODEL), lambda i: (i, 0)),
            pl.BlockSpec((1, D_MODEL), lambda i: (0, 0)),
            pl.BlockSpec((D_MODEL, QK_WIDTH + V_WIDTH), lambda i: (0, 0)),
            pl.BlockSpec((1, QK_WIDTH), lambda i: (0, 0)),
            pl.BlockSpec((2 * LANES, 2 * LANES), lambda i: (0, 0)),
        ],
        out_specs=[
            pl.BlockSpec((PROJ_ROWS, QK_WIDTH), lambda i: (i, 0)),
            pl.BlockSpec((None, tiles_per_step, V_WIDTH, ATT_TILE),
                         lambda i: (i // steps_per_seq, i % steps_per_seq, 0, 0)),
        ],
        out_shape=[
            jax.ShapeDtypeStruct((n, QK_WIDTH), BF16),
            jax.ShapeDtypeStruct((batch, seq // ATT_TILE, V_WIDTH, ATT_TILE), BF16),
        ],
        compiler_params=_params("arbitrary"),
        name="in_proj",
    )(x2, g1, w_perm, post_gain, grp)


def _flash_tiles(qi, n_chains, logits_fn, values_fn, bias_fn, mask_fn, s_ref, p_ref, acc_ref):
    for c in range(n_chains):
        s_ref[0, c] = logits_fn(c, qi)
    p_ref[...] = jnp.zeros_like(p_ref)
    acc_ref[...] = jnp.zeros_like(acc_ref)

    def add_values(c, kv, alpha):
        acc_ref[c] = acc_ref[c] * alpha + jnp.dot(values_fn(c, kv), p_ref[c], preferred_element_type=F32)

    def step(j, carry):
        cur = lax.rem(j, 2)
        kv = qi - j
        out = []
        for c in range(n_chains):
            add_values(c, jnp.minimum(kv + 1, qi), carry[2 * c + 1])
        for c in range(n_chains):
            s = s_ref[cur, c] + bias_fn(c, j)
            tile_max = jnp.max(s, axis=0, keepdims=True)
            shift = mask_fn(c, kv)
            if shift is not None:
                tile_max = tile_max + shift
            m_new = jnp.maximum(carry[2 * c], tile_max)
            alpha = jnp.exp2(carry[2 * c] - m_new)
            p_ref[c] = jnp.exp2(s - (m_new if shift is None else m_new - shift)).astype(BF16)
            out += [m_new, alpha]
        for c in range(n_chains):
            s_ref[1 - cur, c] = logits_fn(c, jnp.maximum(kv - 1, 0))
        return tuple(out)

    m0 = jnp.full((1, ATT_TILE), NEG_INF, F32)
    one = jnp.ones((1, ATT_TILE), F32)
    carry = lax.fori_loop(0, qi + 1, step, (m0, one) * n_chains)
    for c in range(n_chains):
        add_values(c, 0, carry[2 * c + 1])


def _bounded_tiles(n_tiles, n_chains, logits_fn, values_fn, bias_fn, keep_fn, finish_fn, p_ref, acc_ref,
                   between_fn=lambda d: None):
    for d in range(n_tiles):
        for qi in range(d, n_tiles):
            for c in range(n_chains):
                p_ref[qi, c] = jnp.exp2(logits_fn(c, qi, qi - d) + bias_fn(c, d)).astype(BF16)
        between_fn(d)
        for qi in range(d, n_tiles):
            kv = qi - d
            for c in range(n_chains):
                pv = jnp.dot(values_fn(c, kv), p_ref[qi, c], preferred_element_type=F32)
                keep = keep_fn(c, qi, kv)
                if keep is not None:
                    pv = pv * keep
                acc_ref[qi, c] = pv if d == 0 else acc_ref[qi, c] + pv
        finish_fn(d)


def _logits_bounded(q_gain, k_gain, bias_cols):
    bound = (HEAD_DIM * jnp.max(jnp.abs(q_gain)) * jnp.max(jnp.abs(k_gain)) * ROUNDING_MARGIN
             + jnp.max(jnp.abs(bias_cols)) * LOG2E)
    return (bound <= LOGIT_LIMIT).astype(jnp.int32).reshape(1)


def _with_ones_rows(vt):
    return jnp.concatenate([vt, jnp.ones((ONES_ROWS, vt.shape[1]), vt.dtype)], axis=0)


def _attention_scratch(n_tiles, value_rows, n_chains=2):
    return [pltpu.VMEM((n_tiles, n_chains, ATT_TILE, LANES), BF16),
            pltpu.VMEM((2, n_chains, ATT_TILE, ATT_TILE), F32),
            pltpu.VMEM((n_tiles, n_chains, ATT_TILE, ATT_TILE), BF16),
            pltpu.VMEM((n_tiles, n_chains, value_rows + ONES_ROWS, ATT_TILE), F32)]


def _half_lane_split(q):
    lane = lax.broadcasted_iota(jnp.int32, q.shape, 1)
    zero = jnp.zeros_like(q)
    return jnp.where(lane < HEAD_DIM, q, zero), jnp.where(lane >= HEAD_DIM, q, zero)


def _diff_kernel(bounded_ref, lq1_ref, lk1_ref, lq2_ref, lk2_ref, subg_ref, q_ref, k_ref, vt_ref, bias_ref, o_ref,
                 qm_ref, s_ref, p_ref, acc_ref):
    n_tiles = q_ref.shape[0]
    chains = dict(n_chains=2, values_fn=lambda c, kv: _with_ones_rows(vt_ref[kv]))
    width = 2 * HEAD_DIM

    def prepare():
        for qi in range(n_tiles):
            qm_ref[qi, 0], qm_ref[qi, 1] = _half_lane_split(q_ref[qi])

    def finish(qi):
        lam = (jnp.exp(jnp.sum(lq1_ref[...] * lk1_ref[...], keepdims=True))
               - jnp.exp(jnp.sum(lq2_ref[...] * lk2_ref[...], keepdims=True)) + LAMBDA_INIT)
        a1, a2 = acc_ref[qi, 0], acc_ref[qi, 1]
        o = a1[:width] / a1[width:width + 1] - lam * (a2[:width] / a2[width:width + 1])
        ms = jnp.mean(o * o, axis=0, keepdims=True)
        o = o * lax.rsqrt(ms + RMS_EPS) * subg_ref[...] * (1.0 - LAMBDA_INIT)
        o_ref[qi] = o.T.astype(BF16)

    bounded = bounded_ref[0] > 0

    @pl.when(bounded)
    def _():
        prepare()
        _bounded_tiles(
            n_tiles,
            logits_fn=lambda c, qi, kv: lax.dot_general(k_ref[kv], qm_ref[qi, c], _NT, preferred_element_type=F32),
            bias_fn=lambda c, d: bias_ref[d],
            keep_fn=lambda c, qi, kv: None,
            finish_fn=finish,
            p_ref=p_ref, acc_ref=acc_ref, **chains)

    @pl.when(jnp.logical_not(bounded))
    def _():
        prepare()

        def q_tile(qi, carry):
            _flash_tiles(
                qi,
                logits_fn=lambda c, kv: lax.dot_general(k_ref[kv], qm_ref[qi, c], _NT, preferred_element_type=F32),
                bias_fn=lambda c, j: bias_ref[j],
                mask_fn=lambda c, kv: None,
                s_ref=s_ref, p_ref=p_ref.at[0], acc_ref=acc_ref.at[qi], **chains)
            return carry
        lax.fori_loop(0, n_tiles, q_tile, 0)
        for qi in range(n_tiles):
            finish(qi)


def _diff_attention(bounded, lam_vecs, sub_g_col, qk4, vt4, bias):
    batch, n_tiles = qk4.shape[0], qk4.shape[1]
    k_col0 = DIFF_WIDTH // LANES
    vec = pl.BlockSpec((1, HEAD_DIM), lambda h, b: (0, 0))
    return pl.pallas_call(
        _diff_kernel,
        grid=(HEADS_DIFF, batch),
        in_specs=[
            pl.BlockSpec(memory_space=pltpu.SMEM),
            vec, vec, vec, vec,
            pl.BlockSpec((2 * HEAD_DIM, 1), lambda h, b: (0, 0)),
            pl.BlockSpec((None, n_tiles, ATT_TILE, LANES), lambda h, b: (b, 0, 0, h)),
            pl.BlockSpec((None, n_tiles, ATT_TILE, LANES), lambda h, b: (b, 0, 0, k_col0 + h)),
            pl.BlockSpec((None, n_tiles, LANES, ATT_TILE), lambda h, b: (b, 0, h, 0)),
            pl.BlockSpec((None, n_tiles, ATT_TILE, ATT_TILE), lambda h, b: (h, 0, 0, 0)),
        ],
        out_specs=pl.BlockSpec((None, n_tiles, ATT_TILE, LANES), lambda h, b: (b, 0, 0, h)),
        out_shape=jax.ShapeDtypeStruct((batch, n_tiles, ATT_TILE, DIFF_WIDTH), BF16),
        scratch_shapes=_attention_scratch(n_tiles, 2 * HEAD_DIM),
        compiler_params=_params("arbitrary", "arbitrary"),
        name="diff_attention",
    )(bounded, *lam_vecs, sub_g_col, qk4, qk4, vt4, bias)


def _split3(v):
    hi = v.astype(BF16)
    r1 = v - hi.astype(F32)
    mid = r1.astype(BF16)
    lo = (r1 - mid.astype(F32)).astype(BF16)
    return hi, mid, lo


def _block_mask(gate, own):
    row = lax.broadcasted_iota(jnp.int32, gate.shape, 0)
    rank = jnp.zeros(gate.shape, jnp.int32)
    for m in range(own):
        gm = gate[m:m + 1, :]
        beats = (gm > gate) | ((gm == gate) & (row > m))
        rank = rank + jnp.where(beats, 1, 0)
    keep = ((rank < MOBA_TOPK) & (row < own)) | (row == own)
    return jnp.where(keep, 0.0, NEG_INF).astype(F32)


def _moba_kernel(bounded_ref, q_ref, k_ref, vt_ref, bias_ref, o_ref, kmean_ref, mask_ref, qm_ref, s_ref, p_ref,
                 acc_ref):
    n_tiles = q_ref.shape[0]

    def split_queries():
        for qi in range(n_tiles):
            qm_ref[qi, 0], qm_ref[qi, 1] = _half_lane_split(q_ref[qi])

    def block_masks(q_tiles):
        for n in range(n_tiles):
            kmean_ref[n:n + 1, :] = jnp.mean(k_ref[n].astype(F32), axis=0, keepdims=True)
        per_head = zip(*[_half_lane_split(term) for term in _split3(kmean_ref[...])])
        gate_lhs = jnp.concatenate([term for head_terms in per_head for term in head_terms], axis=0)
        n_terms = gate_lhs.shape[0] // (2 * n_tiles)
        for qi in q_tiles:
            terms = lax.dot_general(gate_lhs, q_ref[qi], _NT, preferred_element_type=F32)
            for half in range(2):
                rows = [terms[(half * n_terms + t) * n_tiles:(half * n_terms + t + 1) * n_tiles]
                        for t in range(n_terms)]
                mask_ref[qi, half] = _block_mask(sum(rows[1:], rows[0]), qi)

    def keep_row(c, qi, kv):
        if qi <= MOBA_TOPK or kv == qi:
            return None
        return jnp.where(mask_ref[qi, c, kv:kv + 1, :] < -1.0, 0.0, 1.0)

    def finish(qi):
        halves = [acc_ref[qi, half] for half in range(2)]
        o = jnp.concatenate([a[:HEAD_DIM] / a[HEAD_DIM:HEAD_DIM + 1] for a in halves], axis=0)
        o_ref[qi] = o.T.astype(BF16)

    chains = dict(
        n_chains=2,
        values_fn=lambda c, kv: _with_ones_rows(vt_ref[kv, c * HEAD_DIM:(c + 1) * HEAD_DIM, :]))
    bounded = bounded_ref[0] > 0

    @pl.when(bounded)
    def _():
        split_queries()
        _bounded_tiles(
            n_tiles,
            logits_fn=lambda c, qi, kv: lax.dot_general(k_ref[kv], qm_ref[qi, c], _NT, preferred_element_type=F32),
            bias_fn=lambda c, d: bias_ref[c, d],
            keep_fn=keep_row,
            between_fn=lambda d: block_masks(range(MOBA_TOPK + 1, n_tiles)) if d == 0 else None,
            finish_fn=finish,
            p_ref=p_ref, acc_ref=acc_ref, **chains)

    @pl.when(jnp.logical_not(bounded))
    def _():
        split_queries()
        block_masks(range(n_tiles))

        def q_tile(qi, carry):
            _flash_tiles(
                qi,
                logits_fn=lambda c, kv: lax.dot_general(k_ref[kv], qm_ref[qi, c], _NT, preferred_element_type=F32),
                bias_fn=lambda c, j: bias_ref[c, j],
                mask_fn=lambda c, kv: mask_ref[qi, c, pl.ds(kv, 1), :],
                s_ref=s_ref, p_ref=p_ref.at[0], acc_ref=acc_ref.at[qi], **chains)
            return carry
        lax.fori_loop(0, n_tiles, q_tile, 0)
        for qi in range(n_tiles):
            finish(qi)


def _moba_attention(bounded, qk4, vt4, bias_pairs):
    batch, n_tiles = qk4.shape[0], qk4.shape[1]
    q_col0 = 2 * DIFF_WIDTH // LANES
    k_col0 = q_col0 + MOBA_WIDTH // LANES
    v_row0 = DIFF_WIDTH // LANES
    pair0 = HEADS_DIFF // 2
    return pl.pallas_call(
        _moba_kernel,
        grid=(HEADS_MOBA // 2, batch),
        in_specs=[
            pl.BlockSpec(memory_space=pltpu.SMEM),
            pl.BlockSpec((None, n_tiles, ATT_TILE, LANES), lambda h, b: (b, 0, 0, q_col0 + h)),
            pl.BlockSpec((None, n_tiles, ATT_TILE, LANES), lambda h, b: (b, 0, 0, k_col0 + h)),
            pl.BlockSpec((None, n_tiles, LANES, ATT_TILE), lambda h, b: (b, 0, v_row0 + h, 0)),
            pl.BlockSpec((None, 2, n_tiles, ATT_TILE, ATT_TILE), lambda h, b: (pair0 + h, 0, 0, 0, 0)),
        ],
        out_specs=pl.BlockSpec((None, n_tiles, ATT_TILE, LANES), lambda h, b: (b, 0, 0, h)),
        out_shape=jax.ShapeDtypeStruct((batch, n_tiles, ATT_TILE, MOBA_WIDTH), BF16),
        scratch_shapes=[
            pltpu.VMEM((n_tiles, LANES), F32),
            pltpu.VMEM((n_tiles, 2, n_tiles, ATT_TILE), F32),
        ] + _attention_scratch(n_tiles, HEAD_DIM),
        compiler_params=_params("arbitrary", "arbitrary"),
        name="moba_attention",
    )(bounded, qk4, qk4, vt4, bias_pairs)


def _load_token_rows(ref, n_rows, lead=(), first=0):
    chunks = [ref[lead + (pl.ds(first * ROW_TILE + c, n_rows, stride=ROW_TILE), slice(None))]
              for c in range(ROW_TILE)]
    return jnp.concatenate(chunks, axis=1)


def _store_token_rows(ref, value, first=0):
    n_rows = value.shape[0]
    for c in range(ROW_TILE):
        ref[pl.ds(first * ROW_TILE + c, n_rows, stride=ROW_TILE), :] = value[:, c * LANES:(c + 1) * LANES]


def _token_tile(ref, t):
    return ref.at[pl.ds(pl.multiple_of(t * ROW_TILE, ROW_TILE), ROW_TILE)]


def _first_argmax(v):
    top = jnp.max(v, axis=0, keepdims=True)
    row = lax.broadcasted_iota(jnp.int32, v.shape, 0)
    idx = jnp.min(jnp.where(v == top, row, v.shape[0]), axis=0, keepdims=True)
    return top, idx


def _out_proj_kernel(yd_ref, ym_ref, x_ref, wd_ref, wm_ref, g2_ref, wr_ref, tri_ref,
                     x1_ref, ri_ref, rf_ref, cnt_ref, run_ref):
    @pl.when(lax.rem(pl.program_id(0), pl.num_programs(0) // MOE_TOKEN_PARTS) == 0)
    def _():
        run_ref[...] = jnp.zeros_like(run_ref)

    def project(t):
        rows = slice(t * ROUTE_TILE, (t + 1) * ROUTE_TILE)
        x1 = (x_ref[rows, :] + jnp.dot(yd_ref[rows, :], wd_ref[...], preferred_element_type=F32)
              + jnp.dot(ym_ref[rows, :], wm_ref[...], preferred_element_type=F32))
        _store_token_rows(x1_ref, x1, first=t * ROUTE_TILE)
        ms = jnp.mean(x1 * x1, axis=-1, keepdims=True)
        return (x1 * lax.rsqrt(ms + RMS_EPS) * g2_ref[...]).astype(BF16)

    def pick(h2):
        lg = lax.dot_general(wr_ref[...], h2, _NT, preferred_element_type=F32)
        g_logits = lg[N_EXPERTS:N_EXPERTS + N_GROUPS, :]
        g_top, g_idx = _first_argmax(g_logits)
        p_group = 1.0 / jnp.sum(jnp.exp(g_logits - g_top), axis=0, keepdims=True)
        e_logits = lg[0:EXPERTS_PER_GROUP, :]
        for g in range(1, N_GROUPS):
            e_logits = jnp.where(g_idx == g, lg[g * EXPERTS_PER_GROUP:(g + 1) * EXPERTS_PER_GROUP, :], e_logits)
        v1, i1 = _first_argmax(e_logits)
        row = lax.broadcasted_iota(jnp.int32, e_logits.shape, 0)
        v2, i2 = _first_argmax(jnp.where(row == i1, -jnp.inf, e_logits))
        ratio = jnp.exp(v2 - v1)
        w1 = p_group / (1.0 + ratio)
        w2 = p_group * ratio / (1.0 + ratio)
        e1 = g_idx * EXPERTS_PER_GROUP + i1
        e2 = g_idx * EXPERTS_PER_GROUP + i2
        erow = lax.broadcasted_iota(jnp.int32, (N_EXPERTS, ROUTE_TILE), 0)
        hit1 = erow == e1
        hit2 = erow == e2
        onehot = jnp.where(hit1 | hit2, 1.0, 0.0).astype(F32)
        return e1, e2, w1, w2, hit1, hit2, onehot

    def place(t, e1, e2, w1, w2, hit1, hit2, onehot):
        before = jnp.dot(onehot.astype(BF16), tri_ref[...], preferred_element_type=F32) + run_ref[...]
        r1 = jnp.sum(jnp.where(hit1, before, 0.0), axis=0, keepdims=True).astype(jnp.int32)
        r2 = jnp.sum(jnp.where(hit2, before, 0.0), axis=0, keepdims=True).astype(jnp.int32)
        run_ref[...] = run_ref[...] + jnp.sum(onehot, axis=1, keepdims=True)
        lanes = slice(t * ROUTE_TILE, (t + 1) * ROUTE_TILE)
        out_row = lax.broadcasted_iota(jnp.int32, (SUBLANES, ROUTE_TILE), 0)
        ri_ref[:, lanes] = jnp.where(out_row == 0, e1, jnp.where(out_row == 1, e2,
                                     jnp.where(out_row == 2, r1, jnp.where(out_row == 3, r2, 0))))
        rf_ref[:, lanes] = jnp.where(out_row == 0, w1, jnp.where(out_row == 1, w2, 0.0))

    n_tiles = OUT_ROWS // ROUTE_TILE
    projected, picked = {}, {}
    for t in range(n_tiles + 2):
        if t < n_tiles:
            projected[t] = project(t)
        if 0 <= t - 1 < n_tiles:
            picked[t - 1] = pick(projected.pop(t - 1))
        if 0 <= t - 2 < n_tiles:
            place(t - 2, *picked.pop(t - 2))
    cnt_ref[...] = jnp.broadcast_to(run_ref[...], cnt_ref.shape).astype(jnp.int32)


def _out_proj(y_d, y_m, x2, wo_d, wo_m, g2, w_router, tri):
    n = x2.shape[0]
    const = lambda i: (0, 0)
    steps_per_part = n // OUT_ROWS // MOE_TOKEN_PARTS
    return pl.pallas_call(
        _out_proj_kernel,
        grid=(n // OUT_ROWS,),
        in_specs=[
            pl.BlockSpec((OUT_ROWS, DIFF_WIDTH), lambda i: (i, 0)),
            pl.BlockSpec((OUT_ROWS, MOBA_WIDTH), lambda i: (i, 0)),
            pl.BlockSpec((OUT_ROWS, D_MODEL), lambda i: (i, 0)),
            pl.BlockSpec((DIFF_WIDTH, D_MODEL), const),
            pl.BlockSpec((MOBA_WIDTH, D_MODEL), const),
            pl.BlockSpec((1, D_MODEL), const),
            pl.BlockSpec((ROUTER_ROWS, D_MODEL), const),
            pl.BlockSpec((ROUTE_TILE, ROUTE_TILE), const),
        ],
        out_specs=[
            pl.BlockSpec((OUT_ROWS * ROW_TILE, LANES), lambda i: (i, 0)),
            pl.BlockSpec((SUBLANES, OUT_ROWS), lambda i: (0, i)),
            pl.BlockSpec((SUBLANES, OUT_ROWS), lambda i: (0, i)),
            pl.BlockSpec((None, N_EXPERTS, LANES), lambda i: (i // steps_per_part, 0, 0)),
        ],
        out_shape=[
            jax.ShapeDtypeStruct((n * ROW_TILE, LANES), F32),
            jax.ShapeDtypeStruct((SUBLANES, n), jnp.int32),
            jax.ShapeDtypeStruct((SUBLANES, n), F32),
            jax.ShapeDtypeStruct((MOE_TOKEN_PARTS, N_EXPERTS, LANES), jnp.int32),
        ],
        scratch_shapes=[pltpu.VMEM((N_EXPERTS, 1), F32)],
        compiler_params=_params("arbitrary"),
        name="out_proj_route",
    )(y_d, y_m, x2, wo_d, wo_m, g2, w_router, tri)


def _dispatch_step(i, n_steps, first_tile, zchunk_ref, dest_ref, x1_hbm, xs_hbm, zero_ref, stage_ref, zero_sem,
                   load_sem, row_sem):
    last = n_steps - 1
    slot = lax.rem(i, 2)
    chunk_rows = MOE_CHUNK * ROW_TILE
    tile_tokens = dest_ref.shape[2]
    tile_rows = tile_tokens * ROW_TILE

    def load(t, s):
        start = pl.multiple_of((first_tile + t) * tile_rows, tile_rows)
        return pltpu.make_async_copy(x1_hbm.at[pl.ds(start, tile_rows)], stage_ref.at[s], load_sem.at[s])

    def wait_rows(s):
        for _ in range(EXPERTS_PER_TOKEN):
            pltpu.make_async_copy(stage_ref.at[s], xs_hbm.at[pl.ds(0, tile_rows)], row_sem.at[s]).wait()

    def zero_copy(c):
        start = pl.multiple_of(c * chunk_rows, chunk_rows)
        return pltpu.make_async_copy(zero_ref, xs_hbm.at[pl.ds(start, chunk_rows)], zero_sem)

    @pl.when(i == 0)
    def _():
        zero_ref[...] = jnp.zeros_like(zero_ref)

        def start_one(c, carry):
            @pl.when(zchunk_ref[c] > 0)
            def _():
                zero_copy(c).start()
            return carry

        def wait_one(c, carry):
            @pl.when(zchunk_ref[c] > 0)
            def _():
                zero_copy(c).wait()
            return carry

        lax.fori_loop(0, zchunk_ref.shape[0], start_one, 0)
        lax.fori_loop(0, zchunk_ref.shape[0], wait_one, 0)
        load(0, 0).start()

    @pl.when(i > 0)
    def _():
        wait_rows(1 - slot)

    @pl.when(i < last)
    def _():
        load(i + 1, 1 - slot).start()

    load(i, slot).wait()

    def send_row(r, carry):
        src = _token_tile(stage_ref.at[slot], r)
        for k in range(EXPERTS_PER_TOKEN):
            pltpu.make_async_copy(src, _token_tile(xs_hbm, dest_ref[0, k, r]), row_sem.at[slot]).start(priority=k)
        return carry

    lax.fori_loop(0, tile_tokens, send_row, 0, unroll=8)

    @pl.when(i == last)
    def _():
        wait_rows(slot)


def _dispatch_scratch(tile_tokens):
    return [pltpu.VMEM((MOE_CHUNK * ROW_TILE, LANES), F32),
            pltpu.VMEM((2, tile_tokens * ROW_TILE, LANES), F32),
            pltpu.SemaphoreType.DMA(()),
            pltpu.SemaphoreType.DMA((2,)),
            pltpu.SemaphoreType.DMA((2,))]


def _dispatch_kernel(first_tile, zchunk_ref, dest_ref, x1_hbm, xs_hbm, *scratch):
    _dispatch_step(pl.program_id(0), pl.num_programs(0), first_tile, zchunk_ref, dest_ref, x1_hbm, xs_hbm, *scratch)


def _dispatch(part, zchunk, dest3, x1, n_slots):
    tile_tokens = dest3.shape[2]
    tiles = dest3.shape[0] // MOE_TOKEN_PARTS
    first_tile = part * tiles
    grid_spec = pltpu.PrefetchScalarGridSpec(
        num_scalar_prefetch=1,
        grid=(tiles,),
        in_specs=[
            pl.BlockSpec((1, EXPERTS_PER_TOKEN, tile_tokens), lambda i, zc: (first_tile + i, 0, 0),
                         memory_space=pltpu.SMEM),
            pl.BlockSpec(memory_space=pl.ANY),
        ],
        out_specs=pl.BlockSpec(memory_space=pl.ANY),
        scratch_shapes=_dispatch_scratch(tile_tokens),
    )
    return pl.pallas_call(
        functools.partial(_dispatch_kernel, first_tile),
        grid_spec=grid_spec,
        out_shape=jax.ShapeDtypeStruct((n_slots * ROW_TILE, LANES), F32),
        compiler_params=_params("arbitrary"),
        name="moe_dispatch",
    )(zchunk, dest3, x1)


def _expert_step(c, na_ref, xs_ref, g2_ref, wg_ref, wu_ref, wd_ref, o_ref):
    active = c < na_ref[0]

    @pl.when(jnp.logical_not(active))
    def _():
        o_ref[...] = jnp.zeros_like(o_ref)

    @pl.when(active)
    def _():
        part_rows = MOE_CHUNK // EXPERT_ROW_PARTS

        def gate_up(r):
            x = _load_token_rows(xs_ref, part_rows, first=r * part_rows)
            ms = jnp.mean(x * x, axis=-1, keepdims=True)
            h = (x * lax.rsqrt(ms + RMS_EPS) * g2_ref[...]).astype(BF16)
            return (jnp.dot(h, wg_ref[...], preferred_element_type=F32),
                    jnp.dot(h, wu_ref[...], preferred_element_type=F32))

        def down(r, gate, up):
            hid = (gate * jax.nn.sigmoid(gate) * up).astype(BF16)
            _store_token_rows(o_ref, jnp.dot(hid, wd_ref[...], preferred_element_type=F32), first=r * part_rows)

        pending = None
        for r in range(EXPERT_ROW_PARTS):
            current = (r,) + gate_up(r)
            if pending is not None:
                down(*pending)
            pending = current
        down(*pending)


def _expert_kernel(ce_ref, na_ref, *refs):
    _expert_step(pl.program_id(0), na_ref, *refs)


def _expert_specs():
    rows = lambda c, ce, na, *_: (jnp.minimum(c, na[0] - 1), 0)
    expert = lambda c, ce, na, *_: (ce[c], 0, 0)
    in_specs = [
        pl.BlockSpec((MOE_CHUNK * ROW_TILE, LANES), rows),
        pl.BlockSpec((1, D_MODEL), lambda c, *_: (0, 0)),
        pl.BlockSpec((None, D_MODEL, EXPERT_HIDDEN), expert),
        pl.BlockSpec((None, D_MODEL, EXPERT_HIDDEN), expert),
        pl.BlockSpec((None, EXPERT_HIDDEN, D_MODEL), expert),
    ]
    return in_specs, pl.BlockSpec((MOE_CHUNK * ROW_TILE, LANES), lambda c, *_: (c, 0))


def _experts(chunk_e, n_active, xs, g2, wg, wu, wd):
    n_slots = xs.shape[0] // ROW_TILE
    in_specs, out_spec = _expert_specs()
    grid_spec = pltpu.PrefetchScalarGridSpec(
        num_scalar_prefetch=2, grid=(n_slots // MOE_CHUNK,), in_specs=in_specs, out_specs=out_spec)
    return pl.pallas_call(
        _expert_kernel,
        grid_spec=grid_spec,
        out_shape=jax.ShapeDtypeStruct((n_slots * ROW_TILE, LANES), F32),
        compiler_params=_params("arbitrary"),
        name="moe_experts",
    )(chunk_e, n_active, xs, g2, wg, wu, wd)


def _experts_dispatch_kernel(first_tile, n_tiles, ce_ref, na_ref, zchunk_ref,
                             xs_ref, g2_ref, wg_ref, wu_ref, wd_ref, dest_ref, x1_hbm,
                             o_ref, xs_next_hbm, *scratch):
    i = pl.program_id(0)

    @pl.when(i < n_tiles)
    def _():
        _dispatch_step(i, n_tiles, first_tile, zchunk_ref, dest_ref, x1_hbm, xs_next_hbm, *scratch)

    _expert_step(i, na_ref, xs_ref, g2_ref, wg_ref, wu_ref, wd_ref, o_ref)


def _experts_dispatch(chunk_e, n_active, zchunk_next, xs, g2, wg, wu, wd, next_part, dest3, x1):
    n_slots = xs.shape[0] // ROW_TILE
    n_chunks = n_slots // MOE_CHUNK
    tile_tokens = dest3.shape[2]
    tiles = dest3.shape[0] // MOE_TOKEN_PARTS
    first_tile = next_part * tiles
    assert n_chunks >= tiles
    in_specs, out_spec = _expert_specs()
    in_specs += [
        pl.BlockSpec((1, EXPERTS_PER_TOKEN, tile_tokens),
                     lambda c, *_: (first_tile + jnp.minimum(c, tiles - 1), 0, 0), memory_space=pltpu.SMEM),
        pl.BlockSpec(memory_space=pl.ANY),
    ]
    grid_spec = pltpu.PrefetchScalarGridSpec(
        num_scalar_prefetch=3, grid=(n_chunks,), in_specs=in_specs,
        out_specs=[out_spec, pl.BlockSpec(memory_space=pl.ANY)],
        scratch_shapes=_dispatch_scratch(tile_tokens))
    slots = jax.ShapeDtypeStruct((n_slots * ROW_TILE, LANES), F32)
    return pl.pallas_call(
        functools.partial(_experts_dispatch_kernel, first_tile, tiles),
        grid_spec=grid_spec,
        out_shape=[slots, slots],
        compiler_params=_params("arbitrary"),
        name="moe_experts_dispatch",
    )(chunk_e, n_active, zchunk_next, xs, g2, wg, wu, wd, dest3, x1)


def _combine_kernel(dest_ref, dest_next_ref, w_ref, x1_ref, *rest):
    outs_hbm, (y_ref, buf_ref, sem_ref) = rest[:MOE_TOKEN_PARTS], rest[MOE_TOKEN_PARTS:]
    i = pl.program_id(0)
    slot = lax.rem(i, 2)
    tiles_per_part = pl.num_programs(0) // MOE_TOKEN_PARTS

    def fetch(tile, dref, s):
        for part, src_hbm in enumerate(outs_hbm):
            @pl.when(tile // tiles_per_part == part)
            def _():
                def fetch_row(r, carry):
                    for k in range(EXPERTS_PER_TOKEN):
                        pltpu.make_async_copy(_token_tile(src_hbm, dref[0, k, r]),
                                              _token_tile(buf_ref.at[s, k], r), sem_ref.at[s]).start(priority=k)
                    return carry
                lax.fori_loop(0, MOVE_ROWS, fetch_row, 0, unroll=8)

    @pl.when(i == 0)
    def _():
        fetch(i, dest_ref, 0)

    @pl.when(i + 1 < pl.num_programs(0))
    def _():
        fetch(i + 1, dest_next_ref, 1 - slot)

    for k in range(EXPERTS_PER_TOKEN):
        pltpu.make_async_copy(outs_hbm[0].at[pl.ds(0, MOVE_ROWS * ROW_TILE)], buf_ref.at[slot, k],
                              sem_ref.at[slot]).wait()
    w = w_ref[...]
    y_ref[...] = (_load_token_rows(x1_ref, MOVE_ROWS)
                  + w[:, 0:1] * _load_token_rows(buf_ref, MOVE_ROWS, (slot, 0))
                  + w[:, 1:2] * _load_token_rows(buf_ref, MOVE_ROWS, (slot, 1)))


def _combine(dest3, wts, x1, outs_parts):
    n = x1.shape[0] // ROW_TILE
    steps = n // MOVE_ROWS
    dest_block = (1, EXPERTS_PER_TOKEN, MOVE_ROWS)
    return pl.pallas_call(
        _combine_kernel,
        grid=(steps,),
        in_specs=[
            pl.BlockSpec(dest_block, lambda i: (i, 0, 0), memory_space=pltpu.SMEM),
            pl.BlockSpec(dest_block, lambda i: (jnp.minimum(i + 1, steps - 1), 0, 0), memory_space=pltpu.SMEM),
            pl.BlockSpec((MOVE_ROWS, EXPERTS_PER_TOKEN), lambda i: (i, 0)),
            pl.BlockSpec((MOVE_ROWS * ROW_TILE, LANES), lambda i: (i, 0)),
        ] + [pl.BlockSpec(memory_space=pl.ANY)] * MOE_TOKEN_PARTS,
        out_specs=pl.BlockSpec((MOVE_ROWS, D_MODEL), lambda i: (i, 0)),
        out_shape=jax.ShapeDtypeStruct((n, D_MODEL), F32),
        scratch_shapes=[
            pltpu.VMEM((2, EXPERTS_PER_TOKEN, MOVE_ROWS * ROW_TILE, LANES), F32),
            pltpu.SemaphoreType.DMA((2,)),
        ],
        compiler_params=_params("arbitrary"),
        name="moe_combine",
    )(dest3, dest3, wts, x1, *outs_parts)


def _layer(x, norm1_g, w_in, diff_q_g, diff_k_g, lambda_q1, lambda_k1, lambda_q2, lambda_k2, diff_sub_g,
           moba_q_g, moba_k_g, rel_bias, w_out, norm2_g, router_group, router_expert, w_gate, w_up, w_down):
    batch, seq, _ = x.shape
    n = batch * seq
    n_tiles = seq // ATT_TILE
    scale = HEAD_DIM ** -0.5 * LOG2E

    d3 = 3 * DIFF_WIDTH
    w_perm = jnp.concatenate([w_in[:, :2 * DIFF_WIDTH], w_in[:, d3:d3 + 2 * MOBA_WIDTH],
                              w_in[:, 2 * DIFF_WIDTH:d3], w_in[:, d3 + 2 * MOBA_WIDTH:]], axis=1).astype(BF16)
    reps_d, reps_m = DIFF_WIDTH // HEAD_DIM, MOBA_WIDTH // HEAD_DIM
    post_gain = jnp.concatenate([jnp.tile(diff_q_g * scale, reps_d), jnp.tile(diff_k_g, reps_d),
                                 jnp.tile(moba_q_g * scale, reps_m), jnp.tile(moba_k_g, reps_m)])[None, :]
    head_of = np.arange(2 * LANES) // HEAD_DIM
    grp = jnp.asarray((head_of[:, None] == head_of[None, :]) / HEAD_DIM, BF16)

    bias = _bias_tiles(rel_bias, n_tiles)
    qk, vt4 = _in_proj(x.reshape(n, D_MODEL), norm1_g[None, :], w_perm, post_gain, grp, batch, seq)
    qk4 = qk.reshape(batch, n_tiles, ATT_TILE, QK_WIDTH)
    lam_vecs = [v[None, :] for v in (lambda_q1, lambda_k1, lambda_q2, lambda_k2)]
    y_d = _diff_attention(_logits_bounded(diff_q_g * scale, diff_k_g, rel_bias[:, :HEADS_DIFF]),
                          lam_vecs, diff_sub_g[:, None], qk4, vt4, bias)
    y_m = _moba_attention(_logits_bounded(moba_q_g * scale, moba_k_g, rel_bias[:, HEADS_DIFF:]),
                          qk4, vt4, bias.reshape(N_HEADS_TOTAL // 2, 2, n_tiles, ATT_TILE, ATT_TILE))

    w_router = jnp.concatenate([
        router_expert.transpose(0, 2, 1).reshape(N_EXPERTS, D_MODEL), router_group.T,
        jnp.zeros((ROUTER_ROWS - N_EXPERTS - N_GROUPS, D_MODEL), F32)], axis=0).astype(BF16)
    tri = jnp.asarray(np.triu(np.ones((ROUTE_TILE, ROUTE_TILE)), k=1), BF16)
    wo = w_out.astype(BF16)
    x1, route_i, route_f, cnt = _out_proj(y_d.reshape(n, DIFF_WIDTH), y_m.reshape(n, MOBA_WIDTH),
                                          x.reshape(n, D_MODEL), wo[:DIFF_WIDTH], wo[DIFF_WIDTH:],
                                          norm2_g[None, :], w_router, tri)

    parts = MOE_TOKEN_PARTS
    counts = cnt[:, :, 0]
    pad_counts = ((counts + MOE_CHUNK - 1) // MOE_CHUNK) * MOE_CHUNK
    pad_end = jnp.cumsum(pad_counts, axis=1)
    pad_start = pad_end - pad_counts
    experts = jnp.arange(N_EXPERTS, dtype=jnp.int32)
    picked = route_i[0:2].reshape(EXPERTS_PER_TOKEN, parts, n // parts)
    start_of = jnp.sum(jnp.where(picked[..., None] == experts, pad_start[None, :, None, :], 0), axis=-1)
    dest = start_of.reshape(EXPERTS_PER_TOKEN, n) + route_i[2:4]
    by_tile = lambda rows: dest.reshape(EXPERTS_PER_TOKEN, n // rows, rows).transpose(1, 0, 2)
    dest3, dest3_fused = by_tile(MOVE_ROWS), by_tile(FUSED_MOVE_ROWS)
    n_slots = n // parts * EXPERTS_PER_TOKEN + N_EXPERTS * MOE_CHUNK
    n_chunks = n_slots // MOE_CHUNK
    chunk_id = jnp.arange(n_chunks, dtype=jnp.int32)
    chunk_e = jnp.minimum(jnp.sum((pad_end[:, None, :] <= chunk_id[None, :, None] * MOE_CHUNK).astype(jnp.int32),
                                  axis=2), N_EXPERTS - 1)
    n_active = (pad_end[:, -1:] // MOE_CHUNK).astype(jnp.int32)
    next_e = jnp.concatenate([chunk_e[:, 1:], jnp.full((parts, 1), N_EXPERTS, jnp.int32)], axis=1)
    zchunk = ((chunk_id[None, :] >= n_active - 1) | (chunk_e != next_e)).astype(jnp.int32)

    weights = (norm2_g[None, :], w_gate.astype(BF16), w_up.astype(BF16), w_down.astype(BF16))
    xs = _dispatch(0, zchunk[0], dest3, x1, n_slots)
    outs = []
    for p in range(parts):
        if p + 1 < parts:
            out_p, xs_next = _experts_dispatch(chunk_e[p], n_active[p], zchunk[p + 1], xs, *weights, p + 1,
                                               dest3_fused, x1)
        else:
            out_p, xs_next = _experts(chunk_e[p], n_active[p], xs, *weights), None
        outs.append(out_p)
        xs = xs_next
    y = _combine(dest3, route_f[0:2].T, x1, outs)
    return y.reshape(batch, seq, D_MODEL)


def kernel(x, norm1_g, w_in, diff_q_g, diff_k_g, lambda_q1, lambda_k1, lambda_q2, lambda_k2, diff_sub_g,
           moba_q_g, moba_k_g, rel_bias, w_out, norm2_g, router_group, router_expert, w_gate, w_up, w_down):
    assert x.shape[1] % PROJ_ROWS == 0 and x.shape[2] == D_MODEL and norm1_g.shape[0] == 1
    return _layer(x, norm1_g[0], w_in[0], diff_q_g[0], diff_k_g[0], lambda_q1[0], lambda_k1[0], lambda_q2[0],
                  lambda_k2[0], diff_sub_g[0], moba_q_g[0], moba_k_g[0], rel_bias, w_out[0], norm2_g[0],
                  router_group[0], router_expert[0], w_gate[0], w_up[0], w_down[0])
```

```python
import functools
import math

import numpy as np
import jax
import jax.numpy as jnp
from jax import lax
from jax.experimental import pallas as pl
from jax.experimental.pallas import tpu as pltpu

D_MODEL = 1024
HEAD_DIM = 64
HEADS_DIFF = 4
HEADS_MOBA = 8
N_HEADS_TOTAL = HEADS_DIFF + HEADS_MOBA
DIFF_WIDTH = HEADS_DIFF * 2 * HEAD_DIM
MOBA_WIDTH = HEADS_MOBA * HEAD_DIM
QK_WIDTH = 2 * DIFF_WIDTH + 2 * MOBA_WIDTH
V_WIDTH = DIFF_WIDTH + MOBA_WIDTH
MOBA_TOPK = 3
NUM_BUCKETS = 32
MAX_DISTANCE = 2048
N_GROUPS = 4
EXPERTS_PER_GROUP = 8
N_EXPERTS = N_GROUPS * EXPERTS_PER_GROUP
EXPERTS_PER_TOKEN = 2
EXPERT_HIDDEN = 256
RMS_EPS = 1e-6
NEG_INF = -1e30
LAMBDA_INIT = 0.8 - 0.6 * math.exp(-0.3 * 0)

LANES = 128
ROW_TILE = D_MODEL // LANES
ATT_TILE = 256
ONES_ROWS = 16
LOGIT_LIMIT = 80.0
ROUNDING_MARGIN = 1.05
LOG2E = math.log2(math.e)
PROJ_ROWS = 1024
OUT_ROWS = 1024
ROUTE_TILE = 256
MOE_CHUNK = 512
MOE_TOKEN_PARTS = 2
EXPERT_ROW_PARTS = 4
MOVE_ROWS = 512
FUSED_MOVE_ROWS = 256
SUBLANES = 8
ROUTER_ROWS = -(-(N_EXPERTS + N_GROUPS) // SUBLANES) * SUBLANES
V7X_VMEM_BYTES = 64 * 1024 * 1024
VMEM_LIMIT = V7X_VMEM_BYTES * 3 // 4

F32 = jnp.float32
BF16 = jnp.bfloat16
_NT = (((1,), (1,)), ((), ()))


def _t5_thresholds():
    n = np.arange(0, 1 << 16)
    max_exact = NUM_BUCKETS // 2
    nf = np.maximum(n, 1).astype(np.float32)
    large = max_exact + (np.log(nf / np.float32(max_exact)) / np.float32(math.log(MAX_DISTANCE / max_exact))
                         * np.float32(NUM_BUCKETS - max_exact)).astype(np.int32)
    bucket = np.where(n < max_exact, n, np.minimum(large, NUM_BUCKETS - 1))
    return [int(np.searchsorted(bucket, b, side="left")) for b in range(1, NUM_BUCKETS)]


_T5_THRESHOLDS = _t5_thresholds()


def _params(*sem):
    return pltpu.CompilerParams(dimension_semantics=sem, vmem_limit_bytes=VMEM_LIMIT)


def _bias_kernel(tab_ref, out_ref):
    h = pl.program_id(0)
    kj = lax.broadcasted_iota(jnp.int32, (ATT_TILE, ATT_TILE), 0)
    qi = lax.broadcasted_iota(jnp.int32, (ATT_TILE, ATT_TILE), 1)
    for d in range(out_ref.shape[0]):
        dist = d * ATT_TILE + qi - kj
        lo, hi = d * ATT_TILE - (ATT_TILE - 1), d * ATT_TILE + (ATT_TILE - 1)
        base = sum(thr <= max(lo, 0) for thr in _T5_THRESHOLDS)
        val = jnp.full((ATT_TILE, ATT_TILE), tab_ref[base, h], F32)
        for b, thr in enumerate(_T5_THRESHOLDS, start=1):
            if max(lo, 0) < thr <= hi:
                val = jnp.where(dist >= thr, tab_ref[b, h], val)
        val = val * LOG2E
        out_ref[d] = jnp.where(dist < 0, NEG_INF, val) if lo < 0 else val


def _bias_tiles(rel_bias, n_diag):
    return pl.pallas_call(
        _bias_kernel,
        grid=(N_HEADS_TOTAL,),
        in_specs=[pl.BlockSpec(memory_space=pltpu.SMEM)],
        out_specs=pl.BlockSpec((None, n_diag, ATT_TILE, ATT_TILE), lambda h: (h, 0, 0, 0)),
        out_shape=jax.ShapeDtypeStruct((N_HEADS_TOTAL, n_diag, ATT_TILE, ATT_TILE), F32),
        compiler_params=_params("arbitrary"),
        name="bias_tiles",
    )(rel_bias)


def _in_proj_kernel(x_ref, g1_ref, w_ref, pg_ref, grp_ref, qk_ref, vt_ref):
    n_row_tiles = PROJ_ROWS // ATT_TILE
    h = []
    for t in range(n_row_tiles):
        x = x_ref[t * ATT_TILE:(t + 1) * ATT_TILE, :]
        ms = jnp.mean(x * x, axis=-1, keepdims=True)
        h.append((x * lax.rsqrt(ms + RMS_EPS) * g1_ref[...]).astype(BF16))

    cw = 2 * LANES

    def finish(c, t, p):
        rows = slice(t * ATT_TILE, (t + 1) * ATT_TILE)
        cols = slice(c * cw, (c + 1) * cw)
        if c < QK_WIDTH // cw:
            msq = jnp.dot((p * p).astype(BF16), grp_ref[...], preferred_element_type=F32)
            qk_ref[rows, cols] = (p * lax.rsqrt(msq + RMS_EPS) * pg_ref[:, cols]).astype(BF16)
        else:
            vt_ref[t, c * cw - QK_WIDTH:(c + 1) * cw - QK_WIDTH, :] = p.T.astype(BF16)

    units = [(c, t) for t in range(n_row_tiles) for c in range((QK_WIDTH + V_WIDTH) // cw)]
    pending = None
    for c, t in units:
        p = jnp.dot(h[t], w_ref[:, c * cw:(c + 1) * cw], preferred_element_type=F32)
        if pending is not None:
            finish(*pending)
        pending = (c, t, p)
    finish(*pending)


def _in_proj(x2, g1, w_perm, post_gain, grp, batch, seq):
    n = x2.shape[0]
    steps_per_seq = seq // PROJ_ROWS
    tiles_per_step = PROJ_ROWS // ATT_TILE
    return pl.pallas_call(
        _in_proj_kernel,
        grid=(n // PROJ_ROWS,),
        in_specs=[
            pl.BlockSpec((PROJ_ROWS, D_MODEL), lambda i: (i, 0)),
            pl.BlockSpec((1, D_MODEL), lambda i: (0, 0)),
            pl.BlockSpec((D_MODEL, QK_WIDTH + V_WIDTH), lambda i: (0, 0)),
            pl.BlockSpec((1, QK_WIDTH), lambda i: (0, 0)),
            pl.BlockSpec((2 * LANES, 2 * LANES), lambda i: (0, 0)),
        ],
        out_specs=[
            pl.BlockSpec((PROJ_ROWS, QK_WIDTH), lambda i: (i, 0)),
            pl.BlockSpec((None, tiles_per_step, V_WIDTH, ATT_TILE),
                         lambda i: (i // steps_per_seq, i % steps_per_seq, 0, 0)),
        ],
        out_shape=[
            jax.ShapeDtypeStruct((n, QK_WIDTH), BF16),
            jax.ShapeDtypeStruct((batch, seq // ATT_TILE, V_WIDTH, ATT_TILE), BF16),
        ],
        compiler_params=_params("arbitrary"),
        name="in_proj",
    )(x2, g1, w_perm, post_gain, grp)


def _flash_tiles(qi, n_chains, logits_fn, values_fn, bias_fn, mask_fn, s_ref, p_ref, acc_ref):
    for c in range(n_chains):
        s_ref[0, c] = logits_fn(c, qi)
    p_ref[...] = jnp.zeros_like(p_ref)
    acc_ref[...] = jnp.zeros_like(acc_ref)

    def add_values(c, kv, alpha):
        acc_ref[c] = acc_ref[c] * alpha + jnp.dot(values_fn(c, kv), p_ref[c], preferred_element_type=F32)

    def step(j, carry):
        cur = lax.rem(j, 2)
        kv = qi - j
        out = []
        for c in range(n_chains):
            add_values(c, jnp.minimum(kv + 1, qi), carry[2 * c + 1])
        for c in range(n_chains):
            s = s_ref[cur, c] + bias_fn(c, j)
            tile_max = jnp.max(s, axis=0, keepdims=True)
            shift = mask_fn(c, kv)
            if shift is not None:
                tile_max = tile_max + shift
            m_new = jnp.maximum(carry[2 * c], tile_max)
            alpha = jnp.exp2(carry[2 * c] - m_new)
            p_ref[c] = jnp.exp2(s - (m_new if shift is None else m_new - shift)).astype(BF16)
            out += [m_new, alpha]
        for c in range(n_chains):
            s_ref[1 - cur, c] = logits_fn(c, jnp.maximum(kv - 1, 0))
        return tuple(out)

    m0 = jnp.full((1, ATT_TILE), NEG_INF, F32)
    one = jnp.ones((1, ATT_TILE), F32)
    carry = lax.fori_loop(0, qi + 1, step, (m0, one) * n_chains)
    for c in range(n_chains):
        add_values(c, 0, carry[2 * c + 1])


def _bounded_tiles(n_tiles, n_chains, logits_fn, values_fn, bias_fn, keep_fn, finish_fn, p_ref, acc_ref,
                   between_fn=lambda d: None):
    for d in range(n_tiles):
        for qi in range(d, n_tiles):
            for c in range(n_chains):
                p_ref[qi, c] = jnp.exp2(logits_fn(c, qi, qi - d) + bias_fn(c, d)).astype(BF16)
        between_fn(d)
        for qi in range(d, n_tiles):
            kv = qi - d
            for c in range(n_chains):
                pv = jnp.dot(values_fn(c, kv), p_ref[qi, c], preferred_element_type=F32)
                keep = keep_fn(c, qi, kv)
                if keep is not None:
                    pv = pv * keep
                acc_ref[qi, c] = pv if d == 0 else acc_ref[qi, c] + pv
        finish_fn(d)


def _logits_bounded(q_gain, k_gain, bias_cols):
    bound = (HEAD_DIM * jnp.max(jnp.abs(q_gain)) * jnp.max(jnp.abs(k_gain)) * ROUNDING_MARGIN
             + jnp.max(jnp.abs(bias_cols)) * LOG2E)
    return (bound <= LOGIT_LIMIT).astype(jnp.int32).reshape(1)


def _with_ones_rows(vt):
    return jnp.concatenate([vt, jnp.ones((ONES_ROWS, vt.shape[1]), vt.dtype)], axis=0)


def _attention_scratch(n_tiles, value_rows, n_chains=2):
    return [pltpu.VMEM((n_tiles, n_chains, ATT_TILE, LANES), BF16),
            pltpu.VMEM((2, n_chains, ATT_TILE, ATT_TILE), F32),
            pltpu.VMEM((n_tiles, n_chains, ATT_TILE, ATT_TILE), BF16),
            pltpu.VMEM((n_tiles, n_chains, value_rows + ONES_ROWS, ATT_TILE), F32)]


def _half_lane_split(q):
    lane = lax.broadcasted_iota(jnp.int32, q.shape, 1)
    zero = jnp.zeros_like(q)
    return jnp.where(lane < HEAD_DIM, q, zero), jnp.where(lane >= HEAD_DIM, q, zero)


def _diff_kernel(bounded_ref, lq1_ref, lk1_ref, lq2_ref, lk2_ref, subg_ref, q_ref, k_ref, vt_ref, bias_ref, o_ref,
                 qm_ref, s_ref, p_ref, acc_ref):
    n_tiles = q_ref.shape[0]
    chains = dict(n_chains=2, values_fn=lambda c, kv: _with_ones_rows(vt_ref[kv]))
    width = 2 * HEAD_DIM

    def prepare():
        for qi in range(n_tiles):
            qm_ref[qi, 0], qm_ref[qi, 1] = _half_lane_split(q_ref[qi])

    def finish(qi):
        lam = (jnp.exp(jnp.sum(lq1_ref[...] * lk1_ref[...], keepdims=True))
               - jnp.exp(jnp.sum(lq2_ref[...] * lk2_ref[...], keepdims=True)) + LAMBDA_INIT)
        a1, a2 = acc_ref[qi, 0], acc_ref[qi, 1]
        o = a1[:width] / a1[width:width + 1] - lam * (a2[:width] / a2[width:width + 1])
        ms = jnp.mean(o * o, axis=0, keepdims=True)
        o = o * lax.rsqrt(ms + RMS_EPS) * subg_ref[...] * (1.0 - LAMBDA_INIT)
        o_ref[qi] = o.T.astype(BF16)

    bounded = bounded_ref[0] > 0

    @pl.when(bounded)
    def _():
        prepare()
        _bounded_tiles(
            n_tiles,
            logits_fn=lambda c, qi, kv: lax.dot_general(k_ref[kv], qm_ref[qi, c], _NT, preferred_element_type=F32),
            bias_fn=lambda c, d: bias_ref[d],
            keep_fn=lambda c, qi, kv: None,
            finish_fn=finish,
            p_ref=p_ref, acc_ref=acc_ref, **chains)

    @pl.when(jnp.logical_not(bounded))
    def _():
        prepare()

        def q_tile(qi, carry):
            _flash_tiles(
                qi,
                logits_fn=lambda c, kv: lax.dot_general(k_ref[kv], qm_ref[qi, c], _NT, preferred_element_type=F32),
                bias_fn=lambda c, j: bias_ref[j],
                mask_fn=lambda c, kv: None,
                s_ref=s_ref, p_ref=p_ref.at[0], acc_ref=acc_ref.at[qi], **chains)
            return carry
        lax.fori_loop(0, n_tiles, q_tile, 0)
        for qi in range(n_tiles):
            finish(qi)


def _diff_attention(bounded, lam_vecs, sub_g_col, qk4, vt4, bias):
    batch, n_tiles = qk4.shape[0], qk4.shape[1]
    k_col0 = DIFF_WIDTH // LANES
    vec = pl.BlockSpec((1, HEAD_DIM), lambda h, b: (0, 0))
    return pl.pallas_call(
        _diff_kernel,
        grid=(HEADS_DIFF, batch),
        in_specs=[
            pl.BlockSpec(memory_space=pltpu.SMEM),
            vec, vec, vec, vec,
            pl.BlockSpec((2 * HEAD_DIM, 1), lambda h, b: (0, 0)),
            pl.BlockSpec((None, n_tiles, ATT_TILE, LANES), lambda h, b: (b, 0, 0, h)),
            pl.BlockSpec((None, n_tiles, ATT_TILE, LANES), lambda h, b: (b, 0, 0, k_col0 + h)),
            pl.BlockSpec((None, n_tiles, LANES, ATT_TILE), lambda h, b: (b, 0, h, 0)),
            pl.BlockSpec((None, n_tiles, ATT_TILE, ATT_TILE), lambda h, b: (h, 0, 0, 0)),
        ],
        out_specs=pl.BlockSpec((None, n_tiles, ATT_TILE, LANES), lambda h, b: (b, 0, 0, h)),
        out_shape=jax.ShapeDtypeStruct((batch, n_tiles, ATT_TILE, DIFF_WIDTH), BF16),
        scratch_shapes=_attention_scratch(n_tiles, 2 * HEAD_DIM),
        compiler_params=_params("arbitrary", "arbitrary"),
        name="diff_attention",
    )(bounded, *lam_vecs, sub_g_col, qk4, qk4, vt4, bias)


def _split3(v):
    hi = v.astype(BF16)
    r1 = v - hi.astype(F32)
    mid = r1.astype(BF16)
    lo = (r1 - mid.astype(F32)).astype(BF16)
    return hi, mid, lo


def _block_mask(gate, own):
    row = lax.broadcasted_iota(jnp.int32, gate.shape, 0)
    rank = jnp.zeros(gate.shape, jnp.int32)
    for m in range(own):
        gm = gate[m:m + 1, :]
        beats = (gm > gate) | ((gm == gate) & (row > m))
        rank = rank + jnp.where(beats, 1, 0)
    keep = ((rank < MOBA_TOPK) & (row < own)) | (row == own)
    return jnp.where(keep, 0.0, NEG_INF).astype(F32)


def _moba_kernel(bounded_ref, q_ref, k_ref, vt_ref, bias_ref, o_ref, kmean_ref, mask_ref, qm_ref, s_ref, p_ref,
                 acc_ref):
    n_tiles = q_ref.shape[0]

    def split_queries():
        for qi in range(n_tiles):
            qm_ref[qi, 0], qm_ref[qi, 1] = _half_lane_split(q_ref[qi])

    def block_masks(q_tiles):
        for n in range(n_tiles):
            kmean_ref[n:n + 1, :] = jnp.mean(k_ref[n].astype(F32), axis=0, keepdims=True)
        per_head = zip(*[_half_lane_split(term) for term in _split3(kmean_ref[...])])
        gate_lhs = jnp.concatenate([term for head_terms in per_head for term in head_terms], axis=0)
        n_terms = gate_lhs.shape[0] // (2 * n_tiles)
        for qi in q_tiles:
            terms = lax.dot_general(gate_lhs, q_ref[qi], _NT, preferred_element_type=F32)
            for half in range(2):
                rows = [terms[(half * n_terms + t) * n_tiles:(half * n_terms + t + 1) * n_tiles]
                        for t in range(n_terms)]
                mask_ref[qi, half] = _block_mask(sum(rows[1:], rows[0]), qi)

    def keep_row(c, qi, kv):
        if qi <= MOBA_TOPK or kv == qi:
            return None
        return jnp.where(mask_ref[qi, c, kv:kv + 1, :] < -1.0, 0.0, 1.0)

    def finish(qi):
        halves = [acc_ref[qi, half] for half in range(2)]
        o = jnp.concatenate([a[:HEAD_DIM] / a[HEAD_DIM:HEAD_DIM + 1] for a in halves], axis=0)
        o_ref[qi] = o.T.astype(BF16)

    chains = dict(
        n_chains=2,
        values_fn=lambda c, kv: _with_ones_rows(vt_ref[kv, c * HEAD_DIM:(c + 1) * HEAD_DIM, :]))
    bounded = bounded_ref[0] > 0

    @pl.when(bounded)
    def _():
        split_queries()
        _bounded_tiles(
            n_tiles,
            logits_fn=lambda c, qi, kv: lax.dot_general(k_ref[kv], qm_ref[qi, c], _NT, preferred_element_type=F32),
            bias_fn=lambda c, d: bias_ref[c, d],
            keep_fn=keep_row,
            between_fn=lambda d: block_masks(range(MOBA_TOPK + 1, n_tiles)) if d == 0 else None,
            finish_fn=finish,
            p_ref=p_ref, acc_ref=acc_ref, **chains)

    @pl.when(jnp.logical_not(bounded))
    def _():
        split_queries()
        block_masks(range(n_tiles))

        def q_tile(qi, carry):
            _flash_tiles(
                qi,
                logits_fn=lambda c, kv: lax.dot_general(k_ref[kv], qm_ref[qi, c], _NT, preferred_element_type=F32),
                bias_fn=lambda c, j: bias_ref[c, j],
                mask_fn=lambda c, kv: mask_ref[qi, c, pl.ds(kv, 1), :],
                s_ref=s_ref, p_ref=p_ref.at[0], acc_ref=acc_ref.at[qi], **chains)
            return carry
        lax.fori_loop(0, n_tiles, q_tile, 0)
        for qi in range(n_tiles):
            finish(qi)


def _moba_attention(bounded, qk4, vt4, bias_pairs):
    batch, n_tiles = qk4.shape[0], qk4.shape[1]
    q_col0 = 2 * DIFF_WIDTH // LANES
    k_col0 = q_col0 + MOBA_WIDTH // LANES
    v_row0 = DIFF_WIDTH // LANES
    pair0 = HEADS_DIFF // 2
    return pl.pallas_call(
        _moba_kernel,
        grid=(HEADS_MOBA // 2, batch),
        in_specs=[
            pl.BlockSpec(memory_space=pltpu.SMEM),
            pl.BlockSpec((None, n_tiles, ATT_TILE, LANES), lambda h, b: (b, 0, 0, q_col0 + h)),
            pl.BlockSpec((None, n_tiles, ATT_TILE, LANES), lambda h, b: (b, 0, 0, k_col0 + h)),
            pl.BlockSpec((None, n_tiles, LANES, ATT_TILE), lambda h, b: (b, 0, v_row0 + h, 0)),
            pl.BlockSpec((None, 2, n_tiles, ATT_TILE, ATT_TILE), lambda h, b: (pair0 + h, 0, 0, 0, 0)),
        ],
        out_specs=pl.BlockSpec((None, n_tiles, ATT_TILE, LANES), lambda h, b: (b, 0, 0, h)),
        out_shape=jax.ShapeDtypeStruct((batch, n_tiles, ATT_TILE, MOBA_WIDTH), BF16),
        scratch_shapes=[
            pltpu.VMEM((n_tiles, LANES), F32),
            pltpu.VMEM((n_tiles, 2, n_tiles, ATT_TILE), F32),
        ] + _attention_scratch(n_tiles, HEAD_DIM),
        compiler_params=_params("arbitrary", "arbitrary"),
        name="moba_attention",
    )(bounded, qk4, qk4, vt4, bias_pairs)


def _load_token_rows(ref, n_rows, lead=(), first=0):
    chunks = [ref[lead + (pl.ds(first * ROW_TILE + c, n_rows, stride=ROW_TILE), slice(None))]
              for c in range(ROW_TILE)]
    return jnp.concatenate(chunks, axis=1)


def _store_token_rows(ref, value, first=0):
    n_rows = value.shape[0]
    for c in range(ROW_TILE):
        ref[pl.ds(first * ROW_TILE + c, n_rows, stride=ROW_TILE), :] = value[:, c * LANES:(c + 1) * LANES]


def _token_tile(ref, t):
    return ref.at[pl.ds(pl.multiple_of(t * ROW_TILE, ROW_TILE), ROW_TILE)]


def _first_argmax(v):
    top = jnp.max(v, axis=0, keepdims=True)
    row = lax.broadcasted_iota(jnp.int32, v.shape, 0)
    idx = jnp.min(jnp.where(v == top, row, v.shape[0]), axis=0, keepdims=True)
    return top, idx


def _out_proj_kernel(yd_ref, ym_ref, x_ref, wd_ref, wm_ref, g2_ref, wr_ref, tri_ref,
                     x1_ref, ri_ref, rf_ref, cnt_ref, run_ref):
    @pl.when(lax.rem(pl.program_id(0), pl.num_programs(0) // MOE_TOKEN_PARTS) == 0)
    def _():
        run_ref[...] = jnp.zeros_like(run_ref)

    def project(t):
        rows = slice(t * ROUTE_TILE, (t + 1) * ROUTE_TILE)
        x1 = (x_ref[rows, :] + jnp.dot(yd_ref[rows, :], wd_ref[...], preferred_element_type=F32)
              + jnp.dot(ym_ref[rows, :], wm_ref[...], preferred_element_type=F32))
        _store_token_rows(x1_ref, x1, first=t * ROUTE_TILE)
        ms = jnp.mean(x1 * x1, axis=-1, keepdims=True)
        return (x1 * lax.rsqrt(ms + RMS_EPS) * g2_ref[...]).astype(BF16)

    def pick(h2):
        lg = lax.dot_general(wr_ref[...], h2, _NT, preferred_element_type=F32)
        g_logits = lg[N_EXPERTS:N_EXPERTS + N_GROUPS, :]
        g_top, g_idx = _first_argmax(g_logits)
        p_group = 1.0 / jnp.sum(jnp.exp(g_logits - g_top), axis=0, keepdims=True)
        e_logits = lg[0:EXPERTS_PER_GROUP, :]
        for g in range(1, N_GROUPS):
            e_logits = jnp.where(g_idx == g, lg[g * EXPERTS_PER_GROUP:(g + 1) * EXPERTS_PER_GROUP, :], e_logits)
        v1, i1 = _first_argmax(e_logits)
        row = lax.broadcasted_iota(jnp.int32, e_logits.shape, 0)
        v2, i2 = _first_argmax(jnp.where(row == i1, -jnp.inf, e_logits))
        ratio = jnp.exp(v2 - v1)
        w1 = p_group / (1.0 + ratio)
        w2 = p_group * ratio / (1.0 + ratio)
        e1 = g_idx * EXPERTS_PER_GROUP + i1
        e2 = g_idx * EXPERTS_PER_GROUP + i2
        erow = lax.broadcasted_iota(jnp.int32, (N_EXPERTS, ROUTE_TILE), 0)
        hit1 = erow == e1
        hit2 = erow == e2
        onehot = jnp.where(hit1 | hit2, 1.0, 0.0).astype(F32)
        return e1, e2, w1, w2, hit1, hit2, onehot

    def place(t, e1, e2, w1, w2, hit1, hit2, onehot):
        before = jnp.dot(onehot.astype(BF16), tri_ref[...], preferred_element_type=F32) + run_ref[...]
        r1 = jnp.sum(jnp.where(hit1, before, 0.0), axis=0, keepdims=True).astype(jnp.int32)
        r2 = jnp.sum(jnp.where(hit2, before, 0.0), axis=0, keepdims=True).astype(jnp.int32)
        run_ref[...] = run_ref[...] + jnp.sum(onehot, axis=1, keepdims=True)
        lanes = slice(t * ROUTE_TILE, (t + 1) * ROUTE_TILE)
        out_row = lax.broadcasted_iota(jnp.int32, (SUBLANES, ROUTE_TILE), 0)
        ri_ref[:, lanes] = jnp.where(out_row == 0, e1, jnp.where(out_row == 1, e2,
                                     jnp.where(out_row == 2, r1, jnp.where(out_row == 3, r2, 0))))
        rf_ref[:, lanes] = jnp.where(out_row == 0, w1, jnp.where(out_row == 1, w2, 0.0))

    n_tiles = OUT_ROWS // ROUTE_TILE
    projected, picked = {}, {}
    for t in range(n_tiles + 2):
        if t < n_tiles:
            projected[t] = project(t)
        if 0 <= t - 1 < n_tiles:
            picked[t - 1] = pick(projected.pop(t - 1))
        if 0 <= t - 2 < n_tiles:
            place(t - 2, *picked.pop(t - 2))
    cnt_ref[...] = jnp.broadcast_to(run_ref[...], cnt_ref.shape).astype(jnp.int32)


def _out_proj(y_d, y_m, x2, wo_d, wo_m, g2, w_router, tri):
    n = x2.shape[0]
    const = lambda i: (0, 0)
    steps_per_part = n // OUT_ROWS // MOE_TOKEN_PARTS
    return pl.pallas_call(
        _out_proj_kernel,
        grid=(n // OUT_ROWS,),
        in_specs=[
            pl.BlockSpec((OUT_ROWS, DIFF_WIDTH), lambda i: (i, 0)),
            pl.BlockSpec((OUT_ROWS, MOBA_WIDTH), lambda i: (i, 0)),
            pl.BlockSpec((OUT_ROWS, D_MODEL), lambda i: (i, 0)),
            pl.BlockSpec((DIFF_WIDTH, D_MODEL), const),
            pl.BlockSpec((MOBA_WIDTH, D_MODEL), const),
            pl.BlockSpec((1, D_MODEL), const),
            pl.BlockSpec((ROUTER_ROWS, D_MODEL), const),
            pl.BlockSpec((ROUTE_TILE, ROUTE_TILE), const),
        ],
        out_specs=[
            pl.BlockSpec((OUT_ROWS * ROW_TILE, LANES), lambda i: (i, 0)),
            pl.BlockSpec((SUBLANES, OUT_ROWS), lambda i: (0, i)),
            pl.BlockSpec((SUBLANES, OUT_ROWS), lambda i: (0, i)),
            pl.BlockSpec((None, N_EXPERTS, LANES), lambda i: (i // steps_per_part, 0, 0)),
        ],
        out_shape=[
            jax.ShapeDtypeStruct((n * ROW_TILE, LANES), F32),
            jax.ShapeDtypeStruct((SUBLANES, n), jnp.int32),
            jax.ShapeDtypeStruct((SUBLANES, n), F32),
            jax.ShapeDtypeStruct((MOE_TOKEN_PARTS, N_EXPERTS, LANES), jnp.int32),
        ],
        scratch_shapes=[pltpu.VMEM((N_EXPERTS, 1), F32)],
        compiler_params=_params("arbitrary"),
        name="out_proj_route",
    )(y_d, y_m, x2, wo_d, wo_m, g2, w_router, tri)


def _dispatch_step(i, n_steps, first_tile, zchunk_ref, dest_ref, x1_hbm, xs_hbm, zero_ref, stage_ref, zero_sem,
                   load_sem, row_sem):
    last = n_steps - 1
    slot = lax.rem(i, 2)
    chunk_rows = MOE_CHUNK * ROW_TILE
    tile_tokens = dest_ref.shape[2]
    tile_rows = tile_tokens * ROW_TILE

    def load(t, s):
        start = pl.multiple_of((first_tile + t) * tile_rows, tile_rows)
        return pltpu.make_async_copy(x1_hbm.at[pl.ds(start, tile_rows)], stage_ref.at[s], load_sem.at[s])

    def wait_rows(s):
        for _ in range(EXPERTS_PER_TOKEN):
            pltpu.make_async_copy(stage_ref.at[s], xs_hbm.at[pl.ds(0, tile_rows)], row_sem.at[s]).wait()

    def zero_copy(c):
        start = pl.multiple_of(c * chunk_rows, chunk_rows)
        return pltpu.make_async_copy(zero_ref, xs_hbm.at[pl.ds(start, chunk_rows)], zero_sem)

    @pl.when(i == 0)
    def _():
        zero_ref[...] = jnp.zeros_like(zero_ref)

        def start_one(c, carry):
            @pl.when(zchunk_ref[c] > 0)
            def _():
                zero_copy(c).start()
            return carry

        def wait_one(c, carry):
            @pl.when(zchunk_ref[c] > 0)
            def _():
                zero_copy(c).wait()
            return carry

        lax.fori_loop(0, zchunk_ref.shape[0], start_one, 0)
        lax.fori_loop(0, zchunk_ref.shape[0], wait_one, 0)
        load(0, 0).start()

    @pl.when(i > 0)
    def _():
        wait_rows(1 - slot)

    @pl.when(i < last)
    def _():
        load(i + 1, 1 - slot).start()

    load(i, slot).wait()

    def send_row(r, carry):
        src = _token_tile(stage_ref.at[slot], r)
        for k in range(EXPERTS_PER_TOKEN):
            pltpu.make_async_copy(src, _token_tile(xs_hbm, dest_ref[0, k, r]), row_sem.at[slot]).start(priority=k)
        return carry

    lax.fori_loop(0, tile_tokens, send_row, 0, unroll=8)

    @pl.when(i == last)
    def _():
        wait_rows(slot)


def _dispatch_scratch(tile_tokens):
    return [pltpu.VMEM((MOE_CHUNK * ROW_TILE, LANES), F32),
            pltpu.VMEM((2, tile_tokens * ROW_TILE, LANES), F32),
            pltpu.SemaphoreType.DMA(()),
            pltpu.SemaphoreType.DMA((2,)),
            pltpu.SemaphoreType.DMA((2,))]


def _dispatch_kernel(first_tile, zchunk_ref, dest_ref, x1_hbm, xs_hbm, *scratch):
    _dispatch_step(pl.program_id(0), pl.num_programs(0), first_tile, zchunk_ref, dest_ref, x1_hbm, xs_hbm, *scratch)


def _dispatch(part, zchunk, dest3, x1, n_slots):
    tile_tokens = dest3.shape[2]
    tiles = dest3.shape[0] // MOE_TOKEN_PARTS
    first_tile = part * tiles
    grid_spec = pltpu.PrefetchScalarGridSpec(
        num_scalar_prefetch=1,
        grid=(tiles,),
        in_specs=[
            pl.BlockSpec((1, EXPERTS_PER_TOKEN, tile_tokens), lambda i, zc: (first_tile + i, 0, 0),
                         memory_space=pltpu.SMEM),
            pl.BlockSpec(memory_space=pl.ANY),
        ],
        out_specs=pl.BlockSpec(memory_space=pl.ANY),
        scratch_shapes=_dispatch_scratch(tile_tokens),
    )
    return pl.pallas_call(
        functools.partial(_dispatch_kernel, first_tile),
        grid_spec=grid_spec,
        out_shape=jax.ShapeDtypeStruct((n_slots * ROW_TILE, LANES), F32),
        compiler_params=_params("arbitrary"),
        name="moe_dispatch",
    )(zchunk, dest3, x1)


def _expert_step(c, na_ref, xs_ref, g2_ref, wg_ref, wu_ref, wd_ref, o_ref):
    active = c < na_ref[0]

    @pl.when(jnp.logical_not(active))
    def _():
        o_ref[...] = jnp.zeros_like(o_ref)

    @pl.when(active)
    def _():
        part_rows = MOE_CHUNK // EXPERT_ROW_PARTS

        def gate_up(r):
            x = _load_token_rows(xs_ref, part_rows, first=r * part_rows)
            ms = jnp.mean(x * x, axis=-1, keepdims=True)
            h = (x * lax.rsqrt(ms + RMS_EPS) * g2_ref[...]).astype(BF16)
            return (jnp.dot(h, wg_ref[...], preferred_element_type=F32),
                    jnp.dot(h, wu_ref[...], preferred_element_type=F32))

        def down(r, gate, up):
            hid = (gate * jax.nn.sigmoid(gate) * up).astype(BF16)
            _store_token_rows(o_ref, jnp.dot(hid, wd_ref[...], preferred_element_type=F32), first=r * part_rows)

        pending = None
        for r in range(EXPERT_ROW_PARTS):
            current = (r,) + gate_up(r)
            if pending is not None:
                down(*pending)
            pending = current
        down(*pending)


def _expert_kernel(ce_ref, na_ref, *refs):
    _expert_step(pl.program_id(0), na_ref, *refs)


def _expert_specs():
    rows = lambda c, ce, na, *_: (jnp.minimum(c, na[0] - 1), 0)
    expert = lambda c, ce, na, *_: (ce[c], 0, 0)
    in_specs = [
        pl.BlockSpec((MOE_CHUNK * ROW_TILE, LANES), rows),
        pl.BlockSpec((1, D_MODEL), lambda c, *_: (0, 0)),
        pl.BlockSpec((None, D_MODEL, EXPERT_HIDDEN), expert),
        pl.BlockSpec((None, D_MODEL, EXPERT_HIDDEN), expert),
        pl.BlockSpec((None, EXPERT_HIDDEN, D_MODEL), expert),
    ]
    return in_specs, pl.BlockSpec((MOE_CHUNK * ROW_TILE, LANES), lambda c, *_: (c, 0))


def _experts(chunk_e, n_active, xs, g2, wg, wu, wd):
    n_slots = xs.shape[0] // ROW_TILE
    in_specs, out_spec = _expert_specs()
    grid_spec = pltpu.PrefetchScalarGridSpec(
        num_scalar_prefetch=2, grid=(n_slots // MOE_CHUNK,), in_specs=in_specs, out_specs=out_spec)
    return pl.pallas_call(
        _expert_kernel,
        grid_spec=grid_spec,
        out_shape=jax.ShapeDtypeStruct((n_slots * ROW_TILE, LANES), F32),
        compiler_params=_params("arbitrary"),
        name="moe_experts",
    )(chunk_e, n_active, xs, g2, wg, wu, wd)


def _experts_dispatch_kernel(first_tile, n_tiles, ce_ref, na_ref, zchunk_ref,
                             xs_ref, g2_ref, wg_ref, wu_ref, wd_ref, dest_ref, x1_hbm,
                             o_ref, xs_next_hbm, *scratch):
    i = pl.program_id(0)

    @pl.when(i < n_tiles)
    def _():
        _dispatch_step(i, n_tiles, first_tile, zchunk_ref, dest_ref, x1_hbm, xs_next_hbm, *scratch)

    _expert_step(i, na_ref, xs_ref, g2_ref, wg_ref, wu_ref, wd_ref, o_ref)


def _experts_dispatch(chunk_e, n_active, zchunk_next, xs, g2, wg, wu, wd, next_part, dest3, x1):
    n_slots = xs.shape[0] // ROW_TILE
    n_chunks = n_slots // MOE_CHUNK
    tile_tokens = dest3.shape[2]
    tiles = dest3.shape[0] // MOE_TOKEN_PARTS
    first_tile = next_part * tiles
    assert n_chunks >= tiles
    in_specs, out_spec = _expert_specs()
    in_specs += [
        pl.BlockSpec((1, EXPERTS_PER_TOKEN, tile_tokens),
                     lambda c, *_: (first_tile + jnp.minimum(c, tiles - 1), 0, 0), memory_space=pltpu.SMEM),
        pl.BlockSpec(memory_space=pl.ANY),
    ]
    grid_spec = pltpu.PrefetchScalarGridSpec(
        num_scalar_prefetch=3, grid=(n_chunks,), in_specs=in_specs,
        out_specs=[out_spec, pl.BlockSpec(memory_space=pl.ANY)],
        scratch_shapes=_dispatch_scratch(tile_tokens))
    slots = jax.ShapeDtypeStruct((n_slots * ROW_TILE, LANES), F32)
    return pl.pallas_call(
        functools.partial(_experts_dispatch_kernel, first_tile, tiles),
        grid_spec=grid_spec,
        out_shape=[slots, slots],
        compiler_params=_params("arbitrary"),
        name="moe_experts_dispatch",
    )(chunk_e, n_active, zchunk_next, xs, g2, wg, wu, wd, dest3, x1)


def _combine_kernel(dest_ref, dest_next_ref, w_ref, x1_ref, *rest):
    outs_hbm, (y_ref, buf_ref, sem_ref) = rest[:MOE_TOKEN_PARTS], rest[MOE_TOKEN_PARTS:]
    i = pl.program_id(0)
    slot = lax.rem(i, 2)
    tiles_per_part = pl.num_programs(0) // MOE_TOKEN_PARTS

    def fetch(tile, dref, s):
        for part, src_hbm in enumerate(outs_hbm):
            @pl.when(tile // tiles_per_part == part)
            def _():
                def fetch_row(r, carry):
                    for k in range(EXPERTS_PER_TOKEN):
                        pltpu.make_async_copy(_token_tile(src_hbm, dref[0, k, r]),
                                              _token_tile(buf_ref.at[s, k], r), sem_ref.at[s]).start(priority=k)
                    return carry
                lax.fori_loop(0, MOVE_ROWS, fetch_row, 0, unroll=8)

    @pl.when(i == 0)
    def _():
        fetch(i, dest_ref, 0)

    @pl.when(i + 1 < pl.num_programs(0))
    def _():
        fetch(i + 1, dest_next_ref, 1 - slot)

    for k in range(EXPERTS_PER_TOKEN):
        pltpu.make_async_copy(outs_hbm[0].at[pl.ds(0, MOVE_ROWS * ROW_TILE)], buf_ref.at[slot, k],
                              sem_ref.at[slot]).wait()
    w = w_ref[...]
    y_ref[...] = (_load_token_rows(x1_ref, MOVE_ROWS)
                  + w[:, 0:1] * _load_token_rows(buf_ref, MOVE_ROWS, (slot, 0))
                  + w[:, 1:2] * _load_token_rows(buf_ref, MOVE_ROWS, (slot, 1)))


def _combine(dest3, wts, x1, outs_parts):
    n = x1.shape[0] // ROW_TILE
    steps = n // MOVE_ROWS
    dest_block = (1, EXPERTS_PER_TOKEN, MOVE_ROWS)
    return pl.pallas_call(
        _combine_kernel,
        grid=(steps,),
        in_specs=[
            pl.BlockSpec(dest_block, lambda i: (i, 0, 0), memory_space=pltpu.SMEM),
            pl.BlockSpec(dest_block, lambda i: (jnp.minimum(i + 1, steps - 1), 0, 0), memory_space=pltpu.SMEM),
            pl.BlockSpec((MOVE_ROWS, EXPERTS_PER_TOKEN), lambda i: (i, 0)),
            pl.BlockSpec((MOVE_ROWS * ROW_TILE, LANES), lambda i: (i, 0)),
        ] + [pl.BlockSpec(memory_space=pl.ANY)] * MOE_TOKEN_PARTS,
        out_specs=pl.BlockSpec((MOVE_ROWS, D_MODEL), lambda i: (i, 0)),
        out_shape=jax.ShapeDtypeStruct((n, D_MODEL), F32),
        scratch_shapes=[
            pltpu.VMEM((2, EXPERTS_PER_TOKEN, MOVE_ROWS * ROW_TILE, LANES), F32),
            pltpu.SemaphoreType.DMA((2,)),
        ],
        compiler_params=_params("arbitrary"),
        name="moe_combine",
    )(dest3, dest3, wts, x1, *outs_parts)


def _layer(x, norm1_g, w_in, diff_q_g, diff_k_g, lambda_q1, lambda_k1, lambda_q2, lambda_k2, diff_sub_g,
           moba_q_g, moba_k_g, rel_bias, w_out, norm2_g, router_group, router_expert, w_gate, w_up, w_down):
    batch, seq, _ = x.shape
    n = batch * seq
    n_tiles = seq // ATT_TILE
    scale = HEAD_DIM ** -0.5 * LOG2E

    d3 = 3 * DIFF_WIDTH
    w_perm = jnp.concatenate([w_in[:, :2 * DIFF_WIDTH], w_in[:, d3:d3 + 2 * MOBA_WIDTH],
                              w_in[:, 2 * DIFF_WIDTH:d3], w_in[:, d3 + 2 * MOBA_WIDTH:]], axis=1).astype(BF16)
    reps_d, reps_m = DIFF_WIDTH // HEAD_DIM, MOBA_WIDTH // HEAD_DIM
    post_gain = jnp.concatenate([jnp.tile(diff_q_g * scale, reps_d), jnp.tile(diff_k_g, reps_d),
                                 jnp.tile(moba_q_g * scale, reps_m), jnp.tile(moba_k_g, reps_m)])[None, :]
    head_of = np.arange(2 * LANES) // HEAD_DIM
    grp = jnp.asarray((head_of[:, None] == head_of[None, :]) / HEAD_DIM, BF16)

    bias = _bias_tiles(rel_bias, n_tiles)
    qk, vt4 = _in_proj(x.reshape(n, D_MODEL), norm1_g[None, :], w_perm, post_gain, grp, batch, seq)
    qk4 = qk.reshape(batch, n_tiles, ATT_TILE, QK_WIDTH)
    lam_vecs = [v[None, :] for v in (lambda_q1, lambda_k1, lambda_q2, lambda_k2)]
    y_d = _diff_attention(_logits_bounded(diff_q_g * scale, diff_k_g, rel_bias[:, :HEADS_DIFF]),
                          lam_vecs, diff_sub_g[:, None], qk4, vt4, bias)
    y_m = _moba_attention(_logits_bounded(moba_q_g * scale, moba_k_g, rel_bias[:, HEADS_DIFF:]),
                          qk4, vt4, bias.reshape(N_HEADS_TOTAL // 2, 2, n_tiles, ATT_TILE, ATT_TILE))

    w_router = jnp.concatenate([
        router_expert.transpose(0, 2, 1).reshape(N_EXPERTS, D_MODEL), router_group.T,
        jnp.zeros((ROUTER_ROWS - N_EXPERTS - N_GROUPS, D_MODEL), F32)], axis=0).astype(BF16)
    tri = jnp.asarray(np.triu(np.ones((ROUTE_TILE, ROUTE_TILE)), k=1), BF16)
    wo = w_out.astype(BF16)
    x1, route_i, route_f, cnt = _out_proj(y_d.reshape(n, DIFF_WIDTH), y_m.reshape(n, MOBA_WIDTH),
                                          x.reshape(n, D_MODEL), wo[:DIFF_WIDTH], wo[DIFF_WIDTH:],
                                          norm2_g[None, :], w_router, tri)

    parts = MOE_TOKEN_PARTS
    counts = cnt[:, :, 0]
    pad_counts = ((counts + MOE_CHUNK - 1) // MOE_CHUNK) * MOE_CHUNK
    pad_end = jnp.cumsum(pad_counts, axis=1)
    pad_start = pad_end - pad_counts
    experts = jnp.arange(N_EXPERTS, dtype=jnp.int32)
    picked = route_i[0:2].reshape(EXPERTS_PER_TOKEN, parts, n // parts)
    start_of = jnp.sum(jnp.where(picked[..., None] == experts, pad_start[None, :, None, :], 0), axis=-1)
    dest = start_of.reshape(EXPERTS_PER_TOKEN, n) + route_i[2:4]
    by_tile = lambda rows: dest.reshape(EXPERTS_PER_TOKEN, n // rows, rows).transpose(1, 0, 2)
    dest3, dest3_fused = by_tile(MOVE_ROWS), by_tile(FUSED_MOVE_ROWS)
    n_slots = n // parts * EXPERTS_PER_TOKEN + N_EXPERTS * MOE_CHUNK
    n_chunks = n_slots // MOE_CHUNK
    chunk_id = jnp.arange(n_chunks, dtype=jnp.int32)
    chunk_e = jnp.minimum(jnp.sum((pad_end[:, None, :] <= chunk_id[None, :, None] * MOE_CHUNK).astype(jnp.int32),
                                  axis=2), N_EXPERTS - 1)
    n_active = (pad_end[:, -1:] // MOE_CHUNK).astype(jnp.int32)
    next_e = jnp.concatenate([chunk_e[:, 1:], jnp.full((parts, 1), N_EXPERTS, jnp.int32)], axis=1)
    zchunk = ((chunk_id[None, :] >= n_active - 1) | (chunk_e != next_e)).astype(jnp.int32)

    weights = (norm2_g[None, :], w_gate.astype(BF16), w_up.astype(BF16), w_down.astype(BF16))
    xs = _dispatch(0, zchunk[0], dest3, x1, n_slots)
    outs = []
    for p in range(parts):
        if p + 1 < parts:
            out_p, xs_next = _experts_dispatch(chunk_e[p], n_active[p], zchunk[p + 1], xs, *weights, p + 1,
                                               dest3_fused, x1)
        else:
            out_p, xs_next = _experts(chunk_e[p], n_active[p], xs, *weights), None
        outs.append(out_p)
        xs = xs_next
    y = _combine(dest3, route_f[0:2].T, x1, outs)
    return y.reshape(batch, seq, D_MODEL)


def kernel(x, norm1_g, w_in, diff_q_g, diff_k_g, lambda_q1, lambda_k1, lambda_q2, lambda_k2, diff_sub_g,
           moba_q_g, moba_k_g, rel_bias, w_out, norm2_g, router_group, router_expert, w_gate, w_up, w_down):
    assert x.shape[1] % PROJ_ROWS == 0 and x.shape[2] == D_MODEL and norm1_g.shape[0] == 1
    return _layer(x, norm1_g[0], w_in[0], diff_q_g[0], diff_k_g[0], lambda_q1[0], lambda_k1[0], lambda_q2[0],
                  lambda_k2[0], diff_sub_g[0], moba_q_g[0], moba_k_g[0], rel_bias, w_out[0], norm2_g[0],
                  router_group[0], router_expert[0], w_gate[0], w_up[0], w_down[0])
```

```python
import functools
import math

import numpy as np
import jax
import jax.numpy as jnp
from jax import lax
from jax.experimental import pallas as pl
from jax.experimental.pallas import tpu as pltpu

D_MODEL = 1024
HEAD_DIM = 64
HEADS_DIFF = 4
HEADS_MOBA = 8
N_HEADS_TOTAL = HEADS_DIFF + HEADS_MOBA
DIFF_WIDTH = HEADS_DIFF * 2 * HEAD_DIM
MOBA_WIDTH = HEADS_MOBA * HEAD_DIM
QK_WIDTH = 2 * DIFF_WIDTH + 2 * MOBA_WIDTH
V_WIDTH = DIFF_WIDTH + MOBA_WIDTH
MOBA_TOPK = 3
NUM_BUCKETS = 32
MAX_DISTANCE = 2048
N_GROUPS = 4
EXPERTS_PER_GROUP = 8
N_EXPERTS = N_GROUPS * EXPERTS_PER_GROUP
EXPERTS_PER_TOKEN = 2
EXPERT_HIDDEN = 256
RMS_EPS = 1e-6
NEG_INF = -1e30
LAMBDA_INIT = 0.8 - 0.6 * math.exp(-0.3 * 0)

LANES = 128
ROW_TILE = D_MODEL // LANES
ATT_TILE = 256
ONES_ROWS = 16
LOGIT_LIMIT = 80.0
ROUNDING_MARGIN = 1.05
LOG2E = math.log2(math.e)
PROJ_ROWS = 1024
OUT_ROWS = 1024
ROUTE_TILE = 256
MOE_CHUNK = 512
MOE_TOKEN_PARTS = 2
EXPERT_ROW_PARTS = 4
MOVE_ROWS = 512
FUSED_MOVE_ROWS = 256
SUBLANES = 8
ROUTER_ROWS = -(-(N_EXPERTS + N_GROUPS) // SUBLANES) * SUBLANES
V7X_VMEM_BYTES = 64 * 1024 * 1024
VMEM_LIMIT = V7X_VMEM_BYTES * 3 // 4

F32 = jnp.float32
BF16 = jnp.bfloat16
_NT = (((1,), (1,)), ((), ()))


def _t5_thresholds():
    n = np.arange(0, 1 << 16)
    max_exact = NUM_BUCKETS // 2
    nf = np.maximum(n, 1).astype(np.float32)
    large = max_exact + (np.log(nf / np.float32(max_exact)) / np.float32(math.log(MAX_DISTANCE / max_exact))
                         * np.float32(NUM_BUCKETS - max_exact)).astype(np.int32)
    bucket = np.where(n < max_exact, n, np.minimum(large, NUM_BUCKETS - 1))
    return [int(np.searchsorted(bucket, b, side="left")) for b in range(1, NUM_BUCKETS)]


_T5_THRESHOLDS = _t5_thresholds()


def _params(*sem):
    return pltpu.CompilerParams(dimension_semantics=sem, vmem_limit_bytes=VMEM_LIMIT)


def _bias_kernel(tab_ref, out_ref):
    h = pl.program_id(0)
    kj = lax.broadcasted_iota(jnp.int32, (ATT_TILE, ATT_TILE), 0)
    qi = lax.broadcasted_iota(jnp.int32, (ATT_TILE, ATT_TILE), 1)
    for d in range(out_ref.shape[0]):
        dist = d * ATT_TILE + qi - kj
        lo, hi = d * ATT_TILE - (ATT_TILE - 1), d * ATT_TILE + (ATT_TILE - 1)
        base = sum(thr <= max(lo, 0) for thr in _T5_THRESHOLDS)
        val = jnp.full((ATT_TILE, ATT_TILE), tab_ref[base, h], F32)
        for b, thr in enumerate(_T5_THRESHOLDS, start=1):
            if max(lo, 0) < thr <= hi:
                val = jnp.where(dist >= thr, tab_ref[b, h], val)
        val = val * LOG2E
        out_ref[d] = jnp.where(dist < 0, NEG_INF, val) if lo < 0 else val


def _bias_tiles(rel_bias, n_diag):
    return pl.pallas_call(
        _bias_kernel,
        grid=(N_HEADS_TOTAL,),
        in_specs=[pl.BlockSpec(memory_space=pltpu.SMEM)],
        out_specs=pl.BlockSpec((None, n_diag, ATT_TILE, ATT_TILE), lambda h: (h, 0, 0, 0)),
        out_shape=jax.ShapeDtypeStruct((N_HEADS_TOTAL, n_diag, ATT_TILE, ATT_TILE), F32),
        compiler_params=_params("arbitrary"),
        name="bias_tiles",
    )(rel_bias)


def _in_proj_kernel(x_ref, g1_ref, w_ref, pg_ref, grp_ref, qk_ref, vt_ref):
    n_row_tiles = PROJ_ROWS // ATT_TILE
    h = []
    for t in range(n_row_tiles):
        x = x_ref[t * ATT_TILE:(t + 1) * ATT_TILE, :]
        ms = jnp.mean(x * x, axis=-1, keepdims=True)
        h.append((x * lax.rsqrt(ms + RMS_EPS) * g1_ref[...]).astype(BF16))

    cw = 2 * LANES

    def finish(c, t, p):
        rows = slice(t * ATT_TILE, (t + 1) * ATT_TILE)
        cols = slice(c * cw, (c + 1) * cw)
        if c < QK_WIDTH // cw:
            msq = jnp.dot((p * p).astype(BF16), grp_ref[...], preferred_element_type=F32)
            qk_ref[rows, cols] = (p * lax.rsqrt(msq + RMS_EPS) * pg_ref[:, cols]).astype(BF16)
        else:
            vt_ref[t, c * cw - QK_WIDTH:(c + 1) * cw - QK_WIDTH, :] = p.T.astype(BF16)

    units = [(c, t) for t in range(n_row_tiles) for c in range((QK_WIDTH + V_WIDTH) // cw)]
    pending = None
    for c, t in units:
        p = jnp.dot(h[t], w_ref[:, c * cw:(c + 1) * cw], preferred_element_type=F32)
        if pending is not None:
            finish(*pending)
        pending = (c, t, p)
    finish(*pending)


def _in_proj(x2, g1, w_perm, post_gain, grp, batch, seq):
    n = x2.shape[0]
    steps_per_seq = seq // PROJ_ROWS
    tiles_per_step = PROJ_ROWS // ATT_TILE
    return pl.pallas_call(
        _in_proj_kernel,
        grid=(n // PROJ_ROWS,),
        in_specs=[
            pl.BlockSpec((PROJ_ROWS, D_MODEL), lambda i: (i, 0)),
            pl.BlockSpec((1, D_MODEL), lambda i: (0, 0)),
            pl.BlockSpec((D_MODEL, QK_WIDTH + V_WIDTH), lambda i: (0, 0)),
            pl.BlockSpec((1, QK_WIDTH), lambda i: (0, 0)),
            pl.BlockSpec((2 * LANES, 2 * LANES), lambda i: (0, 0)),
        ],
        out_specs=[
            pl.BlockSpec((PROJ_ROWS, QK_WIDTH), lambda i: (i, 0)),
            pl.BlockSpec((None, tiles_per_step, V_WIDTH, ATT_TILE),
                         lambda i: (i // steps_per_seq, i % steps_per_seq, 0, 0)),
        ],
        out_shape=[
            jax.ShapeDtypeStruct((n, QK_WIDTH), BF16),
            jax.ShapeDtypeStruct((batch, seq // ATT_TILE, V_WIDTH, ATT_TILE), BF16),
        ],
        compiler_params=_params("arbitrary"),
        name="in_proj",
    )(x2, g1, w_perm, post_gain, grp)


def _flash_tiles(qi, n_chains, logits_fn, values_fn, bias_fn, mask_fn, s_ref, p_ref, acc_ref):
    for c in range(n_chains):
        s_ref[0, c] = logits_fn(c, qi)
    p_ref[...] = jnp.zeros_like(p_ref)
    acc_ref[...] = jnp.zeros_like(acc_ref)

    def add_values(c, kv, alpha):
        acc_ref[c] = acc_ref[c] * alpha + jnp.dot(values_fn(c, kv), p_ref[c], preferred_element_type=F32)

    def step(j, carry):
        cur = lax.rem(j, 2)
        kv = qi - j
        out = []
        for c in range(n_chains):
            add_values(c, jnp.minimum(kv + 1, qi), carry[2 * c + 1])
        for c in range(n_chains):
            s = s_ref[cur, c] + bias_fn(c, j)
            tile_max = jnp.max(s, axis=0, keepdims=True)
            shift = mask_fn(c, kv)
            if shift is not None:
                tile_max = tile_max + shift
            m_new = jnp.maximum(carry[2 * c], tile_max)
            alpha = jnp.exp2(carry[2 * c] - m_new)
            p_ref[c] = jnp.exp2(s - (m_new if shift is None else m_new - shift)).astype(BF16)
            out += [m_new, alpha]
        for c in range(n_chains):
            s_ref[1 - cur, c] = logits_fn(c, jnp.maximum(kv - 1, 0))
        return tuple(out)

    m0 = jnp.full((1, ATT_TILE), NEG_INF, F32)
    one = jnp.ones((1, ATT_TILE), F32)
    carry = lax.fori_loop(0, qi + 1, step, (m0, one) * n_chains)
    for c in range(n_chains):
        add_values(c, 0, carry[2 * c + 1])


def _bounded_tiles(n_tiles, n_chains, logits_fn, values_fn, bias_fn, keep_fn, finish_fn, p_ref, acc_ref,
                   between_fn=lambda d: None):
    for d in range(n_tiles):
        for qi in range(d, n_tiles):
            for c in range(n_chains):
                p_ref[qi, c] = jnp.exp2(logits_fn(c, qi, qi - d) + bias_fn(c, d)).astype(BF16)
        between_fn(d)
        for qi in range(d, n_tiles):
            kv = qi - d
            for c in range(n_chains):
                pv = jnp.dot(values_fn(c, kv), p_ref[qi, c], preferred_element_type=F32)
                keep = keep_fn(c, qi, kv)
                if keep is not None:
                    pv = pv * keep
                acc_ref[qi, c] = pv if d == 0 else acc_ref[qi, c] + pv
        finish_fn(d)


def _logits_bounded(q_gain, k_gain, bias_cols):
    bound = (HEAD_DIM * jnp.max(jnp.abs(q_gain)) * jnp.max(jnp.abs(k_gain)) * ROUNDING_MARGIN
             + jnp.max(jnp.abs(bias_cols)) * LOG2E)
    return (bound <= LOGIT_LIMIT).astype(jnp.int32).reshape(1)


def _with_ones_rows(vt):
    return jnp.concatenate([vt, jnp.ones((ONES_ROWS, vt.shape[1]), vt.dtype)], axis=0)


def _attention_scratch(n_tiles, value_rows, n_chains=2):
    return [pltpu.VMEM((n_tiles, n_chains, ATT_TILE, LANES), BF16),
            pltpu.VMEM((2, n_chains, ATT_TILE, ATT_TILE), F32),
            pltpu.VMEM((n_tiles, n_chains, ATT_TILE, ATT_TILE), BF16),
            pltpu.VMEM((n_tiles, n_chains, value_rows + ONES_ROWS, ATT_TILE), F32)]


def _half_lane_split(q):
    lane = lax.broadcasted_iota(jnp.int32, q.shape, 1)
    zero = jnp.zeros_like(q)
    return jnp.where(lane < HEAD_DIM, q, zero), jnp.where(lane >= HEAD_DIM, q, zero)


def _diff_kernel(bounded_ref, lq1_ref, lk1_ref, lq2_ref, lk2_ref, subg_ref, q_ref, k_ref, vt_ref, bias_ref, o_ref,
                 qm_ref, s_ref, p_ref, acc_ref):
    n_tiles = q_ref.shape[0]
    chains = dict(n_chains=2, values_fn=lambda c, kv: _with_ones_rows(vt_ref[kv]))
    width = 2 * HEAD_DIM

    def prepare():
        for qi in range(n_tiles):
            qm_ref[qi, 0], qm_ref[qi, 1] = _half_lane_split(q_ref[qi])

    def finish(qi):
        lam = (jnp.exp(jnp.sum(lq1_ref[...] * lk1_ref[...], keepdims=True))
               - jnp.exp(jnp.sum(lq2_ref[...] * lk2_ref[...], keepdims=True)) + LAMBDA_INIT)
        a1, a2 = acc_ref[qi, 0], acc_ref[qi, 1]
        o = a1[:width] / a1[width:width + 1] - lam * (a2[:width] / a2[width:width + 1])
        ms = jnp.mean(o * o, axis=0, keepdims=True)
        o = o * lax.rsqrt(ms + RMS_EPS) * subg_ref[...] * (1.0 - LAMBDA_INIT)
        o_ref[qi] = o.T.astype(BF16)

    bounded = bounded_ref[0] > 0

    @pl.when(bounded)
    def _():
        prepare()
        _bounded_tiles(
            n_tiles,
            logits_fn=lambda c, qi, kv: lax.dot_general(k_ref[kv], qm_ref[qi, c], _NT, preferred_element_type=F32),
            bias_fn=lambda c, d: bias_ref[d],
            keep_fn=lambda c, qi, kv: None,
            finish_fn=finish,
            p_ref=p_ref, acc_ref=acc_ref, **chains)

    @pl.when(jnp.logical_not(bounded))
    def _():
        prepare()

        def q_tile(qi, carry):
            _flash_tiles(
                qi,
                logits_fn=lambda c, kv: lax.dot_general(k_ref[kv], qm_ref[qi, c], _NT, preferred_element_type=F32),
                bias_fn=lambda c, j: bias_ref[j],
                mask_fn=lambda c, kv: None,
                s_ref=s_ref, p_ref=p_ref.at[0], acc_ref=acc_ref.at[qi], **chains)
            return carry
        lax.fori_loop(0, n_tiles, q_tile, 0)
        for qi in range(n_tiles):
            finish(qi)


def _diff_attention(bounded, lam_vecs, sub_g_col, qk4, vt4, bias):
    batch, n_tiles = qk4.shape[0], qk4.shape[1]
    k_col0 = DIFF_WIDTH // LANES
    vec = pl.BlockSpec((1, HEAD_DIM), lambda h, b: (0, 0))
    return pl.pallas_call(
        _diff_kernel,
        grid=(HEADS_DIFF, batch),
        in_specs=[
            pl.BlockSpec(memory_space=pltpu.SMEM),
            vec, vec, vec, vec,
            pl.BlockSpec((2 * HEAD_DIM, 1), lambda h, b: (0, 0)),
            pl.BlockSpec((None, n_tiles, ATT_TILE, LANES), lambda h, b: (b, 0, 0, h)),
            pl.BlockSpec((None, n_tiles, ATT_TILE, LANES), lambda h, b: (b, 0, 0, k_col0 + h)),
            pl.BlockSpec((None, n_tiles, LANES, ATT_TILE), lambda h, b: (b, 0, h, 0)),
            pl.BlockSpec((None, n_tiles, ATT_TILE, ATT_TILE), lambda h, b: (h, 0, 0, 0)),
        ],
        out_specs=pl.BlockSpec((None, n_tiles, ATT_TILE, LANES), lambda h, b: (b, 0, 0, h)),
        out_shape=jax.ShapeDtypeStruct((batch, n_tiles, ATT_TILE, DIFF_WIDTH), BF16),
        scratch_shapes=_attention_scratch(n_tiles, 2 * HEAD_DIM),
        compiler_params=_params("arbitrary", "arbitrary"),
        name="diff_attention",
    )(bounded, *lam_vecs, sub_g_col, qk4, qk4, vt4, bias)


def _split3(v):
    hi = v.astype(BF16)
    r1 = v - hi.astype(F32)
    mid = r1.astype(BF16)
    lo = (r1 - mid.astype(F32)).astype(BF16)
    return hi, mid, lo


def _block_mask(gate, own):
    row = lax.broadcasted_iota(jnp.int32, gate.shape, 0)
    rank = jnp.zeros(gate.shape, jnp.int32)
    for m in range(own):
        gm = gate[m:m + 1, :]
        beats = (gm > gate) | ((gm == gate) & (row > m))
        rank = rank + jnp.where(beats, 1, 0)
    keep = ((rank < MOBA_TOPK) & (row < own)) | (row == own)
    return jnp.where(keep, 0.0, NEG_INF).astype(F32)


def _moba_kernel(bounded_ref, q_ref, k_ref, vt_ref, bias_ref, o_ref, kmean_ref, mask_ref, qm_ref, s_ref, p_ref,
                 acc_ref):
    n_tiles = q_ref.shape[0]

    def split_queries():
        for qi in range(n_tiles):
            qm_ref[qi, 0], qm_ref[qi, 1] = _half_lane_split(q_ref[qi])

    def block_masks(q_tiles):
        for n in range(n_tiles):
            kmean_ref[n:n + 1, :] = jnp.mean(k_ref[n].astype(F32), axis=0, keepdims=True)
        per_head = zip(*[_half_lane_split(term) for term in _split3(kmean_ref[...])])
        gate_lhs = jnp.concatenate([term for head_terms in per_head for term in head_terms], axis=0)
        n_terms = gate_lhs.shape[0] // (2 * n_tiles)
        for qi in q_tiles:
            terms = lax.dot_general(gate_lhs, q_ref[qi], _NT, preferred_element_type=F32)
            for half in range(2):
                rows = [terms[(half * n_terms + t) * n_tiles:(half * n_terms + t + 1) * n_tiles]
                        for t in range(n_terms)]
                mask_ref[qi, half] = _block_mask(sum(rows[1:], rows[0]), qi)

    def keep_row(c, qi, kv):
        if qi <= MOBA_TOPK or kv == qi:
            return None
        return jnp.where(mask_ref[qi, c, kv:kv + 1, :] < -1.0, 0.0, 1.0)

    def finish(qi):
        halves = [acc_ref[qi, half] for half in range(2)]
        o = jnp.concatenate([a[:HEAD_DIM] / a[HEAD_DIM:HEAD_DIM + 1] for a in halves], axis=0)
        o_ref[qi] = o.T.astype(BF16)

    chains = dict(
        n_chains=2,
        values_fn=lambda c, kv: _with_ones_rows(vt_ref[kv, c * HEAD_DIM:(c + 1) * HEAD_DIM, :]))
    bounded = bounded_ref[0] > 0

    @pl.when(bounded)
    def _():
        split_queries()
        _bounded_tiles(
            n_tiles,
            logits_fn=lambda c, qi, kv: lax.dot_general(k_ref[kv], qm_ref[qi, c], _NT, preferred_element_type=F32),
            bias_fn=lambda c, d: bias_ref[c, d],
            keep_fn=keep_row,
            between_fn=lambda d: block_masks(range(MOBA_TOPK + 1, n_tiles)) if d == 0 else None,
            finish_fn=finish,
            p_ref=p_ref, acc_ref=acc_ref, **chains)

    @pl.when(jnp.logical_not(bounded))
    def _():
        split_queries()
        block_masks(range(n_tiles))

        def q_tile(qi, carry):
            _flash_tiles(
                qi,
                logits_fn=lambda c, kv: lax.dot_general(k_ref[kv], qm_ref[qi, c], _NT, preferred_element_type=F32),
                bias_fn=lambda c, j: bias_ref[c, j],
                mask_fn=lambda c, kv: mask_ref[qi, c, pl.ds(kv, 1), :],
                s_ref=s_ref, p_ref=p_ref.at[0], acc_ref=acc_ref.at[qi], **chains)
            return carry
        lax.fori_loop(0, n_tiles, q_tile, 0)
        for qi in range(n_tiles):
            finish(qi)


def _moba_attention(bounded, qk4, vt4, bias):
    batch, n_tiles = qk4.shape[0], qk4.shape[1]
    q_col0 = 2 * DIFF_WIDTH // LANES
    k_col0 = q_col0 + MOBA_WIDTH // LANES
    v_row0 = DIFF_WIDTH // LANES
    pair0 = HEADS_DIFF // 2
    return pl.pallas_call(
        _moba_kernel,
        grid=(HEADS_MOBA // 2, batch),
        in_specs=[
            pl.BlockSpec(memory_space=pltpu.SMEM),
            pl.BlockSpec((None, n_tiles, ATT_TILE, LANES), lambda h, b: (b, 0, 0, q_col0 + h)),
            pl.BlockSpec((None, n_tiles, ATT_TILE, LANES), lambda h, b: (b, 0, 0, k_col0 + h)),
            pl.BlockSpec((None, n_tiles, LANES, ATT_TILE), lambda h, b: (b, 0, v_row0 + h, 0)),
            pl.BlockSpec((2, n_tiles, ATT_TILE, ATT_TILE), lambda h, b: (pair0 + h, 0, 0, 0)),
        ],
        out_specs=pl.BlockSpec((None, n_tiles, ATT_TILE, LANES), lambda h, b: (b, 0, 0, h)),
        out_shape=jax.ShapeDtypeStruct((batch, n_tiles, ATT_TILE, MOBA_WIDTH), BF16),
        scratch_shapes=[
            pltpu.VMEM((n_tiles, LANES), F32),
            pltpu.VMEM((n_tiles, 2, n_tiles, ATT_TILE), F32),
        ] + _attention_scratch(n_tiles, HEAD_DIM),
        compiler_params=_params("arbitrary", "arbitrary"),
        name="moba_attention",
    )(bounded, qk4, qk4, vt4, bias)


def _load_token_rows(ref, n_rows, lead=(), first=0):
    chunks = [ref[lead + (pl.ds(first * ROW_TILE + c, n_rows, stride=ROW_TILE), slice(None))]
              for c in range(ROW_TILE)]
    return jnp.concatenate(chunks, axis=1)


def _store_token_rows(ref, value, first=0):
    n_rows = value.shape[0]
    for c in range(ROW_TILE):
        ref[pl.ds(first * ROW_TILE + c, n_rows, stride=ROW_TILE), :] = value[:, c * LANES:(c + 1) * LANES]


def _token_tile(ref, t):
    return ref.at[pl.ds(pl.multiple_of(t * ROW_TILE, ROW_TILE), ROW_TILE)]


def _first_argmax(v):
    top = jnp.max(v, axis=0, keepdims=True)
    row = lax.broadcasted_iota(jnp.int32, v.shape, 0)
    idx = jnp.min(jnp.where(v == top, row, v.shape[0]), axis=0, keepdims=True)
    return top, idx


def _out_proj_kernel(yd_ref, ym_ref, x_ref, wd_ref, wm_ref, g2_ref, wr_ref, tri_ref,
                     x1_ref, ri_ref, rf_ref, cnt_ref, run_ref):
    @pl.when(lax.rem(pl.program_id(0), pl.num_programs(0) // MOE_TOKEN_PARTS) == 0)
    def _():
        run_ref[...] = jnp.zeros_like(run_ref)

    def project(t):
        rows = slice(t * ROUTE_TILE, (t + 1) * ROUTE_TILE)
        x1 = (x_ref[rows, :] + jnp.dot(yd_ref[rows, :], wd_ref[...], preferred_element_type=F32)
              + jnp.dot(ym_ref[rows, :], wm_ref[...], preferred_element_type=F32))
        _store_token_rows(x1_ref, x1, first=t * ROUTE_TILE)
        ms = jnp.mean(x1 * x1, axis=-1, keepdims=True)
        return (x1 * lax.rsqrt(ms + RMS_EPS) * g2_ref[...]).astype(BF16)

    def pick(h2):
        lg = lax.dot_general(wr_ref[...], h2, _NT, preferred_element_type=F32)
        g_logits = lg[N_EXPERTS:N_EXPERTS + N_GROUPS, :]
        g_top, g_idx = _first_argmax(g_logits)
        p_group = 1.0 / jnp.sum(jnp.exp(g_logits - g_top), axis=0, keepdims=True)
        e_logits = lg[0:EXPERTS_PER_GROUP, :]
        for g in range(1, N_GROUPS):
            e_logits = jnp.where(g_idx == g, lg[g * EXPERTS_PER_GROUP:(g + 1) * EXPERTS_PER_GROUP, :], e_logits)
        v1, i1 = _first_argmax(e_logits)
        row = lax.broadcasted_iota(jnp.int32, e_logits.shape, 0)
        v2, i2 = _first_argmax(jnp.where(row == i1, -jnp.inf, e_logits))
        ratio = jnp.exp(v2 - v1)
        w1 = p_group / (1.0 + ratio)
        w2 = p_group * ratio / (1.0 + ratio)
        e1 = g_idx * EXPERTS_PER_GROUP + i1
        e2 = g_idx * EXPERTS_PER_GROUP + i2
        erow = lax.broadcasted_iota(jnp.int32, (N_EXPERTS, ROUTE_TILE), 0)
        hit1 = erow == e1
        hit2 = erow == e2
        onehot = jnp.where(hit1 | hit2, 1.0, 0.0).astype(F32)
        return e1, e2, w1, w2, hit1, hit2, onehot

    def place(t, e1, e2, w1, w2, hit1, hit2, onehot):
        before = jnp.dot(onehot.astype(BF16), tri_ref[...], preferred_element_type=F32) + run_ref[...]
        r1 = jnp.sum(jnp.where(hit1, before, 0.0), axis=0, keepdims=True).astype(jnp.int32)
        r2 = jnp.sum(jnp.where(hit2, before, 0.0), axis=0, keepdims=True).astype(jnp.int32)
        run_ref[...] = run_ref[...] + jnp.sum(onehot, axis=1, keepdims=True)
        lanes = slice(t * ROUTE_TILE, (t + 1) * ROUTE_TILE)
        out_row = lax.broadcasted_iota(jnp.int32, (SUBLANES, ROUTE_TILE), 0)
        ri_ref[:, lanes] = jnp.where(out_row == 0, e1, jnp.where(out_row == 1, e2,
                                     jnp.where(out_row == 2, r1, jnp.where(out_row == 3, r2, 0))))
        rf_ref[:, lanes] = jnp.where(out_row == 0, w1, jnp.where(out_row == 1, w2, 0.0))

    n_tiles = OUT_ROWS // ROUTE_TILE
    projected, picked = {}, {}
    for t in range(n_tiles + 2):
        if t < n_tiles:
            projected[t] = project(t)
        if 0 <= t - 1 < n_tiles:
            picked[t - 1] = pick(projected.pop(t - 1))
        if 0 <= t - 2 < n_tiles:
            place(t - 2, *picked.pop(t - 2))
    cnt_ref[...] = jnp.broadcast_to(run_ref[...], cnt_ref.shape).astype(jnp.int32)


def _out_proj(y_d, y_m, x2, wo_d, wo_m, g2, w_router, tri):
    n = x2.shape[0]
    const = lambda i: (0, 0)
    steps_per_part = n // OUT_ROWS // MOE_TOKEN_PARTS
    return pl.pallas_call(
        _out_proj_kernel,
        grid=(n // OUT_ROWS,),
        in_specs=[
            pl.BlockSpec((OUT_ROWS, DIFF_WIDTH), lambda i: (i, 0)),
            pl.BlockSpec((OUT_ROWS, MOBA_WIDTH), lambda i: (i, 0)),
            pl.BlockSpec((OUT_ROWS, D_MODEL), lambda i: (i, 0)),
            pl.BlockSpec((DIFF_WIDTH, D_MODEL), const),
            pl.BlockSpec((MOBA_WIDTH, D_MODEL), const),
            pl.BlockSpec((1, D_MODEL), const),
            pl.BlockSpec((ROUTER_ROWS, D_MODEL), const),
            pl.BlockSpec((ROUTE_TILE, ROUTE_TILE), const),
        ],
        out_specs=[
            pl.BlockSpec((OUT_ROWS * ROW_TILE, LANES), lambda i: (i, 0)),
            pl.BlockSpec((SUBLANES, OUT_ROWS), lambda i: (0, i)),
            pl.BlockSpec((SUBLANES, OUT_ROWS), lambda i: (0, i)),
            pl.BlockSpec((None, N_EXPERTS, LANES), lambda i: (i // steps_per_part, 0, 0)),
        ],
        out_shape=[
            jax.ShapeDtypeStruct((n * ROW_TILE, LANES), F32),
            jax.ShapeDtypeStruct((SUBLANES, n), jnp.int32),
            jax.ShapeDtypeStruct((SUBLANES, n), F32),
            jax.ShapeDtypeStruct((MOE_TOKEN_PARTS, N_EXPERTS, LANES), jnp.int32),
        ],
        scratch_shapes=[pltpu.VMEM((N_EXPERTS, 1), F32)],
        compiler_params=_params("arbitrary"),
        name="out_proj_route",
    )(y_d, y_m, x2, wo_d, wo_m, g2, w_router, tri)


def _dispatch_step(i, n_steps, first_tile, zchunk_ref, dest_ref, x1_hbm, xs_hbm, zero_ref, stage_ref, zero_sem,
                   load_sem, row_sem):
    last = n_steps - 1
    slot = lax.rem(i, 2)
    chunk_rows = MOE_CHUNK * ROW_TILE
    tile_tokens = dest_ref.shape[2]
    tile_rows = tile_tokens * ROW_TILE

    def load(t, s):
        start = pl.multiple_of((first_tile + t) * tile_rows, tile_rows)
        return pltpu.make_async_copy(x1_hbm.at[pl.ds(start, tile_rows)], stage_ref.at[s], load_sem.at[s])

    def wait_rows(s):
        for _ in range(EXPERTS_PER_TOKEN):
            pltpu.make_async_copy(stage_ref.at[s], xs_hbm.at[pl.ds(0, tile_rows)], row_sem.at[s]).wait()

    def zero_copy(c):
        start = pl.multiple_of(c * chunk_rows, chunk_rows)
        return pltpu.make_async_copy(zero_ref, xs_hbm.at[pl.ds(start, chunk_rows)], zero_sem)

    @pl.when(i == 0)
    def _():
        zero_ref[...] = jnp.zeros_like(zero_ref)

        def start_one(c, carry):
            @pl.when(zchunk_ref[c] > 0)
            def _():
                zero_copy(c).start()
            return carry

        def wait_one(c, carry):
            @pl.when(zchunk_ref[c] > 0)
            def _():
                zero_copy(c).wait()
            return carry

        lax.fori_loop(0, zchunk_ref.shape[0], start_one, 0)
        lax.fori_loop(0, zchunk_ref.shape[0], wait_one, 0)
        load(0, 0).start()

    @pl.when(i > 0)
    def _():
        wait_rows(1 - slot)

    @pl.when(i < last)
    def _():
        load(i + 1, 1 - slot).start()

    load(i, slot).wait()

    def send_row(r, carry):
        src = _token_tile(stage_ref.at[slot], r)
        for k in range(EXPERTS_PER_TOKEN):
            pltpu.make_async_copy(src, _token_tile(xs_hbm, dest_ref[0, k, r]), row_sem.at[slot]).start(priority=k)
        return carry

    lax.fori_loop(0, tile_tokens, send_row, 0, unroll=8)

    @pl.when(i == last)
    def _():
        wait_rows(slot)


def _dispatch_scratch(tile_tokens):
    return [pltpu.VMEM((MOE_CHUNK * ROW_TILE, LANES), F32),
            pltpu.VMEM((2, tile_tokens * ROW_TILE, LANES), F32),
            pltpu.SemaphoreType.DMA(()),
            pltpu.SemaphoreType.DMA((2,)),
            pltpu.SemaphoreType.DMA((2,))]


def _dispatch_kernel(first_tile, zchunk_ref, dest_ref, x1_hbm, xs_hbm, *scratch):
    _dispatch_step(pl.program_id(0), pl.num_programs(0), first_tile, zchunk_ref, dest_ref, x1_hbm, xs_hbm, *scratch)


def _dispatch(part, zchunk, dest3, x1, n_slots):
    tile_tokens = dest3.shape[2]
    tiles = dest3.shape[0] // MOE_TOKEN_PARTS
    first_tile = part * tiles
    grid_spec = pltpu.PrefetchScalarGridSpec(
        num_scalar_prefetch=1,
        grid=(tiles,),
        in_specs=[
            pl.BlockSpec((1, EXPERTS_PER_TOKEN, tile_tokens), lambda i, zc: (first_tile + i, 0, 0),
                         memory_space=pltpu.SMEM),
            pl.BlockSpec(memory_space=pl.ANY),
        ],
        out_specs=pl.BlockSpec(memory_space=pl.ANY),
        scratch_shapes=_dispatch_scratch(tile_tokens),
    )
    return pl.pallas_call(
        functools.partial(_dispatch_kernel, first_tile),
        grid_spec=grid_spec,
        out_shape=jax.ShapeDtypeStruct((n_slots * ROW_TILE, LANES), F32),
        compiler_params=_params("arbitrary"),
        name="moe_dispatch",
    )(zchunk, dest3, x1)


def _expert_step(c, na_ref, xs_ref, g2_ref, wg_ref, wu_ref, wd_ref, o_ref):
    active = c < na_ref[0]

    @pl.when(jnp.logical_not(active))
    def _():
        o_ref[...] = jnp.zeros_like(o_ref)

    @pl.when(active)
    def _():
        part_rows = MOE_CHUNK // EXPERT_ROW_PARTS

        def gate_up(r):
            x = _load_token_rows(xs_ref, part_rows, first=r * part_rows)
            ms = jnp.mean(x * x, axis=-1, keepdims=True)
            h = (x * lax.rsqrt(ms + RMS_EPS) * g2_ref[...]).astype(BF16)
            return (jnp.dot(h, wg_ref[...], preferred_element_type=F32),
                    jnp.dot(h, wu_ref[...], preferred_element_type=F32))

        def down(r, gate, up):
            hid = (gate * jax.nn.sigmoid(gate) * up).astype(BF16)
            _store_token_rows(o_ref, jnp.dot(hid, wd_ref[...], preferred_element_type=F32), first=r * part_rows)

        pending = None
        for r in range(EXPERT_ROW_PARTS):
            current = (r,) + gate_up(r)
            if pending is not None:
                down(*pending)
            pending = current
        down(*pending)


def _expert_kernel(ce_ref, na_ref, *refs):
    _expert_step(pl.program_id(0), na_ref, *refs)


def _expert_specs():
    rows = lambda c, ce, na, *_: (jnp.minimum(c, na[0] - 1), 0)
    expert = lambda c, ce, na, *_: (ce[c], 0, 0)
    in_specs = [
        pl.BlockSpec((MOE_CHUNK * ROW_TILE, LANES), rows),
        pl.BlockSpec((1, D_MODEL), lambda c, *_: (0, 0)),
        pl.BlockSpec((None, D_MODEL, EXPERT_HIDDEN), expert),
        pl.BlockSpec((None, D_MODEL, EXPERT_HIDDEN), expert),
        pl.BlockSpec((None, EXPERT_HIDDEN, D_MODEL), expert),
    ]
    return in_specs, pl.BlockSpec((MOE_CHUNK * ROW_TILE, LANES), lambda c, *_: (c, 0))


def _experts(chunk_e, n_active, xs, g2, wg, wu, wd):
    n_slots = xs.shape[0] // ROW_TILE
    in_specs, out_spec = _expert_specs()
    grid_spec = pltpu.PrefetchScalarGridSpec(
        num_scalar_prefetch=2, grid=(n_slots // MOE_CHUNK,), in_specs=in_specs, out_specs=out_spec)
    return pl.pallas_call(
        _expert_kernel,
        grid_spec=grid_spec,
        out_shape=jax.ShapeDtypeStruct((n_slots * ROW_TILE, LANES), F32),
        compiler_params=_params("arbitrary"),
        name="moe_experts",
    )(chunk_e, n_active, xs, g2, wg, wu, wd)


def _experts_dispatch_kernel(first_tile, n_tiles, ce_ref, na_ref, zchunk_ref,
                             xs_ref, g2_ref, wg_ref, wu_ref, wd_ref, dest_ref, x1_hbm,
                             o_ref, xs_next_hbm, *scratch):
    i = pl.program_id(0)

    @pl.when(i < n_tiles)
    def _():
        _dispatch_step(i, n_tiles, first_tile, zchunk_ref, dest_ref, x1_hbm, xs_next_hbm, *scratch)

    _expert_step(i, na_ref, xs_ref, g2_ref, wg_ref, wu_ref, wd_ref, o_ref)


def _experts_dispatch(chunk_e, n_active, zchunk_next, xs, g2, wg, wu, wd, next_part, dest3, x1):
    n_slots = xs.shape[0] // ROW_TILE
    n_chunks = n_slots // MOE_CHUNK
    tile_tokens = dest3.shape[2]
    tiles = dest3.shape[0] // MOE_TOKEN_PARTS
    first_tile = next_part * tiles
    assert n_chunks >= tiles
    in_specs, out_spec = _expert_specs()
    in_specs += [
        pl.BlockSpec((1, EXPERTS_PER_TOKEN, tile_tokens),
                     lambda c, *_: (first_tile + jnp.minimum(c, tiles - 1), 0, 0), memory_space=pltpu.SMEM),
        pl.BlockSpec(memory_space=pl.ANY),
    ]
    grid_spec = pltpu.PrefetchScalarGridSpec(
        num_scalar_prefetch=3, grid=(n_chunks,), in_specs=in_specs,
        out_specs=[out_spec, pl.BlockSpec(memory_space=pl.ANY)],
        scratch_shapes=_dispatch_scratch(tile_tokens))
    slots = jax.ShapeDtypeStruct((n_slots * ROW_TILE, LANES), F32)
    return pl.pallas_call(
        functools.partial(_experts_dispatch_kernel, first_tile, tiles),
        grid_spec=grid_spec,
        out_shape=[slots, slots],
        compiler_params=_params("arbitrary"),
        name="moe_experts_dispatch",
    )(chunk_e, n_active, zchunk_next, xs, g2, wg, wu, wd, dest3, x1)


def _combine_kernel(dest_ref, dest_next_ref, w_ref, x1_ref, *rest):
    outs_hbm, (y_ref, buf_ref, sem_ref) = rest[:MOE_TOKEN_PARTS], rest[MOE_TOKEN_PARTS:]
    i = pl.program_id(0)
    slot = lax.rem(i, 2)
    tiles_per_part = pl.num_programs(0) // MOE_TOKEN_PARTS

    def fetch(tile, dref, s):
        for part, src_hbm in enumerate(outs_hbm):
            @pl.when(tile // tiles_per_part == part)
            def _():
                def fetch_row(r, carry):
                    for k in range(EXPERTS_PER_TOKEN):
                        pltpu.make_async_copy(_token_tile(src_hbm, dref[0, k, r]),
                                              _token_tile(buf_ref.at[s, k], r), sem_ref.at[s]).start(priority=k)
                    return carry
                lax.fori_loop(0, MOVE_ROWS, fetch_row, 0, unroll=8)

    @pl.when(i == 0)
    def _():
        fetch(i, dest_ref, 0)

    @pl.when(i + 1 < pl.num_programs(0))
    def _():
        fetch(i + 1, dest_next_ref, 1 - slot)

    for k in range(EXPERTS_PER_TOKEN):
        pltpu.make_async_copy(outs_hbm[0].at[pl.ds(0, MOVE_ROWS * ROW_TILE)], buf_ref.at[slot, k],
                              sem_ref.at[slot]).wait()
    w = w_ref[...]
    y_ref[...] = (_load_token_rows(x1_ref, MOVE_ROWS)
                  + w[:, 0:1] * _load_token_rows(buf_ref, MOVE_ROWS, (slot, 0))
                  + w[:, 1:2] * _load_token_rows(buf_ref, MOVE_ROWS, (slot, 1)))


def _combine(dest3, wts, x1, outs_parts):
    n = x1.shape[0] // ROW_TILE
    steps = n // MOVE_ROWS
    dest_block = (1, EXPERTS_PER_TOKEN, MOVE_ROWS)
    return pl.pallas_call(
        _combine_kernel,
        grid=(steps,),
        in_specs=[
            pl.BlockSpec(dest_block, lambda i: (i, 0, 0), memory_space=pltpu.SMEM),
            pl.BlockSpec(dest_block, lambda i: (jnp.minimum(i + 1, steps - 1), 0, 0), memory_space=pltpu.SMEM),
            pl.BlockSpec((MOVE_ROWS, EXPERTS_PER_TOKEN), lambda i: (i, 0)),
            pl.BlockSpec((MOVE_ROWS * ROW_TILE, LANES), lambda i: (i, 0)),
        ] + [pl.BlockSpec(memory_space=pl.ANY)] * MOE_TOKEN_PARTS,
        out_specs=pl.BlockSpec((MOVE_ROWS, D_MODEL), lambda i: (i, 0)),
        out_shape=jax.ShapeDtypeStruct((n, D_MODEL), F32),
        scratch_shapes=[
            pltpu.VMEM((2, EXPERTS_PER_TOKEN, MOVE_ROWS * ROW_TILE, LANES), F32),
            pltpu.SemaphoreType.DMA((2,)),
        ],
        compiler_params=_params("arbitrary"),
        name="moe_combine",
    )(dest3, dest3, wts, x1, *outs_parts)


def _layer(x, norm1_g, w_in, diff_q_g, diff_k_g, lambda_q1, lambda_k1, lambda_q2, lambda_k2, diff_sub_g,
           moba_q_g, moba_k_g, rel_bias, w_out, norm2_g, router_group, router_expert, w_gate, w_up, w_down):
    batch, seq, _ = x.shape
    n = batch * seq
    n_tiles = seq // ATT_TILE
    scale = HEAD_DIM ** -0.5 * LOG2E

    d3 = 3 * DIFF_WIDTH
    w_perm = jnp.concatenate([w_in[:, :2 * DIFF_WIDTH], w_in[:, d3:d3 + 2 * MOBA_WIDTH],
                              w_in[:, 2 * DIFF_WIDTH:d3], w_in[:, d3 + 2 * MOBA_WIDTH:]], axis=1).astype(BF16)
    reps_d, reps_m = DIFF_WIDTH // HEAD_DIM, MOBA_WIDTH // HEAD_DIM
    post_gain = jnp.concatenate([jnp.tile(diff_q_g * scale, reps_d), jnp.tile(diff_k_g, reps_d),
                                 jnp.tile(moba_q_g * scale, reps_m), jnp.tile(moba_k_g, reps_m)])[None, :]
    head_of = np.arange(2 * LANES) // HEAD_DIM
    grp = jnp.asarray((head_of[:, None] == head_of[None, :]) / HEAD_DIM, BF16)

    bias = _bias_tiles(rel_bias, n_tiles)
    qk, vt4 = _in_proj(x.reshape(n, D_MODEL), norm1_g[None, :], w_perm, post_gain, grp, batch, seq)
    qk4 = qk.reshape(batch, n_tiles, ATT_TILE, QK_WIDTH)
    lam_vecs = [v[None, :] for v in (lambda_q1, lambda_k1, lambda_q2, lambda_k2)]
    y_d = _diff_attention(_logits_bounded(diff_q_g * scale, diff_k_g, rel_bias[:, :HEADS_DIFF]),
                          lam_vecs, diff_sub_g[:, None], qk4, vt4, bias)
    y_m = _moba_attention(_logits_bounded(moba_q_g * scale, moba_k_g, rel_bias[:, HEADS_DIFF:]),
                          qk4, vt4, bias)

    w_router = jnp.concatenate([
        router_expert.transpose(0, 2, 1).reshape(N_EXPERTS, D_MODEL), router_group.T,
        jnp.zeros((ROUTER_ROWS - N_EXPERTS - N_GROUPS, D_MODEL), F32)], axis=0).astype(BF16)
    tri = jnp.asarray(np.triu(np.ones((ROUTE_TILE, ROUTE_TILE)), k=1), BF16)
    wo = w_out.astype(BF16)
    x1, route_i, route_f, cnt = _out_proj(y_d.reshape(n, DIFF_WIDTH), y_m.reshape(n, MOBA_WIDTH),
                                          x.reshape(n, D_MODEL), wo[:DIFF_WIDTH], wo[DIFF_WIDTH:],
                                          norm2_g[None, :], w_router, tri)

    parts = MOE_TOKEN_PARTS
    counts = cnt[:, :, 0]
    pad_counts = ((counts + MOE_CHUNK - 1) // MOE_CHUNK) * MOE_CHUNK
    pad_end = jnp.cumsum(pad_counts, axis=1)
    pad_start = pad_end - pad_counts
    experts = jnp.arange(N_EXPERTS, dtype=jnp.int32)
    picked = route_i[0:2].reshape(EXPERTS_PER_TOKEN, parts, n // parts)
    start_of = jnp.sum(jnp.where(picked[..., None] == experts, pad_start[None, :, None, :], 0), axis=-1)
    dest = start_of.reshape(EXPERTS_PER_TOKEN, n) + route_i[2:4]
    by_tile = lambda rows: dest.reshape(EXPERTS_PER_TOKEN, n // rows, rows).transpose(1, 0, 2)
    dest3, dest3_fused = by_tile(MOVE_ROWS), by_tile(FUSED_MOVE_ROWS)
    n_slots = n // parts * EXPERTS_PER_TOKEN + N_EXPERTS * MOE_CHUNK
    n_chunks = n_slots // MOE_CHUNK
    chunk_id = jnp.arange(n_chunks, dtype=jnp.int32)
    chunk_e = jnp.minimum(jnp.sum((pad_end[:, None, :] <= chunk_id[None, :, None] * MOE_CHUNK).astype(jnp.int32),
                                  axis=2), N_EXPERTS - 1)
    n_active = (pad_end[:, -1:] // MOE_CHUNK).astype(jnp.int32)
    next_e = jnp.concatenate([chunk_e[:, 1:], jnp.full((parts, 1), N_EXPERTS, jnp.int32)], axis=1)
    zchunk = ((chunk_id[None, :] >= n_active - 1) | (chunk_e != next_e)).astype(jnp.int32)

    weights = (norm2_g[None, :], w_gate.astype(BF16), w_up.astype(BF16), w_down.astype(BF16))
    xs = _dispatch(0, zchunk[0], dest3, x1, n_slots)
    outs = []
    for p in range(parts):
        if p + 1 < parts:
            out_p, xs_next = _experts_dispatch(chunk_e[p], n_active[p], zchunk[p + 1], xs, *weights, p + 1,
                                               dest3_fused, x1)
        else:
            out_p, xs_next = _experts(chunk_e[p], n_active[p], xs, *weights), None
        outs.append(out_p)
        xs = xs_next
    y = _combine(dest3, route_f[0:2].T, x1, outs)
    return y.reshape(batch, seq, D_MODEL)


def kernel(x, norm1_g, w_in, diff_q_g, diff_k_g, lambda_q1, lambda_k1, lambda_q2, lambda_k2, diff_sub_g,
           moba_q_g, moba_k_g, rel_bias, w_out, norm2_g, router_group, router_expert, w_gate, w_up, w_down):
    assert x.shape[1] % PROJ_ROWS == 0 and x.shape[2] == D_MODEL and norm1_g.shape[0] == 1
    return _layer(x, norm1_g[0], w_in[0], diff_q_g[0], diff_k_g[0], lambda_q1[0], lambda_k1[0], lambda_q2[0],
                  lambda_k2[0], diff_sub_g[0], moba_q_g[0], moba_k_g[0], rel_bias, w_out[0], norm2_g[0],
                  router_group[0], router_expert[0], w_gate[0], w_up[0], w_down[0])
```

```python
import functools
import math

import numpy as np
import jax
import jax.numpy as jnp
from jax import lax
from jax.experimental import pallas as pl
from jax.experimental.pallas import tpu as pltpu

D_MODEL = 1024
HEAD_DIM = 64
HEADS_DIFF = 4
HEADS_MOBA = 8
N_HEADS_TOTAL = HEADS_DIFF + HEADS_MOBA
DIFF_WIDTH = HEADS_DIFF * 2 * HEAD_DIM
MOBA_WIDTH = HEADS_MOBA * HEAD_DIM
QK_WIDTH = 2 * DIFF_WIDTH + 2 * MOBA_WIDTH
V_WIDTH = DIFF_WIDTH + MOBA_WIDTH
MOBA_TOPK = 3
NUM_BUCKETS = 32
MAX_DISTANCE = 2048
N_GROUPS = 4
EXPERTS_PER_GROUP = 8
N_EXPERTS = N_GROUPS * EXPERTS_PER_GROUP
EXPERTS_PER_TOKEN = 2
EXPERT_HIDDEN = 256
RMS_EPS = 1e-6
NEG_INF = -1e30
LAMBDA_INIT = 0.8 - 0.6 * math.exp(-0.3 * 0)

LANES = 128
ROW_TILE = D_MODEL // LANES
ATT_TILE = 256
ONES_ROWS = 16
LOGIT_LIMIT = 80.0
ROUNDING_MARGIN = 1.05
LOG2E = math.log2(math.e)
PROJ_ROWS = 1024
OUT_ROWS = 1024
ROUTE_TILE = 256
MOE_CHUNK = 512
MOE_TOKEN_PARTS = 2
EXPERT_ROW_PARTS = 4
MOVE_ROWS = 1024
FUSED_MOVE_ROWS = 256
SUBLANES = 8
ROUTER_ROWS = -(-(N_EXPERTS + N_GROUPS) // SUBLANES) * SUBLANES
V7X_VMEM_BYTES = 64 * 1024 * 1024
VMEM_LIMIT = V7X_VMEM_BYTES * 3 // 4

F32 = jnp.float32
BF16 = jnp.bfloat16
_NT = (((1,), (1,)), ((), ()))


def _t5_thresholds():
    n = np.arange(0, 1 << 16)
    max_exact = NUM_BUCKETS // 2
    nf = np.maximum(n, 1).astype(np.float32)
    large = max_exact + (np.log(nf / np.float32(max_exact)) / np.float32(math.log(MAX_DISTANCE / max_exact))
                         * np.float32(NUM_BUCKETS - max_exact)).astype(np.int32)
    bucket = np.where(n < max_exact, n, np.minimum(large, NUM_BUCKETS - 1))
    return [int(np.searchsorted(bucket, b, side="left")) for b in range(1, NUM_BUCKETS)]


_T5_THRESHOLDS = _t5_thresholds()


def _params(*sem):
    return pltpu.CompilerParams(dimension_semantics=sem, vmem_limit_bytes=VMEM_LIMIT)


def _bias_kernel(tab_ref, out_ref):
    h = pl.program_id(0)
    kj = lax.broadcasted_iota(jnp.int32, (ATT_TILE, ATT_TILE), 0)
    qi = lax.broadcasted_iota(jnp.int32, (ATT_TILE, ATT_TILE), 1)
    for d in range(out_ref.shape[0]):
        dist = d * ATT_TILE + qi - kj
        lo, hi = d * ATT_TILE - (ATT_TILE - 1), d * ATT_TILE + (ATT_TILE - 1)
        base = sum(thr <= max(lo, 0) for thr in _T5_THRESHOLDS)
        val = jnp.full((ATT_TILE, ATT_TILE), tab_ref[base, h], F32)
        for b, thr in enumerate(_T5_THRESHOLDS, start=1):
            if max(lo, 0) < thr <= hi:
                val = jnp.where(dist >= thr, tab_ref[b, h], val)
        val = val * LOG2E
        out_ref[d] = jnp.where(dist < 0, NEG_INF, val) if lo < 0 else val


def _bias_tiles(rel_bias, n_diag):
    return pl.pallas_call(
        _bias_kernel,
        grid=(N_HEADS_TOTAL,),
        in_specs=[pl.BlockSpec(memory_space=pltpu.SMEM)],
        out_specs=pl.BlockSpec((None, n_diag, ATT_TILE, ATT_TILE), lambda h: (h, 0, 0, 0)),
        out_shape=jax.ShapeDtypeStruct((N_HEADS_TOTAL, n_diag, ATT_TILE, ATT_TILE), F32),
        compiler_params=_params("arbitrary"),
        name="bias_tiles",
    )(rel_bias)


def _in_proj_kernel(x_ref, g1_ref, w_ref, pg_ref, grp_ref, qk_ref, vt_ref):
    n_row_tiles = PROJ_ROWS // ATT_TILE
    h = []
    for t in range(n_row_tiles):
        x = x_ref[t * ATT_TILE:(t + 1) * ATT_TILE, :]
        ms = jnp.mean(x * x, axis=-1, keepdims=True)
        h.append((x * lax.rsqrt(ms + RMS_EPS) * g1_ref[...]).astype(BF16))

    cw = 2 * LANES

    def finish(c, t, p):
        rows = slice(t * ATT_TILE, (t + 1) * ATT_TILE)
        cols = slice(c * cw, (c + 1) * cw)
        if c < QK_WIDTH // cw:
            msq = jnp.dot((p * p).astype(BF16), grp_ref[...], preferred_element_type=F32)
            qk_ref[rows, cols] = (p * lax.rsqrt(msq + RMS_EPS) * pg_ref[:, cols]).astype(BF16)
        else:
            vt_ref[t, c * cw - QK_WIDTH:(c + 1) * cw - QK_WIDTH, :] = p.T.astype(BF16)

    units = [(c, t) for t in range(n_row_tiles) for c in range((QK_WIDTH + V_WIDTH) // cw)]
    pending = None
    for c, t in units:
        p = jnp.dot(h[t], w_ref[:, c * cw:(c + 1) * cw], preferred_element_type=F32)
        if pending is not None:
            finish(*pending)
        pending = (c, t, p)
    finish(*pending)


def _in_proj(x2, g1, w_perm, post_gain, grp, batch, seq):
    n = x2.shape[0]
    steps_per_seq = seq // PROJ_ROWS
    tiles_per_step = PROJ_ROWS // ATT_TILE
    return pl.pallas_call(
        _in_proj_kernel,
        grid=(n // PROJ_ROWS,),
        in_specs=[
            pl.BlockSpec((PROJ_ROWS, D_MODEL), lambda i: (i, 0)),
            pl.BlockSpec((1, D_MODEL), lambda i: (0, 0)),
            pl.BlockSpec((D_MODEL, QK_WIDTH + V_WIDTH), lambda i: (0, 0)),
            pl.BlockSpec((1, QK_WIDTH), lambda i: (0, 0)),
            pl.BlockSpec((2 * LANES, 2 * LANES), lambda i: (0, 0)),
        ],
        out_specs=[
            pl.BlockSpec((PROJ_ROWS, QK_WIDTH), lambda i: (i, 0)),
            pl.BlockSpec((None, tiles_per_step, V_WIDTH, ATT_TILE),
                         lambda i: (i // steps_per_seq, i % steps_per_seq, 0, 0)),
        ],
        out_shape=[
            jax.ShapeDtypeStruct((n, QK_WIDTH), BF16),
            jax.ShapeDtypeStruct((batch, seq // ATT_TILE, V_WIDTH, ATT_TILE), BF16),
        ],
        compiler_params=_params("arbitrary"),
        name="in_proj",
    )(x2, g1, w_perm, post_gain, grp)


def _flash_tiles(qi, n_chains, logits_fn, values_fn, bias_fn, mask_fn, s_ref, p_ref, acc_ref):
    for c in range(n_chains):
        s_ref[0, c] = logits_fn(c, qi)
    p_ref[...] = jnp.zeros_like(p_ref)
    acc_ref[...] = jnp.zeros_like(acc_ref)

    def add_values(c, kv, alpha):
        acc_ref[c] = acc_ref[c] * alpha + jnp.dot(values_fn(c, kv), p_ref[c], preferred_element_type=F32)

    def step(j, carry):
        cur = lax.rem(j, 2)
        kv = qi - j
        out = []
        for c in range(n_chains):
            add_values(c, jnp.minimum(kv + 1, qi), carry[2 * c + 1])
        for c in range(n_chains):
            s = s_ref[cur, c] + bias_fn(c, j)
            tile_max = jnp.max(s, axis=0, keepdims=True)
            shift = mask_fn(c, kv)
            if shift is not None:
                tile_max = tile_max + shift
            m_new = jnp.maximum(carry[2 * c], tile_max)
            alpha = jnp.exp2(carry[2 * c] - m_new)
            p_ref[c] = jnp.exp2(s - (m_new if shift is None else m_new - shift)).astype(BF16)
            out += [m_new, alpha]
        for c in range(n_chains):
            s_ref[1 - cur, c] = logits_fn(c, jnp.maximum(kv - 1, 0))
        return tuple(out)

    m0 = jnp.full((1, ATT_TILE), NEG_INF, F32)
    one = jnp.ones((1, ATT_TILE), F32)
    carry = lax.fori_loop(0, qi + 1, step, (m0, one) * n_chains)
    for c in range(n_chains):
        add_values(c, 0, carry[2 * c + 1])


def _bounded_tiles(n_tiles, n_chains, logits_fn, values_fn, bias_fn, keep_fn, finish_fn, p_ref, acc_ref,
                   between_fn=lambda d: None):
    for d in range(n_tiles):
        for qi in range(d, n_tiles):
            for c in range(n_chains):
                p_ref[qi, c] = jnp.exp2(logits_fn(c, qi, qi - d) + bias_fn(c, d)).astype(BF16)
        between_fn(d)
        for qi in range(d, n_tiles):
            kv = qi - d
            for c in range(n_chains):
                pv = jnp.dot(values_fn(c, kv), p_ref[qi, c], preferred_element_type=F32)
                keep = keep_fn(c, qi, kv)
                if keep is not None:
                    pv = pv * keep
                acc_ref[qi, c] = pv if d == 0 else acc_ref[qi, c] + pv
        finish_fn(d)


def _logits_bounded(q_gain, k_gain, bias_cols):
    bound = (HEAD_DIM * jnp.max(jnp.abs(q_gain)) * jnp.max(jnp.abs(k_gain)) * ROUNDING_MARGIN
             + jnp.max(jnp.abs(bias_cols)) * LOG2E)
    return (bound <= LOGIT_LIMIT).astype(jnp.int32).reshape(1)


def _with_ones_rows(vt):
    return jnp.concatenate([vt, jnp.ones((ONES_ROWS, vt.shape[1]), vt.dtype)], axis=0)


def _attention_scratch(n_tiles, value_rows, n_chains=2):
    return [pltpu.VMEM((n_tiles, n_chains, ATT_TILE, LANES), BF16),
            pltpu.VMEM((2, n_chains, ATT_TILE, ATT_TILE), F32),
            pltpu.VMEM((n_tiles, n_chains, ATT_TILE, ATT_TILE), BF16),
            pltpu.VMEM((n_tiles, n_chains, value_rows + ONES_ROWS, ATT_TILE), F32)]


def _half_lane_split(q):
    lane = lax.broadcasted_iota(jnp.int32, q.shape, 1)
    zero = jnp.zeros_like(q)
    return jnp.where(lane < HEAD_DIM, q, zero), jnp.where(lane >= HEAD_DIM, q, zero)


def _diff_kernel(bounded_ref, lq1_ref, lk1_ref, lq2_ref, lk2_ref, subg_ref, q_ref, k_ref, vt_ref, bias_ref, o_ref,
                 qm_ref, s_ref, p_ref, acc_ref):
    n_tiles = q_ref.shape[0]
    chains = dict(n_chains=2, values_fn=lambda c, kv: _with_ones_rows(vt_ref[kv]))
    width = 2 * HEAD_DIM

    def prepare():
        for qi in range(n_tiles):
            qm_ref[qi, 0], qm_ref[qi, 1] = _half_lane_split(q_ref[qi])

    def finish(qi):
        lam = (jnp.exp(jnp.sum(lq1_ref[...] * lk1_ref[...], keepdims=True))
               - jnp.exp(jnp.sum(lq2_ref[...] * lk2_ref[...], keepdims=True)) + LAMBDA_INIT)
        a1, a2 = acc_ref[qi, 0], acc_ref[qi, 1]
        o = a1[:width] / a1[width:width + 1] - lam * (a2[:width] / a2[width:width + 1])
        ms = jnp.mean(o * o, axis=0, keepdims=True)
        o = o * lax.rsqrt(ms + RMS_EPS) * subg_ref[...] * (1.0 - LAMBDA_INIT)
        o_ref[qi] = o.T.astype(BF16)

    bounded = bounded_ref[0] > 0

    @pl.when(bounded)
    def _():
        prepare()
        _bounded_tiles(
            n_tiles,
            logits_fn=lambda c, qi, kv: lax.dot_general(k_ref[kv], qm_ref[qi, c], _NT, preferred_element_type=F32),
            bias_fn=lambda c, d: bias_ref[d],
            keep_fn=lambda c, qi, kv: None,
            finish_fn=finish,
            p_ref=p_ref, acc_ref=acc_ref, **chains)

    @pl.when(jnp.logical_not(bounded))
    def _():
        prepare()

        def q_tile(qi, carry):
            _flash_tiles(
                qi,
                logits_fn=lambda c, kv: lax.dot_general(k_ref[kv], qm_ref[qi, c], _NT, preferred_element_type=F32),
                bias_fn=lambda c, j: bias_ref[j],
                mask_fn=lambda c, kv: None,
                s_ref=s_ref, p_ref=p_ref.at[0], acc_ref=acc_ref.at[qi], **chains)
            return carry
        lax.fori_loop(0, n_tiles, q_tile, 0)
        for qi in range(n_tiles):
            finish(qi)


def _diff_attention(bounded, lam_vecs, sub_g_col, qk4, vt4, bias):
    batch, n_tiles = qk4.shape[0], qk4.shape[1]
    k_col0 = DIFF_WIDTH // LANES
    vec = pl.BlockSpec((1, HEAD_DIM), lambda h, b: (0, 0))
    return pl.pallas_call(
        _diff_kernel,
        grid=(HEADS_DIFF, batch),
        in_specs=[
            pl.BlockSpec(memory_space=pltpu.SMEM),
            vec, vec, vec, vec,
            pl.BlockSpec((2 * HEAD_DIM, 1), lambda h, b: (0, 0)),
            pl.BlockSpec((None, n_tiles, ATT_TILE, LANES), lambda h, b: (b, 0, 0, h)),
            pl.BlockSpec((None, n_tiles, ATT_TILE, LANES), lambda h, b: (b, 0, 0, k_col0 + h)),
            pl.BlockSpec((None, n_tiles, LANES, ATT_TILE), lambda h, b: (b, 0, h, 0)),
            pl.BlockSpec((None, n_tiles, ATT_TILE, ATT_TILE), lambda h, b: (h, 0, 0, 0)),
        ],
        out_specs=pl.BlockSpec((None, n_tiles, ATT_TILE, LANES), lambda h, b: (b, 0, 0, h)),
        out_shape=jax.ShapeDtypeStruct((batch, n_tiles, ATT_TILE, DIFF_WIDTH), BF16),
        scratch_shapes=_attention_scratch(n_tiles, 2 * HEAD_DIM),
        compiler_params=_params("arbitrary", "arbitrary"),
        name="diff_attention",
    )(bounded, *lam_vecs, sub_g_col, qk4, qk4, vt4, bias)


def _split3(v):
    hi = v.astype(BF16)
    r1 = v - hi.astype(F32)
    mid = r1.astype(BF16)
    lo = (r1 - mid.astype(F32)).astype(BF16)
    return hi, mid, lo


def _block_mask(gate, own):
    row = lax.broadcasted_iota(jnp.int32, gate.shape, 0)
    rank = jnp.zeros(gate.shape, jnp.int32)
    for m in range(own):
        gm = gate[m:m + 1, :]
        beats = (gm > gate) | ((gm == gate) & (row > m))
        rank = rank + jnp.where(beats, 1, 0)
    keep = ((rank < MOBA_TOPK) & (row < own)) | (row == own)
    return jnp.where(keep, 0.0, NEG_INF).astype(F32)


def _moba_kernel(bounded_ref, q_ref, k_ref, vt_ref, bias_ref, o_ref, kmean_ref, mask_ref, qm_ref, s_ref, p_ref,
                 acc_ref):
    n_tiles = q_ref.shape[0]

    def split_queries():
        for qi in range(n_tiles):
            qm_ref[qi, 0], qm_ref[qi, 1] = _half_lane_split(q_ref[qi])

    def block_masks(q_tiles):
        for n in range(n_tiles):
            kmean_ref[n:n + 1, :] = jnp.mean(k_ref[n].astype(F32), axis=0, keepdims=True)
        per_head = zip(*[_half_lane_split(term) for term in _split3(kmean_ref[...])])
        gate_lhs = jnp.concatenate([term for head_terms in per_head for term in head_terms], axis=0)
        n_terms = gate_lhs.shape[0] // (2 * n_tiles)
        for qi in q_tiles:
            terms = lax.dot_general(gate_lhs, q_ref[qi], _NT, preferred_element_type=F32)
            for half in range(2):
                rows = [terms[(half * n_terms + t) * n_tiles:(half * n_terms + t + 1) * n_tiles]
                        for t in range(n_terms)]
                mask_ref[qi, half] = _block_mask(sum(rows[1:], rows[0]), qi)

    def keep_row(c, qi, kv):
        if qi <= MOBA_TOPK or kv == qi:
            return None
        return jnp.where(mask_ref[qi, c, kv:kv + 1, :] < -1.0, 0.0, 1.0)

    def finish(qi):
        halves = [acc_ref[qi, half] for half in range(2)]
        o = jnp.concatenate([a[:HEAD_DIM] / a[HEAD_DIM:HEAD_DIM + 1] for a in halves], axis=0)
        o_ref[qi] = o.T.astype(BF16)

    chains = dict(
        n_chains=2,
        values_fn=lambda c, kv: _with_ones_rows(vt_ref[kv, c * HEAD_DIM:(c + 1) * HEAD_DIM, :]))
    bounded = bounded_ref[0] > 0

    @pl.when(bounded)
    def _():
        split_queries()
        _bounded_tiles(
            n_tiles,
            logits_fn=lambda c, qi, kv: lax.dot_general(k_ref[kv], qm_ref[qi, c], _NT, preferred_element_type=F32),
            bias_fn=lambda c, d: bias_ref[c, d],
            keep_fn=keep_row,
            between_fn=lambda d: block_masks(range(MOBA_TOPK + 1, n_tiles)) if d == 0 else None,
            finish_fn=finish,
            p_ref=p_ref, acc_ref=acc_ref, **chains)

    @pl.when(jnp.logical_not(bounded))
    def _():
        split_queries()
        block_masks(range(n_tiles))

        def q_tile(qi, carry):
            _flash_tiles(
                qi,
                logits_fn=lambda c, kv: lax.dot_general(k_ref[kv], qm_ref[qi, c], _NT, preferred_element_type=F32),
                bias_fn=lambda c, j: bias_ref[c, j],
                mask_fn=lambda c, kv: mask_ref[qi, c, pl.ds(kv, 1), :],
                s_ref=s_ref, p_ref=p_ref.at[0], acc_ref=acc_ref.at[qi], **chains)
            return carry
        lax.fori_loop(0, n_tiles, q_tile, 0)
        for qi in range(n_tiles):
            finish(qi)


def _moba_attention(bounded, qk4, vt4, bias_pairs):
    batch, n_tiles = qk4.shape[0], qk4.shape[1]
    q_col0 = 2 * DIFF_WIDTH // LANES
    k_col0 = q_col0 + MOBA_WIDTH // LANES
    v_row0 = DIFF_WIDTH // LANES
    pair0 = HEADS_DIFF // 2
    return pl.pallas_call(
        _moba_kernel,
        grid=(HEADS_MOBA // 2, batch),
        in_specs=[
            pl.BlockSpec(memory_space=pltpu.SMEM),
            pl.BlockSpec((None, n_tiles, ATT_TILE, LANES), lambda h, b: (b, 0, 0, q_col0 + h)),
            pl.BlockSpec((None, n_tiles, ATT_TILE, LANES), lambda h, b: (b, 0, 0, k_col0 + h)),
            pl.BlockSpec((None, n_tiles, LANES, ATT_TILE), lambda h, b: (b, 0, v_row0 + h, 0)),
            pl.BlockSpec((None, 2, n_tiles, ATT_TILE, ATT_TILE), lambda h, b: (pair0 + h, 0, 0, 0, 0)),
        ],
        out_specs=pl.BlockSpec((None, n_tiles, ATT_TILE, LANES), lambda h, b: (b, 0, 0, h)),
        out_shape=jax.ShapeDtypeStruct((batch, n_tiles, ATT_TILE, MOBA_WIDTH), BF16),
        scratch_shapes=[
            pltpu.VMEM((n_tiles, LANES), F32),
            pltpu.VMEM((n_tiles, 2, n_tiles, ATT_TILE), F32),
        ] + _attention_scratch(n_tiles, HEAD_DIM),
        compiler_params=_params("arbitrary", "arbitrary"),
        name="moba_attention",
    )(bounded, qk4, qk4, vt4, bias_pairs)


def _load_token_rows(ref, n_rows, lead=(), first=0):
    chunks = [ref[lead + (pl.ds(first * ROW_TILE + c, n_rows, stride=ROW_TILE), slice(None))]
              for c in range(ROW_TILE)]
    return jnp.concatenate(chunks, axis=1)


def _store_token_rows(ref, value, first=0):
    n_rows = value.shape[0]
    for c in range(ROW_TILE):
        ref[pl.ds(first * ROW_TILE + c, n_rows, stride=ROW_TILE), :] = value[:, c * LANES:(c + 1) * LANES]


def _token_tile(ref, t):
    return ref.at[pl.ds(pl.multiple_of(t * ROW_TILE, ROW_TILE), ROW_TILE)]


def _first_argmax(v):
    top = jnp.max(v, axis=0, keepdims=True)
    row = lax.broadcasted_iota(jnp.int32, v.shape, 0)
    idx = jnp.min(jnp.where(v == top, row, v.shape[0]), axis=0, keepdims=True)
    return top, idx


def _out_proj_kernel(yd_ref, ym_ref, x_ref, wd_ref, wm_ref, g2_ref, wr_ref, tri_ref,
                     x1_ref, ri_ref, rf_ref, cnt_ref, run_ref):
    @pl.when(lax.rem(pl.program_id(0), pl.num_programs(0) // MOE_TOKEN_PARTS) == 0)
    def _():
        run_ref[...] = jnp.zeros_like(run_ref)

    def project(t):
        rows = slice(t * ROUTE_TILE, (t + 1) * ROUTE_TILE)
        x1 = (x_ref[rows, :] + jnp.dot(yd_ref[rows, :], wd_ref[...], preferred_element_type=F32)
              + jnp.dot(ym_ref[rows, :], wm_ref[...], preferred_element_type=F32))
        _store_token_rows(x1_ref, x1, first=t * ROUTE_TILE)
        ms = jnp.mean(x1 * x1, axis=-1, keepdims=True)
        return (x1 * lax.rsqrt(ms + RMS_EPS) * g2_ref[...]).astype(BF16)

    def pick(h2):
        lg = lax.dot_general(wr_ref[...], h2, _NT, preferred_element_type=F32)
        g_logits = lg[N_EXPERTS:N_EXPERTS + N_GROUPS, :]
        g_top, g_idx = _first_argmax(g_logits)
        p_group = 1.0 / jnp.sum(jnp.exp(g_logits - g_top), axis=0, keepdims=True)
        e_logits = lg[0:EXPERTS_PER_GROUP, :]
        for g in range(1, N_GROUPS):
            e_logits = jnp.where(g_idx == g, lg[g * EXPERTS_PER_GROUP:(g + 1) * EXPERTS_PER_GROUP, :], e_logits)
        v1, i1 = _first_argmax(e_logits)
        row = lax.broadcasted_iota(jnp.int32, e_logits.shape, 0)
        v2, i2 = _first_argmax(jnp.where(row == i1, -jnp.inf, e_logits))
        ratio = jnp.exp(v2 - v1)
        w1 = p_group / (1.0 + ratio)
        w2 = p_group * ratio / (1.0 + ratio)
        e1 = g_idx * EXPERTS_PER_GROUP + i1
        e2 = g_idx * EXPERTS_PER_GROUP + i2
        erow = lax.broadcasted_iota(jnp.int32, (N_EXPERTS, ROUTE_TILE), 0)
        hit1 = erow == e1
        hit2 = erow == e2
        onehot = jnp.where(hit1 | hit2, 1.0, 0.0).astype(F32)
        return e1, e2, w1, w2, hit1, hit2, onehot

    def place(t, e1, e2, w1, w2, hit1, hit2, onehot):
        before = jnp.dot(onehot.astype(BF16), tri_ref[...], preferred_element_type=F32) + run_ref[...]
        r1 = jnp.sum(jnp.where(hit1, before, 0.0), axis=0, keepdims=True).astype(jnp.int32)
        r2 = jnp.sum(jnp.where(hit2, before, 0.0), axis=0, keepdims=True).astype(jnp.int32)
        run_ref[...] = run_ref[...] + jnp.sum(onehot, axis=1, keepdims=True)
        lanes = slice(t * ROUTE_TILE, (t + 1) * ROUTE_TILE)
        out_row = lax.broadcasted_iota(jnp.int32, (SUBLANES, ROUTE_TILE), 0)
        ri_ref[:, lanes] = jnp.where(out_row == 0, e1, jnp.where(out_row == 1, e2,
                                     jnp.where(out_row == 2, r1, jnp.where(out_row == 3, r2, 0))))
        rf_ref[:, lanes] = jnp.where(out_row == 0, w1, jnp.where(out_row == 1, w2, 0.0))

    n_tiles = OUT_ROWS // ROUTE_TILE
    projected, picked = {}, {}
    for t in range(n_tiles + 2):
        if t < n_tiles:
            projected[t] = project(t)
        if 0 <= t - 1 < n_tiles:
            picked[t - 1] = pick(projected.pop(t - 1))
        if 0 <= t - 2 < n_tiles:
            place(t - 2, *picked.pop(t - 2))
    cnt_ref[...] = jnp.broadcast_to(run_ref[...], cnt_ref.shape).astype(jnp.int32)


def _out_proj(y_d, y_m, x2, wo_d, wo_m, g2, w_router, tri):
    n = x2.shape[0]
    const = lambda i: (0, 0)
    steps_per_part = n // OUT_ROWS // MOE_TOKEN_PARTS
    return pl.pallas_call(
        _out_proj_kernel,
        grid=(n // OUT_ROWS,),
        in_specs=[
            pl.BlockSpec((OUT_ROWS, DIFF_WIDTH), lambda i: (i, 0)),
            pl.BlockSpec((OUT_ROWS, MOBA_WIDTH), lambda i: (i, 0)),
            pl.BlockSpec((OUT_ROWS, D_MODEL), lambda i: (i, 0)),
            pl.BlockSpec((DIFF_WIDTH, D_MODEL), const),
            pl.BlockSpec((MOBA_WIDTH, D_MODEL), const),
            pl.BlockSpec((1, D_MODEL), const),
            pl.BlockSpec((ROUTER_ROWS, D_MODEL), const),
            pl.BlockSpec((ROUTE_TILE, ROUTE_TILE), const),
        ],
        out_specs=[
            pl.BlockSpec((OUT_ROWS * ROW_TILE, LANES), lambda i: (i, 0)),
            pl.BlockSpec((SUBLANES, OUT_ROWS), lambda i: (0, i)),
            pl.BlockSpec((SUBLANES, OUT_ROWS), lambda i: (0, i)),
            pl.BlockSpec((None, N_EXPERTS, LANES), lambda i: (i // steps_per_part, 0, 0)),
        ],
        out_shape=[
            jax.ShapeDtypeStruct((n * ROW_TILE, LANES), F32),
            jax.ShapeDtypeStruct((SUBLANES, n), jnp.int32),
            jax.ShapeDtypeStruct((SUBLANES, n), F32),
            jax.ShapeDtypeStruct((MOE_TOKEN_PARTS, N_EXPERTS, LANES), jnp.int32),
        ],
        scratch_shapes=[pltpu.VMEM((N_EXPERTS, 1), F32)],
        compiler_params=_params("arbitrary"),
        name="out_proj_route",
    )(y_d, y_m, x2, wo_d, wo_m, g2, w_router, tri)


def _dispatch_step(i, n_steps, first_tile, zchunk_ref, dest_ref, x1_hbm, xs_hbm, zero_ref, stage_ref, zero_sem,
                   load_sem, row_sem):
    last = n_steps - 1
    slot = lax.rem(i, 2)
    chunk_rows = MOE_CHUNK * ROW_TILE
    tile_tokens = dest_ref.shape[2]
    tile_rows = tile_tokens * ROW_TILE

    def load(t, s):
        start = pl.multiple_of((first_tile + t) * tile_rows, tile_rows)
        return pltpu.make_async_copy(x1_hbm.at[pl.ds(start, tile_rows)], stage_ref.at[s], load_sem.at[s])

    def wait_rows(s):
        for _ in range(EXPERTS_PER_TOKEN):
            pltpu.make_async_copy(stage_ref.at[s], xs_hbm.at[pl.ds(0, tile_rows)], row_sem.at[s]).wait()

    def zero_copy(c):
        start = pl.multiple_of(c * chunk_rows, chunk_rows)
        return pltpu.make_async_copy(zero_ref, xs_hbm.at[pl.ds(start, chunk_rows)], zero_sem)

    @pl.when(i == 0)
    def _():
        zero_ref[...] = jnp.zeros_like(zero_ref)

        def start_one(c, carry):
            @pl.when(zchunk_ref[c] > 0)
            def _():
                zero_copy(c).start()
            return carry

        def wait_one(c, carry):
            @pl.when(zchunk_ref[c] > 0)
            def _():
                zero_copy(c).wait()
            return carry

        lax.fori_loop(0, zchunk_ref.shape[0], start_one, 0)
        lax.fori_loop(0, zchunk_ref.shape[0], wait_one, 0)
        load(0, 0).start()

    @pl.when(i > 0)
    def _():
        wait_rows(1 - slot)

    @pl.when(i < last)
    def _():
        load(i + 1, 1 - slot).start()

    load(i, slot).wait()

    def send_row(r, carry):
        src = _token_tile(stage_ref.at[slot], r)
        for k in range(EXPERTS_PER_TOKEN):
            pltpu.make_async_copy(src, _token_tile(xs_hbm, dest_ref[0, k, r]), row_sem.at[slot]).start(priority=k)
        return carry

    lax.fori_loop(0, tile_tokens, send_row, 0, unroll=8)

    @pl.when(i == last)
    def _():
        wait_rows(slot)


def _dispatch_scratch(tile_tokens):
    return [pltpu.VMEM((MOE_CHUNK * ROW_TILE, LANES), F32),
            pltpu.VMEM((2, tile_tokens * ROW_TILE, LANES), F32),
            pltpu.SemaphoreType.DMA(()),
            pltpu.SemaphoreType.DMA((2,)),
            pltpu.SemaphoreType.DMA((2,))]


def _dispatch_kernel(first_tile, zchunk_ref, dest_ref, x1_hbm, xs_hbm, *scratch):
    _dispatch_step(pl.program_id(0), pl.num_programs(0), first_tile, zchunk_ref, dest_ref, x1_hbm, xs_hbm, *scratch)


def _dispatch(part, zchunk, dest3, x1, n_slots):
    tile_tokens = dest3.shape[2]
    tiles = dest3.shape[0] // MOE_TOKEN_PARTS
    first_tile = part * tiles
    grid_spec = pltpu.PrefetchScalarGridSpec(
        num_scalar_prefetch=1,
        grid=(tiles,),
        in_specs=[
            pl.BlockSpec((1, EXPERTS_PER_TOKEN, tile_tokens), lambda i, zc: (first_tile + i, 0, 0),
                         memory_space=pltpu.SMEM),
            pl.BlockSpec(memory_space=pl.ANY),
        ],
        out_specs=pl.BlockSpec(memory_space=pl.ANY),
        scratch_shapes=_dispatch_scratch(tile_tokens),
    )
    return pl.pallas_call(
        functools.partial(_dispatch_kernel, first_tile),
        grid_spec=grid_spec,
        out_shape=jax.ShapeDtypeStruct((n_slots * ROW_TILE, LANES), F32),
        compiler_params=_params("arbitrary"),
        name="moe_dispatch",
    )(zchunk, dest3, x1)


def _expert_step(c, na_ref, xs_ref, g2_ref, wg_ref, wu_ref, wd_ref, o_ref):
    active = c < na_ref[0]

    @pl.when(jnp.logical_not(active))
    def _():
        o_ref[...] = jnp.zeros_like(o_ref)

    @pl.when(active)
    def _():
        part_rows = MOE_CHUNK // EXPERT_ROW_PARTS

        def gate_up(r):
            x = _load_token_rows(xs_ref, part_rows, first=r * part_rows)
            ms = jnp.mean(x * x, axis=-1, keepdims=True)
            h = (x * lax.rsqrt(ms + RMS_EPS) * g2_ref[...]).astype(BF16)
            return (jnp.dot(h, wg_ref[...], preferred_element_type=F32),
                    jnp.dot(h, wu_ref[...], preferred_element_type=F32))

        def down(r, gate, up):
            hid = (gate * jax.nn.sigmoid(gate) * up).astype(BF16)
            _store_token_rows(o_ref, jnp.dot(hid, wd_ref[...], preferred_element_type=F32), first=r * part_rows)

        pending = None
        for r in range(EXPERT_ROW_PARTS):
            current = (r,) + gate_up(r)
            if pending is not None:
                down(*pending)
            pending = current
        down(*pending)


def _expert_kernel(ce_ref, na_ref, *refs):
    _expert_step(pl.program_id(0), na_ref, *refs)


def _expert_specs():
    rows = lambda c, ce, na, *_: (jnp.minimum(c, na[0] - 1), 0)
    expert = lambda c, ce, na, *_: (ce[c], 0, 0)
    in_specs = [
        pl.BlockSpec((MOE_CHUNK * ROW_TILE, LANES), rows),
        pl.BlockSpec((1, D_MODEL), lambda c, *_: (0, 0)),
        pl.BlockSpec((None, D_MODEL, EXPERT_HIDDEN), expert),
        pl.BlockSpec((None, D_MODEL, EXPERT_HIDDEN), expert),
        pl.BlockSpec((None, EXPERT_HIDDEN, D_MODEL), expert),
    ]
    return in_specs, pl.BlockSpec((MOE_CHUNK * ROW_TILE, LANES), lambda c, *_: (c, 0))


def _experts(chunk_e, n_active, xs, g2, wg, wu, wd):
    n_slots = xs.shape[0] // ROW_TILE
    in_specs, out_spec = _expert_specs()
    grid_spec = pltpu.PrefetchScalarGridSpec(
        num_scalar_prefetch=2, grid=(n_slots // MOE_CHUNK,), in_specs=in_specs, out_specs=out_spec)
    return pl.pallas_call(
        _expert_kernel,
        grid_spec=grid_spec,
        out_shape=jax.ShapeDtypeStruct((n_slots * ROW_TILE, LANES), F32),
        compiler_params=_params("arbitrary"),
        name="moe_experts",
    )(chunk_e, n_active, xs, g2, wg, wu, wd)


def _experts_dispatch_kernel(first_tile, n_tiles, ce_ref, na_ref, zchunk_ref,
                             xs_ref, g2_ref, wg_ref, wu_ref, wd_ref, dest_ref, x1_hbm,
                             o_ref, xs_next_hbm, *scratch):
    i = pl.program_id(0)

    @pl.when(i < n_tiles)
    def _():
        _dispatch_step(i, n_tiles, first_tile, zchunk_ref, dest_ref, x1_hbm, xs_next_hbm, *scratch)

    _expert_step(i, na_ref, xs_ref, g2_ref, wg_ref, wu_ref, wd_ref, o_ref)


def _experts_dispatch(chunk_e, n_active, zchunk_next, xs, g2, wg, wu, wd, next_part, dest3, x1):
    n_slots = xs.shape[0] // ROW_TILE
    n_chunks = n_slots // MOE_CHUNK
    tile_tokens = dest3.shape[2]
    tiles = dest3.shape[0] // MOE_TOKEN_PARTS
    first_tile = next_part * tiles
    assert n_chunks >= tiles
    in_specs, out_spec = _expert_specs()
    in_specs += [
        pl.BlockSpec((1, EXPERTS_PER_TOKEN, tile_tokens),
                     lambda c, *_: (first_tile + jnp.minimum(c, tiles - 1), 0, 0), memory_space=pltpu.SMEM),
        pl.BlockSpec(memory_space=pl.ANY),
    ]
    grid_spec = pltpu.PrefetchScalarGridSpec(
        num_scalar_prefetch=3, grid=(n_chunks,), in_specs=in_specs,
        out_specs=[out_spec, pl.BlockSpec(memory_space=pl.ANY)],
        scratch_shapes=_dispatch_scratch(tile_tokens))
    slots = jax.ShapeDtypeStruct((n_slots * ROW_TILE, LANES), F32)
    return pl.pallas_call(
        functools.partial(_experts_dispatch_kernel, first_tile, tiles),
        grid_spec=grid_spec,
        out_shape=[slots, slots],
        compiler_params=_params("arbitrary"),
        name="moe_experts_dispatch",
    )(chunk_e, n_active, zchunk_next, xs, g2, wg, wu, wd, dest3, x1)


def _combine_kernel(dest_ref, dest_next_ref, w_ref, x1_ref, *rest):
    outs_hbm, (y_ref, buf_ref, sem_ref) = rest[:MOE_TOKEN_PARTS], rest[MOE_TOKEN_PARTS:]
    i = pl.program_id(0)
    slot = lax.rem(i, 2)
    tiles_per_part = pl.num_programs(0) // MOE_TOKEN_PARTS

    def fetch(tile, dref, s):
        for part, src_hbm in enumerate(outs_hbm):
            @pl.when(tile // tiles_per_part == part)
            def _():
                def fetch_row(r, carry):
                    for k in range(EXPERTS_PER_TOKEN):
                        pltpu.make_async_copy(_token_tile(src_hbm, dref[0, k, r]),
                                              _token_tile(buf_ref.at[s, k], r), sem_ref.at[s]).start(priority=k)
                    return carry
                lax.fori_loop(0, MOVE_ROWS, fetch_row, 0, unroll=8)

    @pl.when(i == 0)
    def _():
        fetch(i, dest_ref, 0)

    @pl.when(i + 1 < pl.num_programs(0))
    def _():
        fetch(i + 1, dest_next_ref, 1 - slot)

    for k in range(EXPERTS_PER_TOKEN):
        pltpu.make_async_copy(outs_hbm[0].at[pl.ds(0, MOVE_ROWS * ROW_TILE)], buf_ref.at[slot, k],
                              sem_ref.at[slot]).wait()
    w = w_ref[...]
    y_ref[...] = (_load_token_rows(x1_ref, MOVE_ROWS)
                  + w[:, 0:1] * _load_token_rows(buf_ref, MOVE_ROWS, (slot, 0))
                  + w[:, 1:2] * _load_token_rows(buf_ref, MOVE_ROWS, (slot, 1)))


def _combine(dest3, wts, x1, outs_parts):
    n = x1.shape[0] // ROW_TILE
    steps = n // MOVE_ROWS
    dest_block = (1, EXPERTS_PER_TOKEN, MOVE_ROWS)
    return pl.pallas_call(
        _combine_kernel,
        grid=(steps,),
        in_specs=[
            pl.BlockSpec(dest_block, lambda i: (i, 0, 0), memory_space=pltpu.SMEM),
            pl.BlockSpec(dest_block, lambda i: (jnp.minimum(i + 1, steps - 1), 0, 0), memory_space=pltpu.SMEM),
            pl.BlockSpec((MOVE_ROWS, EXPERTS_PER_TOKEN), lambda i: (i, 0)),
            pl.BlockSpec((MOVE_ROWS * ROW_TILE, LANES), lambda i: (i, 0)),
        ] + [pl.BlockSpec(memory_space=pl.ANY)] * MOE_TOKEN_PARTS,
        out_specs=pl.BlockSpec((MOVE_ROWS, D_MODEL), lambda i: (i, 0)),
        out_shape=jax.ShapeDtypeStruct((n, D_MODEL), F32),
        scratch_shapes=[
            pltpu.VMEM((2, EXPERTS_PER_TOKEN, MOVE_ROWS * ROW_TILE, LANES), F32),
            pltpu.SemaphoreType.DMA((2,)),
        ],
        compiler_params=_params("arbitrary"),
        name="moe_combine",
    )(dest3, dest3, wts, x1, *outs_parts)


def _layer(x, norm1_g, w_in, diff_q_g, diff_k_g, lambda_q1, lambda_k1, lambda_q2, lambda_k2, diff_sub_g,
           moba_q_g, moba_k_g, rel_bias, w_out, norm2_g, router_group, router_expert, w_gate, w_up, w_down):
    batch, seq, _ = x.shape
    n = batch * seq
    n_tiles = seq // ATT_TILE
    scale = HEAD_DIM ** -0.5 * LOG2E

    d3 = 3 * DIFF_WIDTH
    w_perm = jnp.concatenate([w_in[:, :2 * DIFF_WIDTH], w_in[:, d3:d3 + 2 * MOBA_WIDTH],
                              w_in[:, 2 * DIFF_WIDTH:d3], w_in[:, d3 + 2 * MOBA_WIDTH:]], axis=1).astype(BF16)
    reps_d, reps_m = DIFF_WIDTH // HEAD_DIM, MOBA_WIDTH // HEAD_DIM
    post_gain = jnp.concatenate([jnp.tile(diff_q_g * scale, reps_d), jnp.tile(diff_k_g, reps_d),
                                 jnp.tile(moba_q_g * scale, reps_m), jnp.tile(moba_k_g, reps_m)])[None, :]
    head_of = np.arange(2 * LANES) // HEAD_DIM
    grp = jnp.asarray((head_of[:, None] == head_of[None, :]) / HEAD_DIM, BF16)

    bias = _bias_tiles(rel_bias, n_tiles)
    qk, vt4 = _in_proj(x.reshape(n, D_MODEL), norm1_g[None, :], w_perm, post_gain, grp, batch, seq)
    qk4 = qk.reshape(batch, n_tiles, ATT_TILE, QK_WIDTH)
    lam_vecs = [v[None, :] for v in (lambda_q1, lambda_k1, lambda_q2, lambda_k2)]
    y_d = _diff_attention(_logits_bounded(diff_q_g * scale, diff_k_g, rel_bias[:, :HEADS_DIFF]),
                          lam_vecs, diff_sub_g[:, None], qk4, vt4, bias)
    y_m = _moba_attention(_logits_bounded(moba_q_g * scale, moba_k_g, rel_bias[:, HEADS_DIFF:]),
                          qk4, vt4, bias.reshape(N_HEADS_TOTAL // 2, 2, n_tiles, ATT_TILE, ATT_TILE))

    w_router = jnp.concatenate([
        router_expert.transpose(0, 2, 1).reshape(N_EXPERTS, D_MODEL), router_group.T,
        jnp.zeros((ROUTER_ROWS - N_EXPERTS - N_GROUPS, D_MODEL), F32)], axis=0).astype(BF16)
    tri = jnp.asarray(np.triu(np.ones((ROUTE_TILE, ROUTE_TILE)), k=1), BF16)
    wo = w_out.astype(BF16)
    x1, route_i, route_f, cnt = _out_proj(y_d.reshape(n, DIFF_WIDTH), y_m.reshape(n, MOBA_WIDTH),
                                          x.reshape(n, D_MODEL), wo[:DIFF_WIDTH], wo[DIFF_WIDTH:],
                                          norm2_g[None, :], w_router, tri)

    parts = MOE_TOKEN_PARTS
    counts = cnt[:, :, 0]
    pad_counts = ((counts + MOE_CHUNK - 1) // MOE_CHUNK) * MOE_CHUNK
    pad_end = jnp.cumsum(pad_counts, axis=1)
    pad_start = pad_end - pad_counts
    experts = jnp.arange(N_EXPERTS, dtype=jnp.int32)
    picked = route_i[0:2].reshape(EXPERTS_PER_TOKEN, parts, n // parts)
    start_of = jnp.sum(jnp.where(picked[..., None] == experts, pad_start[None, :, None, :], 0), axis=-1)
    dest = start_of.reshape(EXPERTS_PER_TOKEN, n) + route_i[2:4]
    by_tile = lambda rows: dest.reshape(EXPERTS_PER_TOKEN, n // rows, rows).transpose(1, 0, 2)
    dest3, dest3_fused = by_tile(MOVE_ROWS), by_tile(FUSED_MOVE_ROWS)
    n_slots = n // parts * EXPERTS_PER_TOKEN + N_EXPERTS * MOE_CHUNK
    n_chunks = n_slots // MOE_CHUNK
    chunk_id = jnp.arange(n_chunks, dtype=jnp.int32)
    chunk_e = jnp.minimum(jnp.sum((pad_end[:, None, :] <= chunk_id[None, :, None] * MOE_CHUNK).astype(jnp.int32),
                                  axis=2), N_EXPERTS - 1)
    n_active = (pad_end[:, -1:] // MOE_CHUNK).astype(jnp.int32)
    next_e = jnp.concatenate([chunk_e[:, 1:], jnp.full((parts, 1), N_EXPERTS, jnp.int32)], axis=1)
    zchunk = ((chunk_id[None, :] >= n_active - 1) | (chunk_e != next_e)).astype(jnp.int32)

    weights = (norm2_g[None, :], w_gate.astype(BF16), w_up.astype(BF16), w_down.astype(BF16))
    xs = _dispatch(0, zchunk[0], dest3, x1, n_slots)
    outs = []
    for p in range(parts):
        if p + 1 < parts:
            out_p, xs_next = _experts_dispatch(chunk_e[p], n_active[p], zchunk[p + 1], xs, *weights, p + 1,
                                               dest3_fused, x1)
        else:
            out_p, xs_next = _experts(chunk_e[p], n_active[p], xs, *weights), None
        outs.append(out_p)
        xs = xs_next
    y = _combine(dest3, route_f[0:2].T, x1, outs)
    return y.reshape(batch, seq, D_MODEL)


def kernel(x, norm1_g, w_in, diff_q_g, diff_k_g, lambda_q1, lambda_k1, lambda_q2, lambda_k2, diff_sub_g,
           moba_q_g, moba_k_g, rel_bias, w_out, norm2_g, router_group, router_expert, w_gate, w_up, w_down):
    assert x.shape[1] % PROJ_ROWS == 0 and x.shape[2] == D_MODEL and norm1_g.shape[0] == 1
    return _layer(x, norm1_g[0], w_in[0], diff_q_g[0], diff_k_g[0], lambda_q1[0], lambda_k1[0], lambda_q2[0],
                  lambda_k2[0], diff_sub_g[0], moba_q_g[0], moba_k_g[0], rel_bias, w_out[0], norm2_g[0],
                  router_group[0], router_expert[0], w_gate[0], w_up[0], w_down[0])
```

```python
import functools
import math

import numpy as np
import jax
import jax.numpy as jnp
from jax import lax
from jax.experimental import pallas as pl
from jax.experimental.pallas import tpu as pltpu

D_MODEL = 1024
HEAD_DIM = 64
HEADS_DIFF = 4
HEADS_MOBA = 8
N_HEADS_TOTAL = HEADS_DIFF + HEADS_MOBA
DIFF_WIDTH = HEADS_DIFF * 2 * HEAD_DIM
MOBA_WIDTH = HEADS_MOBA * HEAD_DIM
QK_WIDTH = 2 * DIFF_WIDTH + 2 * MOBA_WIDTH
V_WIDTH = DIFF_WIDTH + MOBA_WIDTH
MOBA_TOPK = 3
NUM_BUCKETS = 32
MAX_DISTANCE = 2048
N_GROUPS = 4
EXPERTS_PER_GROUP = 8
N_EXPERTS = N_GROUPS * EXPERTS_PER_GROUP
EXPERTS_PER_TOKEN = 2
EXPERT_HIDDEN = 256
RMS_EPS = 1e-6
NEG_INF = -1e30
LAMBDA_INIT = 0.8 - 0.6 * math.exp(-0.3 * 0)

LANES = 128
ROW_TILE = D_MODEL // LANES
ATT_TILE = 256
ONES_ROWS = 16
LOGIT_LIMIT = 80.0
ROUNDING_MARGIN = 1.05
LOG2E = math.log2(math.e)
PROJ_ROWS = 1024
OUT_ROWS = 1024
ROUTE_TILE = 256
MOE_CHUNK = 512
MOE_TOKEN_PARTS = 2
EXPERT_ROW_PARTS = 4
MOVE_ROWS = 512
DISPATCH_ROWS = 1024
FUSED_MOVE_ROWS = 256
SUBLANES = 8
ROUTER_ROWS = -(-(N_EXPERTS + N_GROUPS) // SUBLANES) * SUBLANES
V7X_VMEM_BYTES = 64 * 1024 * 1024
VMEM_LIMIT = V7X_VMEM_BYTES * 3 // 4

F32 = jnp.float32
BF16 = jnp.bfloat16
_NT = (((1,), (1,)), ((), ()))


def _t5_thresholds():
    n = np.arange(0, 1 << 16)
    max_exact = NUM_BUCKETS // 2
    nf = np.maximum(n, 1).astype(np.float32)
    large = max_exact + (np.log(nf / np.float32(max_exact)) / np.float32(math.log(MAX_DISTANCE / max_exact))
                         * np.float32(NUM_BUCKETS - max_exact)).astype(np.int32)
    bucket = np.where(n < max_exact, n, np.minimum(large, NUM_BUCKETS - 1))
    return [int(np.searchsorted(bucket, b, side="left")) for b in range(1, NUM_BUCKETS)]


_T5_THRESHOLDS = _t5_thresholds()


def _params(*sem):
    return pltpu.CompilerParams(dimension_semantics=sem, vmem_limit_bytes=VMEM_LIMIT)


def _bias_kernel(tab_ref, out_ref):
    h = pl.program_id(0)
    kj = lax.broadcasted_iota(jnp.int32, (ATT_TILE, ATT_TILE), 0)
    qi = lax.broadcasted_iota(jnp.int32, (ATT_TILE, ATT_TILE), 1)
    for d in range(out_ref.shape[0]):
        dist = d * ATT_TILE + qi - kj
        lo, hi = d * ATT_TILE - (ATT_TILE - 1), d * ATT_TILE + (ATT_TILE - 1)
        base = sum(thr <= max(lo, 0) for thr in _T5_THRESHOLDS)
        val = jnp.full((ATT_TILE, ATT_TILE), tab_ref[base, h], F32)
        for b, thr in enumerate(_T5_THRESHOLDS, start=1):
            if max(lo, 0) < thr <= hi:
                val = jnp.where(dist >= thr, tab_ref[b, h], val)
        val = val * LOG2E
        out_ref[d] = jnp.where(dist < 0, NEG_INF, val) if lo < 0 else val


def _bias_tiles(rel_bias, n_diag):
    return pl.pallas_call(
        _bias_kernel,
        grid=(N_HEADS_TOTAL,),
        in_specs=[pl.BlockSpec(memory_space=pltpu.SMEM)],
        out_specs=pl.BlockSpec((None, n_diag, ATT_TILE, ATT_TILE), lambda h: (h, 0, 0, 0)),
        out_shape=jax.ShapeDtypeStruct((N_HEADS_TOTAL, n_diag, ATT_TILE, ATT_TILE), F32),
        compiler_params=_params("arbitrary"),
        name="bias_tiles",
    )(rel_bias)


def _in_proj_kernel(x_ref, g1_ref, w_ref, pg_ref, grp_ref, qk_ref, vt_ref):
    n_row_tiles = PROJ_ROWS // ATT_TILE
    h = []
    for t in range(n_row_tiles):
        x = x_ref[t * ATT_TILE:(t + 1) * ATT_TILE, :]
        ms = jnp.mean(x * x, axis=-1, keepdims=True)
        h.append((x * lax.rsqrt(ms + RMS_EPS) * g1_ref[...]).astype(BF16))

    cw = 2 * LANES

    def finish(c, t, p):
        rows = slice(t * ATT_TILE, (t + 1) * ATT_TILE)
        cols = slice(c * cw, (c + 1) * cw)
        if c < QK_WIDTH // cw:
            msq = jnp.dot((p * p).astype(BF16), grp_ref[...], preferred_element_type=F32)
            qk_ref[rows, cols] = (p * lax.rsqrt(msq + RMS_EPS) * pg_ref[:, cols]).astype(BF16)
        else:
            vt_ref[t, c * cw - QK_WIDTH:(c + 1) * cw - QK_WIDTH, :] = p.T.astype(BF16)

    units = [(c, t) for t in range(n_row_tiles) for c in range((QK_WIDTH + V_WIDTH) // cw)]
    pending = None
    for c, t in units:
        p = jnp.dot(h[t], w_ref[:, c * cw:(c + 1) * cw], preferred_element_type=F32)
        if pending is not None:
            finish(*pending)
        pending = (c, t, p)
    finish(*pending)


def _in_proj(x2, g1, w_perm, post_gain, grp, batch, seq):
    n = x2.shape[0]
    steps_per_seq = seq // PROJ_ROWS
    tiles_per_step = PROJ_ROWS // ATT_TILE
    return pl.pallas_call(
        _in_proj_kernel,
        grid=(n // PROJ_ROWS,),
        in_specs=[
            pl.BlockSpec((PROJ_ROWS, D_MODEL), lambda i: (i, 0)),
            pl.BlockSpec((1, D_MODEL), lambda i: (0, 0)),
            pl.BlockSpec((D_MODEL, QK_WIDTH + V_WIDTH), lambda i: (0, 0)),
            pl.BlockSpec((1, QK_WIDTH), lambda i: (0, 0)),
            pl.BlockSpec((2 * LANES, 2 * LANES), lambda i: (0, 0)),
        ],
        out_specs=[
            pl.BlockSpec((PROJ_ROWS, QK_WIDTH), lambda i: (i, 0)),
            pl.BlockSpec((None, tiles_per_step, V_WIDTH, ATT_TILE),
                         lambda i: (i // steps_per_seq, i % steps_per_seq, 0, 0)),
        ],
        out_shape=[
            jax.ShapeDtypeStruct((n, QK_WIDTH), BF16),
            jax.ShapeDtypeStruct((batch, seq // ATT_TILE, V_WIDTH, ATT_TILE), BF16),
        ],
        compiler_params=_params("arbitrary"),
        name="in_proj",
    )(x2, g1, w_perm, post_gain, grp)


def _flash_tiles(qi, n_chains, logits_fn, values_fn, bias_fn, mask_fn, s_ref, p_ref, acc_ref):
    for c in range(n_chains):
        s_ref[0, c] = logits_fn(c, qi)
    p_ref[...] = jnp.zeros_like(p_ref)
    acc_ref[...] = jnp.zeros_like(acc_ref)

    def add_values(c, kv, alpha):
        acc_ref[c] = acc_ref[c] * alpha + jnp.dot(values_fn(c, kv), p_ref[c], preferred_element_type=F32)

    def step(j, carry):
        cur = lax.rem(j, 2)
        kv = qi - j
        out = []
        for c in range(n_chains):
            add_values(c, jnp.minimum(kv + 1, qi), carry[2 * c + 1])
        for c in range(n_chains):
            s = s_ref[cur, c] + bias_fn(c, j)
            tile_max = jnp.max(s, axis=0, keepdims=True)
            shift = mask_fn(c, kv)
            if shift is not None:
                tile_max = tile_max + shift
            m_new = jnp.maximum(carry[2 * c], tile_max)
            alpha = jnp.exp2(carry[2 * c] - m_new)
            p_ref[c] = jnp.exp2(s - (m_new if shift is None else m_new - shift)).astype(BF16)
            out += [m_new, alpha]
        for c in range(n_chains):
            s_ref[1 - cur, c] = logits_fn(c, jnp.maximum(kv - 1, 0))
        return tuple(out)

    m0 = jnp.full((1, ATT_TILE), NEG_INF, F32)
    one = jnp.ones((1, ATT_TILE), F32)
    carry = lax.fori_loop(0, qi + 1, step, (m0, one) * n_chains)
    for c in range(n_chains):
        add_values(c, 0, carry[2 * c + 1])


def _bounded_tiles(n_tiles, n_chains, logits_fn, values_fn, bias_fn, keep_fn, finish_fn, p_ref, acc_ref,
                   between_fn=lambda d: None):
    for d in range(n_tiles):
        for qi in range(d, n_tiles):
            for c in range(n_chains):
                p_ref[qi, c] = jnp.exp2(logits_fn(c, qi, qi - d) + bias_fn(c, d)).astype(BF16)
        between_fn(d)
        for qi in range(d, n_tiles):
            kv = qi - d
            for c in range(n_chains):
                pv = jnp.dot(values_fn(c, kv), p_ref[qi, c], preferred_element_type=F32)
                keep = keep_fn(c, qi, kv)
                if keep is not None:
                    pv = pv * keep
                acc_ref[qi, c] = pv if d == 0 else acc_ref[qi, c] + pv
        finish_fn(d)


def _logits_bounded(q_gain, k_gain, bias_cols):
    bound = (HEAD_DIM * jnp.max(jnp.abs(q_gain)) * jnp.max(jnp.abs(k_gain)) * ROUNDING_MARGIN
             + jnp.max(jnp.abs(bias_cols)) * LOG2E)
    return (bound <= LOGIT_LIMIT).astype(jnp.int32).reshape(1)


def _with_ones_rows(vt):
    return jnp.concatenate([vt, jnp.ones((ONES_ROWS, vt.shape[1]), vt.dtype)], axis=0)


def _attention_scratch(n_tiles, value_rows, n_chains=2):
    return [pltpu.VMEM((n_tiles, n_chains, ATT_TILE, LANES), BF16),
            pltpu.VMEM((2, n_chains, ATT_TILE, ATT_TILE), F32),
            pltpu.VMEM((n_tiles, n_chains, ATT_TILE, ATT_TILE), BF16),
            pltpu.VMEM((n_tiles, n_chains, value_rows + ONES_ROWS, ATT_TILE), F32)]


def _half_lane_split(q):
    lane = lax.broadcasted_iota(jnp.int32, q.shape, 1)
    zero = jnp.zeros_like(q)
    return jnp.where(lane < HEAD_DIM, q, zero), jnp.where(lane >= HEAD_DIM, q, zero)


def _diff_kernel(bounded_ref, lq1_ref, lk1_ref, lq2_ref, lk2_ref, subg_ref, q_ref, k_ref, vt_ref, bias_ref, o_ref,
                 qm_ref, s_ref, p_ref, acc_ref):
    n_tiles = q_ref.shape[0]
    chains = dict(n_chains=2, values_fn=lambda c, kv: _with_ones_rows(vt_ref[kv]))
    width = 2 * HEAD_DIM

    def prepare():
        for qi in range(n_tiles):
            qm_ref[qi, 0], qm_ref[qi, 1] = _half_lane_split(q_ref[qi])

    def finish(qi):
        lam = (jnp.exp(jnp.sum(lq1_ref[...] * lk1_ref[...], keepdims=True))
               - jnp.exp(jnp.sum(lq2_ref[...] * lk2_ref[...], keepdims=True)) + LAMBDA_INIT)
        a1, a2 = acc_ref[qi, 0], acc_ref[qi, 1]
        o = a1[:width] / a1[width:width + 1] - lam * (a2[:width] / a2[width:width + 1])
        ms = jnp.mean(o * o, axis=0, keepdims=True)
        o = o * lax.rsqrt(ms + RMS_EPS) * subg_ref[...] * (1.0 - LAMBDA_INIT)
        o_ref[qi] = o.T.astype(BF16)

    bounded = bounded_ref[0] > 0

    @pl.when(bounded)
    def _():
        prepare()
        _bounded_tiles(
            n_tiles,
            logits_fn=lambda c, qi, kv: lax.dot_general(k_ref[kv], qm_ref[qi, c], _NT, preferred_element_type=F32),
            bias_fn=lambda c, d: bias_ref[d],
            keep_fn=lambda c, qi, kv: None,
            finish_fn=finish,
            p_ref=p_ref, acc_ref=acc_ref, **chains)

    @pl.when(jnp.logical_not(bounded))
    def _():
        prepare()

        def q_tile(qi, carry):
            _flash_tiles(
                qi,
                logits_fn=lambda c, kv: lax.dot_general(k_ref[kv], qm_ref[qi, c], _NT, preferred_element_type=F32),
                bias_fn=lambda c, j: bias_ref[j],
                mask_fn=lambda c, kv: None,
                s_ref=s_ref, p_ref=p_ref.at[0], acc_ref=acc_ref.at[qi], **chains)
            return carry
        lax.fori_loop(0, n_tiles, q_tile, 0)
        for qi in range(n_tiles):
            finish(qi)


def _diff_attention(bounded, lam_vecs, sub_g_col, qk4, vt4, bias):
    batch, n_tiles = qk4.shape[0], qk4.shape[1]
    k_col0 = DIFF_WIDTH // LANES
    vec = pl.BlockSpec((1, HEAD_DIM), lambda h, b: (0, 0))
    return pl.pallas_call(
        _diff_kernel,
        grid=(HEADS_DIFF, batch),
        in_specs=[
            pl.BlockSpec(memory_space=pltpu.SMEM),
            vec, vec, vec, vec,
            pl.BlockSpec((2 * HEAD_DIM, 1), lambda h, b: (0, 0)),
            pl.BlockSpec((None, n_tiles, ATT_TILE, LANES), lambda h, b: (b, 0, 0, h)),
            pl.BlockSpec((None, n_tiles, ATT_TILE, LANES), lambda h, b: (b, 0, 0, k_col0 + h)),
            pl.BlockSpec((None, n_tiles, LANES, ATT_TILE), lambda h, b: (b, 0, h, 0)),
            pl.BlockSpec((None, n_tiles, ATT_TILE, ATT_TILE), lambda h, b: (h, 0, 0, 0)),
        ],
        out_specs=pl.BlockSpec((None, n_tiles, ATT_TILE, LANES), lambda h, b: (b, 0, 0, h)),
        out_shape=jax.ShapeDtypeStruct((batch, n_tiles, ATT_TILE, DIFF_WIDTH), BF16),
        scratch_shapes=_attention_scratch(n_tiles, 2 * HEAD_DIM),
        compiler_params=_params("arbitrary", "arbitrary"),
        name="diff_attention",
    )(bounded, *lam_vecs, sub_g_col, qk4, qk4, vt4, bias)


def _split3(v):
    hi = v.astype(BF16)
    r1 = v - hi.astype(F32)
    mid = r1.astype(BF16)
    lo = (r1 - mid.astype(F32)).astype(BF16)
    return hi, mid, lo


def _block_mask(gate, own):
    row = lax.broadcasted_iota(jnp.int32, gate.shape, 0)
    rank = jnp.zeros(gate.shape, jnp.int32)
    for m in range(own):
        gm = gate[m:m + 1, :]
        beats = (gm > gate) | ((gm == gate) & (row > m))
        rank = rank + jnp.where(beats, 1, 0)
    keep = ((rank < MOBA_TOPK) & (row < own)) | (row == own)
    return jnp.where(keep, 0.0, NEG_INF).astype(F32)


def _moba_kernel(bounded_ref, q_ref, k_ref, vt_ref, bias_ref, o_ref, kmean_ref, mask_ref, qm_ref, s_ref, p_ref,
                 acc_ref):
    n_tiles = q_ref.shape[0]

    def split_queries():
        for qi in range(n_tiles):
            qm_ref[qi, 0], qm_ref[qi, 1] = _half_lane_split(q_ref[qi])

    def block_masks(q_tiles):
        for n in range(n_tiles):
            kmean_ref[n:n + 1, :] = jnp.mean(k_ref[n].astype(F32), axis=0, keepdims=True)
        per_head = zip(*[_half_lane_split(term) for term in _split3(kmean_ref[...])])
        gate_lhs = jnp.concatenate([term for head_terms in per_head for term in head_terms], axis=0)
        n_terms = gate_lhs.shape[0] // (2 * n_tiles)
        for qi in q_tiles:
            terms = lax.dot_general(gate_lhs, q_ref[qi], _NT, preferred_element_type=F32)
            for half in range(2):
                rows = [terms[(half * n_terms + t) * n_tiles:(half * n_terms + t + 1) * n_tiles]
                        for t in range(n_terms)]
                mask_ref[qi, half] = _block_mask(sum(rows[1:], rows[0]), qi)

    def keep_row(c, qi, kv):
        if qi <= MOBA_TOPK or kv == qi:
            return None
        return jnp.where(mask_ref[qi, c, kv:kv + 1, :] < -1.0, 0.0, 1.0)

    def finish(qi):
        halves = [acc_ref[qi, half] for half in range(2)]
        o = jnp.concatenate([a[:HEAD_DIM] / a[HEAD_DIM:HEAD_DIM + 1] for a in halves], axis=0)
        o_ref[qi] = o.T.astype(BF16)

    chains = dict(
        n_chains=2,
        values_fn=lambda c, kv: _with_ones_rows(vt_ref[kv, c * HEAD_DIM:(c + 1) * HEAD_DIM, :]))
    bounded = bounded_ref[0] > 0

    @pl.when(bounded)
    def _():
        split_queries()
        _bounded_tiles(
            n_tiles,
            logits_fn=lambda c, qi, kv: lax.dot_general(k_ref[kv], qm_ref[qi, c], _NT, preferred_element_type=F32),
            bias_fn=lambda c, d: bias_ref[c, d],
            keep_fn=keep_row,
            between_fn=lambda d: block_masks(range(MOBA_TOPK + 1, n_tiles)) if d == 0 else None,
            finish_fn=finish,
            p_ref=p_ref, acc_ref=acc_ref, **chains)

    @pl.when(jnp.logical_not(bounded))
    def _():
        split_queries()
        block_masks(range(n_tiles))

        def q_tile(qi, carry):
            _flash_tiles(
                qi,
                logits_fn=lambda c, kv: lax.dot_general(k_ref[kv], qm_ref[qi, c], _NT, preferred_element_type=F32),
                bias_fn=lambda c, j: bias_ref[c, j],
                mask_fn=lambda c, kv: mask_ref[qi, c, pl.ds(kv, 1), :],
                s_ref=s_ref, p_ref=p_ref.at[0], acc_ref=acc_ref.at[qi], **chains)
            return carry
        lax.fori_loop(0, n_tiles, q_tile, 0)
        for qi in range(n_tiles):
            finish(qi)


def _moba_attention(bounded, qk4, vt4, bias_pairs):
    batch, n_tiles = qk4.shape[0], qk4.shape[1]
    q_col0 = 2 * DIFF_WIDTH // LANES
    k_col0 = q_col0 + MOBA_WIDTH // LANES
    v_row0 = DIFF_WIDTH // LANES
    pair0 = HEADS_DIFF // 2
    return pl.pallas_call(
        _moba_kernel,
        grid=(HEADS_MOBA // 2, batch),
        in_specs=[
            pl.BlockSpec(memory_space=pltpu.SMEM),
            pl.BlockSpec((None, n_tiles, ATT_TILE, LANES), lambda h, b: (b, 0, 0, q_col0 + h)),
            pl.BlockSpec((None, n_tiles, ATT_TILE, LANES), lambda h, b: (b, 0, 0, k_col0 + h)),
            pl.BlockSpec((None, n_tiles, LANES, ATT_TILE), lambda h, b: (b, 0, v_row0 + h, 0)),
            pl.BlockSpec((None, 2, n_tiles, ATT_TILE, ATT_TILE), lambda h, b: (pair0 + h, 0, 0, 0, 0)),
        ],
        out_specs=pl.BlockSpec((None, n_tiles, ATT_TILE, LANES), lambda h, b: (b, 0, 0, h)),
        out_shape=jax.ShapeDtypeStruct((batch, n_tiles, ATT_TILE, MOBA_WIDTH), BF16),
        scratch_shapes=[
            pltpu.VMEM((n_tiles, LANES), F32),
            pltpu.VMEM((n_tiles, 2, n_tiles, ATT_TILE), F32),
        ] + _attention_scratch(n_tiles, HEAD_DIM),
        compiler_params=_params("arbitrary", "arbitrary"),
        name="moba_attention",
    )(bounded, qk4, qk4, vt4, bias_pairs)


def _load_token_rows(ref, n_rows, lead=(), first=0):
    chunks = [ref[lead + (pl.ds(first * ROW_TILE + c, n_rows, stride=ROW_TILE), slice(None))]
              for c in range(ROW_TILE)]
    return jnp.concatenate(chunks, axis=1)


def _store_token_rows(ref, value, first=0):
    n_rows = value.shape[0]
    for c in range(ROW_TILE):
        ref[pl.ds(first * ROW_TILE + c, n_rows, stride=ROW_TILE), :] = value[:, c * LANES:(c + 1) * LANES]


def _token_tile(ref, t):
    return ref.at[pl.ds(pl.multiple_of(t * ROW_TILE, ROW_TILE), ROW_TILE)]


def _first_argmax(v):
    top = jnp.max(v, axis=0, keepdims=True)
    row = lax.broadcasted_iota(jnp.int32, v.shape, 0)
    idx = jnp.min(jnp.where(v == top, row, v.shape[0]), axis=0, keepdims=True)
    return top, idx


def _out_proj_kernel(yd_ref, ym_ref, x_ref, wd_ref, wm_ref, g2_ref, wr_ref, tri_ref,
                     x1_ref, ri_ref, rf_ref, cnt_ref, run_ref):
    @pl.when(lax.rem(pl.program_id(0), pl.num_programs(0) // MOE_TOKEN_PARTS) == 0)
    def _():
        run_ref[...] = jnp.zeros_like(run_ref)

    def project(t):
        rows = slice(t * ROUTE_TILE, (t + 1) * ROUTE_TILE)
        x1 = (x_ref[rows, :] + jnp.dot(yd_ref[rows, :], wd_ref[...], preferred_element_type=F32)
              + jnp.dot(ym_ref[rows, :], wm_ref[...], preferred_element_type=F32))
        _store_token_rows(x1_ref, x1, first=t * ROUTE_TILE)
        ms = jnp.mean(x1 * x1, axis=-1, keepdims=True)
        return (x1 * lax.rsqrt(ms + RMS_EPS) * g2_ref[...]).astype(BF16)

    def pick(h2):
        lg = lax.dot_general(wr_ref[...], h2, _NT, preferred_element_type=F32)
        g_logits = lg[N_EXPERTS:N_EXPERTS + N_GROUPS, :]
        g_top, g_idx = _first_argmax(g_logits)
        p_group = 1.0 / jnp.sum(jnp.exp(g_logits - g_top), axis=0, keepdims=True)
        e_logits = lg[0:EXPERTS_PER_GROUP, :]
        for g in range(1, N_GROUPS):
            e_logits = jnp.where(g_idx == g, lg[g * EXPERTS_PER_GROUP:(g + 1) * EXPERTS_PER_GROUP, :], e_logits)
        v1, i1 = _first_argmax(e_logits)
        row = lax.broadcasted_iota(jnp.int32, e_logits.shape, 0)
        v2, i2 = _first_argmax(jnp.where(row == i1, -jnp.inf, e_logits))
        ratio = jnp.exp(v2 - v1)
        w1 = p_group / (1.0 + ratio)
        w2 = p_group * ratio / (1.0 + ratio)
        e1 = g_idx * EXPERTS_PER_GROUP + i1
        e2 = g_idx * EXPERTS_PER_GROUP + i2
        erow = lax.broadcasted_iota(jnp.int32, (N_EXPERTS, ROUTE_TILE), 0)
        hit1 = erow == e1
        hit2 = erow == e2
        onehot = jnp.where(hit1 | hit2, 1.0, 0.0).astype(F32)
        return e1, e2, w1, w2, hit1, hit2, onehot

    def place(t, e1, e2, w1, w2, hit1, hit2, onehot):
        before = jnp.dot(onehot.astype(BF16), tri_ref[...], preferred_element_type=F32) + run_ref[...]
        r1 = jnp.sum(jnp.where(hit1, before, 0.0), axis=0, keepdims=True).astype(jnp.int32)
        r2 = jnp.sum(jnp.where(hit2, before, 0.0), axis=0, keepdims=True).astype(jnp.int32)
        run_ref[...] = run_ref[...] + jnp.sum(onehot, axis=1, keepdims=True)
        lanes = slice(t * ROUTE_TILE, (t + 1) * ROUTE_TILE)
        out_row = lax.broadcasted_iota(jnp.int32, (SUBLANES, ROUTE_TILE), 0)
        ri_ref[:, lanes] = jnp.where(out_row == 0, e1, jnp.where(out_row == 1, e2,
                                     jnp.where(out_row == 2, r1, jnp.where(out_row == 3, r2, 0))))
        rf_ref[:, lanes] = jnp.where(out_row == 0, w1, jnp.where(out_row == 1, w2, 0.0))

    n_tiles = OUT_ROWS // ROUTE_TILE
    projected, picked = {}, {}
    for t in range(n_tiles + 2):
        if t < n_tiles:
            projected[t] = project(t)
        if 0 <= t - 1 < n_tiles:
            picked[t - 1] = pick(projected.pop(t - 1))
        if 0 <= t - 2 < n_tiles:
            place(t - 2, *picked.pop(t - 2))
    cnt_ref[...] = jnp.broadcast_to(run_ref[...], cnt_ref.shape).astype(jnp.int32)


def _out_proj(y_d, y_m, x2, wo_d, wo_m, g2, w_router, tri):
    n = x2.shape[0]
    const = lambda i: (0, 0)
    steps_per_part = n // OUT_ROWS // MOE_TOKEN_PARTS
    return pl.pallas_call(
        _out_proj_kernel,
        grid=(n // OUT_ROWS,),
        in_specs=[
            pl.BlockSpec((OUT_ROWS, DIFF_WIDTH), lambda i: (i, 0)),
            pl.BlockSpec((OUT_ROWS, MOBA_WIDTH), lambda i: (i, 0)),
            pl.BlockSpec((OUT_ROWS, D_MODEL), lambda i: (i, 0)),
            pl.BlockSpec((DIFF_WIDTH, D_MODEL), const),
            pl.BlockSpec((MOBA_WIDTH, D_MODEL), const),
            pl.BlockSpec((1, D_MODEL), const),
            pl.BlockSpec((ROUTER_ROWS, D_MODEL), const),
            pl.BlockSpec((ROUTE_TILE, ROUTE_TILE), const),
        ],
        out_specs=[
            pl.BlockSpec((OUT_ROWS * ROW_TILE, LANES), lambda i: (i, 0)),
            pl.BlockSpec((SUBLANES, OUT_ROWS), lambda i: (0, i)),
            pl.BlockSpec((SUBLANES, OUT_ROWS), lambda i: (0, i)),
            pl.BlockSpec((None, N_EXPERTS, LANES), lambda i: (i // steps_per_part, 0, 0)),
        ],
        out_shape=[
            jax.ShapeDtypeStruct((n * ROW_TILE, LANES), F32),
            jax.ShapeDtypeStruct((SUBLANES, n), jnp.int32),
            jax.ShapeDtypeStruct((SUBLANES, n), F32),
            jax.ShapeDtypeStruct((MOE_TOKEN_PARTS, N_EXPERTS, LANES), jnp.int32),
        ],
        scratch_shapes=[pltpu.VMEM((N_EXPERTS, 1), F32)],
        compiler_params=_params("arbitrary"),
        name="out_proj_route",
    )(y_d, y_m, x2, wo_d, wo_m, g2, w_router, tri)


def _dispatch_step(i, n_steps, first_tile, zchunk_ref, dest_ref, x1_hbm, xs_hbm, zero_ref, stage_ref, zero_sem,
                   load_sem, row_sem):
    last = n_steps - 1
    slot = lax.rem(i, 2)
    chunk_rows = MOE_CHUNK * ROW_TILE
    tile_tokens = dest_ref.shape[2]
    tile_rows = tile_tokens * ROW_TILE

    def load(t, s):
        start = pl.multiple_of((first_tile + t) * tile_rows, tile_rows)
        return pltpu.make_async_copy(x1_hbm.at[pl.ds(start, tile_rows)], stage_ref.at[s], load_sem.at[s])

    def wait_rows(s):
        for _ in range(EXPERTS_PER_TOKEN):
            pltpu.make_async_copy(stage_ref.at[s], xs_hbm.at[pl.ds(0, tile_rows)], row_sem.at[s]).wait()

    def zero_copy(c):
        start = pl.multiple_of(c * chunk_rows, chunk_rows)
        return pltpu.make_async_copy(zero_ref, xs_hbm.at[pl.ds(start, chunk_rows)], zero_sem)

    @pl.when(i == 0)
    def _():
        zero_ref[...] = jnp.zeros_like(zero_ref)

        def start_one(c, carry):
            @pl.when(zchunk_ref[c] > 0)
            def _():
                zero_copy(c).start()
            return carry

        def wait_one(c, carry):
            @pl.when(zchunk_ref[c] > 0)
            def _():
                zero_copy(c).wait()
            return carry

        lax.fori_loop(0, zchunk_ref.shape[0], start_one, 0)
        lax.fori_loop(0, zchunk_ref.shape[0], wait_one, 0)
        load(0, 0).start()

    @pl.when(i > 0)
    def _():
        wait_rows(1 - slot)

    @pl.when(i < last)
    def _():
        load(i + 1, 1 - slot).start()

    load(i, slot).wait()

    def send_row(r, carry):
        src = _token_tile(stage_ref.at[slot], r)
        for k in range(EXPERTS_PER_TOKEN):
            pltpu.make_async_copy(src, _token_tile(xs_hbm, dest_ref[0, k, r]), row_sem.at[slot]).start(priority=k)
        return carry

    lax.fori_loop(0, tile_tokens, send_row, 0, unroll=8)

    @pl.when(i == last)
    def _():
        wait_rows(slot)


def _dispatch_scratch(tile_tokens):
    return [pltpu.VMEM((MOE_CHUNK * ROW_TILE, LANES), F32),
            pltpu.VMEM((2, tile_tokens * ROW_TILE, LANES), F32),
            pltpu.SemaphoreType.DMA(()),
            pltpu.SemaphoreType.DMA((2,)),
            pltpu.SemaphoreType.DMA((2,))]


def _dispatch_kernel(first_tile, zchunk_ref, dest_ref, x1_hbm, xs_hbm, *scratch):
    _dispatch_step(pl.program_id(0), pl.num_programs(0), first_tile, zchunk_ref, dest_ref, x1_hbm, xs_hbm, *scratch)


def _dispatch(part, zchunk, dest3, x1, n_slots):
    tile_tokens = dest3.shape[2]
    tiles = dest3.shape[0] // MOE_TOKEN_PARTS
    first_tile = part * tiles
    grid_spec = pltpu.PrefetchScalarGridSpec(
        num_scalar_prefetch=1,
        grid=(tiles,),
        in_specs=[
            pl.BlockSpec((1, EXPERTS_PER_TOKEN, tile_tokens), lambda i, zc: (first_tile + i, 0, 0),
                         memory_space=pltpu.SMEM),
            pl.BlockSpec(memory_space=pl.ANY),
        ],
        out_specs=pl.BlockSpec(memory_space=pl.ANY),
        scratch_shapes=_dispatch_scratch(tile_tokens),
    )
    return pl.pallas_call(
        functools.partial(_dispatch_kernel, first_tile),
        grid_spec=grid_spec,
        out_shape=jax.ShapeDtypeStruct((n_slots * ROW_TILE, LANES), F32),
        compiler_params=_params("arbitrary"),
        name="moe_dispatch",
    )(zchunk, dest3, x1)


def _expert_step(c, na_ref, xs_ref, g2_ref, wg_ref, wu_ref, wd_ref, o_ref):
    active = c < na_ref[0]

    @pl.when(jnp.logical_not(active))
    def _():
        o_ref[...] = jnp.zeros_like(o_ref)

    @pl.when(active)
    def _():
        part_rows = MOE_CHUNK // EXPERT_ROW_PARTS

        def gate_up(r):
            x = _load_token_rows(xs_ref, part_rows, first=r * part_rows)
            ms = jnp.mean(x * x, axis=-1, keepdims=True)
            h = (x * lax.rsqrt(ms + RMS_EPS) * g2_ref[...]).astype(BF16)
            return (jnp.dot(h, wg_ref[...], preferred_element_type=F32),
                    jnp.dot(h, wu_ref[...], preferred_element_type=F32))

        def down(r, gate, up):
            hid = (gate * jax.nn.sigmoid(gate) * up).astype(BF16)
            _store_token_rows(o_ref, jnp.dot(hid, wd_ref[...], preferred_element_type=F32), first=r * part_rows)

        pending = None
        for r in range(EXPERT_ROW_PARTS):
            current = (r,) + gate_up(r)
            if pending is not None:
                down(*pending)
            pending = current
        down(*pending)


def _expert_kernel(ce_ref, na_ref, *refs):
    _expert_step(pl.program_id(0), na_ref, *refs)


def _expert_specs():
    rows = lambda c, ce, na, *_: (jnp.minimum(c, na[0] - 1), 0)
    expert = lambda c, ce, na, *_: (ce[c], 0, 0)
    in_specs = [
        pl.BlockSpec((MOE_CHUNK * ROW_TILE, LANES), rows),
        pl.BlockSpec((1, D_MODEL), lambda c, *_: (0, 0)),
        pl.BlockSpec((None, D_MODEL, EXPERT_HIDDEN), expert),
        pl.BlockSpec((None, D_MODEL, EXPERT_HIDDEN), expert),
        pl.BlockSpec((None, EXPERT_HIDDEN, D_MODEL), expert),
    ]
    return in_specs, pl.BlockSpec((MOE_CHUNK * ROW_TILE, LANES), lambda c, *_: (c, 0))


def _experts(chunk_e, n_active, xs, g2, wg, wu, wd):
    n_slots = xs.shape[0] // ROW_TILE
    in_specs, out_spec = _expert_specs()
    grid_spec = pltpu.PrefetchScalarGridSpec(
        num_scalar_prefetch=2, grid=(n_slots // MOE_CHUNK,), in_specs=in_specs, out_specs=out_spec)
    return pl.pallas_call(
        _expert_kernel,
        grid_spec=grid_spec,
        out_shape=jax.ShapeDtypeStruct((n_slots * ROW_TILE, LANES), F32),
        compiler_params=_params("arbitrary"),
        name="moe_experts",
    )(chunk_e, n_active, xs, g2, wg, wu, wd)


def _experts_dispatch_kernel(first_tile, n_tiles, ce_ref, na_ref, zchunk_ref,
                             xs_ref, g2_ref, wg_ref, wu_ref, wd_ref, dest_ref, x1_hbm,
                             o_ref, xs_next_hbm, *scratch):
    i = pl.program_id(0)

    @pl.when(i < n_tiles)
    def _():
        _dispatch_step(i, n_tiles, first_tile, zchunk_ref, dest_ref, x1_hbm, xs_next_hbm, *scratch)

    _expert_step(i, na_ref, xs_ref, g2_ref, wg_ref, wu_ref, wd_ref, o_ref)


def _experts_dispatch(chunk_e, n_active, zchunk_next, xs, g2, wg, wu, wd, next_part, dest3, x1):
    n_slots = xs.shape[0] // ROW_TILE
    n_chunks = n_slots // MOE_CHUNK
    tile_tokens = dest3.shape[2]
    tiles = dest3.shape[0] // MOE_TOKEN_PARTS
    first_tile = next_part * tiles
    assert n_chunks >= tiles
    in_specs, out_spec = _expert_specs()
    in_specs += [
        pl.BlockSpec((1, EXPERTS_PER_TOKEN, tile_tokens),
                     lambda c, *_: (first_tile + jnp.minimum(c, tiles - 1), 0, 0), memory_space=pltpu.SMEM),
        pl.BlockSpec(memory_space=pl.ANY),
    ]
    grid_spec = pltpu.PrefetchScalarGridSpec(
        num_scalar_prefetch=3, grid=(n_chunks,), in_specs=in_specs,
        out_specs=[out_spec, pl.BlockSpec(memory_space=pl.ANY)],
        scratch_shapes=_dispatch_scratch(tile_tokens))
    slots = jax.ShapeDtypeStruct((n_slots * ROW_TILE, LANES), F32)
    return pl.pallas_call(
        functools.partial(_experts_dispatch_kernel, first_tile, tiles),
        grid_spec=grid_spec,
        out_shape=[slots, slots],
        compiler_params=_params("arbitrary"),
        name="moe_experts_dispatch",
    )(chunk_e, n_active, zchunk_next, xs, g2, wg, wu, wd, dest3, x1)


def _combine_kernel(dest_ref, dest_next_ref, w_ref, x1_ref, *rest):
    outs_hbm, (y_ref, buf_ref, sem_ref) = rest[:MOE_TOKEN_PARTS], rest[MOE_TOKEN_PARTS:]
    i = pl.program_id(0)
    slot = lax.rem(i, 2)
    tiles_per_part = pl.num_programs(0) // MOE_TOKEN_PARTS

    def fetch(tile, dref, s):
        for part, src_hbm in enumerate(outs_hbm):
            @pl.when(tile // tiles_per_part == part)
            def _():
                def fetch_row(r, carry):
                    for k in range(EXPERTS_PER_TOKEN):
                        pltpu.make_async_copy(_token_tile(src_hbm, dref[0, k, r]),
                                              _token_tile(buf_ref.at[s, k], r), sem_ref.at[s]).start(priority=k)
                    return carry
                lax.fori_loop(0, MOVE_ROWS, fetch_row, 0, unroll=8)

    @pl.when(i == 0)
    def _():
        fetch(i, dest_ref, 0)

    @pl.when(i + 1 < pl.num_programs(0))
    def _():
        fetch(i + 1, dest_next_ref, 1 - slot)

    for k in range(EXPERTS_PER_TOKEN):
        pltpu.make_async_copy(outs_hbm[0].at[pl.ds(0, MOVE_ROWS * ROW_TILE)], buf_ref.at[slot, k],
                              sem_ref.at[slot]).wait()
    w = w_ref[...]
    y_ref[...] = (_load_token_rows(x1_ref, MOVE_ROWS)
                  + w[:, 0:1] * _load_token_rows(buf_ref, MOVE_ROWS, (slot, 0))
                  + w[:, 1:2] * _load_token_rows(buf_ref, MOVE_ROWS, (slot, 1)))


def _combine(dest3, wts, x1, outs_parts):
    n = x1.shape[0] // ROW_TILE
    steps = n // MOVE_ROWS
    dest_block = (1, EXPERTS_PER_TOKEN, MOVE_ROWS)
    return pl.pallas_call(
        _combine_kernel,
        grid=(steps,),
        in_specs=[
            pl.BlockSpec(dest_block, lambda i: (i, 0, 0), memory_space=pltpu.SMEM),
            pl.BlockSpec(dest_block, lambda i: (jnp.minimum(i + 1, steps - 1), 0, 0), memory_space=pltpu.SMEM),
            pl.BlockSpec((MOVE_ROWS, EXPERTS_PER_TOKEN), lambda i: (i, 0)),
            pl.BlockSpec((MOVE_ROWS * ROW_TILE, LANES), lambda i: (i, 0)),
        ] + [pl.BlockSpec(memory_space=pl.ANY)] * MOE_TOKEN_PARTS,
        out_specs=pl.BlockSpec((MOVE_ROWS, D_MODEL), lambda i: (i, 0)),
        out_shape=jax.ShapeDtypeStruct((n, D_MODEL), F32),
        scratch_shapes=[
            pltpu.VMEM((2, EXPERTS_PER_TOKEN, MOVE_ROWS * ROW_TILE, LANES), F32),
            pltpu.SemaphoreType.DMA((2,)),
        ],
        compiler_params=_params("arbitrary"),
        name="moe_combine",
    )(dest3, dest3, wts, x1, *outs_parts)


def _layer(x, norm1_g, w_in, diff_q_g, diff_k_g, lambda_q1, lambda_k1, lambda_q2, lambda_k2, diff_sub_g,
           moba_q_g, moba_k_g, rel_bias, w_out, norm2_g, router_group, router_expert, w_gate, w_up, w_down):
    batch, seq, _ = x.shape
    n = batch * seq
    n_tiles = seq // ATT_TILE
    scale = HEAD_DIM ** -0.5 * LOG2E

    d3 = 3 * DIFF_WIDTH
    w_perm = jnp.concatenate([w_in[:, :2 * DIFF_WIDTH], w_in[:, d3:d3 + 2 * MOBA_WIDTH],
                              w_in[:, 2 * DIFF_WIDTH:d3], w_in[:, d3 + 2 * MOBA_WIDTH:]], axis=1).astype(BF16)
    reps_d, reps_m = DIFF_WIDTH // HEAD_DIM, MOBA_WIDTH // HEAD_DIM
    post_gain = jnp.concatenate([jnp.tile(diff_q_g * scale, reps_d), jnp.tile(diff_k_g, reps_d),
                                 jnp.tile(moba_q_g * scale, reps_m), jnp.tile(moba_k_g, reps_m)])[None, :]
    head_of = np.arange(2 * LANES) // HEAD_DIM
    grp = jnp.asarray((head_of[:, None] == head_of[None, :]) / HEAD_DIM, BF16)

    bias = _bias_tiles(rel_bias, n_tiles)
    qk, vt4 = _in_proj(x.reshape(n, D_MODEL), norm1_g[None, :], w_perm, post_gain, grp, batch, seq)
    qk4 = qk.reshape(batch, n_tiles, ATT_TILE, QK_WIDTH)
    lam_vecs = [v[None, :] for v in (lambda_q1, lambda_k1, lambda_q2, lambda_k2)]
    y_d = _diff_attention(_logits_bounded(diff_q_g * scale, diff_k_g, rel_bias[:, :HEADS_DIFF]),
                          lam_vecs, diff_sub_g[:, None], qk4, vt4, bias)
    y_m = _moba_attention(_logits_bounded(moba_q_g * scale, moba_k_g, rel_bias[:, HEADS_DIFF:]),
                          qk4, vt4, bias.reshape(N_HEADS_TOTAL // 2, 2, n_tiles, ATT_TILE, ATT_TILE))

    w_router = jnp.concatenate([
        router_expert.transpose(0, 2, 1).reshape(N_EXPERTS, D_MODEL), router_group.T,
        jnp.zeros((ROUTER_ROWS - N_EXPERTS - N_GROUPS, D_MODEL), F32)], axis=0).astype(BF16)
    tri = jnp.asarray(np.triu(np.ones((ROUTE_TILE, ROUTE_TILE)), k=1), BF16)
    wo = w_out.astype(BF16)
    x1, route_i, route_f, cnt = _out_proj(y_d.reshape(n, DIFF_WIDTH), y_m.reshape(n, MOBA_WIDTH),
                                          x.reshape(n, D_MODEL), wo[:DIFF_WIDTH], wo[DIFF_WIDTH:],
                                          norm2_g[None, :], w_router, tri)

    parts = MOE_TOKEN_PARTS
    counts = cnt[:, :, 0]
    pad_counts = ((counts + MOE_CHUNK - 1) // MOE_CHUNK) * MOE_CHUNK
    pad_end = jnp.cumsum(pad_counts, axis=1)
    pad_start = pad_end - pad_counts
    experts = jnp.arange(N_EXPERTS, dtype=jnp.int32)
    picked = route_i[0:2].reshape(EXPERTS_PER_TOKEN, parts, n // parts)
    start_of = jnp.sum(jnp.where(picked[..., None] == experts, pad_start[None, :, None, :], 0), axis=-1)
    dest = start_of.reshape(EXPERTS_PER_TOKEN, n) + route_i[2:4]
    by_tile = lambda rows: dest.reshape(EXPERTS_PER_TOKEN, n // rows, rows).transpose(1, 0, 2)
    dest3, dest3_fused, dest3_first = by_tile(MOVE_ROWS), by_tile(FUSED_MOVE_ROWS), by_tile(DISPATCH_ROWS)
    n_slots = n // parts * EXPERTS_PER_TOKEN + N_EXPERTS * MOE_CHUNK
    n_chunks = n_slots // MOE_CHUNK
    chunk_id = jnp.arange(n_chunks, dtype=jnp.int32)
    chunk_e = jnp.minimum(jnp.sum((pad_end[:, None, :] <= chunk_id[None, :, None] * MOE_CHUNK).astype(jnp.int32),
                                  axis=2), N_EXPERTS - 1)
    n_active = (pad_end[:, -1:] // MOE_CHUNK).astype(jnp.int32)
    next_e = jnp.concatenate([chunk_e[:, 1:], jnp.full((parts, 1), N_EXPERTS, jnp.int32)], axis=1)
    zchunk = ((chunk_id[None, :] >= n_active - 1) | (chunk_e != next_e)).astype(jnp.int32)

    weights = (norm2_g[None, :], w_gate.astype(BF16), w_up.astype(BF16), w_down.astype(BF16))
    xs = _dispatch(0, zchunk[0], dest3_first, x1, n_slots)
    outs = []
    for p in range(parts):
        if p + 1 < parts:
            out_p, xs_next = _experts_dispatch(chunk_e[p], n_active[p], zchunk[p + 1], xs, *weights, p + 1,
                                               dest3_fused, x1)
        else:
            out_p, xs_next = _experts(chunk_e[p], n_active[p], xs, *weights), None
        outs.append(out_p)
        xs = xs_next
    y = _combine(dest3, route_f[0:2].T, x1, outs)
    return y.reshape(batch, seq, D_MODEL)


def kernel(x, norm1_g, w_in, diff_q_g, diff_k_g, lambda_q1, lambda_k1, lambda_q2, lambda_k2, diff_sub_g,
           moba_q_g, moba_k_g, rel_bias, w_out, norm2_g, router_group, router_expert, w_gate, w_up, w_down):
    assert x.shape[1] % PROJ_ROWS == 0 and x.shape[2] == D_MODEL and norm1_g.shape[0] == 1
    return _layer(x, norm1_g[0], w_in[0], diff_q_g[0], diff_k_g[0], lambda_q1[0], lambda_k1[0], lambda_q2[0],
                  lambda_k2[0], diff_sub_g[0], moba_q_g[0], moba_k_g[0], rel_bias, w_out[0], norm2_g[0],
                  router_group[0], router_expert[0], w_gate[0], w_up[0], w_down[0])
```
